```python
import jax, jax.numpy as jnp
from jax import lax
import numpy as np

D_MODEL = 2048
BATCH = 4
SEQ = 4096
DEPTH = 2

HEAD_DIM = 128
D_CONV = D_MODEL // 4
CONV_GROUPS = D_CONV // HEAD_DIM
CONV_WIDTH = 3
N_NSA_HEADS = D_MODEL // 2 // HEAD_DIM
N_NSA_KV = 2
NSA_REP = N_NSA_HEADS // N_NSA_KV
D_NSA = N_NSA_HEADS * HEAD_DIM
D_NSA_KV = N_NSA_KV * HEAD_DIM
CMP_LEN = 32
CMP_STRIDE = 16
CMP_HIDDEN = 256
SLC_LEN = 64
N_SLC = 16
WINDOW = 512
Q_BLOCK = 64
ROPE_THETA = 500000.0
ROPE_DIM = HEAD_DIM // 4
N_RET_HEADS = D_MODEL // 4 // HEAD_DIM
D_RET = N_RET_HEADS * HEAD_DIM
RET_THETA = 10000.0
RET_CHUNK = 128
D_MIX = D_CONV + D_NSA + D_RET
N_EXPERTS = 64
N_EXPERT_GROUPS = 8
TOPK_GROUPS = 4
TOP_K = 8
D_EXPERT = 512
D_SHARED = 512
ROUTED_SCALE = 2.5
MOE_BLOCK = 256
ADA_CHUNKS = 6
EPS = 1e-6
NEG_INF = -1e30
FORCE_SCORE = 1e4
IN_SIZES = (D_CONV, D_CONV, D_CONV, D_NSA, 6 * D_NSA_KV, 3 * N_NSA_HEADS, D_RET, D_RET, D_RET, D_RET)
D_IN = sum(IN_SIZES)

kernel_name = "hybrid_conv_nsa_retention_moe_adaln"


def rms_norm(x, g):
    xf = x.astype(jnp.float32)
    xf = xf * lax.rsqrt(jnp.mean(xf * xf, axis=-1, keepdims=True) + EPS)
    return (xf * g.astype(jnp.float32)).astype(x.dtype)


def group_rms_norm(y, g, n_groups):
    shp = y.shape
    yf = y.astype(jnp.float32).reshape(shp[:-1] + (n_groups, shp[-1] // n_groups))
    yf = yf * lax.rsqrt(jnp.mean(yf * yf, axis=-1, keepdims=True) + EPS)
    return (yf.reshape(shp) * g.astype(jnp.float32)).astype(y.dtype)


def rotary(x, pos, rot_dim, theta):
    half = rot_dim // 2
    inv = theta ** (-jnp.arange(half, dtype=jnp.float32) / half)
    ang = pos.astype(jnp.float32)[..., None] * inv
    ang = ang.reshape(ang.shape[:2] + (1,) * (x.ndim - 3) + (half,))
    cos, sin = jnp.cos(ang), jnp.sin(ang)
    xr = x[..., :rot_dim].astype(jnp.float32)
    x1, x2 = xr[..., :half], xr[..., half:]
    rot = jnp.concatenate([x1 * cos - x2 * sin, x1 * sin + x2 * cos], axis=-1).astype(x.dtype)
    return jnp.concatenate([rot, x[..., rot_dim:]], axis=-1)


def short_conv_mixer(gate_b, gate_c, u, conv_w, norm_g):
    v = jnp.swapaxes(gate_c * u, 1, 2)
    y = lax.conv_general_dilated(v, conv_w.astype(v.dtype)[:, None, :], window_strides=(1,),
                                 padding=[(CONV_WIDTH - 1, 0)], dimension_numbers=("NCH", "OIH", "NCH"),
                                 feature_group_count=D_CONV)
    y = gate_b * jnp.swapaxes(y, 1, 2)
    return group_rms_norm(y, norm_g, CONV_GROUPS)


def compress_blocks(kv, pe, w1, w2):
    b, g, s, hd = kv.shape
    n_cmp = (s - CMP_LEN) // CMP_STRIDE + 1
    idx = np.arange(n_cmp)[:, None] * CMP_STRIDE + np.arange(CMP_LEN)[None, :]
    blocks = (kv[:, :, idx, :] + pe).reshape(b, g, n_cmp, CMP_LEN * hd)
    return jax.nn.gelu(blocks @ w1) @ w2


def cmp_slc_overlap(n_cmp, n_slc):
    cs = np.arange(n_cmp) * CMP_STRIDE
    js = np.arange(n_slc) * SLC_LEN
    ov = np.minimum(cs[:, None] + CMP_LEN, js[None, :] + SLC_LEN) - np.maximum(cs[:, None], js[None, :])
    return (np.clip(ov, 0, None) / CMP_LEN).astype(np.float32)


def nsa_mixer(q, k, v, gate_logits, cmp_pe, cmp_w1, cmp_w2, norm_g):
    b, s = q.shape[:2]
    scale = HEAD_DIM ** -0.5
    qg = q.reshape(b, s, N_NSA_KV, NSA_REP, HEAD_DIM).transpose(0, 2, 3, 1, 4)
    k = k.transpose(2, 0, 3, 1, 4)
    v = v.transpose(2, 0, 3, 1, 4)
    t = jnp.arange(s)
    k_c = compress_blocks(k[0], cmp_pe[0], cmp_w1[0], cmp_w2[0])
    v_c = compress_blocks(v[0], cmp_pe[1], cmp_w1[1], cmp_w2[1])
    n_cmp = k_c.shape[2]
    cmp_end = jnp.arange(n_cmp) * CMP_STRIDE + CMP_LEN - 1
    cmask = cmp_end[None, :] <= t[:, None]
    s_c = jnp.einsum("bgrqd,bgcd->bgrqc", qg, k_c).astype(jnp.float32) * scale
    p_c = jax.nn.softmax(jnp.where(cmask, s_c, NEG_INF), axis=-1) * cmask
    o_cmp = jnp.einsum("bgrqc,bgcd->bgrqd", p_c.astype(v_c.dtype), v_c)
    n_slc = s // SLC_LEN
    n_sel = min(N_SLC, n_slc)
    imp = jnp.einsum("bgrqc,cj->bgqj", p_c, jnp.asarray(cmp_slc_overlap(n_cmp, n_slc)))
    j = jnp.arange(n_slc)[None, :]
    jq = (t // SLC_LEN)[:, None]
    forced = (j == 0) | (j == jq) | (j == jq - 1)
    imp = jnp.where(forced, FORCE_SCORE, jnp.where(j <= jq, imp, -1.0))
    _, sel = lax.top_k(imp, n_sel)
    k_s, v_s = k[1], v[1]
    pad = ((0, 0), (0, 0), (WINDOW, 0), (0, 0))
    k_w, v_w = jnp.pad(k[2], pad), jnp.pad(v[2], pad)
    bi = jnp.arange(b)[:, None, None]
    gi = jnp.arange(N_NSA_KV)[None, :, None]
    span = WINDOW + Q_BLOCK
    n_keys = n_sel * SLC_LEN

    def query_block(i):
        t0 = i * Q_BLOCK
        qb = lax.dynamic_slice_in_dim(qg, t0, Q_BLOCK, axis=3)
        tq = t0 + jnp.arange(Q_BLOCK)
        blk = lax.dynamic_slice_in_dim(sel, t0, Q_BLOCK, axis=2)
        tok = (blk[..., None] * SLC_LEN + jnp.arange(SLC_LEN)).reshape(b, N_NSA_KV, Q_BLOCK * n_keys)
        ks = k_s[bi, gi, tok].reshape(b, N_NSA_KV, Q_BLOCK, n_keys, HEAD_DIM)
        vs = v_s[bi, gi, tok].reshape(b, N_NSA_KV, Q_BLOCK, n_keys, HEAD_DIM)
        smask = (tok.reshape(b, N_NSA_KV, Q_BLOCK, n_keys) <= tq[:, None])[:, :, None]
        ss = jnp.einsum("bgrqd,bgqkd->bgrqk", qb, ks).astype(jnp.float32) * scale
        ps = jax.nn.softmax(jnp.where(smask, ss, NEG_INF), axis=-1)
        o_s = jnp.einsum("bgrqk,bgqkd->bgrqd", ps.astype(vs.dtype), vs)
        kw = lax.dynamic_slice_in_dim(k_w, t0, span, axis=2)
        vw = lax.dynamic_slice_in_dim(v_w, t0, span, axis=2)
        sk = t0 - WINDOW + jnp.arange(span)
        diff = tq[:, None] - sk[None, :]
        wmask = (diff >= 0) & (diff < WINDOW) & (sk[None, :] >= 0)
        sw = jnp.einsum("bgrqd,bgkd->bgrqk", qb, kw).astype(jnp.float32) * scale
        pw = jax.nn.softmax(jnp.where(wmask, sw, NEG_INF), axis=-1)
        o_w = jnp.einsum("bgrqk,bgkd->bgrqd", pw.astype(vw.dtype), vw)
        return o_s, o_w

    o_slc, o_win = lax.map(query_block, jnp.arange(s // Q_BLOCK))
    unblock = lambda o: o.transpose(1, 2, 3, 0, 4, 5).reshape(b, N_NSA_KV, NSA_REP, s, HEAD_DIM)
    o_slc, o_win = unblock(o_slc), unblock(o_win)
    gates = jax.nn.sigmoid(gate_logits.astype(jnp.float32)).reshape(b, s, N_NSA_KV, NSA_REP, 3)
    gates = gates.transpose(0, 2, 3, 1, 4)[..., None]
    o = gates[..., 0, :] * o_cmp + gates[..., 1, :] * o_slc + gates[..., 2, :] * o_win
    o = o.transpose(0, 3, 1, 2, 4).reshape(b, s, D_NSA).astype(q.dtype)
    return group_rms_norm(o, norm_g, N_NSA_HEADS)


def retention_mixer(q, k, v, g, norm_g):
    b, s, h, hd = q.shape
    nc = s // RET_CHUNK
    f32 = jnp.float32
    log_gamma = jnp.log1p(-(2.0 ** (-5.0 - jnp.arange(h, dtype=f32))))
    chunk = lambda a: a.astype(f32).transpose(0, 2, 1, 3).reshape(b, h, nc, RET_CHUNK, hd)
    qc, kc, vc = chunk(q), chunk(k) * hd ** -0.5, chunk(v)
    n = jnp.arange(RET_CHUNK, dtype=f32)
    diff = n[:, None] - n[None, :]
    decay = jnp.where(diff >= 0, jnp.exp(jnp.maximum(diff, 0.0) * log_gamma[:, None, None]), 0.0)
    scores = jnp.einsum("bhcnd,bhcmd->bhcnm", qc, kc) * decay[None, :, None]
    o_intra = jnp.einsum("bhcnm,bhcme->bhcne", scores, vc)
    zeta = jnp.exp((RET_CHUNK - 1 - n) * log_gamma[:, None])
    kv_chunk = jnp.einsum("bhcmd,bhcme->bhcde", kc * zeta[None, :, None, :, None], vc)
    chunk_decay = jnp.exp(RET_CHUNK * log_gamma)[None, :, None, None]

    def step(state, kv):
        return state * chunk_decay + kv, state

    _, states = lax.scan(step, jnp.zeros((b, h, hd, hd), f32), jnp.moveaxis(kv_chunk, 2, 0))
    xi = jnp.exp((n + 1.0) * log_gamma[:, None])
    o_inter = jnp.einsum("bhcnd,cbhde->bhcne", qc * xi[None, :, None, :, None], states)
    o = (o_intra + o_inter).reshape(b, h, s, hd).transpose(0, 2, 1, 3).reshape(b, s, h * hd)
    o = group_rms_norm(o, norm_g, h)
    return (jax.nn.silu(g.astype(f32)) * o.astype(f32)).astype(g.dtype)


def token_mixer(h, positions, w_in, conv_w, conv_g, cmp_pe, cmp_w1, cmp_w2, nsa_g, ret_g, w_out):
    b, s, _ = h.shape
    splits = np.cumsum(IN_SIZES)[:-1].tolist()
    cb, cc, cu, q_n, kv_n, gate_n, q_r, k_r, v_r, g_r = jnp.split(h @ w_in, splits, axis=-1)
    y_conv = short_conv_mixer(cb, cc, cu, conv_w, conv_g)
    q_n = rotary(q_n.reshape(b, s, N_NSA_HEADS, HEAD_DIM), positions, ROPE_DIM, ROPE_THETA)
    kv_n = kv_n.reshape(b, s, 3, 2, N_NSA_KV, HEAD_DIM)
    k_n = rotary(kv_n[:, :, :, 0], positions, ROPE_DIM, ROPE_THETA)
    y_nsa = nsa_mixer(q_n, k_n, kv_n[:, :, :, 1], gate_n, cmp_pe, cmp_w1, cmp_w2, nsa_g)
    heads = lambda a: a.reshape(b, s, N_RET_HEADS, HEAD_DIM)
    q_r = rotary(heads(q_r), positions, HEAD_DIM, RET_THETA)
    k_r = rotary(heads(k_r), positions, HEAD_DIM, RET_THETA)
    y_ret = retention_mixer(q_r, k_r, heads(v_r), g_r, ret_g)
    return jnp.concatenate([y_conv, y_nsa, y_ret], axis=-1) @ w_out


def swiglu(t, w_gu, w_down):
    gate, up = jnp.split(t @ w_gu, 2, axis=-1)
    return (jax.nn.silu(gate) * up) @ w_down


def moe_ffn(h, router_w, router_bias, w_gu, w_down, shared_gu, shared_down):
    b, s, d = h.shape
    n_tok = b * s
    t = h.reshape(n_tok, d)
    scores = jax.nn.sigmoid((t @ router_w).astype(jnp.float32))
    biased = scores + router_bias.astype(jnp.float32)
    per_group = N_EXPERTS // N_EXPERT_GROUPS
    group_score = lax.top_k(biased.reshape(n_tok, N_EXPERT_GROUPS, per_group), 2)[0].sum(-1)
    _, top_groups = lax.top_k(group_score, TOPK_GROUPS)
    group_mask = jnp.any(top_groups[..., None] == jnp.arange(N_EXPERT_GROUPS), axis=1)
    biased = jnp.where(jnp.repeat(group_mask, per_group, axis=1), biased, -jnp.inf)
    _, top_e = lax.top_k(biased, TOP_K)
    w = jnp.take_along_axis(scores, top_e, axis=1)
    w = ROUTED_SCALE * w / jnp.sum(w, axis=-1, keepdims=True)
    n_assign = n_tok * TOP_K
    n_blocks = (n_assign + N_EXPERTS * (MOE_BLOCK - 1) + MOE_BLOCK - 1) // MOE_BLOCK
    flat_e = top_e.reshape(-1)
    order = jnp.argsort(flat_e)
    sorted_e = flat_e[order]
    counts = jnp.bincount(flat_e, length=N_EXPERTS)
    padded = (counts + MOE_BLOCK - 1) // MOE_BLOCK * MOE_BLOCK
    pad_end = jnp.cumsum(padded)
    dest = (pad_end - padded)[sorted_e] + jnp.arange(n_assign) - (jnp.cumsum(counts) - counts)[sorted_e]
    tok = order // TOP_K
    xs = jnp.zeros((n_blocks * MOE_BLOCK, d), t.dtype).at[dest].set(t[tok])
    block_e = jnp.minimum(jnp.searchsorted(pad_end, jnp.arange(n_blocks) * MOE_BLOCK, side="right"), N_EXPERTS - 1)

    def expert_block(args):
        xb, e = args
        return swiglu(xb, w_gu[e], w_down[e])

    ys = lax.map(expert_block, (xs.reshape(n_blocks, MOE_BLOCK, d), block_e)).reshape(n_blocks * MOE_BLOCK, d)
    y = ys[dest] * w.reshape(-1)[order][:, None].astype(ys.dtype)
    routed = jnp.zeros_like(t).at[tok].add(y)
    return (routed + swiglu(t, shared_gu, shared_down)).reshape(b, s, d)


def setup_inputs(seed: int = 0) -> dict:
    key = jax.random.key(seed)
    ks = jax.random.split(key, 24)
    L = DEPTH

    def nrm(k, shape, scale):
        return jax.random.normal(k, shape, jnp.float32) * scale

    offset = jax.random.randint(ks[2], (BATCH, 1), 0, 1024, dtype=jnp.int32)
    positions = offset + jnp.arange(SEQ, dtype=jnp.int32)[None, :]
    return {
        "x": nrm(ks[0], (BATCH, SEQ, D_MODEL), 1.0),
        "c": nrm(ks[1], (BATCH, D_MODEL), 1.0),
        "positions": positions,
        "ada_w": nrm(ks[3], (L, D_MODEL, ADA_CHUNKS * D_MODEL), 0.5 * D_MODEL ** -0.5),
        "ada_b": nrm(ks[4], (L, ADA_CHUNKS * D_MODEL), 0.02),
        "norm1_g": 1.0 + nrm(ks[5], (L, D_MODEL), 0.1),
        "norm2_g": 1.0 + nrm(ks[6], (L, D_MODEL), 0.1),
        "w_in": nrm(ks[7], (L, D_MODEL, D_IN), D_MODEL ** -0.5),
        "conv_w": nrm(ks[8], (L, D_CONV, CONV_WIDTH), CONV_WIDTH ** -0.5),
        "conv_g": 1.0 + nrm(ks[9], (L, D_CONV), 0.1),
        "cmp_pe": nrm(ks[10], (L, 2, CMP_LEN, HEAD_DIM), 0.5),
        "cmp_w1": nrm(ks[11], (L, 2, CMP_LEN * HEAD_DIM, CMP_HIDDEN), (CMP_LEN * HEAD_DIM) ** -0.5),
        "cmp_w2": nrm(ks[12], (L, 2, CMP_HIDDEN, HEAD_DIM), CMP_HIDDEN ** -0.5),
        "nsa_g": 1.0 + nrm(ks[13], (L, D_NSA), 0.1),
        "ret_g": 1.0 + nrm(ks[14], (L, D_RET), 0.1),
        "w_out": nrm(ks[15], (L, D_MIX, D_MODEL), D_MIX ** -0.5),
        "router_w": nrm(ks[16], (L, D_MODEL, N_EXPERTS), D_MODEL ** -0.5),
        "router_bias": nrm(ks[17], (L, N_EXPERTS), 0.01),
        "exp_w_gu": nrm(ks[18], (L, N_EXPERTS, D_MODEL, 2 * D_EXPERT), D_MODEL ** -0.5),
        "exp_w_down": nrm(ks[19], (L, N_EXPERTS, D_EXPERT, D_MODEL), D_EXPERT ** -0.5),
        "shared_w_gu": nrm(ks[20], (L, D_MODEL, 2 * D_SHARED), D_MODEL ** -0.5),
        "shared_w_down": nrm(ks[21], (L, D_SHARED, D_MODEL), D_SHARED ** -0.5),
        "final_g": 1.0 + nrm(ks[22], (D_MODEL,), 0.1),
    }


def reference(x, c, positions, ada_w, ada_b, norm1_g, norm2_g, w_in, conv_w, conv_g, cmp_pe, cmp_w1, cmp_w2,
              nsa_g, ret_g, w_out, router_w, router_bias, exp_w_gu, exp_w_down, shared_w_gu, shared_w_down, final_g):
    c_act = jax.nn.silu(c)
    for l in range(DEPTH):
        mod = (c_act @ ada_w[l] + ada_b[l])[:, None, :]
        sh1, sc1, g1, sh2, sc2, g2 = jnp.split(mod, ADA_CHUNKS, axis=-1)
        h = rms_norm(x, norm1_g[l]) * (1.0 + sc1) + sh1
        x = x + g1 * token_mixer(h, positions, w_in[l], conv_w[l], conv_g[l], cmp_pe[l], cmp_w1[l], cmp_w2[l],
                                 nsa_g[l], ret_g[l], w_out[l])
        h = rms_norm(x, norm2_g[l]) * (1.0 + sc2) + sh2
        x = x + g2 * moe_ffn(h, router_w[l], router_bias[l], exp_w_gu[l], exp_w_down[l],
                             shared_w_gu[l], shared_w_down[l])
    return rms_norm(x, final_g)
```

```python
import functools

import numpy as np
import jax
import jax.numpy as jnp
from jax import lax
from jax.experimental import pallas as pl
from jax.experimental.pallas import tpu as pltpu

F32 = jnp.float32
BF16 = jnp.bfloat16

D_MODEL = 2048
HEAD_DIM = 128
D_CONV = D_MODEL // 4
CONV_GROUPS = D_CONV // HEAD_DIM
CONV_WIDTH = 3
N_NSA_HEADS = D_MODEL // 2 // HEAD_DIM
N_NSA_KV = 2
NSA_REP = N_NSA_HEADS // N_NSA_KV
D_NSA = N_NSA_HEADS * HEAD_DIM
D_NSA_KV = N_NSA_KV * HEAD_DIM
CMP_LEN = 32
CMP_STRIDE = 16
CMP_HIDDEN = 256
SLC_LEN = 64
N_SLC = 16
WINDOW = 512
ROPE_THETA = 500000.0
ROPE_DIM = HEAD_DIM // 4
N_RET_HEADS = D_MODEL // 4 // HEAD_DIM
D_RET = N_RET_HEADS * HEAD_DIM
RET_THETA = 10000.0
D_MIX = D_CONV + D_NSA + D_RET
N_EXPERTS = 64
N_EXPERT_GROUPS = 8
GROUP_SIZE = N_EXPERTS // N_EXPERT_GROUPS
TOPK_GROUPS = 4
TOP_K = 8
D_EXPERT = 512
D_SHARED = 512
ROUTED_SCALE = 2.5
MOE_BLOCK = 256
ADA_CHUNKS = 6
EPS = 1e-6
NEG_INF = -1e30
FORCE_SCORE = 1e4

LANE = 128
GATE_COLS = N_NSA_KV * LANE
D_PROJ = 3 * D_CONV + D_NSA + 6 * D_NSA_KV + 4 * D_RET + GATE_COLS
COL_Q = 3 * D_CONV
COL_KV = COL_Q + D_NSA
COL_RET = COL_KV + 6 * D_NSA_KV
COL_GATE = COL_RET + 4 * D_RET

VMEM_LIMIT = 56 * 1024 * 1024


def _cparams(sem):
    return pltpu.CompilerParams(dimension_semantics=sem, vmem_limit_bytes=VMEM_LIMIT)


def _dot(a, b, **kw):
    return jnp.dot(a, b, preferred_element_type=F32, **kw)


def _dot_nt(a, b, **kw):
    return lax.dot_general(a, b, (((1,), (1,)), ((), ())), preferred_element_type=F32, **kw)


def _rms(x):
    return x * lax.rsqrt(jnp.mean(x * x, axis=-1, keepdims=True) + EPS)


def _ada_kernel(c_ref, w_ref, b_ref, o_ref):
    c = c_ref[...]
    ca = (c * jax.nn.sigmoid(c)).astype(BF16)
    o_ref[0] = _dot(ca, w_ref[0].astype(BF16)) + b_ref[0]


def ada_modulation(c, ada_w, ada_b):
    L, d, n = ada_w.shape
    b = c.shape[0]
    tn = 1024
    return pl.pallas_call(
        _ada_kernel,
        grid=(L, n // tn),
        in_specs=[
            pl.BlockSpec((b, d), lambda l, j: (0, 0)),
            pl.BlockSpec((1, d, tn), lambda l, j: (l, 0, j)),
            pl.BlockSpec((1, 1, tn), lambda l, j: (l, 0, j)),
        ],
        out_specs=pl.BlockSpec((1, b, tn), lambda l, j: (l, 0, j)),
        out_shape=jax.ShapeDtypeStruct((L, b, n), F32),
        compiler_params=_cparams(("arbitrary", "arbitrary")),
        name="ada_modulation",
    )(c, ada_w, ada_b.reshape(L, 1, n))


def _norm_mod(x, g, sc, sh):
    return (_rms(x) * g) * (1.0 + sc) + sh


def _in_proj_kernel(x_ref, sh_ref, sc_ref, g_ref, w_ref, o_ref, h_scr):
    @pl.when(pl.program_id(1) == 0)
    def _():
        h_scr[...] = _norm_mod(x_ref[...], g_ref[...], sc_ref[0], sh_ref[0]).astype(BF16)

    o_ref[...] = _dot(h_scr[...], w_ref[...])


def in_projection(x, mod, norm_g, w, seq):
    t, d = x.shape
    n = w.shape[1]
    tm, tn = 512, 1280
    per = seq // tm
    return pl.pallas_call(
        _in_proj_kernel,
        grid=(t // tm, n // tn),
        in_specs=[
            pl.BlockSpec((tm, d), lambda i, j: (i, 0)),
            pl.BlockSpec((1, 1, d), lambda i, j: (i // per, 0, 0)),
            pl.BlockSpec((1, 1, d), lambda i, j: (i // per, 0, 1)),
            pl.BlockSpec((1, d), lambda i, j: (0, 0)),
            pl.BlockSpec((d, tn), lambda i, j: (0, j)),
        ],
        out_specs=pl.BlockSpec((tm, tn), lambda i, j: (i, j)),
        out_shape=jax.ShapeDtypeStruct((t, n), F32),
        scratch_shapes=[pltpu.VMEM((tm, d), BF16)],
        compiler_params=_cparams(("arbitrary", "arbitrary")),
        name="in_projection",
    )(x, mod, mod, norm_g, w)


def _rope_table_kernel(pos_ref, inv_ref, sgn_ref, cos_ref, sin_ref):
    ang = pos_ref[...].astype(F32) * inv_ref[...]
    cos_ref[...] = jnp.cos(ang)
    sin_ref[...] = jnp.sin(ang) * sgn_ref[...]


def rope_tables(pos_col, rot_dim, theta):
    t = pos_col.shape[0]
    half = rot_dim // 2
    inv_half = theta ** (-jnp.arange(half, dtype=F32) / half)
    inv = jnp.concatenate([inv_half, inv_half, jnp.zeros((LANE - rot_dim,), F32)]).reshape(1, LANE)
    sgn = np.zeros((1, LANE), np.float32)
    sgn[0, :half] = -1.0
    sgn[0, half:rot_dim] = 1.0
    ts = 1024
    return pl.pallas_call(
        _rope_table_kernel,
        grid=(t // ts,),
        in_specs=[
            pl.BlockSpec((ts, 1), lambda i: (i, 0)),
            pl.BlockSpec((1, LANE), lambda i: (0, 0)),
            pl.BlockSpec((1, LANE), lambda i: (0, 0)),
        ],
        out_specs=[pl.BlockSpec((ts, LANE), lambda i: (i, 0))] * 2,
        out_shape=[jax.ShapeDtypeStruct((t, LANE), F32)] * 2,
        compiler_params=_cparams(("arbitrary",)),
        name="rope_tables",
    )(pos_col, inv, jnp.asarray(sgn))


def _rotate(x, cos, sin_signed, half):
    if 2 * half == LANE:
        swapped = pltpu.roll(x, half, 1)
    else:
        lane = lax.broadcasted_iota(jnp.int32, x.shape, 1)
        swapped = jnp.where(lane < half, pltpu.roll(x, LANE - half, 1), pltpu.roll(x, half, 1))
    return x * cos + swapped * sin_signed


def _nsa_rope_kernel(q0_ref, q1_ref, kvc_ref, kvs_ref, kvw_ref, cos_ref, sin_ref,
                     q_out, kvc_out, kvs_out, kvw_out):
    cos, sin = cos_ref[...], sin_ref[...]
    half = ROPE_DIM // 2
    hpb = D_CONV // HEAD_DIM
    for blk, src in enumerate((q0_ref, q1_ref)):
        for h in range(hpb):
            sl = slice(h * HEAD_DIM, (h + 1) * HEAD_DIM)
            q_out[:, blk * D_CONV + h * HEAD_DIM: blk * D_CONV + (h + 1) * HEAD_DIM] = (
                _rotate(src[:, sl], cos, sin, half).astype(q_out.dtype))
    for src, dst in ((kvc_ref, kvc_out), (kvs_ref, kvs_out), (kvw_ref, kvw_out)):
        for g in range(N_NSA_KV):
            sl = slice(g * HEAD_DIM, (g + 1) * HEAD_DIM)
            dst[:, sl] = _rotate(src[:, sl], cos, sin, half).astype(dst.dtype)
        sl = slice(D_NSA_KV, 2 * D_NSA_KV)
        dst[:, sl] = src[:, sl].astype(dst.dtype)


def nsa_rope(proj, cos, sin):
    t = proj.shape[0]
    ts = 512
    w = D_CONV
    blk = lambda j: pl.BlockSpec((ts, w), lambda i, j=j: (i, j))
    tab = pl.BlockSpec((ts, LANE), lambda i: (i, 0))
    out = pl.BlockSpec((ts, w), lambda i: (i, 0))
    qb = COL_Q // w
    kb = COL_KV // w
    return pl.pallas_call(
        _nsa_rope_kernel,
        grid=(t // ts,),
        in_specs=[blk(qb), blk(qb + 1), blk(kb), blk(kb + 1), blk(kb + 2), tab, tab],
        out_specs=[pl.BlockSpec((ts, D_NSA), lambda i: (i, 0)), out, out, out],
        out_shape=[
            jax.ShapeDtypeStruct((t, D_NSA), BF16),
            jax.ShapeDtypeStruct((t, w), F32),
            jax.ShapeDtypeStruct((t, w), BF16),
            jax.ShapeDtypeStruct((t, w), BF16),
        ],
        compiler_params=_cparams(("arbitrary",)),
        name="nsa_rope",
    )(proj, proj, proj, proj, proj, cos, sin)


def _conv_kernel(cb_ref, cc_ref, cu_ref, w_ref, g_ref, o_ref, ext):
    ts = cb_ref.shape[0]

    @pl.when(pl.program_id(1) == 0)
    def _():
        ext[0:8, :] = jnp.zeros((8, ext.shape[1]), F32)

    v = cc_ref[...] * cu_ref[...]
    ext[8:, :] = v
    v1 = ext[pl.ds(7, ts), :]
    v2 = ext[pl.ds(6, ts), :]
    y = cb_ref[...] * (w_ref[0:1, :] * v2 + w_ref[1:2, :] * v1 + w_ref[2:3, :] * v)
    ext[0:8, :] = v[ts - 8:, :]
    for gi in range(CONV_GROUPS):
        sl = slice(gi * HEAD_DIM, (gi + 1) * HEAD_DIM)
        o_ref[:, sl] = (_rms(y[:, sl]) * g_ref[:, sl]).astype(o_ref.dtype)


def short_conv(proj, conv_w_t, conv_g, batch, seq):
    t = proj.shape[0]
    ts = 512
    per = seq // ts
    w = D_CONV
    blk = lambda j: pl.BlockSpec((ts, w), lambda b, s, j=j: (b * per + s, j))
    return pl.pallas_call(
        _conv_kernel,
        grid=(batch, per),
        in_specs=[blk(0), blk(1), blk(2),
                  pl.BlockSpec((CONV_WIDTH, w), lambda b, s: (0, 0)),
                  pl.BlockSpec((1, w), lambda b, s: (0, 0))],
        out_specs=pl.BlockSpec((ts, w), lambda b, s: (b * per + s, 0)),
        out_shape=jax.ShapeDtypeStruct((t, w), BF16),
        scratch_shapes=[pltpu.VMEM((8 + ts, w), F32)],
        compiler_params=_cparams(("arbitrary", "arbitrary")),
        name="short_conv",
    )(proj, proj, proj, conv_w_t, conv_g)


def _compress_kernel(h_ref, pe_ref, w1_ref, w2_ref, o_ref):
    h = h_ref[0, 0]
    n_h, half = h.shape
    pe = pe_ref[0]
    a = _dot((h + pe[:, :half]).astype(BF16), w1_ref[0, :half, :])
    b = _dot((h + pe[:, half:]).astype(BF16), w1_ref[0, half:, :])
    pre = a + pltpu.roll(b, n_h - 1, 0)
    o_ref[0, 0] = _dot(jax.nn.gelu(pre).astype(BF16), w2_ref[0])


def compress(kvc_h, pe_flat, w1, w2):
    b, four, n_h, dh = kvc_h.shape
    return pl.pallas_call(
        _compress_kernel,
        grid=(b, four),
        in_specs=[
            pl.BlockSpec((1, 1, n_h, dh), lambda i, j: (i, j, 0, 0)),
            pl.BlockSpec((1, 1, 2 * dh), lambda i, j: (j // N_NSA_KV, 0, 0)),
            pl.BlockSpec((1, 2 * dh, CMP_HIDDEN), lambda i, j: (j // N_NSA_KV, 0, 0)),
            pl.BlockSpec((1, CMP_HIDDEN, HEAD_DIM), lambda i, j: (j // N_NSA_KV, 0, 0)),
        ],
        out_specs=pl.BlockSpec((1, 1, n_h, HEAD_DIM), lambda i, j: (i, j, 0, 0)),
        out_shape=jax.ShapeDtypeStruct((b, four, n_h, HEAD_DIM), F32),
        compiler_params=_cparams(("arbitrary", "arbitrary")),
        name="nsa_compress",
    )(kvc_h, pe_flat, w1, w2)


def _online_softmax_step(s, valid, v, m_scr, l_scr, acc_scr):
    m_old = m_scr[...]
    m_new = jnp.maximum(m_old, jnp.max(s, axis=-1, keepdims=True))
    alpha = jnp.exp(m_old - m_new)
    p = jnp.where(valid, jnp.exp(s - m_new), 0.0)
    l_scr[...] = alpha * l_scr[...] + jnp.sum(p, axis=-1, keepdims=True)
    acc_scr[...] = alpha * acc_scr[...] + _dot(p.astype(BF16), v)
    m_scr[...] = m_new


def _nsa_kernel(q_ref, kc_ref, vc_ref, ks_ref, vs_ref, kw_ref, vw_ref, gate_ref, ov_ref, eaug_ref, g_ref,
                o_ref, kaug_scr, m_scr, l_scr, acc_scr, *, n_cmp, n_slc, n_sel):
    tq = q_ref.shape[0]
    rows = NSA_REP * tq
    i = pl.program_id(2)
    t0 = i * tq
    scale = HEAD_DIM ** -0.5

    @pl.when(i == 0)
    def _():
        kaug_scr[:, :HEAD_DIM] = ks_ref[...]
        kaug_scr[:, HEAD_DIM:] = eaug_ref[...]

    qs = jnp.concatenate([q_ref[:, r * HEAD_DIM:(r + 1) * HEAD_DIM] for r in range(NSA_REP)], axis=0)
    t_row = t0 + lax.rem(lax.broadcasted_iota(jnp.int32, (rows, 1), 0), tq)

    kc = kc_ref[0, 0].astype(BF16)
    vc = vc_ref[0, 0].astype(BF16)
    n_h = kc.shape[0]
    s = _dot_nt(qs, kc) * scale
    c_idx = lax.broadcasted_iota(jnp.int32, (1, n_h), 1)
    cmask = (c_idx * CMP_STRIDE + (CMP_LEN - 1) <= t_row) & (c_idx < n_cmp)
    sm = jnp.where(cmask, s, NEG_INF)
    p = jnp.where(cmask, jnp.exp(sm - jnp.max(sm, axis=-1, keepdims=True)), 0.0)
    denom = jnp.sum(p, axis=-1, keepdims=True)
    p = p / jnp.where(denom > 0.0, denom, 1.0)
    o_cmp = _dot(p.astype(BF16), vc)

    psum = p[0:tq]
    for r in range(1, NSA_REP):
        psum = psum + p[r * tq:(r + 1) * tq]
    imp_t = _dot(psum, ov_ref[...], precision=lax.Precision.HIGHEST).T
    j_idx = lax.broadcasted_iota(jnp.int32, (n_slc, 1), 0)
    jq = (t0 + lax.broadcasted_iota(jnp.int32, (1, tq), 1)) // SLC_LEN
    forced = (j_idx == 0) | (j_idx == jq) | (j_idx == jq - 1)
    val = jnp.where(forced, FORCE_SCORE, jnp.where(j_idx <= jq, imp_t[:n_slc], -1.0))
    rank = jnp.zeros((n_slc, tq), jnp.int32)
    for jp in range(n_slc):
        vp = val[jp:jp + 1, :]
        ahead = (vp > val) | ((vp == val) & (j_idx > jp))
        rank = rank + ahead.astype(jnp.int32)
    bias_t = jnp.where(rank < n_sel, 0.0, NEG_INF)
    if n_slc < LANE:
        bias_t = jnp.concatenate([bias_t, jnp.zeros((LANE - n_slc, tq), F32)], axis=0)
    bias = bias_t.T.astype(BF16)
    qaug = jnp.concatenate([qs, jnp.concatenate([bias] * NSA_REP, axis=0)], axis=1)

    m_scr[...] = jnp.full((rows, 1), NEG_INF, F32)
    l_scr[...] = jnp.zeros((rows, 1), F32)
    acc_scr[...] = jnp.zeros((rows, HEAD_DIM), F32)

    def sel_body(jt, carry):
        k0 = pl.multiple_of(jt * tq, tq)
        s = _dot_nt(qaug, kaug_scr[pl.ds(k0, tq), :]) * scale
        key = k0 + lax.broadcasted_iota(jnp.int32, (1, tq), 1)
        valid = key <= t_row
        _online_softmax_step(jnp.where(valid, s, NEG_INF), valid, vs_ref[pl.ds(k0, tq), :], m_scr, l_scr, acc_scr)
        return carry

    lax.fori_loop(0, i + 1, sel_body, 0)
    o_slc = acc_scr[...] / l_scr[...]

    m_scr[...] = jnp.full((rows, 1), NEG_INF, F32)
    l_scr[...] = jnp.zeros((rows, 1), F32)
    acc_scr[...] = jnp.zeros((rows, HEAD_DIM), F32)

    def win_body(jt, carry):
        k0 = pl.multiple_of(jt * tq, tq)
        s = _dot_nt(qs, kw_ref[pl.ds(k0, tq), :]) * scale
        key = k0 + lax.broadcasted_iota(jnp.int32, (1, tq), 1)
        valid = (key <= t_row) & (key > t_row - WINDOW)
        _online_softmax_step(jnp.where(valid, s, NEG_INF), valid, vw_ref[pl.ds(k0, tq), :], m_scr, l_scr, acc_scr)
        return carry

    lax.fori_loop(jnp.maximum(i - WINDOW // tq, 0), i + 1, win_body, 0)
    o_win = acc_scr[...] / l_scr[...]

    gates = jax.nn.sigmoid(gate_ref[...])
    for r in range(NSA_REP):
        rs = slice(r * tq, (r + 1) * tq)
        o = (gates[:, 3 * r:3 * r + 1] * o_cmp[rs] + gates[:, 3 * r + 1:3 * r + 2] * o_slc[rs]
             + gates[:, 3 * r + 2:3 * r + 3] * o_win[rs])
        sl = slice(r * HEAD_DIM, (r + 1) * HEAD_DIM)
        o_ref[:, sl] = (_rms(o) * g_ref[:, sl]).astype(o_ref.dtype)


def _overlap_matrix(n_h, n_cmp, n_slc):
    cs = np.arange(n_cmp) * CMP_STRIDE
    js = np.arange(n_slc) * SLC_LEN
    ov = np.minimum(cs[:, None] + CMP_LEN, js[None, :] + SLC_LEN) - np.maximum(cs[:, None], js[None, :])
    out = np.zeros((n_h, LANE), np.float32)
    out[:n_cmp, :n_slc] = np.clip(ov, 0, None) / CMP_LEN
    return out


def nsa_attention(q_rot, kvc_c, kvs, kvw, proj, nsa_g, batch, seq):
    t = q_rot.shape[0]
    tq = 256
    per = seq // tq
    n_h = seq // CMP_STRIDE
    n_cmp = (seq - CMP_LEN) // CMP_STRIDE + 1
    n_slc = seq // SLC_LEN
    n_sel = min(N_SLC, n_slc)
    assert n_slc <= LANE and WINDOW % tq == 0
    ov = jnp.asarray(_overlap_matrix(n_h, n_cmp, n_slc))
    eaug = np.zeros((seq, LANE), np.float32)
    eaug[np.arange(seq), np.arange(seq) // SLC_LEN] = 1.0
    eaug = jnp.asarray(eaug, BF16)
    hd = HEAD_DIM
    qw = NSA_REP * hd
    cmp_spec = lambda off: pl.BlockSpec((1, 1, n_h, hd), lambda b, g, i, off=off: (b, off + g, 0, 0))
    seq_spec = lambda off: pl.BlockSpec((seq, hd), lambda b, g, i, off=off: (b, off + g))
    kern = functools.partial(_nsa_kernel, n_cmp=n_cmp, n_slc=n_slc, n_sel=n_sel)
    return pl.pallas_call(
        kern,
        grid=(batch, N_NSA_KV, per),
        in_specs=[
            pl.BlockSpec((tq, qw), lambda b, g, i: (b * per + i, g)),
            cmp_spec(0), cmp_spec(N_NSA_KV),
            seq_spec(0), seq_spec(N_NSA_KV), seq_spec(0), seq_spec(N_NSA_KV),
            pl.BlockSpec((tq, LANE), lambda b, g, i: (b * per + i, COL_GATE // LANE + g)),
            pl.BlockSpec((n_h, LANE), lambda b, g, i: (0, 0)),
            pl.BlockSpec((seq, LANE), lambda b, g, i: (0, 0)),
            pl.BlockSpec((1, qw), lambda b, g, i: (0, g)),
        ],
        out_specs=pl.BlockSpec((tq, qw), lambda b, g, i: (b * per + i, g)),
        out_shape=jax.ShapeDtypeStruct((t, D_NSA), BF16),
        scratch_shapes=[
            pltpu.VMEM((seq, 2 * hd), BF16),
            pltpu.VMEM((NSA_REP * tq, 1), F32),
            pltpu.VMEM((NSA_REP * tq, 1), F32),
            pltpu.VMEM((NSA_REP * tq, hd), F32),
        ],
        compiler_params=_cparams(("arbitrary", "arbitrary", "arbitrary")),
        name="nsa_attention",
    )(q_rot, kvc_c, kvc_c, kvs, kvs, kvw, kvw, proj, ov, eaug, nsa_g)


def _retention_kernel(q_ref, k_ref, v_ref, gate_ref, cos_ref, sin_ref, lg_ref, g_ref, o_ref, state):
    c = q_ref.shape[0]

    @pl.when(pl.program_id(2) == 0)
    def _():
        state[...] = jnp.zeros(state.shape, F32)

    lg = lg_ref[0][:, 0:1]
    cos, sin = cos_ref[...], sin_ref[...]
    half = HEAD_DIM // 2
    q = _rotate(q_ref[...], cos, sin, half)
    k = _rotate(k_ref[...], cos, sin, half) * HEAD_DIM ** -0.5
    v = v_ref[...].astype(BF16)
    n_row = lax.broadcasted_iota(jnp.int32, (c, 1), 0).astype(F32)
    n_col = lax.broadcasted_iota(jnp.int32, (1, c), 1).astype(F32)
    diff = n_row - n_col
    decay = jnp.where(diff >= 0.0, jnp.exp(jnp.maximum(diff, 0.0) * lg), 0.0)
    scores = _dot_nt(q.astype(BF16), k.astype(BF16)) * decay
    o = _dot(scores.astype(BF16), v)
    xi = jnp.exp((n_row + 1.0) * lg)
    o = o + _dot((q * xi).astype(BF16), state[...].astype(BF16))
    zeta = jnp.exp((c - 1.0 - n_row) * lg)
    kz_t = (k * zeta).T.astype(BF16)
    state[...] = state[...] * jnp.exp(c * lg) + _dot(kz_t, v)
    gate = gate_ref[...]
    o_ref[...] = (gate * jax.nn.sigmoid(gate) * (_rms(o) * g_ref[...])).astype(o_ref.dtype)


def retention(proj, cos, sin, ret_g, batch, seq):
    t = proj.shape[0]
    c = 256
    per = seq // c
    hd = HEAD_DIM
    base = COL_RET // hd
    blk = lambda off: pl.BlockSpec((c, hd), lambda b, h, s, off=off: (b * per + s, base + off * N_RET_HEADS + h))
    tab = pl.BlockSpec((c, hd), lambda b, h, s: (b * per + s, 0))
    lg = jnp.log1p(-(2.0 ** (-5.0 - jnp.arange(N_RET_HEADS, dtype=F32))))
    lg = jnp.broadcast_to(lg[:, None, None], (N_RET_HEADS, 1, LANE))
    return pl.pallas_call(
        _retention_kernel,
        grid=(batch, N_RET_HEADS, per),
        in_specs=[blk(0), blk(1), blk(2), blk(3), tab, tab,
                  pl.BlockSpec((1, 1, LANE), lambda b, h, s: (h, 0, 0)),
                  pl.BlockSpec((1, hd), lambda b, h, s: (0, h))],
        out_specs=pl.BlockSpec((c, hd), lambda b, h, s: (b * per + s, h)),
        out_shape=jax.ShapeDtypeStruct((t, D_RET), BF16),
        scratch_shapes=[pltpu.VMEM((hd, hd), F32)],
        compiler_params=_cparams(("arbitrary", "arbitrary", "arbitrary")),
        name="retention",
    )(proj, proj, proj, proj, cos, sin, lg, ret_g)


def _out_proj_kernel(yc_ref, yn_ref, yr_ref, w_ref, x_ref, g_ref, o_ref):
    acc = _dot(yc_ref[...], w_ref[0:D_CONV, :])
    acc = acc + _dot(yn_ref[...], w_ref[D_CONV:D_CONV + D_NSA, :])
    acc = acc + _dot(yr_ref[...], w_ref[D_CONV + D_NSA:, :])
    o_ref[...] = x_ref[...] + g_ref[0] * acc


def out_projection(y_conv, y_nsa, y_ret, w_out, x, mod, seq):
    t, d = x.shape
    tm = 512
    per = seq // tm
    row = lambda w: pl.BlockSpec((tm, w), lambda i: (i, 0))
    return pl.pallas_call(
        _out_proj_kernel,
        grid=(t // tm,),
        in_specs=[row(D_CONV), row(D_NSA), row(D_RET),
                  pl.BlockSpec((D_MIX, d), lambda i: (0, 0)),
                  row(d),
                  pl.BlockSpec((1, 1, d), lambda i: (i // per, 0, 2))],
        out_specs=row(d),
        out_shape=jax.ShapeDtypeStruct((t, d), F32),
        compiler_params=_cparams(("arbitrary",)),
        name="out_projection",
    )(y_conv, y_nsa, y_ret, w_out, x, mod)


def _rank_rows(val, n):
    idx = lax.broadcasted_iota(jnp.int32, (n, 1), 0)
    rank = jnp.zeros(val.shape, jnp.int32)
    for rp in range(n):
        vp = val[rp:rp + 1, :]
        ahead = (vp > val) | ((vp == val) & (idx > rp))
        rank = rank + ahead.astype(jnp.int32)
    return rank


def _router_kernel(x_ref, sh_ref, sc_ref, g_ref, rw_ref, rb_ref, tri_ref, h_ref, pos_ref, w_ref, cnt_ref, carry):
    @pl.when(pl.program_id(0) == 0)
    def _():
        carry[...] = jnp.zeros(carry.shape, F32)

    h = _norm_mod(x_ref[...], g_ref[...], sc_ref[0], sh_ref[0])
    h_ref[...] = h
    tm = h.shape[0]
    scores = jax.nn.sigmoid(_dot_nt(rw_ref[...], h, precision=lax.Precision.HIGHEST))
    biased = scores + rb_ref[...]
    sub = lax.broadcasted_iota(jnp.int32, (GROUP_SIZE, 1), 0)
    gs = []
    for g in range(N_EXPERT_GROUPS):
        bg = biased[g * GROUP_SIZE:(g + 1) * GROUP_SIZE, :]
        m1 = jnp.max(bg, axis=0, keepdims=True)
        first = jnp.min(jnp.where(bg == m1, sub, GROUP_SIZE), axis=0, keepdims=True)
        m2 = jnp.max(jnp.where(sub == first, -jnp.inf, bg), axis=0, keepdims=True)
        gs.append(m1 + m2)
    gs = jnp.concatenate(gs, axis=0)
    gkeep = _rank_rows(gs, N_EXPERT_GROUPS) < TOPK_GROUPS
    keep = jnp.concatenate(
        [jnp.broadcast_to(gkeep[g:g + 1, :], (GROUP_SIZE, tm)) for g in range(N_EXPERT_GROUPS)], axis=0)
    masked = jnp.where(keep, biased, -jnp.inf)
    sel = _rank_rows(masked, N_EXPERTS) < TOP_K
    w = jnp.where(sel, scores, 0.0)
    w_ref[...] = ROUTED_SCALE * w / jnp.sum(w, axis=0, keepdims=True)
    self32 = sel.astype(F32)
    pos = carry[:, 0:1] + _dot(self32.astype(BF16), tri_ref[...])
    pos_ref[...] = jnp.where(sel, pos.astype(jnp.int32), -1)
    carry[...] = carry[...] + jnp.sum(self32, axis=1, keepdims=True)
    cnt_ref[...] = carry[...].astype(jnp.int32)


def moe_route(x, mod, norm_g, router_w_t, router_bias, seq):
    t, d = x.shape
    tm = 512
    per = seq // tm
    tri = jnp.asarray(np.triu(np.ones((tm, tm), np.float32), 1), BF16)
    ecol = pl.BlockSpec((N_EXPERTS, tm), lambda i: (0, i))
    return pl.pallas_call(
        _router_kernel,
        grid=(t // tm,),
        in_specs=[
            pl.BlockSpec((tm, d), lambda i: (i, 0)),
            pl.BlockSpec((1, 1, d), lambda i: (i // per, 0, 3)),
            pl.BlockSpec((1, 1, d), lambda i: (i // per, 0, 4)),
            pl.BlockSpec((1, d), lambda i: (0, 0)),
            pl.BlockSpec((N_EXPERTS, d), lambda i: (0, 0)),
            pl.BlockSpec((N_EXPERTS, 1), lambda i: (0, 0)),
            pl.BlockSpec((tm, tm), lambda i: (0, 0)),
        ],
        out_specs=[pl.BlockSpec((tm, d), lambda i: (i, 0)), ecol, ecol,
                   pl.BlockSpec((N_EXPERTS, LANE), lambda i: (0, 0))],
        out_shape=[
            jax.ShapeDtypeStruct((t, d), F32),
            jax.ShapeDtypeStruct((N_EXPERTS, t), jnp.int32),
            jax.ShapeDtypeStruct((N_EXPERTS, t), F32),
            jax.ShapeDtypeStruct((N_EXPERTS, LANE), jnp.int32),
        ],
        scratch_shapes=[pltpu.VMEM((N_EXPERTS, LANE), F32)],
        compiler_params=_cparams(("arbitrary",)),
        name="moe_route",
    )(x, mod, mod, norm_g, router_w_t, router_bias, tri)


def _slots_kernel(pos_ref, w_ref, start_ref, tri_ref, slot_ref, w8_ref):
    pos = pos_ref[...]
    sel = pos >= 0
    slot = pos + start_ref[...]
    order = _dot(tri_ref[...], sel.astype(F32).astype(BF16))
    w = w_ref[...]
    for k in range(TOP_K):
        mk = sel & (order == float(k))
        slot_ref[k:k + 1, :] = jnp.sum(jnp.where(mk, slot, 0), axis=0, keepdims=True)
        w8_ref[k:k + 1, :] = jnp.sum(jnp.where(mk, w, 0.0), axis=0, keepdims=True)


def moe_slots(pos_t, w_t, pad_start):
    e, t = pos_t.shape
    tm = 2048
    tri = jnp.asarray(np.tril(np.ones((e, e), np.float32), -1), BF16)
    ecol = pl.BlockSpec((e, tm), lambda i: (0, i))
    kcol = pl.BlockSpec((TOP_K, tm), lambda i: (0, i))
    return pl.pallas_call(
        _slots_kernel,
        grid=(t // tm,),
        in_specs=[ecol, ecol, pl.BlockSpec((e, 1), lambda i: (0, 0)), pl.BlockSpec((e, e), lambda i: (0, 0))],
        out_specs=[kcol, kcol],
        out_shape=[jax.ShapeDtypeStruct((TOP_K, t), jnp.int32), jax.ShapeDtypeStruct((TOP_K, t), F32)],
        compiler_params=_cparams(("arbitrary",)),
        name="moe_slots",
    )(pos_t, w_t, pad_start, tri)


def _dispatch_kernel(slot_ref, h_ref, xs_in, xs_ref, sem):
    del xs_in
    tm = h_ref.shape[0]

    def row_copy(t, slot):
        return pltpu.make_async_copy(h_ref.at[pl.ds(t, 1), :], xs_ref.at[pl.ds(slot, 1), :], sem)

    def start(t, carry):
        for k in range(TOP_K):
            row_copy(t, slot_ref[k, t]).start()
        return carry

    lax.fori_loop(0, tm, start, 0)

    def wait(t, carry):
        for k in range(TOP_K):
            row_copy(t, slot_ref[k, t]).wait()
        return carry

    lax.fori_loop(0, tm, wait, 0)


def moe_dispatch(slot8, h, n_slots):
    t, d = h.shape
    tm = 256
    xs0 = jnp.zeros((n_slots, d), F32)
    return pl.pallas_call(
        _dispatch_kernel,
        grid=(t // tm,),
        in_specs=[
            pl.BlockSpec((TOP_K, tm), lambda i: (0, i), memory_space=pltpu.SMEM),
            pl.BlockSpec((tm, d), lambda i: (i, 0)),
            pl.BlockSpec(memory_space=pl.ANY),
        ],
        out_specs=pl.BlockSpec(memory_space=pl.ANY),
        out_shape=jax.ShapeDtypeStruct((n_slots, d), F32),
        scratch_shapes=[pltpu.SemaphoreType.DMA(())],
        input_output_aliases={2: 0},
        compiler_params=_cparams(("arbitrary",)),
        name="moe_dispatch",
    )(slot8, h, xs0)


def _swiglu(x_bf16, w_gu, w_down, d_hidden):
    gu = _dot(x_bf16, w_gu)
    gate, up = gu[:, :d_hidden], gu[:, d_hidden:]
    return _dot((gate * jax.nn.sigmoid(gate) * up).astype(BF16), w_down)


def _experts_kernel(be_ref, nb_ref, xs_ref, wgu_ref, wd_ref, ys_ref):
    del be_ref

    @pl.when(pl.program_id(0) < nb_ref[0])
    def _():
        ys_ref[...] = _swiglu(xs_ref[...].astype(BF16), wgu_ref[0], wd_ref[0], D_EXPERT)

    @pl.when(pl.program_id(0) >= nb_ref[0])
    def _():
        ys_ref[...] = jnp.zeros(ys_ref.shape, F32)


def moe_experts(block_e, n_used, xs, w_gu, w_down):
    n_slots, d = xs.shape
    n_blocks = n_slots // MOE_BLOCK
    grid_spec = pltpu.PrefetchScalarGridSpec(
        num_scalar_prefetch=2,
        grid=(n_blocks,),
        in_specs=[
            pl.BlockSpec((MOE_BLOCK, d), lambda b, be, nb: (jnp.minimum(b, nb[0] - 1), 0)),
            pl.BlockSpec((1, d, 2 * D_EXPERT), lambda b, be, nb: (be[b], 0, 0)),
            pl.BlockSpec((1, D_EXPERT, d), lambda b, be, nb: (be[b], 0, 0)),
        ],
        out_specs=pl.BlockSpec((MOE_BLOCK, d), lambda b, be, nb: (b, 0)),
    )
    return pl.pallas_call(
        _experts_kernel,
        grid_spec=grid_spec,
        out_shape=jax.ShapeDtypeStruct((n_slots, d), F32),
        compiler_params=_cparams(("arbitrary",)),
        name="moe_experts",
    )(block_e, n_used, xs, w_gu, w_down)


def _shared_kernel(h_ref, wgu_ref, wd_ref, o_ref):
    o_ref[...] = _swiglu(h_ref[...].astype(BF16), wgu_ref[...], wd_ref[...], D_SHARED)


def shared_expert(h, w_gu, w_down):
    t, d = h.shape
    tm = 512
    return pl.pallas_call(
        _shared_kernel,
        grid=(t // tm,),
        in_specs=[pl.BlockSpec((tm, d), lambda i: (i, 0)),
                  pl.BlockSpec((d, 2 * D_SHARED), lambda i: (0, 0)),
                  pl.BlockSpec((D_SHARED, d), lambda i: (0, 0))],
        out_specs=pl.BlockSpec((tm, d), lambda i: (i, 0)),
        out_shape=jax.ShapeDtypeStruct((t, d), F32),
        compiler_params=_cparams(("arbitrary",)),
        name="shared_expert",
    )(h, w_gu, w_down)


def _combine_kernel(slot_ref, w_ref, sh_ref, x_ref, g_ref, fg_ref, ys_ref, o_ref, buf, sem, *, final_norm):
    tm = x_ref.shape[0]

    def row_copy(t, k, slot):
        return pltpu.make_async_copy(ys_ref.at[pl.ds(slot, 1), :], buf.at[k, pl.ds(t, 1), :], sem)

    def start(t, carry):
        for k in range(TOP_K):
            row_copy(t, k, slot_ref[k, t]).start()
        return carry

    lax.fori_loop(0, tm, start, 0)

    def wait(t, carry):
        for k in range(TOP_K):
            row_copy(t, k, slot_ref[k, t]).wait()
        return carry

    lax.fori_loop(0, tm, wait, 0)
    w = w_ref[...]
    routed = w[:, 0:1] * buf[0]
    for k in range(1, TOP_K):
        routed = routed + w[:, k:k + 1] * buf[k]
    out = x_ref[...] + g_ref[0] * (routed + sh_ref[...])
    if final_norm:
        out = _rms(out) * fg_ref[...]
    o_ref[...] = out


def moe_combine(slot8, w8_rows, shared, x, mod, final_g, ys, seq, final_norm):
    t, d = x.shape
    tm = 128
    per = seq // tm
    row = pl.BlockSpec((tm, d), lambda i: (i, 0))
    return pl.pallas_call(
        functools.partial(_combine_kernel, final_norm=final_norm),
        grid=(t // tm,),
        in_specs=[
            pl.BlockSpec((TOP_K, tm), lambda i: (0, i), memory_space=pltpu.SMEM),
            pl.BlockSpec((tm, TOP_K), lambda i: (i, 0)),
            row, row,
            pl.BlockSpec((1, 1, d), lambda i: (i // per, 0, 5)),
            pl.BlockSpec((1, d), lambda i: (0, 0)),
            pl.BlockSpec(memory_space=pl.ANY),
        ],
        out_specs=row,
        out_shape=jax.ShapeDtypeStruct((t, d), F32),
        scratch_shapes=[pltpu.VMEM((TOP_K, tm, d), F32), pltpu.SemaphoreType.DMA(())],
        compiler_params=_cparams(("arbitrary",)),
        name="moe_combine",
    )(slot8, w8_rows, shared, x, mod, final_g, ys)


def token_mixer_layer(x, mod, tabs, p, batch, seq):
    cos_n, sin_n, cos_r, sin_r = tabs
    proj = in_projection(x, mod, p["norm1_g"], p["w_in"], seq)
    y_conv = short_conv(proj, p["conv_w_t"], p["conv_g"], batch, seq)
    q_rot, kvc, kvs, kvw = nsa_rope(proj, cos_n, sin_n)
    n_h = seq // CMP_STRIDE
    kvc_h = kvc.reshape(batch, seq, 2 * N_NSA_KV, HEAD_DIM).transpose(0, 2, 1, 3)
    kvc_h = kvc_h.reshape(batch, 2 * N_NSA_KV, n_h, CMP_STRIDE * HEAD_DIM)
    kvc_c = compress(kvc_h, p["cmp_pe"], p["cmp_w1"], p["cmp_w2"])
    y_nsa = nsa_attention(q_rot, kvc_c, kvs, kvw, proj, p["nsa_g"], batch, seq)
    y_ret = retention(proj, cos_r, sin_r, p["ret_g"], batch, seq)
    return out_projection(y_conv, y_nsa, y_ret, p["w_out"], x, mod, seq)


def moe_layer(x, mod, p, seq, final_g, final_norm):
    t, d = x.shape
    n_assign = t * TOP_K
    n_blocks = (n_assign + N_EXPERTS * (MOE_BLOCK - 1) + MOE_BLOCK - 1) // MOE_BLOCK
    h, pos_t, w_t, counts = moe_route(x, mod, p["norm2_g"], p["router_w_t"], p["router_bias"], seq)
    counts = counts[:, 0]
    padded = (counts + MOE_BLOCK - 1) // MOE_BLOCK * MOE_BLOCK
    pad_end = jnp.cumsum(padded)
    pad_start = (pad_end - padded).astype(jnp.int32)
    blk0 = jnp.arange(n_blocks, dtype=jnp.int32) * MOE_BLOCK
    block_e = jnp.minimum(jnp.sum(pad_end[None, :] <= blk0[:, None], axis=1), N_EXPERTS - 1).astype(jnp.int32)
    n_used = (pad_end[-1:] // MOE_BLOCK).astype(jnp.int32)
    slot8, w8 = moe_slots(pos_t, w_t, pad_start[:, None])
    xs = moe_dispatch(slot8, h, n_blocks * MOE_BLOCK)
    ys = moe_experts(block_e, n_used, xs, p["exp_w_gu"], p["exp_w_down"])
    shared = shared_expert(h, p["shared_w_gu"], p["shared_w_down"])
    return moe_combine(slot8, w8.T, shared, x, mod, final_g, ys, seq, final_norm)


def _reorder_w_in(w_in):
    c_gate = COL_KV + 6 * D_NSA_KV
    gates = w_in[:, c_gate:c_gate + 3 * N_NSA_HEADS]
    per_group = 3 * NSA_REP
    padded = [jnp.pad(gates[:, g * per_group:(g + 1) * per_group], ((0, 0), (0, LANE - per_group)))
              for g in range(N_NSA_KV)]
    return jnp.concatenate([w_in[:, :c_gate], w_in[:, c_gate + 3 * N_NSA_HEADS:]] + padded, axis=1)


def kernel(x, c, positions, ada_w, ada_b, norm1_g, norm2_g, w_in, conv_w, conv_g, cmp_pe, cmp_w1, cmp_w2,
           nsa_g, ret_g, w_out, router_w, router_bias, exp_w_gu, exp_w_down, shared_w_gu, shared_w_down, final_g):
    batch, seq, d = x.shape
    depth = ada_w.shape[0]
    t = batch * seq
    mod_all = ada_modulation(c, ada_w, ada_b)
    pos_col = positions.reshape(t, 1)
    tabs = rope_tables(pos_col, ROPE_DIM, ROPE_THETA) + rope_tables(pos_col, HEAD_DIM, RET_THETA)
    xt = x.reshape(t, d)
    final_g2 = final_g.reshape(1, d)
    for l in range(depth):
        mod = mod_all[l].reshape(batch, 1, ADA_CHUNKS * d)
        p = dict(
            norm1_g=norm1_g[l].reshape(1, d),
            norm2_g=norm2_g[l].reshape(1, d),
            w_in=_reorder_w_in(w_in[l]).astype(BF16),
            conv_w_t=conv_w[l].T,
            conv_g=conv_g[l].reshape(1, D_CONV),
            cmp_pe=cmp_pe[l].reshape(2, 1, CMP_LEN * HEAD_DIM),
            cmp_w1=cmp_w1[l].astype(BF16),
            cmp_w2=cmp_w2[l].astype(BF16),
            nsa_g=nsa_g[l].reshape(1, D_NSA),
            ret_g=ret_g[l].reshape(1, D_RET),
            w_out=w_out[l].astype(BF16),
            router_w_t=router_w[l].T,
            router_bias=router_bias[l].reshape(N_EXPERTS, 1),
            exp_w_gu=exp_w_gu[l].astype(BF16),
            exp_w_down=exp_w_down[l].astype(BF16),
            shared_w_gu=shared_w_gu[l].astype(BF16),
            shared_w_down=shared_w_down[l].astype(BF16),
        )
        xt = token_mixer_layer(xt, mod, tabs, p, batch, seq)
        xt = moe_layer(xt, mod, p, seq, final_g2, final_norm=(l == depth - 1))
    return xt.reshape(batch, seq, d)
```

```python
import functools

import numpy as np
import jax
import jax.numpy as jnp
from jax import lax
from jax.experimental import pallas as pl
from jax.experimental.pallas import tpu as pltpu

F32 = jnp.float32
BF16 = jnp.bfloat16

D_MODEL = 2048
HEAD_DIM = 128
D_CONV = D_MODEL // 4
CONV_GROUPS = D_CONV // HEAD_DIM
CONV_WIDTH = 3
N_NSA_HEADS = D_MODEL // 2 // HEAD_DIM
N_NSA_KV = 2
NSA_REP = N_NSA_HEADS // N_NSA_KV
D_NSA = N_NSA_HEADS * HEAD_DIM
D_NSA_KV = N_NSA_KV * HEAD_DIM
CMP_LEN = 32
CMP_STRIDE = 16
CMP_HIDDEN = 256
SLC_LEN = 64
N_SLC = 16
WINDOW = 512
ROPE_THETA = 500000.0
ROPE_DIM = HEAD_DIM // 4
N_RET_HEADS = D_MODEL // 4 // HEAD_DIM
D_RET = N_RET_HEADS * HEAD_DIM
RET_THETA = 10000.0
D_MIX = D_CONV + D_NSA + D_RET
N_EXPERTS = 64
N_EXPERT_GROUPS = 8
GROUP_SIZE = N_EXPERTS // N_EXPERT_GROUPS
TOPK_GROUPS = 4
TOP_K = 8
D_EXPERT = 512
D_SHARED = 512
ROUTED_SCALE = 2.5
MOE_BLOCK = 256
ADA_CHUNKS = 6
EPS = 1e-6
NEG_INF = -1e30
FORCE_SCORE = 1e4

LANE = 128
NSA_TILE = 256
GATE_COLS = N_NSA_KV * LANE
D_PROJ = 3 * D_CONV + D_NSA + 6 * D_NSA_KV + 4 * D_RET + GATE_COLS
COL_Q = 3 * D_CONV
COL_KV = COL_Q + D_NSA
COL_RET = COL_KV + 6 * D_NSA_KV
COL_GATE = COL_RET + 4 * D_RET

VMEM_LIMIT = 56 * 1024 * 1024


def _cparams(sem, row_dma=False):
    return pltpu.CompilerParams(dimension_semantics=sem, vmem_limit_bytes=VMEM_LIMIT, disable_bounds_checks=row_dma)


def _dot(a, b, **kw):
    return jnp.dot(a, b, preferred_element_type=F32, **kw)


def _dot_nt(a, b, **kw):
    return lax.dot_general(a, b, (((1,), (1,)), ((), ())), preferred_element_type=F32, **kw)


def _rms(x):
    return x * lax.rsqrt(jnp.mean(x * x, axis=-1, keepdims=True) + EPS)


def _ada_kernel(c_ref, w_ref, b_ref, o_ref):
    c = c_ref[...]
    ca = (c * jax.nn.sigmoid(c)).astype(BF16)
    o_ref[0] = _dot(ca, w_ref[0].astype(BF16)) + b_ref[0]


def ada_modulation(c, ada_w, ada_b):
    L, d, n = ada_w.shape
    b = c.shape[0]
    tn = 1024
    return pl.pallas_call(
        _ada_kernel,
        grid=(L, n // tn),
        in_specs=[
            pl.BlockSpec((b, d), lambda l, j: (0, 0)),
            pl.BlockSpec((1, d, tn), lambda l, j: (l, 0, j)),
            pl.BlockSpec((1, 1, tn), lambda l, j: (l, 0, j)),
        ],
        out_specs=pl.BlockSpec((1, b, tn), lambda l, j: (l, 0, j)),
        out_shape=jax.ShapeDtypeStruct((L, b, n), F32),
        compiler_params=_cparams(("arbitrary", "arbitrary")),
        name="ada_modulation",
    )(c, ada_w, ada_b.reshape(L, 1, n))


def _norm_mod(x, g, sc, sh):
    return (_rms(x) * g) * (1.0 + sc) + sh


def _in_proj_kernel(x_ref, sh_ref, sc_ref, g_ref, w_ref, o_ref, h_scr):
    @pl.when(pl.program_id(1) == 0)
    def _():
        h_scr[...] = _norm_mod(x_ref[...], g_ref[...], sc_ref[0], sh_ref[0]).astype(BF16)

    o_ref[...] = _dot(h_scr[...], w_ref[...])


def in_projection(x, mod, norm_g, w, seq):
    t, d = x.shape
    n = w.shape[1]
    tm, tn = 512, 1280
    per = seq // tm
    return pl.pallas_call(
        _in_proj_kernel,
        grid=(t // tm, n // tn),
        in_specs=[
            pl.BlockSpec((tm, d), lambda i, j: (i, 0)),
            pl.BlockSpec((1, 1, d), lambda i, j: (i // per, 0, 0)),
            pl.BlockSpec((1, 1, d), lambda i, j: (i // per, 0, 1)),
            pl.BlockSpec((1, d), lambda i, j: (0, 0)),
            pl.BlockSpec((d, tn), lambda i, j: (0, j)),
        ],
        out_specs=pl.BlockSpec((tm, tn), lambda i, j: (i, j)),
        out_shape=jax.ShapeDtypeStruct((t, n), F32),
        scratch_shapes=[pltpu.VMEM((tm, d), BF16)],
        compiler_params=_cparams(("arbitrary", "arbitrary")),
        name="in_projection",
    )(x, mod, mod, norm_g, w)


def _rope_table_kernel(pos_ref, inv_ref, sgn_ref, cos_ref, sin_ref):
    ang = pos_ref[...].astype(F32) * inv_ref[...]
    cos_ref[...] = jnp.cos(ang)
    sin_ref[...] = jnp.sin(ang) * sgn_ref[...]


def rope_tables(pos_col, rot_dim, theta):
    t = pos_col.shape[0]
    half = rot_dim // 2
    inv_half = theta ** (-jnp.arange(half, dtype=F32) / half)
    inv = jnp.concatenate([inv_half, inv_half, jnp.zeros((LANE - rot_dim,), F32)]).reshape(1, LANE)
    sgn = np.zeros((1, LANE), np.float32)
    sgn[0, :half] = -1.0
    sgn[0, half:rot_dim] = 1.0
    ts = 1024
    return pl.pallas_call(
        _rope_table_kernel,
        grid=(t // ts,),
        in_specs=[
            pl.BlockSpec((ts, 1), lambda i: (i, 0)),
            pl.BlockSpec((1, LANE), lambda i: (0, 0)),
            pl.BlockSpec((1, LANE), lambda i: (0, 0)),
        ],
        out_specs=[pl.BlockSpec((ts, LANE), lambda i: (i, 0))] * 2,
        out_shape=[jax.ShapeDtypeStruct((t, LANE), F32)] * 2,
        compiler_params=_cparams(("arbitrary",)),
        name="rope_tables",
    )(pos_col, inv, jnp.asarray(sgn))


def _rotate(x, cos, sin_signed, half):
    if 2 * half == LANE:
        swapped = pltpu.roll(x, half, 1)
    else:
        lane = lax.broadcasted_iota(jnp.int32, x.shape, 1)
        swapped = jnp.where(lane < half, pltpu.roll(x, LANE - half, 1), pltpu.roll(x, half, 1))
    return x * cos + swapped * sin_signed


def _nsa_rope_kernel(q0_ref, q1_ref, kvc_ref, kvs_ref, kvw_ref, cos_ref, sin_ref,
                     q_out, kvc_out, k_out, vt_out):
    cos, sin = cos_ref[...], sin_ref[...]
    half = ROPE_DIM // 2
    hpb = D_CONV // HEAD_DIM
    scale = HEAD_DIM ** -0.5
    for blk, src in enumerate((q0_ref, q1_ref)):
        for h in range(hpb):
            sl = slice(h * HEAD_DIM, (h + 1) * HEAD_DIM)
            q_out[:, blk * D_CONV + h * HEAD_DIM: blk * D_CONV + (h + 1) * HEAD_DIM] = (
                (_rotate(src[:, sl], cos, sin, half) * scale).astype(q_out.dtype))
    for g in range(N_NSA_KV):
        sl = slice(g * HEAD_DIM, (g + 1) * HEAD_DIM)
        kvc_out[:, sl] = _rotate(kvc_ref[:, sl], cos, sin, half)
    kvc_out[:, D_NSA_KV:] = kvc_ref[:, D_NSA_KV:]
    for br, src in enumerate((kvs_ref, kvw_ref)):
        for g in range(N_NSA_KV):
            sl = slice(g * HEAD_DIM, (g + 1) * HEAD_DIM)
            dst = slice((br * N_NSA_KV + g) * HEAD_DIM, (br * N_NSA_KV + g + 1) * HEAD_DIM)
            k_out[:, dst] = _rotate(src[:, sl], cos, sin, half).astype(k_out.dtype)
            vsl = slice(D_NSA_KV + g * HEAD_DIM, D_NSA_KV + (g + 1) * HEAD_DIM)
            for tile in range(vt_out.shape[2]):
                rows = slice(tile * NSA_TILE, (tile + 1) * NSA_TILE)
                vt_out[0, br * N_NSA_KV + g, tile] = src[rows, vsl].T.astype(vt_out.dtype)


def nsa_rope(proj, cos, sin, batch, seq):
    t = proj.shape[0]
    ts = 512
    per = seq // ts
    w = D_CONV
    blk = lambda j: pl.BlockSpec((ts, w), lambda i, j=j: (i, j))
    tab = pl.BlockSpec((ts, LANE), lambda i: (i, 0))
    out = pl.BlockSpec((ts, w), lambda i: (i, 0))
    qb = COL_Q // w
    kb = COL_KV // w
    return pl.pallas_call(
        _nsa_rope_kernel,
        grid=(t // ts,),
        in_specs=[blk(qb), blk(qb + 1), blk(kb), blk(kb + 1), blk(kb + 2), tab, tab],
        out_specs=[pl.BlockSpec((ts, D_NSA), lambda i: (i, 0)), out, out,
                   pl.BlockSpec((1, 2 * N_NSA_KV, ts // NSA_TILE, HEAD_DIM, NSA_TILE),
                                lambda i: (i // per, 0, i % per, 0, 0))],
        out_shape=[
            jax.ShapeDtypeStruct((t, D_NSA), BF16),
            jax.ShapeDtypeStruct((t, w), F32),
            jax.ShapeDtypeStruct((t, w), BF16),
            jax.ShapeDtypeStruct((batch, 2 * N_NSA_KV, seq // NSA_TILE, HEAD_DIM, NSA_TILE), BF16),
        ],
        compiler_params=_cparams(("arbitrary",)),
        name="nsa_rope",
    )(proj, proj, proj, proj, proj, cos, sin)


def _conv_kernel(cb_ref, cc_ref, cu_ref, w_ref, g_ref, o_ref, ext):
    ts = cb_ref.shape[0]

    @pl.when(pl.program_id(1) == 0)
    def _():
        ext[0:8, :] = jnp.zeros((8, ext.shape[1]), F32)

    v = cc_ref[...] * cu_ref[...]
    ext[8:, :] = v
    v1 = ext[pl.ds(7, ts), :]
    v2 = ext[pl.ds(6, ts), :]
    y = cb_ref[...] * (w_ref[0:1, :] * v2 + w_ref[1:2, :] * v1 + w_ref[2:3, :] * v)
    ext[0:8, :] = v[ts - 8:, :]
    for gi in range(CONV_GROUPS):
        sl = slice(gi * HEAD_DIM, (gi + 1) * HEAD_DIM)
        o_ref[:, sl] = (_rms(y[:, sl]) * g_ref[:, sl]).astype(o_ref.dtype)


def short_conv(proj, conv_w_t, conv_g, batch, seq):
    t = proj.shape[0]
    ts = 512
    per = seq // ts
    w = D_CONV
    blk = lambda j: pl.BlockSpec((ts, w), lambda b, s, j=j: (b * per + s, j))
    return pl.pallas_call(
        _conv_kernel,
        grid=(batch, per),
        in_specs=[blk(0), blk(1), blk(2),
                  pl.BlockSpec((CONV_WIDTH, w), lambda b, s: (0, 0)),
                  pl.BlockSpec((1, w), lambda b, s: (0, 0))],
        out_specs=pl.BlockSpec((ts, w), lambda b, s: (b * per + s, 0)),
        out_shape=jax.ShapeDtypeStruct((t, w), BF16),
        scratch_shapes=[pltpu.VMEM((8 + ts, w), F32)],
        compiler_params=_cparams(("arbitrary", "arbitrary")),
        name="short_conv",
    )(proj, proj, proj, conv_w_t, conv_g)


def _compress_kernel(h_ref, pe_ref, w1_ref, w2_ref, o_ref, ot_ref):
    h = h_ref[0, 0]
    n_h, half = h.shape
    pe = pe_ref[0]
    a = _dot((h + pe[:, :half]).astype(BF16), w1_ref[0, :half, :])
    b = _dot((h + pe[:, half:]).astype(BF16), w1_ref[0, half:, :])
    pre = a + pltpu.roll(b, n_h - 1, 0)
    out = _dot(jax.nn.gelu(pre).astype(BF16), w2_ref[0])
    o_ref[0, 0] = out.astype(o_ref.dtype)
    ot_ref[0, 0] = out.T.astype(ot_ref.dtype)


def compress(kvc_h, pe_flat, w1, w2):
    b, four, n_h, dh = kvc_h.shape
    return pl.pallas_call(
        _compress_kernel,
        grid=(b, four),
        in_specs=[
            pl.BlockSpec((1, 1, n_h, dh), lambda i, j: (i, j, 0, 0)),
            pl.BlockSpec((1, 1, 2 * dh), lambda i, j: (j // N_NSA_KV, 0, 0)),
            pl.BlockSpec((1, 2 * dh, CMP_HIDDEN), lambda i, j: (j // N_NSA_KV, 0, 0)),
            pl.BlockSpec((1, CMP_HIDDEN, HEAD_DIM), lambda i, j: (j // N_NSA_KV, 0, 0)),
        ],
        out_specs=[pl.BlockSpec((1, 1, n_h, HEAD_DIM), lambda i, j: (i, j, 0, 0)),
                   pl.BlockSpec((1, 1, HEAD_DIM, n_h), lambda i, j: (i, j, 0, 0))],
        out_shape=[jax.ShapeDtypeStruct((b, four, n_h, HEAD_DIM), BF16),
                   jax.ShapeDtypeStruct((b, four, HEAD_DIM, n_h), BF16)],
        compiler_params=_cparams(("arbitrary", "arbitrary")),
        name="nsa_compress",
    )(kvc_h, pe_flat, w1, w2)


def _nsa_kernel(q_ref, kc_ref, vct_ref, ks_ref, kw_ref, vst_ref, vwt_ref, gate_ref, ovt_ref, eaug_ref, g_ref,
                o_ref, kaug_scr, qaug_scr, m_scr, l_scr, acc_scr, out_scr, *, n_cmp, n_slc, n_sel):
    tq = q_ref.shape[0]
    rows = NSA_REP * tq
    cw = rows
    n_col = rows // cw
    i = pl.program_id(2)
    t0 = i * tq
    hd = HEAD_DIM

    @pl.when(i == 0)
    def _():
        kaug_scr[:, :hd] = ks_ref[...]
        kaug_scr[:, hd:] = eaug_ref[...]

    for r in range(NSA_REP):
        qaug_scr[r * tq:(r + 1) * tq, :hd] = q_ref[:, r * hd:(r + 1) * hd]

    def cols(c):
        return slice(c * cw, (c + 1) * cw)

    t_all = t0 + lax.rem(lax.broadcasted_iota(jnp.int32, (1, rows), 1), tq)
    gates_t = jax.nn.sigmoid(gate_ref[...]).T
    gate_rows = [jnp.concatenate([gates_t[3 * r + br:3 * r + br + 1, :] for r in range(NSA_REP)], axis=1)
                 for br in range(3)]

    kc = kc_ref[0, 0]
    vct = vct_ref[0, 0]
    n_h = kc.shape[0]
    c_idx = lax.broadcasted_iota(jnp.int32, (n_h, 1), 0)
    c_end = jnp.where(c_idx < n_cmp, c_idx * CMP_STRIDE + (CMP_LEN - 1), jnp.iinfo(jnp.int32).max)
    psum_t = None
    for c in range(n_col):
        t_c = t_all[:, cols(c)]
        s = _dot_nt(kc, qaug_scr[cols(c), :hd])
        sm = jnp.where(c_end <= t_c, s, NEG_INF)
        e = jnp.exp(sm - jnp.max(sm, axis=0, keepdims=True))
        inv = jnp.where(t_c >= CMP_LEN - 1, 1.0 / jnp.sum(e, axis=0, keepdims=True), 0.0)
        p = e * inv
        out_scr[:, cols(c)] = gate_rows[0][:, cols(c)] * _dot(vct, p.astype(BF16))
        for piece in range(cw // tq):
            ph = p[:, piece * tq:(piece + 1) * tq]
            psum_t = ph if psum_t is None else psum_t + ph
    imp_t = _dot(ovt_ref[...], psum_t, precision=lax.Precision.HIGHEST)

    j_idx = lax.broadcasted_iota(jnp.int32, (n_slc, 1), 0)
    jq = (t0 + lax.broadcasted_iota(jnp.int32, (1, tq), 1)) // SLC_LEN
    forced = (j_idx == 0) | (j_idx == jq) | (j_idx == jq - 1)
    val = jnp.where(forced, FORCE_SCORE, jnp.where(j_idx <= jq, imp_t[:n_slc], -1.0))
    rank = jnp.zeros((n_slc, tq), jnp.int32)
    for jp in range(n_slc):
        vp = val[jp:jp + 1, :]
        ahead = (vp > val) | ((vp == val) & (j_idx > jp))
        rank = rank + ahead.astype(jnp.int32)
    bias_t = jnp.where(rank < n_sel, 0.0, NEG_INF)
    if n_slc < LANE:
        bias_t = jnp.concatenate([bias_t, jnp.zeros((LANE - n_slc, tq), F32)], axis=0)
    bias = bias_t.T.astype(BF16)
    for r in range(NSA_REP):
        qaug_scr[r * tq:(r + 1) * tq, hd:] = bias

    def reset():
        m_scr[...] = jnp.full(m_scr.shape, NEG_INF, F32)
        l_scr[...] = jnp.zeros(l_scr.shape, F32)
        acc_scr[...] = jnp.zeros(acc_scr.shape, F32)

    def attend(k_tile, vt_tile, kdim, mask_fn):
        for c in range(n_col):
            s = _dot_nt(k_tile, qaug_scr[cols(c), :kdim])
            mask = None if mask_fn is None else mask_fn(t_all[:, cols(c)])
            if mask is not None:
                s = jnp.where(mask, s, NEG_INF)
            m_old = m_scr[:, cols(c)]
            m_new = jnp.maximum(m_old, jnp.max(s, axis=0, keepdims=True))
            alpha = jnp.exp(m_old - m_new)
            p = jnp.exp(s - m_new)
            if mask is not None:
                p = jnp.where(mask, p, 0.0)
            l_scr[:, cols(c)] = alpha * l_scr[:, cols(c)] + jnp.sum(p, axis=0, keepdims=True)
            acc_scr[:, cols(c)] = alpha * acc_scr[:, cols(c)] + _dot(vt_tile, p.astype(BF16))
            m_scr[:, cols(c)] = m_new

    def add_branch(branch):
        out_scr[...] += (gate_rows[branch] / l_scr[...]) * acc_scr[...]

    key_sub = lax.broadcasted_iota(jnp.int32, (tq, 1), 0)

    reset()

    def sel_body(jt, carry):
        k0 = pl.multiple_of(jt * tq, tq)
        attend(kaug_scr[pl.ds(k0, tq), :], vst_ref[0, 0, jt], 2 * hd, None)
        return carry

    lax.fori_loop(0, i, sel_body, 0)
    k_diag = pl.multiple_of(t0, tq)
    causal = lambda t: t0 + key_sub <= t
    attend(kaug_scr[pl.ds(k_diag, tq), :], vst_ref[0, 0, i], 2 * hd, causal)
    add_branch(1)

    reset()
    n_back = WINDOW // tq
    for back in range(n_back, 0, -1):
        @pl.when(i >= back)
        def _(back=back):
            k0 = pl.multiple_of(t0 - back * tq, tq)
            lower = (lambda t: k0 + key_sub > t - WINDOW) if back == n_back else None
            attend(kw_ref[pl.ds(k0, tq), :], vwt_ref[0, 0, i - back], hd, lower)
    attend(kw_ref[pl.ds(k_diag, tq), :], vwt_ref[0, 0, i], hd, causal)
    add_branch(2)

    for r in range(NSA_REP):
        o_t = out_scr[:, r * tq:(r + 1) * tq]
        o_t = o_t * lax.rsqrt(jnp.mean(o_t * o_t, axis=0, keepdims=True) + EPS)
        sl = slice(r * hd, (r + 1) * hd)
        o_ref[:, sl] = (o_t.T * g_ref[:, sl]).astype(o_ref.dtype)


def _overlap_matrix_t(n_h, n_cmp, n_slc):
    cs = np.arange(n_cmp) * CMP_STRIDE
    js = np.arange(n_slc) * SLC_LEN
    ov = np.minimum(cs[:, None] + CMP_LEN, js[None, :] + SLC_LEN) - np.maximum(cs[:, None], js[None, :])
    out = np.zeros((LANE, n_h), np.float32)
    out[:n_slc, :n_cmp] = (np.clip(ov, 0, None) / CMP_LEN).T
    return out


def nsa_attention(q_rot, kc, kct, k_sw, vt_sw, proj, nsa_g, batch, seq):
    t = q_rot.shape[0]
    tq = NSA_TILE
    per = seq // tq
    n_h = seq // CMP_STRIDE
    n_cmp = (seq - CMP_LEN) // CMP_STRIDE + 1
    n_slc = seq // SLC_LEN
    n_sel = min(N_SLC, n_slc)
    assert n_slc <= LANE and WINDOW % tq == 0 and tq % LANE == 0
    ovt = jnp.asarray(_overlap_matrix_t(n_h, n_cmp, n_slc))
    eaug = np.zeros((seq, LANE), np.float32)
    eaug[np.arange(seq), np.arange(seq) // SLC_LEN] = 1.0
    eaug = jnp.asarray(eaug, BF16)
    hd = HEAD_DIM
    qw = NSA_REP * hd
    rows = NSA_REP * tq
    seq_spec = lambda off: pl.BlockSpec((seq, hd), lambda b, g, i, off=off: (b, off + g))
    vt_spec = lambda off: pl.BlockSpec((1, 1, per, hd, tq), lambda b, g, i, off=off: (b, off + g, 0, 0, 0))
    kern = functools.partial(_nsa_kernel, n_cmp=n_cmp, n_slc=n_slc, n_sel=n_sel)
    return pl.pallas_call(
        kern,
        grid=(batch, N_NSA_KV, per),
        in_specs=[
            pl.BlockSpec((tq, qw), lambda b, g, i: (b * per + i, g)),
            pl.BlockSpec((1, 1, n_h, hd), lambda b, g, i: (b, g, 0, 0)),
            pl.BlockSpec((1, 1, hd, n_h), lambda b, g, i: (b, N_NSA_KV + g, 0, 0)),
            seq_spec(0), seq_spec(N_NSA_KV), vt_spec(0), vt_spec(N_NSA_KV),
            pl.BlockSpec((tq, LANE), lambda b, g, i: (b * per + i, COL_GATE // LANE + g)),
            pl.BlockSpec((LANE, n_h), lambda b, g, i: (0, 0)),
            pl.BlockSpec((seq, LANE), lambda b, g, i: (0, 0)),
            pl.BlockSpec((1, qw), lambda b, g, i: (0, g)),
        ],
        out_specs=pl.BlockSpec((tq, qw), lambda b, g, i: (b * per + i, g)),
        out_shape=jax.ShapeDtypeStruct((t, D_NSA), BF16),
        scratch_shapes=[
            pltpu.VMEM((seq, 2 * hd), BF16),
            pltpu.VMEM((rows, 2 * hd), BF16),
            pltpu.VMEM((1, rows), F32),
            pltpu.VMEM((1, rows), F32),
            pltpu.VMEM((hd, rows), F32),
            pltpu.VMEM((hd, rows), F32),
        ],
        compiler_params=_cparams(("arbitrary", "arbitrary", "arbitrary")),
        name="nsa_attention",
    )(q_rot, kc, kct, k_sw, k_sw, vt_sw, vt_sw, proj, ovt, eaug, nsa_g)


def _retention_kernel(q_ref, k_ref, v_ref, gate_ref, cos_ref, sin_ref, lg_ref, g_ref, o_ref, state):
    c = q_ref.shape[0]

    @pl.when(pl.program_id(2) == 0)
    def _():
        state[...] = jnp.zeros(state.shape, F32)

    lg = lg_ref[0][:, 0:1]
    cos, sin = cos_ref[...], sin_ref[...]
    half = HEAD_DIM // 2
    q = _rotate(q_ref[...], cos, sin, half)
    k = _rotate(k_ref[...], cos, sin, half) * HEAD_DIM ** -0.5
    v = v_ref[...].astype(BF16)
    n_row = lax.broadcasted_iota(jnp.int32, (c, 1), 0).astype(F32)
    n_col = lax.broadcasted_iota(jnp.int32, (1, c), 1).astype(F32)
    diff = n_row - n_col
    decay = jnp.where(diff >= 0.0, jnp.exp(jnp.maximum(diff, 0.0) * lg), 0.0)
    scores = _dot_nt(q.astype(BF16), k.astype(BF16)) * decay
    o = _dot(scores.astype(BF16), v)
    xi = jnp.exp((n_row + 1.0) * lg)
    o = o + _dot((q * xi).astype(BF16), state[...].astype(BF16))
    zeta = jnp.exp((c - 1.0 - n_row) * lg)
    kz_t = (k * zeta).T.astype(BF16)
    state[...] = state[...] * jnp.exp(c * lg) + _dot(kz_t, v)
    gate = gate_ref[...]
    o_ref[...] = (gate * jax.nn.sigmoid(gate) * (_rms(o) * g_ref[...])).astype(o_ref.dtype)


def retention(proj, cos, sin, ret_g, batch, seq):
    t = proj.shape[0]
    c = 256
    per = seq // c
    hd = HEAD_DIM
    base = COL_RET // hd
    blk = lambda off: pl.BlockSpec((c, hd), lambda b, h, s, off=off: (b * per + s, base + off * N_RET_HEADS + h))
    tab = pl.BlockSpec((c, hd), lambda b, h, s: (b * per + s, 0))
    lg = jnp.log1p(-(2.0 ** (-5.0 - jnp.arange(N_RET_HEADS, dtype=F32))))
    lg = jnp.broadcast_to(lg[:, None, None], (N_RET_HEADS, 1, LANE))
    return pl.pallas_call(
        _retention_kernel,
        grid=(batch, N_RET_HEADS, per),
        in_specs=[blk(0), blk(1), blk(2), blk(3), tab, tab,
                  pl.BlockSpec((1, 1, LANE), lambda b, h, s: (h, 0, 0)),
                  pl.BlockSpec((1, hd), lambda b, h, s: (0, h))],
        out_specs=pl.BlockSpec((c, hd), lambda b, h, s: (b * per + s, h)),
        out_shape=jax.ShapeDtypeStruct((t, D_RET), BF16),
        scratch_shapes=[pltpu.VMEM((hd, hd), F32)],
        compiler_params=_cparams(("arbitrary", "arbitrary", "arbitrary")),
        name="retention",
    )(proj, proj, proj, proj, cos, sin, lg, ret_g)


def _out_proj_kernel(yc_ref, yn_ref, yr_ref, w_ref, x_ref, g_ref, o_ref):
    acc = _dot(yc_ref[...], w_ref[0:D_CONV, :])
    acc = acc + _dot(yn_ref[...], w_ref[D_CONV:D_CONV + D_NSA, :])
    acc = acc + _dot(yr_ref[...], w_ref[D_CONV + D_NSA:, :])
    o_ref[...] = x_ref[...] + g_ref[0] * acc


def out_projection(y_conv, y_nsa, y_ret, w_out, x, mod, seq):
    t, d = x.shape
    tm = 512
    per = seq // tm
    row = lambda w: pl.BlockSpec((tm, w), lambda i: (i, 0))
    return pl.pallas_call(
        _out_proj_kernel,
        grid=(t // tm,),
        in_specs=[row(D_CONV), row(D_NSA), row(D_RET),
                  pl.BlockSpec((D_MIX, d), lambda i: (0, 0)),
                  row(d),
                  pl.BlockSpec((1, 1, d), lambda i: (i // per, 0, 2))],
        out_specs=row(d),
        out_shape=jax.ShapeDtypeStruct((t, d), F32),
        compiler_params=_cparams(("arbitrary",)),
        name="out_projection",
    )(y_conv, y_nsa, y_ret, w_out, x, mod)


def _rank_rows(val, n):
    idx = lax.broadcasted_iota(jnp.int32, (n, 1), 0)
    rank = jnp.zeros(val.shape, jnp.int32)
    for rp in range(n):
        vp = val[rp:rp + 1, :]
        ahead = (vp > val) | ((vp == val) & (idx > rp))
        rank = rank + ahead.astype(jnp.int32)
    return rank


def _router_kernel(x_ref, sh_ref, sc_ref, g_ref, rw_ref, rb_ref, tri_ref, h_ref, pos_ref, w_ref, cnt_ref, carry):
    @pl.when(pl.program_id(0) == 0)
    def _():
        carry[...] = jnp.zeros(carry.shape, F32)

    h = _norm_mod(x_ref[...], g_ref[...], sc_ref[0], sh_ref[0])
    h_ref[...] = h
    tm = h.shape[0]
    scores = jax.nn.sigmoid(_dot_nt(rw_ref[...], h, precision=lax.Precision.HIGHEST))
    biased = scores + rb_ref[...]
    sub = lax.broadcasted_iota(jnp.int32, (GROUP_SIZE, 1), 0)
    gs = []
    for g in range(N_EXPERT_GROUPS):
        bg = biased[g * GROUP_SIZE:(g + 1) * GROUP_SIZE, :]
        m1 = jnp.max(bg, axis=0, keepdims=True)
        first = jnp.min(jnp.where(bg == m1, sub, GROUP_SIZE), axis=0, keepdims=True)
        m2 = jnp.max(jnp.where(sub == first, -jnp.inf, bg), axis=0, keepdims=True)
        gs.append(m1 + m2)
    gs = jnp.concatenate(gs, axis=0)
    gkeep = _rank_rows(gs, N_EXPERT_GROUPS) < TOPK_GROUPS
    keep = jnp.concatenate(
        [jnp.broadcast_to(gkeep[g:g + 1, :], (GROUP_SIZE, tm)) for g in range(N_EXPERT_GROUPS)], axis=0)
    masked = jnp.where(keep, biased, -jnp.inf)
    sel = _rank_rows(masked, N_EXPERTS) < TOP_K
    w = jnp.where(sel, scores, 0.0)
    w_ref[...] = ROUTED_SCALE * w / jnp.sum(w, axis=0, keepdims=True)
    self32 = sel.astype(F32)
    pos = carry[:, 0:1] + _dot(self32.astype(BF16), tri_ref[...])
    pos_ref[...] = jnp.where(sel, pos.astype(jnp.int32), -1)
    carry[...] = carry[...] + jnp.sum(self32, axis=1, keepdims=True)
    cnt_ref[...] = carry[...].astype(jnp.int32)


def moe_route(x, mod, norm_g, router_w_t, router_bias, seq):
    t, d = x.shape
    tm = 512
    per = seq // tm
    tri = jnp.asarray(np.triu(np.ones((tm, tm), np.float32), 1), BF16)
    ecol = pl.BlockSpec((N_EXPERTS, tm), lambda i: (0, i))
    return pl.pallas_call(
        _router_kernel,
        grid=(t // tm,),
        in_specs=[
            pl.BlockSpec((tm, d), lambda i: (i, 0)),
            pl.BlockSpec((1, 1, d), lambda i: (i // per, 0, 3)),
            pl.BlockSpec((1, 1, d), lambda i: (i // per, 0, 4)),
            pl.BlockSpec((1, d), lambda i: (0, 0)),
            pl.BlockSpec((N_EXPERTS, d), lambda i: (0, 0)),
            pl.BlockSpec((N_EXPERTS, 1), lambda i: (0, 0)),
            pl.BlockSpec((tm, tm), lambda i: (0, 0)),
        ],
        out_specs=[pl.BlockSpec((tm, d), lambda i: (i, 0)), ecol, ecol,
                   pl.BlockSpec((N_EXPERTS, LANE), lambda i: (0, 0))],
        out_shape=[
            jax.ShapeDtypeStruct((t, d), F32),
            jax.ShapeDtypeStruct((N_EXPERTS, t), jnp.int32),
            jax.ShapeDtypeStruct((N_EXPERTS, t), F32),
            jax.ShapeDtypeStruct((N_EXPERTS, LANE), jnp.int32),
        ],
        scratch_shapes=[pltpu.VMEM((N_EXPERTS, LANE), F32)],
        compiler_params=_cparams(("arbitrary",)),
        name="moe_route",
    )(x, mod, mod, norm_g, router_w_t, router_bias, tri)


def _slots_kernel(pos_ref, w_ref, start_ref, tri_ref, slot_ref, w8_ref):
    pos = pos_ref[...]
    sel = pos >= 0
    slot = pos + start_ref[...]
    order = _dot(tri_ref[...], sel.astype(F32).astype(BF16))
    w = w_ref[...]
    for k in range(TOP_K):
        mk = sel & (order == float(k))
        slot_ref[k:k + 1, :] = jnp.sum(jnp.where(mk, slot, 0), axis=0, keepdims=True)
        w8_ref[k:k + 1, :] = jnp.sum(jnp.where(mk, w, 0.0), axis=0, keepdims=True)


def moe_slots(pos_t, w_t, pad_start):
    e, t = pos_t.shape
    tm = 2048
    tri = jnp.asarray(np.tril(np.ones((e, e), np.float32), -1), BF16)
    ecol = pl.BlockSpec((e, tm), lambda i: (0, i))
    kcol = pl.BlockSpec((TOP_K, tm), lambda i: (0, i))
    return pl.pallas_call(
        _slots_kernel,
        grid=(t // tm,),
        in_specs=[ecol, ecol, pl.BlockSpec((e, 1), lambda i: (0, 0)), pl.BlockSpec((e, e), lambda i: (0, 0))],
        out_specs=[kcol, kcol],
        out_shape=[jax.ShapeDtypeStruct((TOP_K, t), jnp.int32), jax.ShapeDtypeStruct((TOP_K, t), F32)],
        compiler_params=_cparams(("arbitrary",)),
        name="moe_slots",
    )(pos_t, w_t, pad_start, tri)


def _dispatch_kernel(pad_lo_ref, n_pad_ref, slot_ref, h_ref, xs_ref, zrow, sem, zsem):
    tm = h_ref.shape[0]

    @pl.when(pl.program_id(0) == 0)
    def _():
        zrow[...] = jnp.zeros(zrow.shape, zrow.dtype)

        def zero_copy(slot):
            return pltpu.make_async_copy(zrow.at[pl.ds(0, 1), :], xs_ref.at[pl.ds(slot, 1), :], zsem)

        def per_expert(e, carry):
            lo, n = pad_lo_ref[e], n_pad_ref[e]

            def start(r, c):
                zero_copy(lo + r).start()
                return c

            def wait(r, c):
                zero_copy(lo + r).wait()
                return c

            lax.fori_loop(0, n, start, 0)
            lax.fori_loop(0, n, wait, 0)
            return carry

        lax.fori_loop(0, N_EXPERTS, per_expert, 0)

    def row_copy(t, slot):
        return pltpu.make_async_copy(h_ref.at[pl.ds(t, 1), :], xs_ref.at[pl.ds(slot, 1), :], sem)

    def start(t, carry):
        for k in range(TOP_K):
            row_copy(t, slot_ref[k, t]).start()
        return carry

    lax.fori_loop(0, tm, start, 0)

    def wait(t, carry):
        for k in range(TOP_K):
            row_copy(t, slot_ref[k, t]).wait()
        return carry

    lax.fori_loop(0, tm, wait, 0)


def moe_dispatch(pad_lo, n_pad, slot8, h, n_slots):
    t, d = h.shape
    tm = 256
    grid_spec = pltpu.PrefetchScalarGridSpec(
        num_scalar_prefetch=2,
        grid=(t // tm,),
        in_specs=[
            pl.BlockSpec((TOP_K, tm), lambda i, lo, n: (0, i), memory_space=pltpu.SMEM),
            pl.BlockSpec((tm, d), lambda i, lo, n: (i, 0)),
        ],
        out_specs=pl.BlockSpec(memory_space=pl.ANY),
        scratch_shapes=[pltpu.VMEM((8, d), F32), pltpu.SemaphoreType.DMA(()), pltpu.SemaphoreType.DMA(())],
    )
    return pl.pallas_call(
        _dispatch_kernel,
        grid_spec=grid_spec,
        out_shape=jax.ShapeDtypeStruct((n_slots, d), F32),
        compiler_params=_cparams(("arbitrary",), row_dma=True),
        name="moe_dispatch",
    )(pad_lo, n_pad, slot8, h)


def _swiglu(x_bf16, w_gu, w_down, d_hidden):
    gu = _dot(x_bf16, w_gu)
    gate, up = gu[:, :d_hidden], gu[:, d_hidden:]
    return _dot((gate * jax.nn.sigmoid(gate) * up).astype(BF16), w_down)


def _experts_kernel(be_ref, first_ref, nb_ref, xs_ref, wgu_ref, wd_ref, ys_ref, wgu_bf, wd_bf):
    del be_ref
    b = pl.program_id(0)
    used = b < nb_ref[0]

    @pl.when(used & (first_ref[b] == 1))
    def _():
        wgu_bf[...] = wgu_ref[0, 0].astype(BF16)
        wd_bf[...] = wd_ref[0, 0].astype(BF16)

    @pl.when(used)
    def _():
        ys_ref[...] = _swiglu(xs_ref[...].astype(BF16), wgu_bf[...], wd_bf[...], D_EXPERT)

    @pl.when(jnp.logical_not(used))
    def _():
        ys_ref[...] = jnp.zeros(ys_ref.shape, F32)


def moe_experts(block_e, first, n_used, xs, w_gu, w_down, layer):
    n_slots, d = xs.shape
    n_blocks = n_slots // MOE_BLOCK
    grid_spec = pltpu.PrefetchScalarGridSpec(
        num_scalar_prefetch=3,
        grid=(n_blocks,),
        in_specs=[
            pl.BlockSpec((MOE_BLOCK, d), lambda b, be, fi, nb: (jnp.minimum(b, nb[0] - 1), 0)),
            pl.BlockSpec((1, 1, d, 2 * D_EXPERT), lambda b, be, fi, nb: (layer, be[b], 0, 0)),
            pl.BlockSpec((1, 1, D_EXPERT, d), lambda b, be, fi, nb: (layer, be[b], 0, 0)),
        ],
        out_specs=pl.BlockSpec((MOE_BLOCK, d), lambda b, be, fi, nb: (b, 0)),
        scratch_shapes=[pltpu.VMEM((d, 2 * D_EXPERT), BF16), pltpu.VMEM((D_EXPERT, d), BF16)],
    )
    return pl.pallas_call(
        _experts_kernel,
        grid_spec=grid_spec,
        out_shape=jax.ShapeDtypeStruct((n_slots, d), F32),
        compiler_params=_cparams(("arbitrary",)),
        name="moe_experts",
    )(block_e, first, n_used, xs, w_gu, w_down)


def _shared_kernel(h_ref, wgu_ref, wd_ref, o_ref):
    o_ref[...] = _swiglu(h_ref[...].astype(BF16), wgu_ref[...], wd_ref[...], D_SHARED)


def shared_expert(h, w_gu, w_down):
    t, d = h.shape
    tm = 512
    return pl.pallas_call(
        _shared_kernel,
        grid=(t // tm,),
        in_specs=[pl.BlockSpec((tm, d), lambda i: (i, 0)),
                  pl.BlockSpec((d, 2 * D_SHARED), lambda i: (0, 0)),
                  pl.BlockSpec((D_SHARED, d), lambda i: (0, 0))],
        out_specs=pl.BlockSpec((tm, d), lambda i: (i, 0)),
        out_shape=jax.ShapeDtypeStruct((t, d), F32),
        compiler_params=_cparams(("arbitrary",)),
        name="shared_expert",
    )(h, w_gu, w_down)


def _combine_kernel(slot_ref, w_ref, sh_ref, x_ref, g_ref, fg_ref, ys_ref, o_ref, buf, sem, *, final_norm):
    tm = x_ref.shape[0]

    def row_copy(t, k, slot):
        return pltpu.make_async_copy(ys_ref.at[pl.ds(slot, 1), :], buf.at[k, pl.ds(t, 1), :], sem)

    def start(t, carry):
        for k in range(TOP_K):
            row_copy(t, k, slot_ref[k, t]).start()
        return carry

    lax.fori_loop(0, tm, start, 0)

    def wait(t, carry):
        for k in range(TOP_K):
            row_copy(t, k, slot_ref[k, t]).wait()
        return carry

    lax.fori_loop(0, tm, wait, 0)
    w = w_ref[...]
    routed = w[:, 0:1] * buf[0]
    for k in range(1, TOP_K):
        routed = routed + w[:, k:k + 1] * buf[k]
    out = x_ref[...] + g_ref[0] * (routed + sh_ref[...])
    if final_norm:
        out = _rms(out) * fg_ref[...]
    o_ref[...] = out


def moe_combine(slot8, w8_rows, shared, x, mod, final_g, ys, seq, final_norm):
    t, d = x.shape
    tm = 128
    per = seq // tm
    row = pl.BlockSpec((tm, d), lambda i: (i, 0))
    return pl.pallas_call(
        functools.partial(_combine_kernel, final_norm=final_norm),
        grid=(t // tm,),
        in_specs=[
            pl.BlockSpec((TOP_K, tm), lambda i: (0, i), memory_space=pltpu.SMEM),
            pl.BlockSpec((tm, TOP_K), lambda i: (i, 0)),
            row, row,
            pl.BlockSpec((1, 1, d), lambda i: (i // per, 0, 5)),
            pl.BlockSpec((1, d), lambda i: (0, 0)),
            pl.BlockSpec(memory_space=pl.ANY),
        ],
        out_specs=row,
        out_shape=jax.ShapeDtypeStruct((t, d), F32),
        scratch_shapes=[pltpu.VMEM((TOP_K, tm, d), F32), pltpu.SemaphoreType.DMA(())],
        compiler_params=_cparams(("arbitrary",), row_dma=True),
        name="moe_combine",
    )(slot8, w8_rows, shared, x, mod, final_g, ys)


def token_mixer_layer(x, mod, tabs, p, batch, seq):
    cos_n, sin_n, cos_r, sin_r = tabs
    proj = in_projection(x, mod, p["norm1_g"], p["w_in"], seq)
    y_conv = short_conv(proj, p["conv_w_t"], p["conv_g"], batch, seq)
    q_rot, kvc, k_sw, vt_sw = nsa_rope(proj, cos_n, sin_n, batch, seq)
    n_h = seq // CMP_STRIDE
    kvc_h = kvc.reshape(batch, seq, 2 * N_NSA_KV, HEAD_DIM).transpose(0, 2, 1, 3)
    kvc_h = kvc_h.reshape(batch, 2 * N_NSA_KV, n_h, CMP_STRIDE * HEAD_DIM)
    kc, kct = compress(kvc_h, p["cmp_pe"], p["cmp_w1"], p["cmp_w2"])
    y_nsa = nsa_attention(q_rot, kc, kct, k_sw, vt_sw, proj, p["nsa_g"], batch, seq)
    y_ret = retention(proj, cos_r, sin_r, p["ret_g"], batch, seq)
    return out_projection(y_conv, y_nsa, y_ret, p["w_out"], x, mod, seq)


def moe_layer(x, mod, p, seq, final_g, final_norm):
    t, d = x.shape
    n_assign = t * TOP_K
    n_blocks = (n_assign + N_EXPERTS * (MOE_BLOCK - 1) + MOE_BLOCK - 1) // MOE_BLOCK
    h, pos_t, w_t, counts = moe_route(x, mod, p["norm2_g"], p["router_w_t"], p["router_bias"], seq)
    counts = counts[:, 0]
    padded = (counts + MOE_BLOCK - 1) // MOE_BLOCK * MOE_BLOCK
    pad_end = jnp.cumsum(padded)
    pad_start = (pad_end - padded).astype(jnp.int32)
    blk0 = jnp.arange(n_blocks, dtype=jnp.int32) * MOE_BLOCK
    block_e = jnp.minimum(jnp.sum(pad_end[None, :] <= blk0[:, None], axis=1), N_EXPERTS - 1).astype(jnp.int32)
    n_used = (pad_end[-1:] // MOE_BLOCK).astype(jnp.int32)
    first = jnp.concatenate([jnp.ones((1,), jnp.int32), (block_e[1:] != block_e[:-1]).astype(jnp.int32)])
    slot8, w8 = moe_slots(pos_t, w_t, pad_start[:, None])
    xs = moe_dispatch(pad_start + counts, (padded - counts).astype(jnp.int32), slot8, h, n_blocks * MOE_BLOCK)
    ys = moe_experts(block_e, first, n_used, xs, p["exp_w_gu"], p["exp_w_down"], p["layer"])
    shared = shared_expert(h, p["shared_w_gu"], p["shared_w_down"])
    return moe_combine(slot8, w8.T, shared, x, mod, final_g, ys, seq, final_norm)


def _reorder_w_in(w_in):
    c_gate = COL_KV + 6 * D_NSA_KV
    gates = w_in[:, c_gate:c_gate + 3 * N_NSA_HEADS]
    per_group = 3 * NSA_REP
    padded = [jnp.pad(gates[:, g * per_group:(g + 1) * per_group], ((0, 0), (0, LANE - per_group)))
              for g in range(N_NSA_KV)]
    return jnp.concatenate([w_in[:, :c_gate], w_in[:, c_gate + 3 * N_NSA_HEADS:]] + padded, axis=1)


def kernel(x, c, positions, ada_w, ada_b, norm1_g, norm2_g, w_in, conv_w, conv_g, cmp_pe, cmp_w1, cmp_w2,
           nsa_g, ret_g, w_out, router_w, router_bias, exp_w_gu, exp_w_down, shared_w_gu, shared_w_down, final_g):
    batch, seq, d = x.shape
    depth = ada_w.shape[0]
    t = batch * seq
    mod_all = ada_modulation(c, ada_w, ada_b)
    pos_col = positions.reshape(t, 1)
    tabs = rope_tables(pos_col, ROPE_DIM, ROPE_THETA) + rope_tables(pos_col, HEAD_DIM, RET_THETA)
    xt = x.reshape(t, d)
    final_g2 = final_g.reshape(1, d)
    for l in range(depth):
        mod = mod_all[l].reshape(batch, 1, ADA_CHUNKS * d)
        p = dict(
            norm1_g=norm1_g[l].reshape(1, d),
            norm2_g=norm2_g[l].reshape(1, d),
            w_in=_reorder_w_in(w_in[l]).astype(BF16),
            conv_w_t=conv_w[l].T,
            conv_g=conv_g[l].reshape(1, D_CONV),
            cmp_pe=cmp_pe[l].reshape(2, 1, CMP_LEN * HEAD_DIM),
            cmp_w1=cmp_w1[l].astype(BF16),
            cmp_w2=cmp_w2[l].astype(BF16),
            nsa_g=nsa_g[l].reshape(1, D_NSA),
            ret_g=ret_g[l].reshape(1, D_RET),
            w_out=w_out[l].astype(BF16),
            router_w_t=router_w[l].T,
            router_bias=router_bias[l].reshape(N_EXPERTS, 1),
            layer=l,
            exp_w_gu=exp_w_gu,
            exp_w_down=exp_w_down,
            shared_w_gu=shared_w_gu[l].astype(BF16),
            shared_w_down=shared_w_down[l].astype(BF16),
        )
        xt = token_mixer_layer(xt, mod, tabs, p, batch, seq)
        xt = moe_layer(xt, mod, p, seq, final_g2, final_norm=(l == depth - 1))
    return xt.reshape(batch, seq, d)
```

```python
import functools

import numpy as np
import jax
import jax.numpy as jnp
from jax import lax
from jax.experimental import pallas as pl
from jax.experimental.pallas import tpu as pltpu

F32 = jnp.float32
BF16 = jnp.bfloat16

D_MODEL = 2048
HEAD_DIM = 128
D_CONV = D_MODEL // 4
CONV_GROUPS = D_CONV // HEAD_DIM
CONV_WIDTH = 3
N_NSA_HEADS = D_MODEL // 2 // HEAD_DIM
N_NSA_KV = 2
NSA_REP = N_NSA_HEADS // N_NSA_KV
D_NSA = N_NSA_HEADS * HEAD_DIM
D_NSA_KV = N_NSA_KV * HEAD_DIM
CMP_LEN = 32
CMP_STRIDE = 16
CMP_HIDDEN = 256
SLC_LEN = 64
N_SLC = 16
WINDOW = 512
ROPE_THETA = 500000.0
ROPE_DIM = HEAD_DIM // 4
N_RET_HEADS = D_MODEL // 4 // HEAD_DIM
D_RET = N_RET_HEADS * HEAD_DIM
RET_THETA = 10000.0
D_MIX = D_CONV + D_NSA + D_RET
N_EXPERTS = 64
N_EXPERT_GROUPS = 8
GROUP_SIZE = N_EXPERTS // N_EXPERT_GROUPS
TOPK_GROUPS = 4
TOP_K = 8
D_EXPERT = 512
D_SHARED = 512
ROUTED_SCALE = 2.5
MOE_BLOCK = 256
ADA_CHUNKS = 6
EPS = 1e-6
NEG_INF = -1e30
FORCE_SCORE = 1e4

LANE = 128
LOG2_E = 1.4426950408889634
NSA_TILE = 256
GATE_COLS = N_NSA_KV * LANE
D_PROJ = 3 * D_CONV + D_NSA + 6 * D_NSA_KV + 4 * D_RET + GATE_COLS
COL_Q = 3 * D_CONV
COL_KV = COL_Q + D_NSA
COL_RET = COL_KV + 6 * D_NSA_KV
COL_GATE = COL_RET + 4 * D_RET

VMEM_LIMIT = 56 * 1024 * 1024


def _cparams(sem):
    return pltpu.CompilerParams(dimension_semantics=sem, vmem_limit_bytes=VMEM_LIMIT)


def _dot(a, b, **kw):
    return jnp.dot(a, b, preferred_element_type=F32, **kw)


def _dot_nt(a, b, **kw):
    return lax.dot_general(a, b, (((1,), (1,)), ((), ())), preferred_element_type=F32, **kw)


def _rms(x):
    return x * lax.rsqrt(jnp.mean(x * x, axis=-1, keepdims=True) + EPS)


def _ada_kernel(c_ref, w_ref, b_ref, o_ref):
    c = c_ref[...]
    ca = (c * jax.nn.sigmoid(c)).astype(BF16)
    o_ref[0] = _dot(ca, w_ref[0].astype(BF16)) + b_ref[0]


def ada_modulation(c, ada_w, ada_b):
    L, d, n = ada_w.shape
    b = c.shape[0]
    tn = 1024
    return pl.pallas_call(
        _ada_kernel,
        grid=(L, n // tn),
        in_specs=[
            pl.BlockSpec((b, d), lambda l, j: (0, 0)),
            pl.BlockSpec((1, d, tn), lambda l, j: (l, 0, j)),
            pl.BlockSpec((1, 1, tn), lambda l, j: (l, 0, j)),
        ],
        out_specs=pl.BlockSpec((1, b, tn), lambda l, j: (l, 0, j)),
        out_shape=jax.ShapeDtypeStruct((L, b, n), F32),
        compiler_params=_cparams(("arbitrary", "arbitrary")),
        name="ada_modulation",
    )(c, ada_w, ada_b.reshape(L, 1, n))


def _norm_mod(x, g, sc, sh):
    return (_rms(x) * g) * (1.0 + sc) + sh


def _in_proj_kernel(x_ref, sh_ref, sc_ref, g_ref, w_ref, o_ref, h_scr):
    @pl.when(pl.program_id(1) == 0)
    def _():
        h_scr[...] = _norm_mod(x_ref[...], g_ref[...], sc_ref[0], sh_ref[0]).astype(BF16)

    o_ref[...] = _dot(h_scr[...], w_ref[...])


def in_projection(x, mod, norm_g, w, seq):
    t, d = x.shape
    n = w.shape[1]
    tm, tn = 512, 1280
    per = seq // tm
    return pl.pallas_call(
        _in_proj_kernel,
        grid=(t // tm, n // tn),
        in_specs=[
            pl.BlockSpec((tm, d), lambda i, j: (i, 0)),
            pl.BlockSpec((1, 1, d), lambda i, j: (i // per, 0, 0)),
            pl.BlockSpec((1, 1, d), lambda i, j: (i // per, 0, 1)),
            pl.BlockSpec((1, d), lambda i, j: (0, 0)),
            pl.BlockSpec((d, tn), lambda i, j: (0, j)),
        ],
        out_specs=pl.BlockSpec((tm, tn), lambda i, j: (i, j)),
        out_shape=jax.ShapeDtypeStruct((t, n), F32),
        scratch_shapes=[pltpu.VMEM((tm, d), BF16)],
        compiler_params=_cparams(("arbitrary", "arbitrary")),
        name="in_projection",
    )(x, mod, mod, norm_g, w)


def _rope_table_kernel(pos_ref, inv_ref, sgn_ref, cos_ref, sin_ref):
    ang = pos_ref[...].astype(F32) * inv_ref[...]
    cos_ref[...] = jnp.cos(ang)
    sin_ref[...] = jnp.sin(ang) * sgn_ref[...]


def rope_tables(pos_col, rot_dim, theta):
    t = pos_col.shape[0]
    half = rot_dim // 2
    inv_half = theta ** (-jnp.arange(half, dtype=F32) / half)
    inv = jnp.concatenate([inv_half, inv_half, jnp.zeros((LANE - rot_dim,), F32)]).reshape(1, LANE)
    sgn = np.zeros((1, LANE), np.float32)
    sgn[0, :half] = -1.0
    sgn[0, half:rot_dim] = 1.0
    ts = 1024
    return pl.pallas_call(
        _rope_table_kernel,
        grid=(t // ts,),
        in_specs=[
            pl.BlockSpec((ts, 1), lambda i: (i, 0)),
            pl.BlockSpec((1, LANE), lambda i: (0, 0)),
            pl.BlockSpec((1, LANE), lambda i: (0, 0)),
        ],
        out_specs=[pl.BlockSpec((ts, LANE), lambda i: (i, 0))] * 2,
        out_shape=[jax.ShapeDtypeStruct((t, LANE), F32)] * 2,
        compiler_params=_cparams(("arbitrary",)),
        name="rope_tables",
    )(pos_col, inv, jnp.asarray(sgn))


def _rotate(x, cos, sin_signed, half):
    if 2 * half == LANE:
        swapped = pltpu.roll(x, half, 1)
    else:
        lane = lax.broadcasted_iota(jnp.int32, x.shape, 1)
        swapped = jnp.where(lane < half, pltpu.roll(x, LANE - half, 1), pltpu.roll(x, half, 1))
    return x * cos + swapped * sin_signed


def _nsa_rope_kernel(q0_ref, q1_ref, kvc_ref, kvs_ref, kvw_ref, cos_ref, sin_ref,
                     q_out, kvc_out, k_out, vt_out):
    cos, sin = cos_ref[...], sin_ref[...]
    half = ROPE_DIM // 2
    hpb = D_CONV // HEAD_DIM
    scale = HEAD_DIM ** -0.5 * LOG2_E
    for blk, src in enumerate((q0_ref, q1_ref)):
        for h in range(hpb):
            sl = slice(h * HEAD_DIM, (h + 1) * HEAD_DIM)
            q_out[:, blk * D_CONV + h * HEAD_DIM: blk * D_CONV + (h + 1) * HEAD_DIM] = (
                (_rotate(src[:, sl], cos, sin, half) * scale).astype(q_out.dtype))
    for g in range(N_NSA_KV):
        sl = slice(g * HEAD_DIM, (g + 1) * HEAD_DIM)
        kvc_out[:, sl] = _rotate(kvc_ref[:, sl], cos, sin, half)
    kvc_out[:, D_NSA_KV:] = kvc_ref[:, D_NSA_KV:]
    for br, src in enumerate((kvs_ref, kvw_ref)):
        for g in range(N_NSA_KV):
            sl = slice(g * HEAD_DIM, (g + 1) * HEAD_DIM)
            dst = slice((br * N_NSA_KV + g) * HEAD_DIM, (br * N_NSA_KV + g + 1) * HEAD_DIM)
            k_out[:, dst] = _rotate(src[:, sl], cos, sin, half).astype(k_out.dtype)
            vsl = slice(D_NSA_KV + g * HEAD_DIM, D_NSA_KV + (g + 1) * HEAD_DIM)
            for tile in range(vt_out.shape[2]):
                rows = slice(tile * NSA_TILE, (tile + 1) * NSA_TILE)
                vt_out[0, br * N_NSA_KV + g, tile] = src[rows, vsl].T.astype(vt_out.dtype)


def nsa_rope(proj, cos, sin, batch, seq):
    t = proj.shape[0]
    ts = 512
    per = seq // ts
    w = D_CONV
    blk = lambda j: pl.BlockSpec((ts, w), lambda i, j=j: (i, j))
    tab = pl.BlockSpec((ts, LANE), lambda i: (i, 0))
    out = pl.BlockSpec((ts, w), lambda i: (i, 0))
    qb = COL_Q // w
    kb = COL_KV // w
    return pl.pallas_call(
        _nsa_rope_kernel,
        grid=(t // ts,),
        in_specs=[blk(qb), blk(qb + 1), blk(kb), blk(kb + 1), blk(kb + 2), tab, tab],
        out_specs=[pl.BlockSpec((ts, D_NSA), lambda i: (i, 0)), out, out,
                   pl.BlockSpec((1, 2 * N_NSA_KV, ts // NSA_TILE, HEAD_DIM, NSA_TILE),
                                lambda i: (i // per, 0, i % per, 0, 0))],
        out_shape=[
            jax.ShapeDtypeStruct((t, D_NSA), BF16),
            jax.ShapeDtypeStruct((t, w), F32),
            jax.ShapeDtypeStruct((t, w), BF16),
            jax.ShapeDtypeStruct((batch, 2 * N_NSA_KV, seq // NSA_TILE, HEAD_DIM, NSA_TILE), BF16),
        ],
        compiler_params=_cparams(("arbitrary",)),
        name="nsa_rope",
    )(proj, proj, proj, proj, proj, cos, sin)


def _conv_kernel(cb_ref, cc_ref, cu_ref, w_ref, g_ref, o_ref, ext):
    ts = cb_ref.shape[0]

    @pl.when(pl.program_id(1) == 0)
    def _():
        ext[0:8, :] = jnp.zeros((8, ext.shape[1]), F32)

    v = cc_ref[...] * cu_ref[...]
    ext[8:, :] = v
    v1 = ext[pl.ds(7, ts), :]
    v2 = ext[pl.ds(6, ts), :]
    y = cb_ref[...] * (w_ref[0:1, :] * v2 + w_ref[1:2, :] * v1 + w_ref[2:3, :] * v)
    ext[0:8, :] = v[ts - 8:, :]
    for gi in range(CONV_GROUPS):
        sl = slice(gi * HEAD_DIM, (gi + 1) * HEAD_DIM)
        o_ref[:, sl] = (_rms(y[:, sl]) * g_ref[:, sl]).astype(o_ref.dtype)


def short_conv(proj, conv_w_t, conv_g, batch, seq):
    t = proj.shape[0]
    ts = 512
    per = seq // ts
    w = D_CONV
    blk = lambda j: pl.BlockSpec((ts, w), lambda b, s, j=j: (b * per + s, j))
    return pl.pallas_call(
        _conv_kernel,
        grid=(batch, per),
        in_specs=[blk(0), blk(1), blk(2),
                  pl.BlockSpec((CONV_WIDTH, w), lambda b, s: (0, 0)),
                  pl.BlockSpec((1, w), lambda b, s: (0, 0))],
        out_specs=pl.BlockSpec((ts, w), lambda b, s: (b * per + s, 0)),
        out_shape=jax.ShapeDtypeStruct((t, w), BF16),
        scratch_shapes=[pltpu.VMEM((8 + ts, w), F32)],
        compiler_params=_cparams(("arbitrary", "arbitrary")),
        name="short_conv",
    )(proj, proj, proj, conv_w_t, conv_g)


def _compress_kernel(h_ref, pe_ref, w1_ref, w2_ref, o_ref, ot_ref):
    h = h_ref[0, 0]
    n_h, half = h.shape
    pe = pe_ref[0]
    a = _dot((h + pe[:, :half]).astype(BF16), w1_ref[0, :half, :])
    b = _dot((h + pe[:, half:]).astype(BF16), w1_ref[0, half:, :])
    pre = a + pltpu.roll(b, n_h - 1, 0)
    out = _dot(jax.nn.gelu(pre).astype(BF16), w2_ref[0])
    o_ref[0, 0] = out.astype(o_ref.dtype)
    ot_ref[0, 0] = out.T.astype(ot_ref.dtype)


def compress(kvc_h, pe_flat, w1, w2):
    b, four, n_h, dh = kvc_h.shape
    return pl.pallas_call(
        _compress_kernel,
        grid=(b, four),
        in_specs=[
            pl.BlockSpec((1, 1, n_h, dh), lambda i, j: (i, j, 0, 0)),
            pl.BlockSpec((1, 1, 2 * dh), lambda i, j: (j // N_NSA_KV, 0, 0)),
            pl.BlockSpec((1, 2 * dh, CMP_HIDDEN), lambda i, j: (j // N_NSA_KV, 0, 0)),
            pl.BlockSpec((1, CMP_HIDDEN, HEAD_DIM), lambda i, j: (j // N_NSA_KV, 0, 0)),
        ],
        out_specs=[pl.BlockSpec((1, 1, n_h, HEAD_DIM), lambda i, j: (i, j, 0, 0)),
                   pl.BlockSpec((1, 1, HEAD_DIM, n_h), lambda i, j: (i, j, 0, 0))],
        out_shape=[jax.ShapeDtypeStruct((b, four, n_h, HEAD_DIM), BF16),
                   jax.ShapeDtypeStruct((b, four, HEAD_DIM, n_h), BF16)],
        compiler_params=_cparams(("arbitrary", "arbitrary")),
        name="nsa_compress",
    )(kvc_h, pe_flat, w1, w2)


def _nsa_kernel(q_ref, kc_ref, vct_ref, ks_ref, kw_ref, vst_ref, vwt_ref, gate_ref, ovt_ref, eaug_ref, g_ref,
                o_ref, kaug_scr, qaug_scr, s_scr, m_scr, l_scr, acc_scr, out_scr, *, n_cmp, n_slc, n_sel):
    tq = q_ref.shape[0]
    rows = NSA_REP * tq
    i = pl.program_id(2)
    t0 = i * tq
    hd = HEAD_DIM

    @pl.when(i == 0)
    def _():
        kaug_scr[:, :hd] = ks_ref[...]
        kaug_scr[:, hd:] = eaug_ref[...]

    for r in range(NSA_REP):
        qaug_scr[r * tq:(r + 1) * tq, :hd] = q_ref[:, r * hd:(r + 1) * hd]

    t_all =t0 + lax.rem(lax.broadcasted_iota(jnp.int32, (1, rows), 1), tq)
    gates_t = jax.nn.sigmoid(gate_ref[...]).T
    gate_rows = [jnp.concatenate([gates_t[3 * r + br:3 * r + br + 1, :] for r in range(NSA_REP)], axis=1)
                 for br in range(3)]

    key_sub = lax.broadcasted_iota(jnp.int32, (tq, 1), 0)
    causal = t0 + key_sub <= t_all

    n_back = WINDOW // tq
    pieces = []
    for back in range(n_back, -1, -1):
        jt = jnp.maximum(i - back, 0)
        sj = _dot_nt(kw_ref[pl.ds(pl.multiple_of(jt * tq, tq), tq), :], qaug_scr[:, :hd])
        if back == 0:
            sj = jnp.where(causal, sj, NEG_INF)
        elif back == n_back:
            sj = jnp.where((jt * tq + key_sub > t_all - WINDOW) & (i >= back), sj, NEG_INF)
        else:
            sj = jnp.where(i >= back, sj, NEG_INF)
        pieces.append((sj, vwt_ref[0, 0, jt]))
    m_w = pieces[0][0].max(axis=0, keepdims=True)
    for sj, _ in pieces[1:]:
        m_w = jnp.maximum(m_w, sj.max(axis=0, keepdims=True))
    l_w = None
    acc_w = None
    for sj, vt_tile in pieces:
        p = jnp.exp2(sj - m_w)
        lj = jnp.sum(p, axis=0, keepdims=True)
        aj = _dot(vt_tile, p.astype(BF16))
        l_w = lj if l_w is None else l_w + lj
        acc_w = aj if acc_w is None else acc_w + aj
    out_scr[...] = (gate_rows[2] / l_w) * acc_w

    kc = kc_ref[0, 0]
    vct = vct_ref[0, 0]
    n_h = kc.shape[0]
    c_idx = lax.broadcasted_iota(jnp.int32, (n_h, 1), 0)
    c_end = jnp.where(c_idx < n_cmp, c_idx * CMP_STRIDE + (CMP_LEN - 1), jnp.iinfo(jnp.int32).max)
    s = _dot_nt(kc, qaug_scr[:, :hd])
    sm = jnp.where(c_end <= t_all, s, NEG_INF)
    e = jnp.exp2(sm - jnp.max(sm, axis=0, keepdims=True))
    inv = jnp.where(t_all >= CMP_LEN - 1, 1.0 / jnp.sum(e, axis=0, keepdims=True), 0.0)
    p = e * inv
    out_scr[...] += gate_rows[0] * _dot(vct, p.astype(BF16))
    psum_t = p[:, 0:tq]
    for r in range(1, NSA_REP):
        psum_t = psum_t + p[:, r * tq:(r + 1) * tq]
    imp_t =_dot(ovt_ref[...], psum_t, precision=lax.Precision.HIGHEST)

    j_idx = lax.broadcasted_iota(jnp.int32, (n_slc, 1), 0)
    jq = (t0 + lax.broadcasted_iota(jnp.int32, (1, tq), 1)) // SLC_LEN
    forced = (j_idx == 0) | (j_idx == jq) | (j_idx == jq - 1)
    val = jnp.where(forced, FORCE_SCORE, jnp.where(j_idx <= jq, imp_t[:n_slc], -1.0))
    bias_t = jnp.full((n_slc, tq), NEG_INF, F32)
    for _ in range(n_sel):
        top = jnp.max(val, axis=0, keepdims=True)
        first = jnp.min(jnp.where(val == top, j_idx, n_slc), axis=0, keepdims=True)
        hit = j_idx == first
        bias_t = jnp.where(hit, 0.0, bias_t)
        val = jnp.where(hit, -jnp.inf, val)
    if n_slc < LANE:
        bias_t = jnp.concatenate([bias_t, jnp.zeros((LANE - n_slc, tq), F32)], axis=0)
    bias = bias_t.T.astype(BF16)
    for r in range(NSA_REP):
        qaug_scr[r * tq:(r + 1) * tq, hd:] = bias

    def sel_scores(jt):
        k0 = pl.multiple_of(jt * tq, tq)
        return _dot_nt(kaug_scr[pl.ds(k0, tq), :], qaug_scr[...])

    def sel_update(s, vt_tile, mask):
        if mask is not None:
            s = jnp.where(mask, s, NEG_INF)
        m_old = m_scr[...]
        m_new = jnp.maximum(m_old, jnp.max(s, axis=0, keepdims=True))
        alpha = jnp.exp2(m_old - m_new)
        p = jnp.exp2(s - m_new)
        l_scr[...] = alpha * l_scr[...] + jnp.sum(p, axis=0, keepdims=True)
        acc_scr[...] = alpha * acc_scr[...] + _dot(vt_tile, p.astype(BF16))
        m_scr[...] = m_new

    m_scr[...] = jnp.full(m_scr.shape, NEG_INF, F32)
    l_scr[...] = jnp.zeros(l_scr.shape, F32)
    acc_scr[...] = jnp.zeros(acc_scr.shape, F32)
    s_scr[...] = sel_scores(0)

    def sel_body(jt, carry):
        s_cur = s_scr[...]
        s_next = sel_scores(jt + 1)
        sel_update(s_cur, vst_ref[0, 0, jt], None)
        s_scr[...] = s_next
        return carry

    lax.fori_loop(0, i, sel_body, 0)
    sel_update(s_scr[...], vst_ref[0, 0, i], causal)
    out_scr[...] += (gate_rows[1] / l_scr[...]) * acc_scr[...]


    for r in range(NSA_REP):
        o_t = out_scr[:, r * tq:(r + 1) * tq]
        o_t = o_t * lax.rsqrt(jnp.mean(o_t * o_t, axis=0, keepdims=True) + EPS)
        sl = slice(r * hd, (r + 1) * hd)
        o_ref[:, sl] = (o_t.T * g_ref[:, sl]).astype(o_ref.dtype)


def _overlap_matrix_t(n_h, n_cmp, n_slc):
    cs = np.arange(n_cmp) * CMP_STRIDE
    js = np.arange(n_slc) * SLC_LEN
    ov = np.minimum(cs[:, None] + CMP_LEN, js[None, :] + SLC_LEN) - np.maximum(cs[:, None], js[None, :])
    out = np.zeros((LANE, n_h), np.float32)
    out[:n_slc, :n_cmp] = (np.clip(ov, 0, None) / CMP_LEN).T
    return out


def nsa_attention(q_rot, kc, kct, k_sw, vt_sw, proj, nsa_g, batch, seq):
    t = q_rot.shape[0]
    tq = NSA_TILE
    per = seq // tq
    n_h = seq // CMP_STRIDE
    n_cmp = (seq - CMP_LEN) // CMP_STRIDE + 1
    n_slc = seq // SLC_LEN
    n_sel = min(N_SLC, n_slc)
    assert n_slc <= LANE and WINDOW % tq == 0 and tq % LANE == 0
    ovt = jnp.asarray(_overlap_matrix_t(n_h, n_cmp, n_slc))
    eaug = np.zeros((seq, LANE), np.float32)
    eaug[np.arange(seq), np.arange(seq) // SLC_LEN] = 1.0
    eaug = jnp.asarray(eaug, BF16)
    hd = HEAD_DIM
    qw = NSA_REP * hd
    rows = NSA_REP * tq
    seq_spec = lambda off: pl.BlockSpec((seq, hd), lambda b, g, i, off=off: (b, off + g))
    vt_spec = lambda off: pl.BlockSpec((1, 1, per, hd, tq), lambda b, g, i, off=off: (b, off + g, 0, 0, 0))
    kern = functools.partial(_nsa_kernel, n_cmp=n_cmp, n_slc=n_slc, n_sel=n_sel)
    return pl.pallas_call(
        kern,
        grid=(batch, N_NSA_KV, per),
        in_specs=[
            pl.BlockSpec((tq, qw), lambda b, g, i: (b * per + i, g)),
            pl.BlockSpec((1, 1, n_h, hd), lambda b, g, i: (b, g, 0, 0)),
            pl.BlockSpec((1, 1, hd, n_h), lambda b, g, i: (b, N_NSA_KV + g, 0, 0)),
            seq_spec(0), seq_spec(N_NSA_KV), vt_spec(0), vt_spec(N_NSA_KV),
            pl.BlockSpec((tq, LANE), lambda b, g, i: (b * per + i, COL_GATE // LANE + g)),
            pl.BlockSpec((LANE, n_h), lambda b, g, i: (0, 0)),
            pl.BlockSpec((seq, LANE), lambda b, g, i: (0, 0)),
            pl.BlockSpec((1, qw), lambda b, g, i: (0, g)),
        ],
        out_specs=pl.BlockSpec((tq, qw), lambda b, g, i: (b * per + i, g)),
        out_shape=jax.ShapeDtypeStruct((t, D_NSA), BF16),
        scratch_shapes=[
            pltpu.VMEM((seq, 2 * hd), BF16),
            pltpu.VMEM((rows, 2 * hd), BF16),
            pltpu.VMEM((tq, rows), F32),
            pltpu.VMEM((1, rows), F32),
            pltpu.VMEM((1, rows), F32),
            pltpu.VMEM((hd, rows), F32),
            pltpu.VMEM((hd, rows), F32),
        ],
        compiler_params=_cparams(("arbitrary", "arbitrary", "arbitrary")),
        name="nsa_attention",
    )(q_rot, kc, kct, k_sw, k_sw, vt_sw, vt_sw, proj, ovt, eaug, nsa_g)


def _retention_kernel(q_ref, k_ref, v_ref, gate_ref, cos_ref, sin_ref, lg_ref, g_ref, o_ref, state):
    c = q_ref.shape[0]

    @pl.when(pl.program_id(2) == 0)
    def _():
        state[...] = jnp.zeros(state.shape, F32)

    lg = lg_ref[0][:, 0:1]
    cos, sin = cos_ref[...], sin_ref[...]
    half = HEAD_DIM // 2
    q = _rotate(q_ref[...], cos, sin, half)
    k = _rotate(k_ref[...], cos, sin, half) * HEAD_DIM ** -0.5
    v = v_ref[...].astype(BF16)
    n_row = lax.broadcasted_iota(jnp.int32, (c, 1), 0).astype(F32)
    n_col = lax.broadcasted_iota(jnp.int32, (1, c), 1).astype(F32)
    diff = n_row - n_col
    decay = jnp.where(diff >= 0.0, jnp.exp(jnp.maximum(diff, 0.0) * lg), 0.0)
    scores = _dot_nt(q.astype(BF16), k.astype(BF16)) * decay
    o = _dot(scores.astype(BF16), v)
    xi = jnp.exp((n_row + 1.0) * lg)
    o = o + _dot((q * xi).astype(BF16), state[...].astype(BF16))
    zeta = jnp.exp((c - 1.0 - n_row) * lg)
    kz_t = (k * zeta).T.astype(BF16)
    state[...] = state[...] * jnp.exp(c * lg) + _dot(kz_t, v)
    gate = gate_ref[...]
    o_ref[...] = (gate * jax.nn.sigmoid(gate) * (_rms(o) * g_ref[...])).astype(o_ref.dtype)


def retention(proj, cos, sin, ret_g, batch, seq):
    t = proj.shape[0]
    c = 256
    per = seq // c
    hd = HEAD_DIM
    base = COL_RET // hd
    blk = lambda off: pl.BlockSpec((c, hd), lambda b, h, s, off=off: (b * per + s, base + off * N_RET_HEADS + h))
    tab = pl.BlockSpec((c, hd), lambda b, h, s: (b * per + s, 0))
    lg = jnp.log1p(-(2.0 ** (-5.0 - jnp.arange(N_RET_HEADS, dtype=F32))))
    lg = jnp.broadcast_to(lg[:, None, None], (N_RET_HEADS, 1, LANE))
    return pl.pallas_call(
        _retention_kernel,
        grid=(batch, N_RET_HEADS, per),
        in_specs=[blk(0), blk(1), blk(2), blk(3), tab, tab,
                  pl.BlockSpec((1, 1, LANE), lambda b, h, s: (h, 0, 0)),
                  pl.BlockSpec((1, hd), lambda b, h, s: (0, h))],
        out_specs=pl.BlockSpec((c, hd), lambda b, h, s: (b * per + s, h)),
        out_shape=jax.ShapeDtypeStruct((t, D_RET), BF16),
        scratch_shapes=[pltpu.VMEM((hd, hd), F32)],
        compiler_params=_cparams(("arbitrary", "arbitrary", "arbitrary")),
        name="retention",
    )(proj, proj, proj, proj, cos, sin, lg, ret_g)


def _out_proj_kernel(yc_ref, yn_ref, yr_ref, w_ref, x_ref, g_ref, o_ref):
    acc = _dot(yc_ref[...], w_ref[0:D_CONV, :])
    acc = acc + _dot(yn_ref[...], w_ref[D_CONV:D_CONV + D_NSA, :])
    acc = acc + _dot(yr_ref[...], w_ref[D_CONV + D_NSA:, :])
    o_ref[...] = x_ref[...] + g_ref[0] * acc


def out_projection(y_conv, y_nsa, y_ret, w_out, x, mod, seq):
    t, d = x.shape
    tm = 512
    per = seq // tm
    row = lambda w: pl.BlockSpec((tm, w), lambda i: (i, 0))
    return pl.pallas_call(
        _out_proj_kernel,
        grid=(t // tm,),
        in_specs=[row(D_CONV), row(D_NSA), row(D_RET),
                  pl.BlockSpec((D_MIX, d), lambda i: (0, 0)),
                  row(d),
                  pl.BlockSpec((1, 1, d), lambda i: (i // per, 0, 2))],
        out_specs=row(d),
        out_shape=jax.ShapeDtypeStruct((t, d), F32),
        compiler_params=_cparams(("arbitrary",)),
        name="out_projection",
    )(y_conv, y_nsa, y_ret, w_out, x, mod)


def _rank_rows(val, n):
    idx = lax.broadcasted_iota(jnp.int32, (n, 1), 0)
    rank = jnp.zeros(val.shape, jnp.int32)
    for rp in range(n):
        vp = val[rp:rp + 1, :]
        ahead = (vp > val) | ((vp == val) & (idx > rp))
        rank = rank + ahead.astype(jnp.int32)
    return rank


def _router_kernel(x_ref, sh_ref, sc_ref, g_ref, rw_ref, rb_ref, tri_ref, h_ref, pos_ref, w_ref, cnt_ref, carry):
    @pl.when(pl.program_id(0) == 0)
    def _():
        carry[...] = jnp.zeros(carry.shape, F32)

    h = _norm_mod(x_ref[...], g_ref[...], sc_ref[0], sh_ref[0])
    h_ref[...] = h
    tm = h.shape[0]
    scores = jax.nn.sigmoid(_dot_nt(rw_ref[...], h, precision=lax.Precision.HIGHEST))
    biased = scores + rb_ref[...]
    sub = lax.broadcasted_iota(jnp.int32, (GROUP_SIZE, 1), 0)
    gs = []
    for g in range(N_EXPERT_GROUPS):
        bg = biased[g * GROUP_SIZE:(g + 1) * GROUP_SIZE, :]
        m1 = jnp.max(bg, axis=0, keepdims=True)
        first = jnp.min(jnp.where(bg == m1, sub, GROUP_SIZE), axis=0, keepdims=True)
        m2 = jnp.max(jnp.where(sub == first, -jnp.inf, bg), axis=0, keepdims=True)
        gs.append(m1 + m2)
    gs = jnp.concatenate(gs, axis=0)
    gkeep = _rank_rows(gs, N_EXPERT_GROUPS) < TOPK_GROUPS
    keep = jnp.concatenate(
        [jnp.broadcast_to(gkeep[g:g + 1, :], (GROUP_SIZE, tm)) for g in range(N_EXPERT_GROUPS)], axis=0)
    masked = jnp.where(keep, biased, -jnp.inf)
    sel = _rank_rows(masked, N_EXPERTS) < TOP_K
    w = jnp.where(sel, scores, 0.0)
    w_ref[...] = ROUTED_SCALE * w / jnp.sum(w, axis=0, keepdims=True)
    self32 = sel.astype(F32)
    pos = carry[:, 0:1] + _dot(self32.astype(BF16), tri_ref[...])
    pos_ref[...] = jnp.where(sel, pos.astype(jnp.int32), -1)
    carry[...] = carry[...] + jnp.sum(self32, axis=1, keepdims=True)
    cnt_ref[...] = carry[...].astype(jnp.int32)


def moe_route(x, mod, norm_g, router_w_t, router_bias, seq):
    t, d = x.shape
    tm = 512
    per = seq // tm
    tri = jnp.asarray(np.triu(np.ones((tm, tm), np.float32), 1), BF16)
    ecol = pl.BlockSpec((N_EXPERTS, tm), lambda i: (0, i))
    return pl.pallas_call(
        _router_kernel,
        grid=(t // tm,),
        in_specs=[
            pl.BlockSpec((tm, d), lambda i: (i, 0)),
            pl.BlockSpec((1, 1, d), lambda i: (i // per, 0, 3)),
            pl.BlockSpec((1, 1, d), lambda i: (i // per, 0, 4)),
            pl.BlockSpec((1, d), lambda i: (0, 0)),
            pl.BlockSpec((N_EXPERTS, d), lambda i: (0, 0)),
            pl.BlockSpec((N_EXPERTS, 1), lambda i: (0, 0)),
            pl.BlockSpec((tm, tm), lambda i: (0, 0)),
        ],
        out_specs=[pl.BlockSpec((tm, d), lambda i: (i, 0)), ecol, ecol,
                   pl.BlockSpec((N_EXPERTS, LANE), lambda i: (0, 0))],
        out_shape=[
            jax.ShapeDtypeStruct((t, d), F32),
            jax.ShapeDtypeStruct((N_EXPERTS, t), jnp.int32),
            jax.ShapeDtypeStruct((N_EXPERTS, t), F32),
            jax.ShapeDtypeStruct((N_EXPERTS, LANE), jnp.int32),
        ],
        scratch_shapes=[pltpu.VMEM((N_EXPERTS, LANE), F32)],
        compiler_params=_cparams(("arbitrary",)),
        name="moe_route",
    )(x, mod, mod, norm_g, router_w_t, router_bias, tri)


def _slots_kernel(pos_ref, w_ref, start_ref, tri_ref, slot_ref, w8_ref):
    pos = pos_ref[...]
    sel = pos >= 0
    slot = pos + start_ref[...]
    order = _dot(tri_ref[...], sel.astype(F32).astype(BF16))
    w = w_ref[...]
    for k in range(TOP_K):
        mk = sel & (order == float(k))
        slot_ref[k:k + 1, :] = jnp.sum(jnp.where(mk, slot, 0), axis=0, keepdims=True)
        w8_ref[k:k + 1, :] = jnp.sum(jnp.where(mk, w, 0.0), axis=0, keepdims=True)


def moe_slots(pos_t, w_t, pad_start):
    e, t = pos_t.shape
    tm = 2048
    tri = jnp.asarray(np.tril(np.ones((e, e), np.float32), -1), BF16)
    ecol = pl.BlockSpec((e, tm), lambda i: (0, i))
    kcol = pl.BlockSpec((TOP_K, tm), lambda i: (0, i))
    return pl.pallas_call(
        _slots_kernel,
        grid=(t // tm,),
        in_specs=[ecol, ecol, pl.BlockSpec((e, 1), lambda i: (0, 0)), pl.BlockSpec((e, e), lambda i: (0, 0))],
        out_specs=[kcol, kcol],
        out_shape=[jax.ShapeDtypeStruct((TOP_K, t), jnp.int32), jax.ShapeDtypeStruct((TOP_K, t), F32)],
        compiler_params=_cparams(("arbitrary",)),
        name="moe_slots",
    )(pos_t, w_t, pad_start, tri)


def _dispatch_kernel(pad_lo_ref, n_pad_ref, slot_ref, h_ref, xs_ref, zrow, sem, zsem):
    tm = h_ref.shape[0]

    @pl.when(pl.program_id(0) == 0)
    def _():
        zrow[...] = jnp.zeros(zrow.shape, zrow.dtype)

        def zero_copy(slot):
            return pltpu.make_async_copy(zrow.at[pl.ds(0, 1), :], xs_ref.at[pl.ds(slot, 1), :], zsem)

        def per_expert(e, carry):
            lo, n = pad_lo_ref[e], n_pad_ref[e]

            def start(r, c):
                zero_copy(lo + r).start()
                return c

            def wait(r, c):
                zero_copy(lo + r).wait()
                return c

            lax.fori_loop(0, n, start, 0)
            lax.fori_loop(0, n, wait, 0)
            return carry

        lax.fori_loop(0, N_EXPERTS, per_expert, 0)

    def row_copy(t, slot):
        return pltpu.make_async_copy(h_ref.at[pl.ds(t, 1), :], xs_ref.at[pl.ds(slot, 1), :], sem)

    def start(t, carry):
        for k in range(TOP_K):
            row_copy(t, slot_ref[k, t]).start()
        return carry

    lax.fori_loop(0, tm, start, 0)

    def wait(t, carry):
        for k in range(TOP_K):
            row_copy(t, slot_ref[k, t]).wait()
        return carry

    lax.fori_loop(0, tm, wait, 0)


def moe_dispatch(pad_lo, n_pad, slot8, h, n_slots):
    t, d = h.shape
    tm = 256
    grid_spec = pltpu.PrefetchScalarGridSpec(
        num_scalar_prefetch=2,
        grid=(t // tm,),
        in_specs=[
            pl.BlockSpec((TOP_K, tm), lambda i, lo, n: (0, i), memory_space=pltpu.SMEM),
            pl.BlockSpec((tm, d), lambda i, lo, n: (i, 0)),
        ],
        out_specs=pl.BlockSpec(memory_space=pl.ANY),
        scratch_shapes=[pltpu.VMEM((8, d), F32), pltpu.SemaphoreType.DMA(()), pltpu.SemaphoreType.DMA(())],
    )
    return pl.pallas_call(
        _dispatch_kernel,
        grid_spec=grid_spec,
        out_shape=jax.ShapeDtypeStruct((n_slots, d), F32),
        compiler_params=_cparams(("arbitrary",)),
        name="moe_dispatch",
    )(pad_lo, n_pad, slot8, h)


def _swiglu(x_bf16, w_gu, w_down, d_hidden):
    gu = _dot(x_bf16, w_gu)
    gate, up = gu[:, :d_hidden], gu[:, d_hidden:]
    return _dot((gate * jax.nn.sigmoid(gate) * up).astype(BF16), w_down)


def _experts_kernel(be_ref, first_ref, nb_ref, xs_ref, wgu_ref, wd_ref, ys_ref, wgu_bf, wd_bf):
    del be_ref
    b = pl.program_id(0)
    used = b < nb_ref[0]

    @pl.when(used & (first_ref[b] == 1))
    def _():
        wgu_bf[...] = wgu_ref[0, 0].astype(BF16)
        wd_bf[...] = wd_ref[0, 0].astype(BF16)

    @pl.when(used)
    def _():
        ys_ref[...] = _swiglu(xs_ref[...].astype(BF16), wgu_bf[...], wd_bf[...], D_EXPERT)

    @pl.when(jnp.logical_not(used))
    def _():
        ys_ref[...] = jnp.zeros(ys_ref.shape, F32)


def moe_experts(block_e, first, n_used, xs, w_gu, w_down, layer):
    n_slots, d = xs.shape
    n_blocks = n_slots // MOE_BLOCK
    grid_spec = pltpu.PrefetchScalarGridSpec(
        num_scalar_prefetch=3,
        grid=(n_blocks,),
        in_specs=[
            pl.BlockSpec((MOE_BLOCK, d), lambda b, be, fi, nb: (jnp.minimum(b, nb[0] - 1), 0)),
            pl.BlockSpec((1, 1, d, 2 * D_EXPERT), lambda b, be, fi, nb: (layer, be[b], 0, 0)),
            pl.BlockSpec((1, 1, D_EXPERT, d), lambda b, be, fi, nb: (layer, be[b], 0, 0)),
        ],
        out_specs=pl.BlockSpec((MOE_BLOCK, d), lambda b, be, fi, nb: (b, 0)),
        scratch_shapes=[pltpu.VMEM((d, 2 * D_EXPERT), BF16), pltpu.VMEM((D_EXPERT, d), BF16)],
    )
    return pl.pallas_call(
        _experts_kernel,
        grid_spec=grid_spec,
        out_shape=jax.ShapeDtypeStruct((n_slots, d), F32),
        compiler_params=_cparams(("arbitrary",)),
        name="moe_experts",
    )(block_e, first, n_used, xs, w_gu, w_down)


def _shared_kernel(h_ref, wgu_ref, wd_ref, o_ref):
    o_ref[...] = _swiglu(h_ref[...].astype(BF16), wgu_ref[...], wd_ref[...], D_SHARED)


def shared_expert(h, w_gu, w_down):
    t, d = h.shape
    tm = 512
    return pl.pallas_call(
        _shared_kernel,
        grid=(t // tm,),
        in_specs=[pl.BlockSpec((tm, d), lambda i: (i, 0)),
                  pl.BlockSpec((d, 2 * D_SHARED), lambda i: (0, 0)),
                  pl.BlockSpec((D_SHARED, d), lambda i: (0, 0))],
        out_specs=pl.BlockSpec((tm, d), lambda i: (i, 0)),
        out_shape=jax.ShapeDtypeStruct((t, d), F32),
        compiler_params=_cparams(("arbitrary",)),
        name="shared_expert",
    )(h, w_gu, w_down)


def _combine_kernel(slot_ref, w_ref, sh_ref, x_ref, g_ref, fg_ref, ys_ref, o_ref, buf, sem, *, final_norm):
    tm = x_ref.shape[0]

    def row_copy(t, k, slot):
        return pltpu.make_async_copy(ys_ref.at[pl.ds(slot, 1), :], buf.at[k, pl.ds(t, 1), :], sem)

    def start(t, carry):
        for k in range(TOP_K):
            row_copy(t, k, slot_ref[k, t]).start()
        return carry

    lax.fori_loop(0, tm, start, 0)

    def wait(t, carry):
        for k in range(TOP_K):
            row_copy(t, k, slot_ref[k, t]).wait()
        return carry

    lax.fori_loop(0, tm, wait, 0)
    w = w_ref[...]
    routed = w[:, 0:1] * buf[0]
    for k in range(1, TOP_K):
        routed = routed + w[:, k:k + 1] * buf[k]
    out = x_ref[...] + g_ref[0] * (routed + sh_ref[...])
    if final_norm:
        out = _rms(out) * fg_ref[...]
    o_ref[...] = out


def moe_combine(slot8, w8_rows, shared, x, mod, final_g, ys, seq, final_norm):
    t, d = x.shape
    tm = 128
    per = seq // tm
    row = pl.BlockSpec((tm, d), lambda i: (i, 0))
    return pl.pallas_call(
        functools.partial(_combine_kernel, final_norm=final_norm),
        grid=(t // tm,),
        in_specs=[
            pl.BlockSpec((TOP_K, tm), lambda i: (0, i), memory_space=pltpu.SMEM),
            pl.BlockSpec((tm, TOP_K), lambda i: (i, 0)),
            row, row,
            pl.BlockSpec((1, 1, d), lambda i: (i // per, 0, 5)),
            pl.BlockSpec((1, d), lambda i: (0, 0)),
            pl.BlockSpec(memory_space=pl.ANY),
        ],
        out_specs=row,
        out_shape=jax.ShapeDtypeStruct((t, d), F32),
        scratch_shapes=[pltpu.VMEM((TOP_K, tm, d), F32), pltpu.SemaphoreType.DMA(())],
        compiler_params=_cparams(("arbitrary",)),
        name="moe_combine",
    )(slot8, w8_rows, shared, x, mod, final_g, ys)


def token_mixer_layer(x, mod, tabs, p, batch, seq):
    cos_n, sin_n, cos_r, sin_r = tabs
    proj = in_projection(x, mod, p["norm1_g"], p["w_in"], seq)
    y_conv = short_conv(proj, p["conv_w_t"], p["conv_g"], batch, seq)
    q_rot, kvc, k_sw, vt_sw = nsa_rope(proj, cos_n, sin_n, batch, seq)
    n_h = seq // CMP_STRIDE
    kvc_h = kvc.reshape(batch, seq, 2 * N_NSA_KV, HEAD_DIM).transpose(0, 2, 1, 3)
    kvc_h = kvc_h.reshape(batch, 2 * N_NSA_KV, n_h, CMP_STRIDE * HEAD_DIM)
    kc, kct = compress(kvc_h, p["cmp_pe"], p["cmp_w1"], p["cmp_w2"])
    y_nsa = nsa_attention(q_rot, kc, kct, k_sw, vt_sw, proj, p["nsa_g"], batch, seq)
    y_ret = retention(proj, cos_r, sin_r, p["ret_g"], batch, seq)
    return out_projection(y_conv, y_nsa, y_ret, p["w_out"], x, mod, seq)


def moe_layer(x, mod, p, seq, final_g, final_norm):
    t, d = x.shape
    n_assign = t * TOP_K
    n_blocks = (n_assign + N_EXPERTS * (MOE_BLOCK - 1) + MOE_BLOCK - 1) // MOE_BLOCK
    h, pos_t, w_t, counts = moe_route(x, mod, p["norm2_g"], p["router_w_t"], p["router_bias"], seq)
    counts = counts[:, 0]
    padded = (counts + MOE_BLOCK - 1) // MOE_BLOCK * MOE_BLOCK
    pad_end = jnp.cumsum(padded)
    pad_start = (pad_end - padded).astype(jnp.int32)
    blk0 = jnp.arange(n_blocks, dtype=jnp.int32) * MOE_BLOCK
    block_e = jnp.minimum(jnp.sum(pad_end[None, :] <= blk0[:, None], axis=1), N_EXPERTS - 1).astype(jnp.int32)
    n_used = (pad_end[-1:] // MOE_BLOCK).astype(jnp.int32)
    first = jnp.concatenate([jnp.ones((1,), jnp.int32), (block_e[1:] != block_e[:-1]).astype(jnp.int32)])
    slot8, w8 = moe_slots(pos_t, w_t, pad_start[:, None])
    xs = moe_dispatch(pad_start + counts, (padded - counts).astype(jnp.int32), slot8, h, n_blocks * MOE_BLOCK)
    ys = moe_experts(block_e, first, n_used, xs, p["exp_w_gu"], p["exp_w_down"], p["layer"])
    shared = shared_expert(h, p["shared_w_gu"], p["shared_w_down"])
    return moe_combine(slot8, w8.T, shared, x, mod, final_g, ys, seq, final_norm)


def _reorder_w_in(w_in):
    c_gate = COL_KV + 6 * D_NSA_KV
    gates = w_in[:, c_gate:c_gate + 3 * N_NSA_HEADS]
    per_group = 3 * NSA_REP
    padded = [jnp.pad(gates[:, g * per_group:(g + 1) * per_group], ((0, 0), (0, LANE - per_group)))
              for g in range(N_NSA_KV)]
    return jnp.concatenate([w_in[:, :c_gate], w_in[:, c_gate + 3 * N_NSA_HEADS:]] + padded, axis=1)


def kernel(x, c, positions, ada_w, ada_b, norm1_g, norm2_g, w_in, conv_w, conv_g, cmp_pe, cmp_w1, cmp_w2,
           nsa_g, ret_g, w_out, router_w, router_bias, exp_w_gu, exp_w_down, shared_w_gu, shared_w_down, final_g):
    batch, seq, d = x.shape
    depth = ada_w.shape[0]
    t = batch * seq
    mod_all = ada_modulation(c, ada_w, ada_b)
    pos_col = positions.reshape(t, 1)
    tabs = rope_tables(pos_col, ROPE_DIM, ROPE_THETA) + rope_tables(pos_col, HEAD_DIM, RET_THETA)
    xt = x.reshape(t, d)
    final_g2 = final_g.reshape(1, d)
    for l in range(depth):
        mod = mod_all[l].reshape(batch, 1, ADA_CHUNKS * d)
        p = dict(
            norm1_g=norm1_g[l].reshape(1, d),
            norm2_g=norm2_g[l].reshape(1, d),
            w_in=_reorder_w_in(w_in[l]).astype(BF16),
            conv_w_t=conv_w[l].T,
            conv_g=conv_g[l].reshape(1, D_CONV),
            cmp_pe=cmp_pe[l].reshape(2, 1, CMP_LEN * HEAD_DIM),
            cmp_w1=cmp_w1[l].astype(BF16),
            cmp_w2=cmp_w2[l].astype(BF16),
            nsa_g=nsa_g[l].reshape(1, D_NSA),
            ret_g=ret_g[l].reshape(1, D_RET),
            w_out=w_out[l].astype(BF16),
            router_w_t=router_w[l].T,
            router_bias=router_bias[l].reshape(N_EXPERTS, 1),
            layer=l,
            exp_w_gu=exp_w_gu,
            exp_w_down=exp_w_down,
            shared_w_gu=shared_w_gu[l].astype(BF16),
            shared_w_down=shared_w_down[l].astype(BF16),
        )
        xt = token_mixer_layer(xt, mod, tabs, p, batch, seq)
        xt = moe_layer(xt, mod, p, seq, final_g2, final_norm=(l == depth - 1))
    return xt.reshape(batch, seq, d)
```

```python
import functools

import numpy as np
import jax
import jax.numpy as jnp
from jax import lax
from jax.experimental import pallas as pl
from jax.experimental.pallas import tpu as pltpu

F32 = jnp.float32
BF16 = jnp.bfloat16

D_MODEL = 2048
HEAD_DIM = 128
D_CONV = D_MODEL // 4
CONV_GROUPS = D_CONV // HEAD_DIM
CONV_WIDTH = 3
N_NSA_HEADS = D_MODEL // 2 // HEAD_DIM
N_NSA_KV = 2
NSA_REP = N_NSA_HEADS // N_NSA_KV
D_NSA = N_NSA_HEADS * HEAD_DIM
D_NSA_KV = N_NSA_KV * HEAD_DIM
CMP_LEN = 32
CMP_STRIDE = 16
CMP_HIDDEN = 256
SLC_LEN = 64
N_SLC = 16
WINDOW = 512
ROPE_THETA = 500000.0
ROPE_DIM = HEAD_DIM // 4
N_RET_HEADS = D_MODEL // 4 // HEAD_DIM
D_RET = N_RET_HEADS * HEAD_DIM
RET_THETA = 10000.0
D_MIX = D_CONV + D_NSA + D_RET
N_EXPERTS = 64
N_EXPERT_GROUPS = 8
GROUP_SIZE = N_EXPERTS // N_EXPERT_GROUPS
TOPK_GROUPS = 4
TOP_K = 8
D_EXPERT = 512
D_SHARED = 512
ROUTED_SCALE = 2.5
MOE_BLOCK = 256
ADA_CHUNKS = 6
EPS = 1e-6
NEG_INF = -1e30
FORCE_SCORE = 1e4

LANE = 128
LOG2_E = 1.4426950408889634
NSA_TILE = 256
GATE_COLS = N_NSA_KV * LANE
D_PROJ = 3 * D_CONV + D_NSA + 6 * D_NSA_KV + 4 * D_RET + GATE_COLS
COL_Q = 3 * D_CONV
COL_KV = COL_Q + D_NSA
COL_RET = COL_KV + 6 * D_NSA_KV
COL_GATE = COL_RET + 4 * D_RET

VMEM_LIMIT = 56 * 1024 * 1024


def _cparams(sem):
    return pltpu.CompilerParams(dimension_semantics=sem, vmem_limit_bytes=VMEM_LIMIT)


def _dot(a, b, **kw):
    return jnp.dot(a, b, preferred_element_type=F32, **kw)


def _dot_nt(a, b, **kw):
    return lax.dot_general(a, b, (((1,), (1,)), ((), ())), preferred_element_type=F32, **kw)


def _rms(x):
    return x * lax.rsqrt(jnp.mean(x * x, axis=-1, keepdims=True) + EPS)


def _ada_kernel(c_ref, w_ref, b_ref, o_ref):
    c = c_ref[...]
    ca = (c * jax.nn.sigmoid(c)).astype(BF16)
    o_ref[0] = _dot(ca, w_ref[0].astype(BF16)) + b_ref[0]


def ada_modulation(c, ada_w, ada_b):
    L, d, n = ada_w.shape
    b = c.shape[0]
    tn = 1024
    return pl.pallas_call(
        _ada_kernel,
        grid=(L, n // tn),
        in_specs=[
            pl.BlockSpec((b, d), lambda l, j: (0, 0)),
            pl.BlockSpec((1, d, tn), lambda l, j: (l, 0, j)),
            pl.BlockSpec((1, 1, tn), lambda l, j: (l, 0, j)),
        ],
        out_specs=pl.BlockSpec((1, b, tn), lambda l, j: (l, 0, j)),
        out_shape=jax.ShapeDtypeStruct((L, b, n), F32),
        compiler_params=_cparams(("arbitrary", "arbitrary")),
        name="ada_modulation",
    )(c, ada_w, ada_b.reshape(L, 1, n))


def _norm_mod(x, g, sc, sh):
    return (_rms(x) * g) * (1.0 + sc) + sh


def _in_proj_kernel(x_ref, sh_ref, sc_ref, g_ref, w_ref, o_ref, h_scr):
    @pl.when(pl.program_id(1) == 0)
    def _():
        h_scr[...] = _norm_mod(x_ref[...], g_ref[...], sc_ref[0], sh_ref[0]).astype(BF16)

    o_ref[...] = _dot(h_scr[...], w_ref[...])


def in_projection(x, mod, norm_g, w, seq):
    t, d = x.shape
    n = w.shape[1]
    tm, tn = 512, 1280
    per = seq // tm
    return pl.pallas_call(
        _in_proj_kernel,
        grid=(t // tm, n // tn),
        in_specs=[
            pl.BlockSpec((tm, d), lambda i, j: (i, 0)),
            pl.BlockSpec((1, 1, d), lambda i, j: (i // per, 0, 0)),
            pl.BlockSpec((1, 1, d), lambda i, j: (i // per, 0, 1)),
            pl.BlockSpec((1, d), lambda i, j: (0, 0)),
            pl.BlockSpec((d, tn), lambda i, j: (0, j)),
        ],
        out_specs=pl.BlockSpec((tm, tn), lambda i, j: (i, j)),
        out_shape=jax.ShapeDtypeStruct((t, n), F32),
        scratch_shapes=[pltpu.VMEM((tm, d), BF16)],
        compiler_params=_cparams(("arbitrary", "arbitrary")),
        name="in_projection",
    )(x, mod, mod, norm_g, w)


def _rope_table_kernel(pos_ref, inv_ref, sgn_ref, cos_ref, sin_ref):
    ang = pos_ref[...].astype(F32) * inv_ref[...]
    cos_ref[...] = jnp.cos(ang)
    sin_ref[...] = jnp.sin(ang) * sgn_ref[...]


def rope_tables(pos_col, rot_dim, theta):
    t = pos_col.shape[0]
    half = rot_dim // 2
    inv_half = theta ** (-jnp.arange(half, dtype=F32) / half)
    inv = jnp.concatenate([inv_half, inv_half, jnp.zeros((LANE - rot_dim,), F32)]).reshape(1, LANE)
    sgn = np.zeros((1, LANE), np.float32)
    sgn[0, :half] = -1.0
    sgn[0, half:rot_dim] = 1.0
    ts = 1024
    return pl.pallas_call(
        _rope_table_kernel,
        grid=(t // ts,),
        in_specs=[
            pl.BlockSpec((ts, 1), lambda i: (i, 0)),
            pl.BlockSpec((1, LANE), lambda i: (0, 0)),
            pl.BlockSpec((1, LANE), lambda i: (0, 0)),
        ],
        out_specs=[pl.BlockSpec((ts, LANE), lambda i: (i, 0))] * 2,
        out_shape=[jax.ShapeDtypeStruct((t, LANE), F32)] * 2,
        compiler_params=_cparams(("arbitrary",)),
        name="rope_tables",
    )(pos_col, inv, jnp.asarray(sgn))


def _rotate(x, cos, sin_signed, half):
    if 2 * half == LANE:
        swapped = pltpu.roll(x, half, 1)
    else:
        lane = lax.broadcasted_iota(jnp.int32, x.shape, 1)
        swapped = jnp.where(lane < half, pltpu.roll(x, LANE - half, 1), pltpu.roll(x, half, 1))
    return x * cos + swapped * sin_signed


def _nsa_rope_kernel(q0_ref, q1_ref, kvc_ref, kvs_ref, kvw_ref, cos_ref, sin_ref,
                     q_out, kvc_out, k_out, vt_out):
    cos, sin = cos_ref[...], sin_ref[...]
    half = ROPE_DIM // 2
    hpb = D_CONV // HEAD_DIM
    scale = HEAD_DIM ** -0.5 * LOG2_E
    for blk, src in enumerate((q0_ref, q1_ref)):
        for h in range(hpb):
            sl = slice(h * HEAD_DIM, (h + 1) * HEAD_DIM)
            q_out[:, blk * D_CONV + h * HEAD_DIM: blk * D_CONV + (h + 1) * HEAD_DIM] = (
                (_rotate(src[:, sl], cos, sin, half) * scale).astype(q_out.dtype))
    for g in range(N_NSA_KV):
        sl = slice(g * HEAD_DIM, (g + 1) * HEAD_DIM)
        kvc_out[:, sl] = _rotate(kvc_ref[:, sl], cos, sin, half)
    kvc_out[:, D_NSA_KV:] = kvc_ref[:, D_NSA_KV:]
    for br, src in enumerate((kvs_ref, kvw_ref)):
        for g in range(N_NSA_KV):
            sl = slice(g * HEAD_DIM, (g + 1) * HEAD_DIM)
            dst = slice((br * N_NSA_KV + g) * HEAD_DIM, (br * N_NSA_KV + g + 1) * HEAD_DIM)
            k_out[:, dst] = _rotate(src[:, sl], cos, sin, half).astype(k_out.dtype)
            vsl = slice(D_NSA_KV + g * HEAD_DIM, D_NSA_KV + (g + 1) * HEAD_DIM)
            for tile in range(vt_out.shape[2]):
                rows = slice(tile * NSA_TILE, (tile + 1) * NSA_TILE)
                vt_out[0, br * N_NSA_KV + g, tile] = src[rows, vsl].T.astype(vt_out.dtype)


def nsa_rope(proj, cos, sin, batch, seq):
    t = proj.shape[0]
    ts = 512
    per = seq // ts
    w = D_CONV
    blk = lambda j: pl.BlockSpec((ts, w), lambda i, j=j: (i, j))
    tab = pl.BlockSpec((ts, LANE), lambda i: (i, 0))
    out = pl.BlockSpec((ts, w), lambda i: (i, 0))
    qb = COL_Q // w
    kb = COL_KV // w
    return pl.pallas_call(
        _nsa_rope_kernel,
        grid=(t // ts,),
        in_specs=[blk(qb), blk(qb + 1), blk(kb), blk(kb + 1), blk(kb + 2), tab, tab],
        out_specs=[pl.BlockSpec((ts, D_NSA), lambda i: (i, 0)), out, out,
                   pl.BlockSpec((1, 2 * N_NSA_KV, ts // NSA_TILE, HEAD_DIM, NSA_TILE),
                                lambda i: (i // per, 0, i % per, 0, 0))],
        out_shape=[
            jax.ShapeDtypeStruct((t, D_NSA), BF16),
            jax.ShapeDtypeStruct((t, w), F32),
            jax.ShapeDtypeStruct((t, w), BF16),
            jax.ShapeDtypeStruct((batch, 2 * N_NSA_KV, seq // NSA_TILE, HEAD_DIM, NSA_TILE), BF16),
        ],
        compiler_params=_cparams(("arbitrary",)),
        name="nsa_rope",
    )(proj, proj, proj, proj, proj, cos, sin)


def _conv_kernel(cb_ref, cc_ref, cu_ref, w_ref, g_ref, o_ref, ext):
    ts = cb_ref.shape[0]

    @pl.when(pl.program_id(1) == 0)
    def _():
        ext[0:8, :] = jnp.zeros((8, ext.shape[1]), F32)

    v = cc_ref[...] * cu_ref[...]
    ext[8:, :] = v
    v1 = ext[pl.ds(7, ts), :]
    v2 = ext[pl.ds(6, ts), :]
    y = cb_ref[...] * (w_ref[0:1, :] * v2 + w_ref[1:2, :] * v1 + w_ref[2:3, :] * v)
    ext[0:8, :] = v[ts - 8:, :]
    for gi in range(CONV_GROUPS):
        sl = slice(gi * HEAD_DIM, (gi + 1) * HEAD_DIM)
        o_ref[:, sl] = (_rms(y[:, sl]) * g_ref[:, sl]).astype(o_ref.dtype)


def short_conv(proj, conv_w_t, conv_g, batch, seq):
    t = proj.shape[0]
    ts = 512
    per = seq // ts
    w = D_CONV
    blk = lambda j: pl.BlockSpec((ts, w), lambda b, s, j=j: (b * per + s, j))
    return pl.pallas_call(
        _conv_kernel,
        grid=(batch, per),
        in_specs=[blk(0), blk(1), blk(2),
                  pl.BlockSpec((CONV_WIDTH, w), lambda b, s: (0, 0)),
                  pl.BlockSpec((1, w), lambda b, s: (0, 0))],
        out_specs=pl.BlockSpec((ts, w), lambda b, s: (b * per + s, 0)),
        out_shape=jax.ShapeDtypeStruct((t, w), BF16),
        scratch_shapes=[pltpu.VMEM((8 + ts, w), F32)],
        compiler_params=_cparams(("arbitrary", "arbitrary")),
        name="short_conv",
    )(proj, proj, proj, conv_w_t, conv_g)


def _compress_kernel(h_ref, pe_ref, w1_ref, w2_ref, o_ref, ot_ref):
    h = h_ref[0, 0]
    n_h, half = h.shape
    pe = pe_ref[0]
    a = _dot((h + pe[:, :half]).astype(BF16), w1_ref[0, :half, :])
    b = _dot((h + pe[:, half:]).astype(BF16), w1_ref[0, half:, :])
    pre = a + pltpu.roll(b, n_h - 1, 0)
    out = _dot(jax.nn.gelu(pre).astype(BF16), w2_ref[0])
    o_ref[0, 0] = out.astype(o_ref.dtype)
    ot_ref[0, 0] = out.T.astype(ot_ref.dtype)


def compress(kvc_h, pe_flat, w1, w2):
    b, four, n_h, dh = kvc_h.shape
    return pl.pallas_call(
        _compress_kernel,
        grid=(b, four),
        in_specs=[
            pl.BlockSpec((1, 1, n_h, dh), lambda i, j: (i, j, 0, 0)),
            pl.BlockSpec((1, 1, 2 * dh), lambda i, j: (j // N_NSA_KV, 0, 0)),
            pl.BlockSpec((1, 2 * dh, CMP_HIDDEN), lambda i, j: (j // N_NSA_KV, 0, 0)),
            pl.BlockSpec((1, CMP_HIDDEN, HEAD_DIM), lambda i, j: (j // N_NSA_KV, 0, 0)),
        ],
        out_specs=[pl.BlockSpec((1, 1, n_h, HEAD_DIM), lambda i, j: (i, j, 0, 0)),
                   pl.BlockSpec((1, 1, HEAD_DIM, n_h), lambda i, j: (i, j, 0, 0))],
        out_shape=[jax.ShapeDtypeStruct((b, four, n_h, HEAD_DIM), BF16),
                   jax.ShapeDtypeStruct((b, four, HEAD_DIM, n_h), BF16)],
        compiler_params=_cparams(("arbitrary", "arbitrary")),
        name="nsa_compress",
    )(kvc_h, pe_flat, w1, w2)


def _nsa_kernel(q_ref, kc_ref, vct_ref, ks_ref, kw_ref, vst_ref, vwt_ref, gate_ref, ovt_ref, eaug_ref, g_ref,
                o_ref, kaug_scr, qaug_scr, s_scr, m_scr, l_scr, acc_scr, out_scr, *, n_cmp, n_slc, n_sel):
    tq = q_ref.shape[0]
    rows = NSA_REP * tq
    i = pl.program_id(2)
    t0 = i * tq
    hd = HEAD_DIM

    @pl.when(i == 0)
    def _():
        kaug_scr[:, :hd] = ks_ref[...]
        kaug_scr[:, hd:] = eaug_ref[...]

    for r in range(NSA_REP):
        qaug_scr[r * tq:(r + 1) * tq, :hd] = q_ref[:, r * hd:(r + 1) * hd]

    t_all =t0 + lax.rem(lax.broadcasted_iota(jnp.int32, (1, rows), 1), tq)
    gates_t = jax.nn.sigmoid(gate_ref[...]).T
    gate_rows = [jnp.concatenate([gates_t[3 * r + br:3 * r + br + 1, :] for r in range(NSA_REP)], axis=1)
                 for br in range(3)]

    key_sub = lax.broadcasted_iota(jnp.int32, (tq, 1), 0)
    causal = t0 + key_sub <= t_all

    n_back = WINDOW // tq
    pieces = []
    for back in range(n_back, -1, -1):
        jt = jnp.maximum(i - back, 0)
        sj = _dot_nt(kw_ref[pl.ds(pl.multiple_of(jt * tq, tq), tq), :], qaug_scr[:, :hd])
        if back == 0:
            sj = jnp.where(causal, sj, NEG_INF)
        elif back == n_back:
            sj = jnp.where((jt * tq + key_sub > t_all - WINDOW) & (i >= back), sj, NEG_INF)
        else:
            sj = jnp.where(i >= back, sj, NEG_INF)
        pieces.append((sj, vwt_ref[0, 0, jt]))
    m_w = pieces[0][0].max(axis=0, keepdims=True)
    for sj, _ in pieces[1:]:
        m_w = jnp.maximum(m_w, sj.max(axis=0, keepdims=True))
    l_w = None
    acc_w = None
    for sj, vt_tile in pieces:
        p = jnp.exp2(sj - m_w)
        lj = jnp.sum(p, axis=0, keepdims=True)
        aj = _dot(vt_tile, p.astype(BF16))
        l_w = lj if l_w is None else l_w + lj
        acc_w = aj if acc_w is None else acc_w + aj
    out_scr[...] = (gate_rows[2] / l_w) * acc_w

    kc = kc_ref[0, 0]
    vct = vct_ref[0, 0]
    n_h = kc.shape[0]
    c_idx = lax.broadcasted_iota(jnp.int32, (n_h, 1), 0)
    c_end = jnp.where(c_idx < n_cmp, c_idx * CMP_STRIDE + (CMP_LEN - 1), jnp.iinfo(jnp.int32).max)
    s = _dot_nt(kc, qaug_scr[:, :hd])
    sm = jnp.where(c_end <= t_all, s, NEG_INF)
    e = jnp.exp2(sm - jnp.max(sm, axis=0, keepdims=True))
    inv = jnp.where(t_all >= CMP_LEN - 1, 1.0 / jnp.sum(e, axis=0, keepdims=True), 0.0)
    p = e * inv
    out_scr[...] += gate_rows[0] * _dot(vct, p.astype(BF16))
    psum_t = p[:, 0:tq]
    for r in range(1, NSA_REP):
        psum_t = psum_t + p[:, r * tq:(r + 1) * tq]
    imp_t =_dot(ovt_ref[...], psum_t, precision=lax.Precision.HIGHEST)

    j_idx = lax.broadcasted_iota(jnp.int32, (n_slc, 1), 0)
    jq = (t0 + lax.broadcasted_iota(jnp.int32, (1, tq), 1)) // SLC_LEN
    forced = (j_idx == 0) | (j_idx == jq) | (j_idx == jq - 1)
    val = jnp.where(forced, FORCE_SCORE, jnp.where(j_idx <= jq, imp_t[:n_slc], -1.0))
    bias_t = jnp.full((n_slc, tq), NEG_INF, F32)
    for _ in range(n_sel):
        top = jnp.max(val, axis=0, keepdims=True)
        first = jnp.min(jnp.where(val == top, j_idx, n_slc), axis=0, keepdims=True)
        hit = j_idx == first
        bias_t = jnp.where(hit, 0.0, bias_t)
        val = jnp.where(hit, -jnp.inf, val)
    if n_slc < LANE:
        bias_t = jnp.concatenate([bias_t, jnp.zeros((LANE - n_slc, tq), F32)], axis=0)
    bias = bias_t.T.astype(BF16)
    for r in range(NSA_REP):
        qaug_scr[r * tq:(r + 1) * tq, hd:] = bias

    def sel_scores(jt):
        k0 = pl.multiple_of(jt * tq, tq)
        return _dot_nt(kaug_scr[pl.ds(k0, tq), :], qaug_scr[...])

    def sel_update(s, vt_tile, mask):
        if mask is not None:
            s = jnp.where(mask, s, NEG_INF)
        m_old = m_scr[...]
        m_new = jnp.maximum(m_old, jnp.max(s, axis=0, keepdims=True))
        alpha = jnp.exp2(m_old - m_new)
        p = jnp.exp2(s - m_new)
        l_scr[...] = alpha * l_scr[...] + jnp.sum(p, axis=0, keepdims=True)
        acc_scr[...] = alpha * acc_scr[...] + _dot(vt_tile, p.astype(BF16))
        m_scr[...] = m_new

    m_scr[...] = jnp.full(m_scr.shape, NEG_INF, F32)
    l_scr[...] = jnp.zeros(l_scr.shape, F32)
    acc_scr[...] = jnp.zeros(acc_scr.shape, F32)
    s_scr[...] = sel_scores(0)

    def sel_body(jt, carry):
        s_cur = s_scr[...]
        s_next = sel_scores(jt + 1)
        sel_update(s_cur, vst_ref[0, 0, jt], None)
        s_scr[...] = s_next
        return carry

    lax.fori_loop(0, i, sel_body, 0)
    sel_update(s_scr[...], vst_ref[0, 0, i], causal)
    out_scr[...] += (gate_rows[1] / l_scr[...]) * acc_scr[...]


    for r in range(NSA_REP):
        o_t = out_scr[:, r * tq:(r + 1) * tq]
        o_t = o_t * lax.rsqrt(jnp.mean(o_t * o_t, axis=0, keepdims=True) + EPS)
        sl = slice(r * hd, (r + 1) * hd)
        o_ref[:, sl] = (o_t.T * g_ref[:, sl]).astype(o_ref.dtype)


def _overlap_matrix_t(n_h, n_cmp, n_slc):
    cs = np.arange(n_cmp) * CMP_STRIDE
    js = np.arange(n_slc) * SLC_LEN
    ov = np.minimum(cs[:, None] + CMP_LEN, js[None, :] + SLC_LEN) - np.maximum(cs[:, None], js[None, :])
    out = np.zeros((LANE, n_h), np.float32)
    out[:n_slc, :n_cmp] = (np.clip(ov, 0, None) / CMP_LEN).T
    return out


def nsa_attention(q_rot, kc, kct, k_sw, vt_sw, proj, nsa_g, batch, seq):
    t = q_rot.shape[0]
    tq = NSA_TILE
    per = seq // tq
    n_h = seq // CMP_STRIDE
    n_cmp = (seq - CMP_LEN) // CMP_STRIDE + 1
    n_slc = seq // SLC_LEN
    n_sel = min(N_SLC, n_slc)
    assert n_slc <= LANE and WINDOW % tq == 0 and tq % LANE == 0
    ovt = jnp.asarray(_overlap_matrix_t(n_h, n_cmp, n_slc))
    eaug = np.zeros((seq, LANE), np.float32)
    eaug[np.arange(seq), np.arange(seq) // SLC_LEN] = 1.0
    eaug = jnp.asarray(eaug, BF16)
    hd = HEAD_DIM
    qw = NSA_REP * hd
    rows = NSA_REP * tq
    seq_spec = lambda off: pl.BlockSpec((seq, hd), lambda b, g, i, off=off: (b, off + g))
    vt_spec = lambda off: pl.BlockSpec((1, 1, per, hd, tq), lambda b, g, i, off=off: (b, off + g, 0, 0, 0))
    kern = functools.partial(_nsa_kernel, n_cmp=n_cmp, n_slc=n_slc, n_sel=n_sel)
    return pl.pallas_call(
        kern,
        grid=(batch, N_NSA_KV, per),
        in_specs=[
            pl.BlockSpec((tq, qw), lambda b, g, i: (b * per + i, g)),
            pl.BlockSpec((1, 1, n_h, hd), lambda b, g, i: (b, g, 0, 0)),
            pl.BlockSpec((1, 1, hd, n_h), lambda b, g, i: (b, N_NSA_KV + g, 0, 0)),
            seq_spec(0), seq_spec(N_NSA_KV), vt_spec(0), vt_spec(N_NSA_KV),
            pl.BlockSpec((tq, LANE), lambda b, g, i: (b * per + i, COL_GATE // LANE + g)),
            pl.BlockSpec((LANE, n_h), lambda b, g, i: (0, 0)),
            pl.BlockSpec((seq, LANE), lambda b, g, i: (0, 0)),
            pl.BlockSpec((1, qw), lambda b, g, i: (0, g)),
        ],
        out_specs=pl.BlockSpec((tq, qw), lambda b, g, i: (b * per + i, g)),
        out_shape=jax.ShapeDtypeStruct((t, D_NSA), BF16),
        scratch_shapes=[
            pltpu.VMEM((seq, 2 * hd), BF16),
            pltpu.VMEM((rows, 2 * hd), BF16),
            pltpu.VMEM((tq, rows), F32),
            pltpu.VMEM((1, rows), F32),
            pltpu.VMEM((1, rows), F32),
            pltpu.VMEM((hd, rows), F32),
            pltpu.VMEM((hd, rows), F32),
        ],
        compiler_params=_cparams(("arbitrary", "arbitrary", "arbitrary")),
        name="nsa_attention",
    )(q_rot, kc, kct, k_sw, k_sw, vt_sw, vt_sw, proj, ovt, eaug, nsa_g)


def _retention_kernel(q_ref, k_ref, v_ref, gate_ref, cos_ref, sin_ref, lg_ref, g_ref, o_ref, state):
    c = q_ref.shape[0]

    @pl.when(pl.program_id(2) == 0)
    def _():
        state[...] = jnp.zeros(state.shape, F32)

    lg = lg_ref[0][:, 0:1]
    cos, sin = cos_ref[...], sin_ref[...]
    half = HEAD_DIM // 2
    q = _rotate(q_ref[...], cos, sin, half)
    k = _rotate(k_ref[...], cos, sin, half) * HEAD_DIM ** -0.5
    v = v_ref[...].astype(BF16)
    n_row = lax.broadcasted_iota(jnp.int32, (c, 1), 0).astype(F32)
    n_col = lax.broadcasted_iota(jnp.int32, (1, c), 1).astype(F32)
    diff = n_row - n_col
    decay = jnp.where(diff >= 0.0, jnp.exp(jnp.maximum(diff, 0.0) * lg), 0.0)
    scores = _dot_nt(q.astype(BF16), k.astype(BF16)) * decay
    o = _dot(scores.astype(BF16), v)
    xi = jnp.exp((n_row + 1.0) * lg)
    o = o + _dot((q * xi).astype(BF16), state[...].astype(BF16))
    zeta = jnp.exp((c - 1.0 - n_row) * lg)
    kz_t = (k * zeta).T.astype(BF16)
    state[...] = state[...] * jnp.exp(c * lg) + _dot(kz_t, v)
    gate = gate_ref[...]
    o_ref[...] = (gate * jax.nn.sigmoid(gate) * (_rms(o) * g_ref[...])).astype(o_ref.dtype)


def retention(proj, cos, sin, ret_g, batch, seq):
    t = proj.shape[0]
    c = 256
    per = seq // c
    hd = HEAD_DIM
    base = COL_RET // hd
    blk = lambda off: pl.BlockSpec((c, hd), lambda b, h, s, off=off: (b * per + s, base + off * N_RET_HEADS + h))
    tab = pl.BlockSpec((c, hd), lambda b, h, s: (b * per + s, 0))
    lg = jnp.log1p(-(2.0 ** (-5.0 - jnp.arange(N_RET_HEADS, dtype=F32))))
    lg = jnp.broadcast_to(lg[:, None, None], (N_RET_HEADS, 1, LANE))
    return pl.pallas_call(
        _retention_kernel,
        grid=(batch, N_RET_HEADS, per),
        in_specs=[blk(0), blk(1), blk(2), blk(3), tab, tab,
                  pl.BlockSpec((1, 1, LANE), lambda b, h, s: (h, 0, 0)),
                  pl.BlockSpec((1, hd), lambda b, h, s: (0, h))],
        out_specs=pl.BlockSpec((c, hd), lambda b, h, s: (b * per + s, h)),
        out_shape=jax.ShapeDtypeStruct((t, D_RET), BF16),
        scratch_shapes=[pltpu.VMEM((hd, hd), F32)],
        compiler_params=_cparams(("arbitrary", "arbitrary", "arbitrary")),
        name="retention",
    )(proj, proj, proj, proj, cos, sin, lg, ret_g)


def _out_proj_kernel(yc_ref, yn_ref, yr_ref, w_ref, x_ref, g_ref, o_ref):
    acc = _dot(yc_ref[...], w_ref[0:D_CONV, :])
    acc = acc + _dot(yn_ref[...], w_ref[D_CONV:D_CONV + D_NSA, :])
    acc = acc + _dot(yr_ref[...], w_ref[D_CONV + D_NSA:, :])
    o_ref[...] = x_ref[...] + g_ref[0] * acc


def out_projection(y_conv, y_nsa, y_ret, w_out, x, mod, seq):
    t, d = x.shape
    tm = 512
    per = seq // tm
    row = lambda w: pl.BlockSpec((tm, w), lambda i: (i, 0))
    return pl.pallas_call(
        _out_proj_kernel,
        grid=(t // tm,),
        in_specs=[row(D_CONV), row(D_NSA), row(D_RET),
                  pl.BlockSpec((D_MIX, d), lambda i: (0, 0)),
                  row(d),
                  pl.BlockSpec((1, 1, d), lambda i: (i // per, 0, 2))],
        out_specs=row(d),
        out_shape=jax.ShapeDtypeStruct((t, d), F32),
        compiler_params=_cparams(("arbitrary",)),
        name="out_projection",
    )(y_conv, y_nsa, y_ret, w_out, x, mod)


def _rank_rows(val, n):
    idx = lax.broadcasted_iota(jnp.int32, (n, 1), 0)
    rank = jnp.zeros(val.shape, jnp.int32)
    for rp in range(n):
        vp = val[rp:rp + 1, :]
        ahead = (vp > val) | ((vp == val) & (idx > rp))
        rank = rank + ahead.astype(jnp.int32)
    return rank


def _router_kernel(x_ref, sh_ref, sc_ref, g_ref, rw_ref, rb_ref, tri_ref, h_ref, pos_ref, w_ref, cnt_ref, carry):
    @pl.when(pl.program_id(0) == 0)
    def _():
        carry[...] = jnp.zeros(carry.shape, F32)

    h = _norm_mod(x_ref[...], g_ref[...], sc_ref[0], sh_ref[0])
    h_ref[...] = h
    tm = h.shape[0]
    scores = jax.nn.sigmoid(_dot_nt(rw_ref[...], h, precision=lax.Precision.HIGHEST))
    biased = scores + rb_ref[...]
    sub = lax.broadcasted_iota(jnp.int32, (GROUP_SIZE, 1), 0)
    gs = []
    for g in range(N_EXPERT_GROUPS):
        bg = biased[g * GROUP_SIZE:(g + 1) * GROUP_SIZE, :]
        m1 = jnp.max(bg, axis=0, keepdims=True)
        first = jnp.min(jnp.where(bg == m1, sub, GROUP_SIZE), axis=0, keepdims=True)
        m2 = jnp.max(jnp.where(sub == first, -jnp.inf, bg), axis=0, keepdims=True)
        gs.append(m1 + m2)
    gs = jnp.concatenate(gs, axis=0)
    gkeep = _rank_rows(gs, N_EXPERT_GROUPS) < TOPK_GROUPS
    keep = jnp.concatenate(
        [jnp.broadcast_to(gkeep[g:g + 1, :], (GROUP_SIZE, tm)) for g in range(N_EXPERT_GROUPS)], axis=0)
    masked = jnp.where(keep, biased, -jnp.inf)
    sel = _rank_rows(masked, N_EXPERTS) < TOP_K
    w = jnp.where(sel, scores, 0.0)
    w_ref[...] = ROUTED_SCALE * w / jnp.sum(w, axis=0, keepdims=True)
    self32 = sel.astype(F32)
    pos = carry[:, 0:1] + _dot(self32.astype(BF16), tri_ref[...])
    pos_ref[...] = jnp.where(sel, pos.astype(jnp.int32), -1)
    carry[...] = carry[...] + jnp.sum(self32, axis=1, keepdims=True)
    cnt_ref[...] = carry[...].astype(jnp.int32)


def moe_route(x, mod, norm_g, router_w_t, router_bias, seq):
    t, d = x.shape
    tm = 512
    per = seq // tm
    tri = jnp.asarray(np.triu(np.ones((tm, tm), np.float32), 1), BF16)
    ecol = pl.BlockSpec((N_EXPERTS, tm), lambda i: (0, i))
    return pl.pallas_call(
        _router_kernel,
        grid=(t // tm,),
        in_specs=[
            pl.BlockSpec((tm, d), lambda i: (i, 0)),
            pl.BlockSpec((1, 1, d), lambda i: (i // per, 0, 3)),
            pl.BlockSpec((1, 1, d), lambda i: (i // per, 0, 4)),
            pl.BlockSpec((1, d), lambda i: (0, 0)),
            pl.BlockSpec((N_EXPERTS, d), lambda i: (0, 0)),
            pl.BlockSpec((N_EXPERTS, 1), lambda i: (0, 0)),
            pl.BlockSpec((tm, tm), lambda i: (0, 0)),
        ],
        out_specs=[pl.BlockSpec((tm, d), lambda i: (i, 0)), ecol, ecol,
                   pl.BlockSpec((N_EXPERTS, LANE), lambda i: (0, 0))],
        out_shape=[
            jax.ShapeDtypeStruct((t, d), F32),
            jax.ShapeDtypeStruct((N_EXPERTS, t), jnp.int32),
            jax.ShapeDtypeStruct((N_EXPERTS, t), F32),
            jax.ShapeDtypeStruct((N_EXPERTS, LANE), jnp.int32),
        ],
        scratch_shapes=[pltpu.VMEM((N_EXPERTS, LANE), F32)],
        compiler_params=_cparams(("arbitrary",)),
        name="moe_route",
    )(x, mod, mod, norm_g, router_w_t, router_bias, tri)


def _slots_kernel(pos_ref, w_ref, start_ref, tri_ref, slot_ref, w8_ref):
    pos = pos_ref[...]
    sel = pos >= 0
    slot = pos + start_ref[...]
    order = _dot(tri_ref[...], sel.astype(F32).astype(BF16))
    w = w_ref[...]
    for k in range(TOP_K):
        mk = sel & (order == float(k))
        slot_ref[k:k + 1, :] = jnp.sum(jnp.where(mk, slot, 0), axis=0, keepdims=True)
        w8_ref[k:k + 1, :] = jnp.sum(jnp.where(mk, w, 0.0), axis=0, keepdims=True)


def moe_slots(pos_t, w_t, pad_start):
    e, t = pos_t.shape
    tm = 2048
    tri = jnp.asarray(np.tril(np.ones((e, e), np.float32), -1), BF16)
    ecol = pl.BlockSpec((e, tm), lambda i: (0, i))
    kcol = pl.BlockSpec((TOP_K, tm), lambda i: (0, i))
    return pl.pallas_call(
        _slots_kernel,
        grid=(t // tm,),
        in_specs=[ecol, ecol, pl.BlockSpec((e, 1), lambda i: (0, 0)), pl.BlockSpec((e, e), lambda i: (0, 0))],
        out_specs=[kcol, kcol],
        out_shape=[jax.ShapeDtypeStruct((TOP_K, t), jnp.int32), jax.ShapeDtypeStruct((TOP_K, t), F32)],
        compiler_params=_cparams(("arbitrary",)),
        name="moe_slots",
    )(pos_t, w_t, pad_start, tri)


def _inverse_kernel(slot_ref, init_ref, inv_ref, sem, *, n_tok):
    tm = slot_ref.shape[1]
    i = pl.program_id(0)

    @pl.when(i == 0)
    def _():
        cp = pltpu.make_async_copy(init_ref, inv_ref, sem)
        cp.start()
        cp.wait()

    def body(t, carry):
        for k in range(TOP_K):
            inv_ref[slot_ref[k, t]] = k * n_tok + i * tm + t
        return carry

    lax.fori_loop(0, tm, body, 0)


def moe_inverse(slot8, n_slots):
    n_tok = slot8.shape[1]
    tm = 2048
    init = TOP_K * n_tok + jnp.arange(n_slots, dtype=jnp.int32)
    return pl.pallas_call(
        functools.partial(_inverse_kernel, n_tok=n_tok),
        grid=(n_tok // tm,),
        in_specs=[pl.BlockSpec((TOP_K, tm), lambda i: (0, i), memory_space=pltpu.SMEM),
                  pl.BlockSpec(memory_space=pl.ANY)],
        out_specs=pl.BlockSpec(memory_space=pltpu.SMEM),
        out_shape=jax.ShapeDtypeStruct((n_slots,), jnp.int32),
        scratch_shapes=[pltpu.SemaphoreType.DMA(())],
        compiler_params=_cparams(("arbitrary",)),
        name="moe_inverse",
    )(slot8, init)


def _swiglu(x_bf16, w_gu, w_down, d_hidden):
    gu = _dot(x_bf16, w_gu)
    gate, up = gu[:, :d_hidden], gu[:, d_hidden:]
    return _dot((gate * jax.nn.sigmoid(gate) * up).astype(BF16), w_down)


def _experts_kernel(be_ref, first_ref, nb_ref, src_ref, dst_ref, h_hbm, wgu_ref, wd_ref, y_hbm,
                    x0, x1, y0, y1, wgu_bf, wd_bf, gsem, ssem, *, n_tok, n_blocks):
    del be_ref
    s = pl.program_id(0)
    cb = s - 1
    used = (cb >= 0) & (cb < nb_ref[0])
    rows = x0.shape[0]

    @pl.when(s == 0)
    def _():
        y0[...] = jnp.zeros(y0.shape, F32)
        y1[...] = jnp.zeros(y1.shape, F32)

    @pl.when(used & (first_ref[jnp.clip(cb, 0, n_blocks - 1)] == 1))
    def _():
        wgu_bf[...] = wgu_ref[0, 0].astype(BF16)
        wd_bf[...] = wd_ref[0, 0].astype(BF16)

    def step(p, x_out, x_in, y_out, y_in):
        def gathered(buf, sem):
            return pltpu.make_async_copy(h_hbm.at[pl.ds(0, rows), :], buf, sem)

        def scattered(buf, sem):
            return pltpu.make_async_copy(buf, y_hbm.at[pl.ds(0, rows), :], sem)

        @pl.when(s >= 1)
        def _():
            gathered(x_in, gsem.at[1 - p]).wait()

        @pl.when(s >= 1)
        def _():
            scattered(y_out, ssem.at[1 - p]).wait()

        def issue_rows():
            for r in range(rows):
                tok = src_ref[0, 0, r] & (n_tok - 1)
                pltpu.make_async_copy(h_hbm.at[pl.ds(tok, 1), :], x_out.at[pl.ds(r, 1), :], gsem.at[p]).start()
                pltpu.make_async_copy(y_in.at[pl.ds(r, 1), :], y_hbm.at[pl.ds(dst_ref[0, 0, r], 1), :],
                                      ssem.at[p]).start()

        @pl.when(used)
        def _():
            issue_rows()
            y_out[...] = _swiglu(x_in[...].astype(BF16), wgu_bf[...], wd_bf[...], D_EXPERT)

        @pl.when(jnp.logical_not(used))
        def _():
            issue_rows()

        @pl.when(s == pl.num_programs(0) - 1)
        def _():
            gathered(x_out, gsem.at[p]).wait()
            scattered(y_in, ssem.at[p]).wait()

    @pl.when(lax.rem(s, 2) == 0)
    def _():
        step(0, x0, x1, y1, y0)

    @pl.when(lax.rem(s, 2) == 1)
    def _():
        step(1, x1, x0, y0, y1)


def moe_experts(block_e, first, n_used, inv, h, w_gu, w_down, layer):
    n_tok, d = h.shape
    n_slots = inv.shape[0]
    n_blocks = n_slots // MOE_BLOCK
    assert n_tok & (n_tok - 1) == 0, "source token = inv mod T uses a power-of-two T"
    src = inv.reshape(n_blocks, 1, MOE_BLOCK)
    n_rows = TOP_K * n_tok + n_slots + 2 * MOE_BLOCK
    warmup = TOP_K * n_tok + n_slots + jnp.arange(2 * MOE_BLOCK, dtype=jnp.int32)
    dst = jnp.concatenate([warmup, inv]).reshape(n_blocks + 2, 1, MOE_BLOCK)
    clamp = lambda b: jnp.clip(b, 0, n_blocks - 1)
    smem_blk = lambda f: pl.BlockSpec((1, 1, MOE_BLOCK), f, memory_space=pltpu.SMEM)
    grid_spec = pltpu.PrefetchScalarGridSpec(
        num_scalar_prefetch=3,
        grid=(n_blocks + 2,),
        in_specs=[
            smem_blk(lambda s, be, fi, nb: (clamp(s), 0, 0)),
            smem_blk(lambda s, be, fi, nb: (s, 0, 0)),
            pl.BlockSpec(memory_space=pl.ANY),
            pl.BlockSpec((1, 1, d, 2 * D_EXPERT), lambda s, be, fi, nb: (layer, be[clamp(s - 1)], 0, 0)),
            pl.BlockSpec((1, 1, D_EXPERT, d), lambda s, be, fi, nb: (layer, be[clamp(s - 1)], 0, 0)),
        ],
        out_specs=pl.BlockSpec(memory_space=pl.ANY),
        scratch_shapes=[pltpu.VMEM((MOE_BLOCK, d), F32)] * 4 + [
            pltpu.VMEM((d, 2 * D_EXPERT), BF16), pltpu.VMEM((D_EXPERT, d), BF16),
            pltpu.SemaphoreType.DMA((2,)), pltpu.SemaphoreType.DMA((2,))],
    )
    return pl.pallas_call(
        functools.partial(_experts_kernel, n_tok=n_tok, n_blocks=n_blocks),
        grid_spec=grid_spec,
        out_shape=jax.ShapeDtypeStruct((n_rows, d), F32),
        compiler_params=_cparams(("arbitrary",)),
        name="moe_experts",
    )(block_e, first, n_used, src, dst, h, w_gu, w_down)


def _shared_kernel(h_ref, wgu_ref, wd_ref, o_ref):
    o_ref[...] = _swiglu(h_ref[...].astype(BF16), wgu_ref[...], wd_ref[...], D_SHARED)


def shared_expert(h, w_gu, w_down):
    t, d = h.shape
    tm = 512
    return pl.pallas_call(
        _shared_kernel,
        grid=(t // tm,),
        in_specs=[pl.BlockSpec((tm, d), lambda i: (i, 0)),
                  pl.BlockSpec((d, 2 * D_SHARED), lambda i: (0, 0)),
                  pl.BlockSpec((D_SHARED, d), lambda i: (0, 0))],
        out_specs=pl.BlockSpec((tm, d), lambda i: (i, 0)),
        out_shape=jax.ShapeDtypeStruct((t, d), F32),
        compiler_params=_cparams(("arbitrary",)),
        name="shared_expert",
    )(h, w_gu, w_down)


def _combine_kernel(w_ref, sh_ref, x_ref, g_ref, fg_ref, *rest, final_norm):
    y_refs, o_ref = rest[:TOP_K], rest[TOP_K]
    w = w_ref[...]
    routed = w[:, 0:1] * y_refs[0][...]
    for k in range(1, TOP_K):
        routed = routed + w[:, k:k + 1] * y_refs[k][...]
    out = x_ref[...] + g_ref[0] * (routed + sh_ref[...])
    if final_norm:
        out = _rms(out) * fg_ref[...]
    o_ref[...] = out


def moe_combine(w8_rows, shared, x, mod, final_g, y, seq, final_norm):
    t, d = x.shape
    tm = 256
    per = seq // tm
    row = pl.BlockSpec((tm, d), lambda i: (i, 0))
    y_specs = [pl.BlockSpec((tm, d), lambda i, k=k: (k * (t // tm) + i, 0)) for k in range(TOP_K)]
    return pl.pallas_call(
        functools.partial(_combine_kernel, final_norm=final_norm),
        grid=(t // tm,),
        in_specs=[
            pl.BlockSpec((tm, TOP_K), lambda i: (i, 0)),
            row, row,
            pl.BlockSpec((1, 1, d), lambda i: (i // per, 0, 5)),
            pl.BlockSpec((1, d), lambda i: (0, 0)),
        ] + y_specs,
        out_specs=row,
        out_shape=jax.ShapeDtypeStruct((t, d), F32),
        compiler_params=_cparams(("arbitrary",)),
        name="moe_combine",
    )(w8_rows, shared, x, mod, final_g, *([y] * TOP_K))


def token_mixer_layer(x, mod, tabs, p, batch, seq):
    cos_n, sin_n, cos_r, sin_r = tabs
    proj = in_projection(x, mod, p["norm1_g"], p["w_in"], seq)
    y_conv = short_conv(proj, p["conv_w_t"], p["conv_g"], batch, seq)
    q_rot, kvc, k_sw, vt_sw = nsa_rope(proj, cos_n, sin_n, batch, seq)
    n_h = seq // CMP_STRIDE
    kvc_h = kvc.reshape(batch, seq, 2 * N_NSA_KV, HEAD_DIM).transpose(0, 2, 1, 3)
    kvc_h = kvc_h.reshape(batch, 2 * N_NSA_KV, n_h, CMP_STRIDE * HEAD_DIM)
    kc, kct = compress(kvc_h, p["cmp_pe"], p["cmp_w1"], p["cmp_w2"])
    y_nsa = nsa_attention(q_rot, kc, kct, k_sw, vt_sw, proj, p["nsa_g"], batch, seq)
    y_ret = retention(proj, cos_r, sin_r, p["ret_g"], batch, seq)
    return out_projection(y_conv, y_nsa, y_ret, p["w_out"], x, mod, seq)


def moe_layer(x, mod, p, seq, final_g, final_norm):
    t, d = x.shape
    n_assign = t * TOP_K
    n_blocks = (n_assign + N_EXPERTS * (MOE_BLOCK - 1) + MOE_BLOCK - 1) // MOE_BLOCK
    h, pos_t, w_t, counts = moe_route(x, mod, p["norm2_g"], p["router_w_t"], p["router_bias"], seq)
    counts = counts[:, 0]
    padded = (counts + MOE_BLOCK - 1) // MOE_BLOCK * MOE_BLOCK
    pad_end = jnp.cumsum(padded)
    pad_start = (pad_end - padded).astype(jnp.int32)
    blk0 = jnp.arange(n_blocks, dtype=jnp.int32) * MOE_BLOCK
    block_e = jnp.minimum(jnp.sum(pad_end[None, :] <= blk0[:, None], axis=1), N_EXPERTS - 1).astype(jnp.int32)
    n_used = (pad_end[-1:] // MOE_BLOCK).astype(jnp.int32)
    first = jnp.concatenate([jnp.ones((1,), jnp.int32), (block_e[1:] != block_e[:-1]).astype(jnp.int32)])
    slot8, w8 = moe_slots(pos_t, w_t, pad_start[:, None])
    inv = moe_inverse(slot8, n_blocks * MOE_BLOCK)
    y = moe_experts(block_e, first, n_used, inv, h, p["exp_w_gu"], p["exp_w_down"], p["layer"])
    shared = shared_expert(h, p["shared_w_gu"], p["shared_w_down"])
    return moe_combine(w8.T, shared, x, mod, final_g, y, seq, final_norm)


def _reorder_w_in(w_in):
    c_gate = COL_KV + 6 * D_NSA_KV
    gates = w_in[:, c_gate:c_gate + 3 * N_NSA_HEADS]
    per_group = 3 * NSA_REP
    padded = [jnp.pad(gates[:, g * per_group:(g + 1) * per_group], ((0, 0), (0, LANE - per_group)))
              for g in range(N_NSA_KV)]
    return jnp.concatenate([w_in[:, :c_gate], w_in[:, c_gate + 3 * N_NSA_HEADS:]] + padded, axis=1)


def kernel(x, c, positions, ada_w, ada_b, norm1_g, norm2_g, w_in, conv_w, conv_g, cmp_pe, cmp_w1, cmp_w2,
           nsa_g, ret_g, w_out, router_w, router_bias, exp_w_gu, exp_w_down, shared_w_gu, shared_w_down, final_g):
    batch, seq, d = x.shape
    depth = ada_w.shape[0]
    t = batch * seq
    mod_all = ada_modulation(c, ada_w, ada_b)
    pos_col = positions.reshape(t, 1)
    tabs = rope_tables(pos_col, ROPE_DIM, ROPE_THETA) + rope_tables(pos_col, HEAD_DIM, RET_THETA)
    xt = x.reshape(t, d)
    final_g2 = final_g.reshape(1, d)
    for l in range(depth):
        mod = mod_all[l].reshape(batch, 1, ADA_CHUNKS * d)
        p = dict(
            norm1_g=norm1_g[l].reshape(1, d),
            norm2_g=norm2_g[l].reshape(1, d),
            w_in=_reorder_w_in(w_in[l]).astype(BF16),
            conv_w_t=conv_w[l].T,
            conv_g=conv_g[l].reshape(1, D_CONV),
            cmp_pe=cmp_pe[l].reshape(2, 1, CMP_LEN * HEAD_DIM),
            cmp_w1=cmp_w1[l].astype(BF16),
            cmp_w2=cmp_w2[l].astype(BF16),
            nsa_g=nsa_g[l].reshape(1, D_NSA),
            ret_g=ret_g[l].reshape(1, D_RET),
            w_out=w_out[l].astype(BF16),
            router_w_t=router_w[l].T,
            router_bias=router_bias[l].reshape(N_EXPERTS, 1),
            layer=l,
            exp_w_gu=exp_w_gu,
            exp_w_down=exp_w_down,
            shared_w_gu=shared_w_gu[l].astype(BF16),
            shared_w_down=shared_w_down[l].astype(BF16),
        )
        xt = token_mixer_layer(xt, mod, tabs, p, batch, seq)
        xt = moe_layer(xt, mod, p, seq, final_g2, final_norm=(l == depth - 1))
    return xt.reshape(batch, seq, d)
```

```python
import functools

import numpy as np
import jax
import jax.numpy as jnp
from jax import lax
from jax.experimental import pallas as pl
from jax.experimental.pallas import tpu as pltpu

F32 = jnp.float32
BF16 = jnp.bfloat16

D_MODEL = 2048
HEAD_DIM = 128
D_CONV = D_MODEL // 4
CONV_GROUPS = D_CONV // HEAD_DIM
CONV_WIDTH = 3
N_NSA_HEADS = D_MODEL // 2 // HEAD_DIM
N_NSA_KV = 2
NSA_REP = N_NSA_HEADS // N_NSA_KV
D_NSA = N_NSA_HEADS * HEAD_DIM
D_NSA_KV = N_NSA_KV * HEAD_DIM
CMP_LEN = 32
CMP_STRIDE = 16
CMP_HIDDEN = 256
SLC_LEN = 64
N_SLC = 16
WINDOW = 512
ROPE_THETA = 500000.0
ROPE_DIM = HEAD_DIM // 4
N_RET_HEADS = D_MODEL // 4 // HEAD_DIM
D_RET = N_RET_HEADS * HEAD_DIM
RET_THETA = 10000.0
D_MIX = D_CONV + D_NSA + D_RET
N_EXPERTS = 64
N_EXPERT_GROUPS = 8
GROUP_SIZE = N_EXPERTS // N_EXPERT_GROUPS
TOPK_GROUPS = 4
TOP_K = 8
D_EXPERT = 512
D_SHARED = 512
ROUTED_SCALE = 2.5
MOE_BLOCK = 256
ADA_CHUNKS = 6
EPS = 1e-6
NEG_INF = -1e30
FORCE_SCORE = 1e4

LANE = 128
ROW_CHUNKS = D_MODEL // LANE
LOG2_E = 1.4426950408889634
NSA_TILE = 256
GATE_COLS = N_NSA_KV * LANE
D_PROJ = 3 * D_CONV + D_NSA + 6 * D_NSA_KV + 4 * D_RET + GATE_COLS
COL_Q = 3 * D_CONV
COL_KV = COL_Q + D_NSA
COL_RET = COL_KV + 6 * D_NSA_KV
COL_GATE = COL_RET + 4 * D_RET

VMEM_LIMIT = 56 * 1024 * 1024


def _cparams(sem):
    return pltpu.CompilerParams(dimension_semantics=sem, vmem_limit_bytes=VMEM_LIMIT)


def _dot(a, b, **kw):
    return jnp.dot(a, b, preferred_element_type=F32, **kw)


def _dot_nt(a, b, **kw):
    return lax.dot_general(a, b, (((1,), (1,)), ((), ())), preferred_element_type=F32, **kw)


def _rms(x):
    return x * lax.rsqrt(jnp.mean(x * x, axis=-1, keepdims=True) + EPS)


def _ada_kernel(c_ref, w_ref, b_ref, o_ref):
    c = c_ref[...]
    ca = (c * jax.nn.sigmoid(c)).astype(BF16)
    o_ref[0] = _dot(ca, w_ref[0].astype(BF16)) + b_ref[0]


def ada_modulation(c, ada_w, ada_b):
    L, d, n = ada_w.shape
    b = c.shape[0]
    tn = 1024
    return pl.pallas_call(
        _ada_kernel,
        grid=(L, n // tn),
        in_specs=[
            pl.BlockSpec((b, d), lambda l, j: (0, 0)),
            pl.BlockSpec((1, d, tn), lambda l, j: (l, 0, j)),
            pl.BlockSpec((1, 1, tn), lambda l, j: (l, 0, j)),
        ],
        out_specs=pl.BlockSpec((1, b, tn), lambda l, j: (l, 0, j)),
        out_shape=jax.ShapeDtypeStruct((L, b, n), F32),
        compiler_params=_cparams(("arbitrary", "arbitrary")),
        name="ada_modulation",
    )(c, ada_w, ada_b.reshape(L, 1, n))


def _norm_mod(x, g, sc, sh):
    return (_rms(x) * g) * (1.0 + sc) + sh


def _in_proj_kernel(x_ref, sh_ref, sc_ref, g_ref, w_ref, o_ref, h_scr):
    @pl.when(pl.program_id(1) == 0)
    def _():
        h_scr[...] = _norm_mod(x_ref[...], g_ref[...], sc_ref[0], sh_ref[0]).astype(BF16)

    o_ref[...] = _dot(h_scr[...], w_ref[...])


def in_projection(x, mod, norm_g, w, seq):
    t, d = x.shape
    n = w.shape[1]
    tm, tn = 512, 1280
    per = seq // tm
    return pl.pallas_call(
        _in_proj_kernel,
        grid=(t // tm, n // tn),
        in_specs=[
            pl.BlockSpec((tm, d), lambda i, j: (i, 0)),
            pl.BlockSpec((1, 1, d), lambda i, j: (i // per, 0, 0)),
            pl.BlockSpec((1, 1, d), lambda i, j: (i // per, 0, 1)),
            pl.BlockSpec((1, d), lambda i, j: (0, 0)),
            pl.BlockSpec((d, tn), lambda i, j: (0, j)),
        ],
        out_specs=pl.BlockSpec((tm, tn), lambda i, j: (i, j)),
        out_shape=jax.ShapeDtypeStruct((t, n), F32),
        scratch_shapes=[pltpu.VMEM((tm, d), BF16)],
        compiler_params=_cparams(("arbitrary", "arbitrary")),
        name="in_projection",
    )(x, mod, mod, norm_g, w)


def _rope_table_kernel(pos_ref, inv_ref, sgn_ref, cos_ref, sin_ref):
    ang = pos_ref[...].astype(F32) * inv_ref[...]
    cos_ref[...] = jnp.cos(ang)
    sin_ref[...] = jnp.sin(ang) * sgn_ref[...]


def rope_tables(pos_col, rot_dim, theta):
    t = pos_col.shape[0]
    half = rot_dim // 2
    inv_half = theta ** (-jnp.arange(half, dtype=F32) / half)
    inv = jnp.concatenate([inv_half, inv_half, jnp.zeros((LANE - rot_dim,), F32)]).reshape(1, LANE)
    sgn = np.zeros((1, LANE), np.float32)
    sgn[0, :half] = -1.0
    sgn[0, half:rot_dim] = 1.0
    ts = 1024
    return pl.pallas_call(
        _rope_table_kernel,
        grid=(t // ts,),
        in_specs=[
            pl.BlockSpec((ts, 1), lambda i: (i, 0)),
            pl.BlockSpec((1, LANE), lambda i: (0, 0)),
            pl.BlockSpec((1, LANE), lambda i: (0, 0)),
        ],
        out_specs=[pl.BlockSpec((ts, LANE), lambda i: (i, 0))] * 2,
        out_shape=[jax.ShapeDtypeStruct((t, LANE), F32)] * 2,
        compiler_params=_cparams(("arbitrary",)),
        name="rope_tables",
    )(pos_col, inv, jnp.asarray(sgn))


def _rotate(x, cos, sin_signed, half):
    if 2 * half == LANE:
        swapped = pltpu.roll(x, half, 1)
    else:
        lane = lax.broadcasted_iota(jnp.int32, x.shape, 1)
        swapped = jnp.where(lane < half, pltpu.roll(x, LANE - half, 1), pltpu.roll(x, half, 1))
    return x * cos + swapped * sin_signed


def _nsa_rope_kernel(q0_ref, q1_ref, kvc_ref, kvs_ref, kvw_ref, cos_ref, sin_ref,
                     q_out, kvc_out, k_out, vt_out):
    cos, sin = cos_ref[...], sin_ref[...]
    half = ROPE_DIM // 2
    hpb = D_CONV // HEAD_DIM
    scale = HEAD_DIM ** -0.5 * LOG2_E
    for blk, src in enumerate((q0_ref, q1_ref)):
        for h in range(hpb):
            sl = slice(h * HEAD_DIM, (h + 1) * HEAD_DIM)
            q_out[:, blk * D_CONV + h * HEAD_DIM: blk * D_CONV + (h + 1) * HEAD_DIM] = (
                (_rotate(src[:, sl], cos, sin, half) * scale).astype(q_out.dtype))
    for g in range(N_NSA_KV):
        sl = slice(g * HEAD_DIM, (g + 1) * HEAD_DIM)
        kvc_out[:, sl] = _rotate(kvc_ref[:, sl], cos, sin, half)
    kvc_out[:, D_NSA_KV:] = kvc_ref[:, D_NSA_KV:]
    for br, src in enumerate((kvs_ref, kvw_ref)):
        for g in range(N_NSA_KV):
            sl = slice(g * HEAD_DIM, (g + 1) * HEAD_DIM)
            dst = slice((br * N_NSA_KV + g) * HEAD_DIM, (br * N_NSA_KV + g + 1) * HEAD_DIM)
            k_out[:, dst] = _rotate(src[:, sl], cos, sin, half).astype(k_out.dtype)
            vsl = slice(D_NSA_KV + g * HEAD_DIM, D_NSA_KV + (g + 1) * HEAD_DIM)
            for tile in range(vt_out.shape[2]):
                rows = slice(tile * NSA_TILE, (tile + 1) * NSA_TILE)
                vt_out[0, br * N_NSA_KV + g, tile] = src[rows, vsl].T.astype(vt_out.dtype)


def nsa_rope(proj, cos, sin, batch, seq):
    t = proj.shape[0]
    ts = 512
    per = seq // ts
    w = D_CONV
    blk = lambda j: pl.BlockSpec((ts, w), lambda i, j=j: (i, j))
    tab = pl.BlockSpec((ts, LANE), lambda i: (i, 0))
    out = pl.BlockSpec((ts, w), lambda i: (i, 0))
    qb = COL_Q // w
    kb = COL_KV // w
    return pl.pallas_call(
        _nsa_rope_kernel,
        grid=(t // ts,),
        in_specs=[blk(qb), blk(qb + 1), blk(kb), blk(kb + 1), blk(kb + 2), tab, tab],
        out_specs=[pl.BlockSpec((ts, D_NSA), lambda i: (i, 0)), out, out,
                   pl.BlockSpec((1, 2 * N_NSA_KV, ts // NSA_TILE, HEAD_DIM, NSA_TILE),
                                lambda i: (i // per, 0, i % per, 0, 0))],
        out_shape=[
            jax.ShapeDtypeStruct((t, D_NSA), BF16),
            jax.ShapeDtypeStruct((t, w), F32),
            jax.ShapeDtypeStruct((t, w), BF16),
            jax.ShapeDtypeStruct((batch, 2 * N_NSA_KV, seq // NSA_TILE, HEAD_DIM, NSA_TILE), BF16),
        ],
        compiler_params=_cparams(("arbitrary",)),
        name="nsa_rope",
    )(proj, proj, proj, proj, proj, cos, sin)


def _conv_kernel(cb_ref, cc_ref, cu_ref, w_ref, g_ref, o_ref, ext):
    ts = cb_ref.shape[0]

    @pl.when(pl.program_id(1) == 0)
    def _():
        ext[0:8, :] = jnp.zeros((8, ext.shape[1]), F32)

    v = cc_ref[...] * cu_ref[...]
    ext[8:, :] = v
    v1 = ext[pl.ds(7, ts), :]
    v2 = ext[pl.ds(6, ts), :]
    y = cb_ref[...] * (w_ref[0:1, :] * v2 + w_ref[1:2, :] * v1 + w_ref[2:3, :] * v)
    ext[0:8, :] = v[ts - 8:, :]
    for gi in range(CONV_GROUPS):
        sl = slice(gi * HEAD_DIM, (gi + 1) * HEAD_DIM)
        o_ref[:, sl] = (_rms(y[:, sl]) * g_ref[:, sl]).astype(o_ref.dtype)


def short_conv(proj, conv_w_t, conv_g, batch, seq):
    t = proj.shape[0]
    ts = 512
    per = seq // ts
    w = D_CONV
    blk = lambda j: pl.BlockSpec((ts, w), lambda b, s, j=j: (b * per + s, j))
    return pl.pallas_call(
        _conv_kernel,
        grid=(batch, per),
        in_specs=[blk(0), blk(1), blk(2),
                  pl.BlockSpec((CONV_WIDTH, w), lambda b, s: (0, 0)),
                  pl.BlockSpec((1, w), lambda b, s: (0, 0))],
        out_specs=pl.BlockSpec((ts, w), lambda b, s: (b * per + s, 0)),
        out_shape=jax.ShapeDtypeStruct((t, w), BF16),
        scratch_shapes=[pltpu.VMEM((8 + ts, w), F32)],
        compiler_params=_cparams(("arbitrary", "arbitrary")),
        name="short_conv",
    )(proj, proj, proj, conv_w_t, conv_g)


def _compress_kernel(h_ref, pe_ref, w1_ref, w2_ref, o_ref, ot_ref):
    h = h_ref[0, 0]
    n_h, half = h.shape
    pe = pe_ref[0]
    a = _dot((h + pe[:, :half]).astype(BF16), w1_ref[0, :half, :])
    b = _dot((h + pe[:, half:]).astype(BF16), w1_ref[0, half:, :])
    pre = a + pltpu.roll(b, n_h - 1, 0)
    out = _dot(jax.nn.gelu(pre).astype(BF16), w2_ref[0])
    o_ref[0, 0] = out.astype(o_ref.dtype)
    ot_ref[0, 0] = out.T.astype(ot_ref.dtype)


def compress(kvc_h, pe_flat, w1, w2):
    b, four, n_h, dh = kvc_h.shape
    return pl.pallas_call(
        _compress_kernel,
        grid=(b, four),
        in_specs=[
            pl.BlockSpec((1, 1, n_h, dh), lambda i, j: (i, j, 0, 0)),
            pl.BlockSpec((1, 1, 2 * dh), lambda i, j: (j // N_NSA_KV, 0, 0)),
            pl.BlockSpec((1, 2 * dh, CMP_HIDDEN), lambda i, j: (j // N_NSA_KV, 0, 0)),
            pl.BlockSpec((1, CMP_HIDDEN, HEAD_DIM), lambda i, j: (j // N_NSA_KV, 0, 0)),
        ],
        out_specs=[pl.BlockSpec((1, 1, n_h, HEAD_DIM), lambda i, j: (i, j, 0, 0)),
                   pl.BlockSpec((1, 1, HEAD_DIM, n_h), lambda i, j: (i, j, 0, 0))],
        out_shape=[jax.ShapeDtypeStruct((b, four, n_h, HEAD_DIM), BF16),
                   jax.ShapeDtypeStruct((b, four, HEAD_DIM, n_h), BF16)],
        compiler_params=_cparams(("arbitrary", "arbitrary")),
        name="nsa_compress",
    )(kvc_h, pe_flat, w1, w2)


def _nsa_kernel(q_ref, kc_ref, vct_ref, ks_ref, kw_ref, vst_ref, vwt_ref, gate_ref, ovt_ref, eaug_ref, g_ref,
                o_ref, kaug_scr, qaug_scr, s_scr, m_scr, l_scr, acc_scr, out_scr, *, n_cmp, n_slc, n_sel):
    tq = q_ref.shape[0]
    rows = NSA_REP * tq
    i = pl.program_id(2)
    t0 = i * tq
    hd = HEAD_DIM

    @pl.when(i == 0)
    def _():
        kaug_scr[:, :hd] = ks_ref[...]
        kaug_scr[:, hd:] = eaug_ref[...]

    for r in range(NSA_REP):
        qaug_scr[r * tq:(r + 1) * tq, :hd] = q_ref[:, r * hd:(r + 1) * hd]

    t_all =t0 + lax.rem(lax.broadcasted_iota(jnp.int32, (1, rows), 1), tq)
    gates_t = jax.nn.sigmoid(gate_ref[...]).T
    gate_rows = [jnp.concatenate([gates_t[3 * r + br:3 * r + br + 1, :] for r in range(NSA_REP)], axis=1)
                 for br in range(3)]

    key_sub = lax.broadcasted_iota(jnp.int32, (tq, 1), 0)
    causal = t0 + key_sub <= t_all

    n_back = WINDOW // tq
    pieces = []
    for back in range(n_back, -1, -1):
        jt = jnp.maximum(i - back, 0)
        sj = _dot_nt(kw_ref[pl.ds(pl.multiple_of(jt * tq, tq), tq), :], qaug_scr[:, :hd])
        if back == 0:
            sj = jnp.where(causal, sj, NEG_INF)
        elif back == n_back:
            sj = jnp.where((jt * tq + key_sub > t_all - WINDOW) & (i >= back), sj, NEG_INF)
        else:
            sj = jnp.where(i >= back, sj, NEG_INF)
        pieces.append((sj, vwt_ref[0, 0, jt]))
    m_w = pieces[0][0].max(axis=0, keepdims=True)
    for sj, _ in pieces[1:]:
        m_w = jnp.maximum(m_w, sj.max(axis=0, keepdims=True))
    l_w = None
    acc_w = None
    for sj, vt_tile in pieces:
        p = jnp.exp2(sj - m_w)
        lj = jnp.sum(p, axis=0, keepdims=True)
        aj = _dot(vt_tile, p.astype(BF16))
        l_w = lj if l_w is None else l_w + lj
        acc_w = aj if acc_w is None else acc_w + aj
    out_scr[...] = (gate_rows[2] / l_w) * acc_w

    kc = kc_ref[0, 0]
    vct = vct_ref[0, 0]
    n_h = kc.shape[0]
    c_idx = lax.broadcasted_iota(jnp.int32, (n_h, 1), 0)
    c_end = jnp.where(c_idx < n_cmp, c_idx * CMP_STRIDE + (CMP_LEN - 1), jnp.iinfo(jnp.int32).max)
    s = _dot_nt(kc, qaug_scr[:, :hd])
    sm = jnp.where(c_end <= t_all, s, NEG_INF)
    e = jnp.exp2(sm - jnp.max(sm, axis=0, keepdims=True))
    inv = jnp.where(t_all >= CMP_LEN - 1, 1.0 / jnp.sum(e, axis=0, keepdims=True), 0.0)
    p = e * inv
    out_scr[...] += gate_rows[0] * _dot(vct, p.astype(BF16))
    psum_t = p[:, 0:tq]
    for r in range(1, NSA_REP):
        psum_t = psum_t + p[:, r * tq:(r + 1) * tq]
    imp_t =_dot(ovt_ref[...], psum_t, precision=lax.Precision.HIGHEST)

    j_idx = lax.broadcasted_iota(jnp.int32, (n_slc, 1), 0)
    jq = (t0 + lax.broadcasted_iota(jnp.int32, (1, tq), 1)) // SLC_LEN
    forced = (j_idx == 0) | (j_idx == jq) | (j_idx == jq - 1)
    val = jnp.where(forced, FORCE_SCORE, jnp.where(j_idx <= jq, imp_t[:n_slc], -1.0))
    bias_t = jnp.full((n_slc, tq), NEG_INF, F32)
    for _ in range(n_sel):
        top = jnp.max(val, axis=0, keepdims=True)
        first = jnp.min(jnp.where(val == top, j_idx, n_slc), axis=0, keepdims=True)
        hit = j_idx == first
        bias_t = jnp.where(hit, 0.0, bias_t)
        val = jnp.where(hit, -jnp.inf, val)
    if n_slc < LANE:
        bias_t = jnp.concatenate([bias_t, jnp.zeros((LANE - n_slc, tq), F32)], axis=0)
    bias = bias_t.T.astype(BF16)
    for r in range(NSA_REP):
        qaug_scr[r * tq:(r + 1) * tq, hd:] = bias

    def sel_scores(jt):
        k0 = pl.multiple_of(jt * tq, tq)
        return _dot_nt(kaug_scr[pl.ds(k0, tq), :], qaug_scr[...])

    def sel_update(s, vt_tile, mask):
        if mask is not None:
            s = jnp.where(mask, s, NEG_INF)
        m_old = m_scr[...]
        m_new = jnp.maximum(m_old, jnp.max(s, axis=0, keepdims=True))
        alpha = jnp.exp2(m_old - m_new)
        p = jnp.exp2(s - m_new)
        l_scr[...] = alpha * l_scr[...] + jnp.sum(p, axis=0, keepdims=True)
        acc_scr[...] = alpha * acc_scr[...] + _dot(vt_tile, p.astype(BF16))
        m_scr[...] = m_new

    m_scr[...] = jnp.full(m_scr.shape, NEG_INF, F32)
    l_scr[...] = jnp.zeros(l_scr.shape, F32)
    acc_scr[...] = jnp.zeros(acc_scr.shape, F32)
    s_scr[...] = sel_scores(0)

    def sel_body(jt, carry):
        s_cur = s_scr[...]
        s_next = sel_scores(jt + 1)
        sel_update(s_cur, vst_ref[0, 0, jt], None)
        s_scr[...] = s_next
        return carry

    lax.fori_loop(0, i, sel_body, 0)
    sel_update(s_scr[...], vst_ref[0, 0, i], causal)
    out_scr[...] += (gate_rows[1] / l_scr[...]) * acc_scr[...]


    for r in range(NSA_REP):
        o_t = out_scr[:, r * tq:(r + 1) * tq]
        o_t = o_t * lax.rsqrt(jnp.mean(o_t * o_t, axis=0, keepdims=True) + EPS)
        sl = slice(r * hd, (r + 1) * hd)
        o_ref[:, sl] = (o_t.T * g_ref[:, sl]).astype(o_ref.dtype)


def _overlap_matrix_t(n_h, n_cmp, n_slc):
    cs = np.arange(n_cmp) * CMP_STRIDE
    js = np.arange(n_slc) * SLC_LEN
    ov = np.minimum(cs[:, None] + CMP_LEN, js[None, :] + SLC_LEN) - np.maximum(cs[:, None], js[None, :])
    out = np.zeros((LANE, n_h), np.float32)
    out[:n_slc, :n_cmp] = (np.clip(ov, 0, None) / CMP_LEN).T
    return out


def nsa_attention(q_rot, kc, kct, k_sw, vt_sw, proj, nsa_g, batch, seq):
    t = q_rot.shape[0]
    tq = NSA_TILE
    per = seq // tq
    n_h = seq // CMP_STRIDE
    n_cmp = (seq - CMP_LEN) // CMP_STRIDE + 1
    n_slc = seq // SLC_LEN
    n_sel = min(N_SLC, n_slc)
    assert n_slc <= LANE and WINDOW % tq == 0 and tq % LANE == 0
    ovt = jnp.asarray(_overlap_matrix_t(n_h, n_cmp, n_slc))
    eaug = np.zeros((seq, LANE), np.float32)
    eaug[np.arange(seq), np.arange(seq) // SLC_LEN] = 1.0
    eaug = jnp.asarray(eaug, BF16)
    hd = HEAD_DIM
    qw = NSA_REP * hd
    rows = NSA_REP * tq
    seq_spec = lambda off: pl.BlockSpec((seq, hd), lambda b, g, i, off=off: (b, off + g))
    vt_spec = lambda off: pl.BlockSpec((1, 1, per, hd, tq), lambda b, g, i, off=off: (b, off + g, 0, 0, 0))
    kern = functools.partial(_nsa_kernel, n_cmp=n_cmp, n_slc=n_slc, n_sel=n_sel)
    return pl.pallas_call(
        kern,
        grid=(batch, N_NSA_KV, per),
        in_specs=[
            pl.BlockSpec((tq, qw), lambda b, g, i: (b * per + i, g)),
            pl.BlockSpec((1, 1, n_h, hd), lambda b, g, i: (b, g, 0, 0)),
            pl.BlockSpec((1, 1, hd, n_h), lambda b, g, i: (b, N_NSA_KV + g, 0, 0)),
            seq_spec(0), seq_spec(N_NSA_KV), vt_spec(0), vt_spec(N_NSA_KV),
            pl.BlockSpec((tq, LANE), lambda b, g, i: (b * per + i, COL_GATE // LANE + g)),
            pl.BlockSpec((LANE, n_h), lambda b, g, i: (0, 0)),
            pl.BlockSpec((seq, LANE), lambda b, g, i: (0, 0)),
            pl.BlockSpec((1, qw), lambda b, g, i: (0, g)),
        ],
        out_specs=pl.BlockSpec((tq, qw), lambda b, g, i: (b * per + i, g)),
        out_shape=jax.ShapeDtypeStruct((t, D_NSA), BF16),
        scratch_shapes=[
            pltpu.VMEM((seq, 2 * hd), BF16),
            pltpu.VMEM((rows, 2 * hd), BF16),
            pltpu.VMEM((tq, rows), F32),
            pltpu.VMEM((1, rows), F32),
            pltpu.VMEM((1, rows), F32),
            pltpu.VMEM((hd, rows), F32),
            pltpu.VMEM((hd, rows), F32),
        ],
        compiler_params=_cparams(("arbitrary", "arbitrary", "arbitrary")),
        name="nsa_attention",
    )(q_rot, kc, kct, k_sw, k_sw, vt_sw, vt_sw, proj, ovt, eaug, nsa_g)


def _retention_kernel(q_ref, k_ref, v_ref, gate_ref, cos_ref, sin_ref, lg_ref, g_ref, o_ref, state):
    c = q_ref.shape[0]

    @pl.when(pl.program_id(2) == 0)
    def _():
        state[...] = jnp.zeros(state.shape, F32)

    lg = lg_ref[0][:, 0:1]
    cos, sin = cos_ref[...], sin_ref[...]
    half = HEAD_DIM // 2
    q = _rotate(q_ref[...], cos, sin, half)
    k = _rotate(k_ref[...], cos, sin, half) * HEAD_DIM ** -0.5
    v = v_ref[...].astype(BF16)
    n_row = lax.broadcasted_iota(jnp.int32, (c, 1), 0).astype(F32)
    n_col = lax.broadcasted_iota(jnp.int32, (1, c), 1).astype(F32)
    diff = n_row - n_col
    decay = jnp.where(diff >= 0.0, jnp.exp(jnp.maximum(diff, 0.0) * lg), 0.0)
    scores = _dot_nt(q.astype(BF16), k.astype(BF16)) * decay
    o = _dot(scores.astype(BF16), v)
    xi = jnp.exp((n_row + 1.0) * lg)
    o = o + _dot((q * xi).astype(BF16), state[...].astype(BF16))
    zeta = jnp.exp((c - 1.0 - n_row) * lg)
    kz_t = (k * zeta).T.astype(BF16)
    state[...] = state[...] * jnp.exp(c * lg) + _dot(kz_t, v)
    gate = gate_ref[...]
    o_ref[...] = (gate * jax.nn.sigmoid(gate) * (_rms(o) * g_ref[...])).astype(o_ref.dtype)


def retention(proj, cos, sin, ret_g, batch, seq):
    t = proj.shape[0]
    c = 256
    per = seq // c
    hd = HEAD_DIM
    base = COL_RET // hd
    blk = lambda off: pl.BlockSpec((c, hd), lambda b, h, s, off=off: (b * per + s, base + off * N_RET_HEADS + h))
    tab = pl.BlockSpec((c, hd), lambda b, h, s: (b * per + s, 0))
    lg = jnp.log1p(-(2.0 ** (-5.0 - jnp.arange(N_RET_HEADS, dtype=F32))))
    lg = jnp.broadcast_to(lg[:, None, None], (N_RET_HEADS, 1, LANE))
    return pl.pallas_call(
        _retention_kernel,
        grid=(batch, N_RET_HEADS, per),
        in_specs=[blk(0), blk(1), blk(2), blk(3), tab, tab,
                  pl.BlockSpec((1, 1, LANE), lambda b, h, s: (h, 0, 0)),
                  pl.BlockSpec((1, hd), lambda b, h, s: (0, h))],
        out_specs=pl.BlockSpec((c, hd), lambda b, h, s: (b * per + s, h)),
        out_shape=jax.ShapeDtypeStruct((t, D_RET), BF16),
        scratch_shapes=[pltpu.VMEM((hd, hd), F32)],
        compiler_params=_cparams(("arbitrary", "arbitrary", "arbitrary")),
        name="retention",
    )(proj, proj, proj, proj, cos, sin, lg, ret_g)


def _out_proj_kernel(yc_ref, yn_ref, yr_ref, w_ref, x_ref, g_ref, o_ref):
    acc = _dot(yc_ref[...], w_ref[0:D_CONV, :])
    acc = acc + _dot(yn_ref[...], w_ref[D_CONV:D_CONV + D_NSA, :])
    acc = acc + _dot(yr_ref[...], w_ref[D_CONV + D_NSA:, :])
    o_ref[...] = x_ref[...] + g_ref[0] * acc


def out_projection(y_conv, y_nsa, y_ret, w_out, x, mod, seq):
    t, d = x.shape
    tm = 512
    per = seq // tm
    row = lambda w: pl.BlockSpec((tm, w), lambda i: (i, 0))
    return pl.pallas_call(
        _out_proj_kernel,
        grid=(t // tm,),
        in_specs=[row(D_CONV), row(D_NSA), row(D_RET),
                  pl.BlockSpec((D_MIX, d), lambda i: (0, 0)),
                  row(d),
                  pl.BlockSpec((1, 1, d), lambda i: (i // per, 0, 2))],
        out_specs=row(d),
        out_shape=jax.ShapeDtypeStruct((t, d), F32),
        compiler_params=_cparams(("arbitrary",)),
        name="out_projection",
    )(y_conv, y_nsa, y_ret, w_out, x, mod)


def _rank_rows(val, n):
    idx = lax.broadcasted_iota(jnp.int32, (n, 1), 0)
    rank = jnp.zeros(val.shape, jnp.int32)
    for rp in range(n):
        vp = val[rp:rp + 1, :]
        ahead = (vp > val) | ((vp == val) & (idx > rp))
        rank = rank + ahead.astype(jnp.int32)
    return rank


def _to_token_rows(ref, val):
    for c in range(ROW_CHUNKS):
        ref[pl.ds(c, val.shape[0], stride=ROW_CHUNKS), :] = val[:, c * LANE:(c + 1) * LANE]


def _from_token_rows(ref, n):
    return jnp.concatenate([ref[pl.ds(c, n, stride=ROW_CHUNKS), :] for c in range(ROW_CHUNKS)], axis=1)


def _router_kernel(x_ref, sh_ref, sc_ref, g_ref, rw_ref, rb_ref, tri_ref, h_ref, hrow_ref, pos_ref, w_ref, cnt_ref,
                   carry):
    @pl.when(pl.program_id(0) == 0)
    def _():
        carry[...] = jnp.zeros(carry.shape, F32)

    h = _norm_mod(x_ref[...], g_ref[...], sc_ref[0], sh_ref[0])
    h_ref[...] = h
    _to_token_rows(hrow_ref, h)
    tm = h.shape[0]
    scores = jax.nn.sigmoid(_dot_nt(rw_ref[...], h, precision=lax.Precision.HIGHEST))
    biased = scores + rb_ref[...]
    sub = lax.broadcasted_iota(jnp.int32, (GROUP_SIZE, 1), 0)
    gs = []
    for g in range(N_EXPERT_GROUPS):
        bg = biased[g * GROUP_SIZE:(g + 1) * GROUP_SIZE, :]
        m1 = jnp.max(bg, axis=0, keepdims=True)
        first = jnp.min(jnp.where(bg == m1, sub, GROUP_SIZE), axis=0, keepdims=True)
        m2 = jnp.max(jnp.where(sub == first, -jnp.inf, bg), axis=0, keepdims=True)
        gs.append(m1 + m2)
    gs = jnp.concatenate(gs, axis=0)
    gkeep = _rank_rows(gs, N_EXPERT_GROUPS) < TOPK_GROUPS
    keep = jnp.concatenate(
        [jnp.broadcast_to(gkeep[g:g + 1, :], (GROUP_SIZE, tm)) for g in range(N_EXPERT_GROUPS)], axis=0)
    masked = jnp.where(keep, biased, -jnp.inf)
    sel = _rank_rows(masked, N_EXPERTS) < TOP_K
    w = jnp.where(sel, scores, 0.0)
    w_ref[...] = ROUTED_SCALE * w / jnp.sum(w, axis=0, keepdims=True)
    self32 = sel.astype(F32)
    pos = carry[:, 0:1] + _dot(self32.astype(BF16), tri_ref[...])
    pos_ref[...] = jnp.where(sel, pos.astype(jnp.int32), -1)
    carry[...] = carry[...] + jnp.sum(self32, axis=1, keepdims=True)
    cnt_ref[...] = carry[...].astype(jnp.int32)


def moe_route(x, mod, norm_g, router_w_t, router_bias, seq):
    t, d = x.shape
    tm = 512
    per = seq // tm
    tri = jnp.asarray(np.triu(np.ones((tm, tm), np.float32), 1), BF16)
    ecol = pl.BlockSpec((N_EXPERTS, tm), lambda i: (0, i))
    return pl.pallas_call(
        _router_kernel,
        grid=(t // tm,),
        in_specs=[
            pl.BlockSpec((tm, d), lambda i: (i, 0)),
            pl.BlockSpec((1, 1, d), lambda i: (i // per, 0, 3)),
            pl.BlockSpec((1, 1, d), lambda i: (i // per, 0, 4)),
            pl.BlockSpec((1, d), lambda i: (0, 0)),
            pl.BlockSpec((N_EXPERTS, d), lambda i: (0, 0)),
            pl.BlockSpec((N_EXPERTS, 1), lambda i: (0, 0)),
            pl.BlockSpec((tm, tm), lambda i: (0, 0)),
        ],
        out_specs=[pl.BlockSpec((tm, d), lambda i: (i, 0)),
                   pl.BlockSpec((tm * ROW_CHUNKS, LANE), lambda i: (i, 0)), ecol, ecol,
                   pl.BlockSpec((N_EXPERTS, LANE), lambda i: (0, 0))],
        out_shape=[
            jax.ShapeDtypeStruct((t, d), F32),
            jax.ShapeDtypeStruct((t * ROW_CHUNKS, LANE), F32),
            jax.ShapeDtypeStruct((N_EXPERTS, t), jnp.int32),
            jax.ShapeDtypeStruct((N_EXPERTS, t), F32),
            jax.ShapeDtypeStruct((N_EXPERTS, LANE), jnp.int32),
        ],
        scratch_shapes=[pltpu.VMEM((N_EXPERTS, LANE), F32)],
        compiler_params=_cparams(("arbitrary",)),
        name="moe_route",
    )(x, mod, mod, norm_g, router_w_t, router_bias, tri)


def _slots_kernel(pos_ref, w_ref, start_ref, tri_ref, slot_ref, w8_ref):
    pos = pos_ref[...]
    sel = pos >= 0
    slot = pos + start_ref[...]
    order = _dot(tri_ref[...], sel.astype(F32).astype(BF16))
    w = w_ref[...]
    for k in range(TOP_K):
        mk = sel & (order == float(k))
        slot_ref[k:k + 1, :] = jnp.sum(jnp.where(mk, slot, 0), axis=0, keepdims=True)
        w8_ref[k:k + 1, :] = jnp.sum(jnp.where(mk, w, 0.0), axis=0, keepdims=True)


def moe_slots(pos_t, w_t, pad_start):
    e, t = pos_t.shape
    tm = 2048
    tri = jnp.asarray(np.tril(np.ones((e, e), np.float32), -1), BF16)
    ecol = pl.BlockSpec((e, tm), lambda i: (0, i))
    kcol = pl.BlockSpec((TOP_K, tm), lambda i: (0, i))
    return pl.pallas_call(
        _slots_kernel,
        grid=(t // tm,),
        in_specs=[ecol, ecol, pl.BlockSpec((e, 1), lambda i: (0, 0)), pl.BlockSpec((e, e), lambda i: (0, 0))],
        out_specs=[kcol, kcol],
        out_shape=[jax.ShapeDtypeStruct((TOP_K, t), jnp.int32), jax.ShapeDtypeStruct((TOP_K, t), F32)],
        compiler_params=_cparams(("arbitrary",)),
        name="moe_slots",
    )(pos_t, w_t, pad_start, tri)


def _inverse_kernel(slot_ref, init_ref, inv_ref, sem, *, n_tok):
    tm = slot_ref.shape[0] // TOP_K
    i = pl.program_id(0)

    @pl.when(i == 0)
    def _():
        cp = pltpu.make_async_copy(init_ref, inv_ref, sem)
        cp.start()
        cp.wait()

    def body(t, carry):
        for k in range(TOP_K):
            inv_ref[slot_ref[t * TOP_K + k]] = k * n_tok + i * tm + t
        return carry

    lax.fori_loop(0, tm, body, 0, unroll=2)


def moe_inverse(slot8, n_slots):
    n_tok = slot8.shape[1]
    tm = 2048
    init = TOP_K * n_tok + jnp.arange(n_slots, dtype=jnp.int32)
    slot_flat = slot8.T.reshape(-1)
    return pl.pallas_call(
        functools.partial(_inverse_kernel, n_tok=n_tok),
        grid=(n_tok // tm,),
        in_specs=[pl.BlockSpec((TOP_K * tm,), lambda i: (i,), memory_space=pltpu.SMEM),
                  pl.BlockSpec(memory_space=pl.ANY)],
        out_specs=pl.BlockSpec(memory_space=pltpu.SMEM),
        out_shape=jax.ShapeDtypeStruct((n_slots,), jnp.int32),
        scratch_shapes=[pltpu.SemaphoreType.DMA(())],
        compiler_params=_cparams(("arbitrary",)),
        name="moe_inverse",
    )(slot_flat, init)


def _swiglu(x_bf16, w_gu, w_down, d_hidden):
    gu = _dot(x_bf16, w_gu)
    gate, up = gu[:, :d_hidden], gu[:, d_hidden:]
    return _dot((gate * jax.nn.sigmoid(gate) * up).astype(BF16), w_down)


def _experts_kernel(be_ref, first_ref, nb_ref, src_ref, dst_ref, h_hbm, wgu_ref, wd_ref, y_hbm,
                    x0, x1, y0, y1, wgu_bf, wd_bf, gsem, ssem, *, n_tok, n_blocks):
    del be_ref
    s = pl.program_id(0)
    cb = s - 1
    used = (cb >= 0) & (cb < nb_ref[0])
    rc = ROW_CHUNKS
    rows = x0.shape[0] // rc

    @pl.when(s == 0)
    def _():
        y0[...] = jnp.zeros(y0.shape, F32)
        y1[...] = jnp.zeros(y1.shape, F32)

    @pl.when(used & (first_ref[jnp.clip(cb, 0, n_blocks - 1)] == 1))
    def _():
        wgu_bf[...] = wgu_ref[0, 0].astype(BF16)
        wd_bf[...] = wd_ref[0, 0].astype(BF16)

    def step(p, x_out, x_in, y_out, y_in):
        def gathered(buf, sem):
            return pltpu.make_async_copy(h_hbm.at[pl.ds(0, rows * rc), :], buf, sem)

        def scattered(buf, sem):
            return pltpu.make_async_copy(buf, y_hbm.at[pl.ds(0, rows * rc), :], sem)

        @pl.when(s >= 1)
        def _():
            gathered(x_in, gsem.at[1 - p]).wait()

        @pl.when(s >= 1)
        def _():
            scattered(y_out, ssem.at[1 - p]).wait()

        def issue_rows():
            for r in range(rows):
                src = pl.multiple_of((src_ref[0, 0, r] & (n_tok - 1)) * rc, rc)
                dst = pl.multiple_of(dst_ref[0, 0, r] * rc, rc)
                pltpu.make_async_copy(h_hbm.at[pl.ds(src, rc), :], x_out.at[pl.ds(r * rc, rc), :], gsem.at[p]).start()
                pltpu.make_async_copy(y_in.at[pl.ds(r * rc, rc), :], y_hbm.at[pl.ds(dst, rc), :], ssem.at[p]).start()

        @pl.when(used)
        def _():
            issue_rows()
            x = _from_token_rows(x_in, rows).astype(BF16)
            _to_token_rows(y_out, _swiglu(x, wgu_bf[...], wd_bf[...], D_EXPERT))

        @pl.when(jnp.logical_not(used))
        def _():
            issue_rows()

        @pl.when(s == pl.num_programs(0) - 1)
        def _():
            gathered(x_out, gsem.at[p]).wait()
            scattered(y_in, ssem.at[p]).wait()

    @pl.when(lax.rem(s, 2) == 0)
    def _():
        step(0, x0, x1, y1, y0)

    @pl.when(lax.rem(s, 2) == 1)
    def _():
        step(1, x1, x0, y0, y1)


def moe_experts(block_e, first, n_used, inv, h_rows, w_gu, w_down, layer):
    n_tok, d = h_rows.shape[0] // ROW_CHUNKS, D_MODEL
    n_slots = inv.shape[0]
    n_blocks = n_slots // MOE_BLOCK
    assert n_tok & (n_tok - 1) == 0, "source token = inv mod T uses a power-of-two T"
    src = inv.reshape(n_blocks, 1, MOE_BLOCK)
    n_rows = TOP_K * n_tok + n_slots + 2 * MOE_BLOCK
    warmup = TOP_K * n_tok + n_slots + jnp.arange(2 * MOE_BLOCK, dtype=jnp.int32)
    dst = jnp.concatenate([warmup, inv]).reshape(n_blocks + 2, 1, MOE_BLOCK)
    clamp = lambda b: jnp.clip(b, 0, n_blocks - 1)
    smem_blk = lambda f: pl.BlockSpec((1, 1, MOE_BLOCK), f, memory_space=pltpu.SMEM)
    grid_spec = pltpu.PrefetchScalarGridSpec(
        num_scalar_prefetch=3,
        grid=(n_blocks + 2,),
        in_specs=[
            smem_blk(lambda s, be, fi, nb: (clamp(s), 0, 0)),
            smem_blk(lambda s, be, fi, nb: (s, 0, 0)),
            pl.BlockSpec(memory_space=pl.ANY),
            pl.BlockSpec((1, 1, d, 2 * D_EXPERT), lambda s, be, fi, nb: (layer, be[clamp(s - 1)], 0, 0)),
            pl.BlockSpec((1, 1, D_EXPERT, d), lambda s, be, fi, nb: (layer, be[clamp(s - 1)], 0, 0)),
        ],
        out_specs=pl.BlockSpec(memory_space=pl.ANY),
        scratch_shapes=[pltpu.VMEM((MOE_BLOCK * ROW_CHUNKS, LANE), F32)] * 4 + [
            pltpu.VMEM((d, 2 * D_EXPERT), BF16), pltpu.VMEM((D_EXPERT, d), BF16),
            pltpu.SemaphoreType.DMA((2,)), pltpu.SemaphoreType.DMA((2,))],
    )
    return pl.pallas_call(
        functools.partial(_experts_kernel, n_tok=n_tok, n_blocks=n_blocks),
        grid_spec=grid_spec,
        out_shape=jax.ShapeDtypeStruct((n_rows * ROW_CHUNKS, LANE), F32),
        compiler_params=_cparams(("arbitrary",)),
        name="moe_experts",
    )(block_e, first, n_used, src, dst, h_rows, w_gu, w_down)


def _shared_kernel(h_ref, wgu_ref, wd_ref, o_ref):
    o_ref[...] = _swiglu(h_ref[...].astype(BF16), wgu_ref[...], wd_ref[...], D_SHARED)


def shared_expert(h, w_gu, w_down):
    t, d = h.shape
    tm = 512
    return pl.pallas_call(
        _shared_kernel,
        grid=(t // tm,),
        in_specs=[pl.BlockSpec((tm, d), lambda i: (i, 0)),
                  pl.BlockSpec((d, 2 * D_SHARED), lambda i: (0, 0)),
                  pl.BlockSpec((D_SHARED, d), lambda i: (0, 0))],
        out_specs=pl.BlockSpec((tm, d), lambda i: (i, 0)),
        out_shape=jax.ShapeDtypeStruct((t, d), F32),
        compiler_params=_cparams(("arbitrary",)),
        name="shared_expert",
    )(h, w_gu, w_down)


def _combine_kernel(w_ref, sh_ref, x_ref, g_ref, fg_ref, *rest, final_norm):
    y_refs, o_ref, sum_scr = rest[:TOP_K], rest[TOP_K], rest[TOP_K + 1]
    tm = x_ref.shape[0]

    def token(t, carry):
        rows = pl.ds(pl.multiple_of(t * ROW_CHUNKS, ROW_CHUNKS), ROW_CHUNKS)
        acc = w_ref[0, t] * y_refs[0][rows, :]
        for k in range(1, TOP_K):
            acc = acc + w_ref[k, t] * y_refs[k][rows, :]
        sum_scr[rows, :] = acc
        return carry

    lax.fori_loop(0, tm, token, 0, unroll=4)
    routed = _from_token_rows(sum_scr, tm)
    out = x_ref[...] + g_ref[0] * (routed + sh_ref[...])
    if final_norm:
        out = _rms(out) * fg_ref[...]
    o_ref[...] = out


def moe_combine(w8, shared, x, mod, final_g, y, seq, final_norm):
    t, d = x.shape
    tm = 256
    per = seq // tm
    row = pl.BlockSpec((tm, d), lambda i: (i, 0))
    y_specs = [pl.BlockSpec((tm * ROW_CHUNKS, LANE), lambda i, k=k: (k * (t // tm) + i, 0)) for k in range(TOP_K)]
    return pl.pallas_call(
        functools.partial(_combine_kernel, final_norm=final_norm),
        grid=(t // tm,),
        in_specs=[
            pl.BlockSpec((TOP_K, tm), lambda i: (0, i), memory_space=pltpu.SMEM),
            row, row,
            pl.BlockSpec((1, 1, d), lambda i: (i // per, 0, 5)),
            pl.BlockSpec((1, d), lambda i: (0, 0)),
        ] + y_specs,
        out_specs=row,
        out_shape=jax.ShapeDtypeStruct((t, d), F32),
        scratch_shapes=[pltpu.VMEM((tm * ROW_CHUNKS, LANE), F32)],
        compiler_params=_cparams(("arbitrary",)),
        name="moe_combine",
    )(w8, shared, x, mod, final_g, *([y] * TOP_K))


def token_mixer_layer(x, mod, tabs, p, batch, seq):
    cos_n, sin_n, cos_r, sin_r = tabs
    proj = in_projection(x, mod, p["norm1_g"], p["w_in"], seq)
    y_conv = short_conv(proj, p["conv_w_t"], p["conv_g"], batch, seq)
    q_rot, kvc, k_sw, vt_sw = nsa_rope(proj, cos_n, sin_n, batch, seq)
    n_h = seq // CMP_STRIDE
    kvc_h = kvc.reshape(batch, seq, 2 * N_NSA_KV, HEAD_DIM).transpose(0, 2, 1, 3)
    kvc_h = kvc_h.reshape(batch, 2 * N_NSA_KV, n_h, CMP_STRIDE * HEAD_DIM)
    kc, kct = compress(kvc_h, p["cmp_pe"], p["cmp_w1"], p["cmp_w2"])
    y_nsa = nsa_attention(q_rot, kc, kct, k_sw, vt_sw, proj, p["nsa_g"], batch, seq)
    y_ret = retention(proj, cos_r, sin_r, p["ret_g"], batch, seq)
    return out_projection(y_conv, y_nsa, y_ret, p["w_out"], x, mod, seq)


def moe_layer(x, mod, p, seq, final_g, final_norm):
    t, d = x.shape
    n_assign = t * TOP_K
    n_blocks = (n_assign + N_EXPERTS * (MOE_BLOCK - 1) + MOE_BLOCK - 1) // MOE_BLOCK
    h, h_rows, pos_t, w_t, counts = moe_route(x, mod, p["norm2_g"], p["router_w_t"], p["router_bias"], seq)
    counts = counts[:, 0]
    padded = (counts + MOE_BLOCK - 1) // MOE_BLOCK * MOE_BLOCK
    pad_end = jnp.cumsum(padded)
    pad_start = (pad_end - padded).astype(jnp.int32)
    blk0 = jnp.arange(n_blocks, dtype=jnp.int32) * MOE_BLOCK
    block_e = jnp.minimum(jnp.sum(pad_end[None, :] <= blk0[:, None], axis=1), N_EXPERTS - 1).astype(jnp.int32)
    n_used = (pad_end[-1:] // MOE_BLOCK).astype(jnp.int32)
    first = jnp.concatenate([jnp.ones((1,), jnp.int32), (block_e[1:] != block_e[:-1]).astype(jnp.int32)])
    slot8, w8 = moe_slots(pos_t, w_t, pad_start[:, None])
    inv = moe_inverse(slot8, n_blocks * MOE_BLOCK)
    y = moe_experts(block_e, first, n_used, inv, h_rows, p["exp_w_gu"], p["exp_w_down"], p["layer"])
    shared = shared_expert(h, p["shared_w_gu"], p["shared_w_down"])
    return moe_combine(w8, shared, x, mod, final_g, y, seq, final_norm)


def _reorder_w_in(w_in):
    c_gate = COL_KV + 6 * D_NSA_KV
    gates = w_in[:, c_gate:c_gate + 3 * N_NSA_HEADS]
    per_group = 3 * NSA_REP
    padded = [jnp.pad(gates[:, g * per_group:(g + 1) * per_group], ((0, 0), (0, LANE - per_group)))
              for g in range(N_NSA_KV)]
    return jnp.concatenate([w_in[:, :c_gate], w_in[:, c_gate + 3 * N_NSA_HEADS:]] + padded, axis=1)


def kernel(x, c, positions, ada_w, ada_b, norm1_g, norm2_g, w_in, conv_w, conv_g, cmp_pe, cmp_w1, cmp_w2,
           nsa_g, ret_g, w_out, router_w, router_bias, exp_w_gu, exp_w_down, shared_w_gu, shared_w_down, final_g):
    batch, seq, d = x.shape
    depth = ada_w.shape[0]
    t = batch * seq
    mod_all = ada_modulation(c, ada_w, ada_b)
    pos_col = positions.reshape(t, 1)
    tabs = rope_tables(pos_col, ROPE_DIM, ROPE_THETA) + rope_tables(pos_col, HEAD_DIM, RET_THETA)
    xt = x.reshape(t, d)
    final_g2 = final_g.reshape(1, d)
    for l in range(depth):
        mod = mod_all[l].reshape(batch, 1, ADA_CHUNKS * d)
        p = dict(
            norm1_g=norm1_g[l].reshape(1, d),
            norm2_g=norm2_g[l].reshape(1, d),
            w_in=_reorder_w_in(w_in[l]).astype(BF16),
            conv_w_t=conv_w[l].T,
            conv_g=conv_g[l].reshape(1, D_CONV),
            cmp_pe=cmp_pe[l].reshape(2, 1, CMP_LEN * HEAD_DIM),
            cmp_w1=cmp_w1[l].astype(BF16),
            cmp_w2=cmp_w2[l].astype(BF16),
            nsa_g=nsa_g[l].reshape(1, D_NSA),
            ret_g=ret_g[l].reshape(1, D_RET),
            w_out=w_out[l].astype(BF16),
            router_w_t=router_w[l].T,
            router_bias=router_bias[l].reshape(N_EXPERTS, 1),
            layer=l,
            exp_w_gu=exp_w_gu,
            exp_w_down=exp_w_down,
            shared_w_gu=shared_w_gu[l].astype(BF16),
            shared_w_down=shared_w_down[l].astype(BF16),
        )
        xt = token_mixer_layer(xt, mod, tabs, p, batch, seq)
        xt = moe_layer(xt, mod, p, seq, final_g2, final_norm=(l == depth - 1))
    return xt.reshape(batch, seq, d)
```

```python
import functools

import numpy as np
import jax
import jax.numpy as jnp
from jax import lax
from jax.experimental import pallas as pl
from jax.experimental.pallas import tpu as pltpu

F32 = jnp.float32
BF16 = jnp.bfloat16

D_MODEL = 2048
HEAD_DIM = 128
D_CONV = D_MODEL // 4
CONV_GROUPS = D_CONV // HEAD_DIM
CONV_WIDTH = 3
N_NSA_HEADS = D_MODEL // 2 // HEAD_DIM
N_NSA_KV = 2
NSA_REP = N_NSA_HEADS // N_NSA_KV
D_NSA = N_NSA_HEADS * HEAD_DIM
D_NSA_KV = N_NSA_KV * HEAD_DIM
CMP_LEN = 32
CMP_STRIDE = 16
CMP_HIDDEN = 256
SLC_LEN = 64
N_SLC = 16
WINDOW = 512
ROPE_THETA = 500000.0
ROPE_DIM = HEAD_DIM // 4
N_RET_HEADS = D_MODEL // 4 // HEAD_DIM
D_RET = N_RET_HEADS * HEAD_DIM
RET_THETA = 10000.0
D_MIX = D_CONV + D_NSA + D_RET
N_EXPERTS = 64
N_EXPERT_GROUPS = 8
GROUP_SIZE = N_EXPERTS // N_EXPERT_GROUPS
TOPK_GROUPS = 4
TOP_K = 8
D_EXPERT = 512
D_SHARED = 512
ROUTED_SCALE = 2.5
MOE_BLOCK = 256
ADA_CHUNKS = 6
EPS = 1e-6
NEG_INF = -1e30
FORCE_SCORE = 1e4

LANE = 128
ROW_CHUNKS = D_MODEL // LANE
SMEM_CHUNK = 1024
LOG2_E = 1.4426950408889634
NSA_TILE = 256
GATE_COLS = N_NSA_KV * LANE
D_PROJ = 3 * D_CONV + D_NSA + 6 * D_NSA_KV + 4 * D_RET + GATE_COLS
COL_Q = 3 * D_CONV
COL_KV = COL_Q + D_NSA
COL_RET = COL_KV + 6 * D_NSA_KV
COL_GATE = COL_RET + 4 * D_RET

VMEM_LIMIT = 56 * 1024 * 1024


def _cparams(sem):
    return pltpu.CompilerParams(dimension_semantics=sem, vmem_limit_bytes=VMEM_LIMIT)


def _dot(a, b, **kw):
    return jnp.dot(a, b, preferred_element_type=F32, **kw)


def _dot_nt(a, b, **kw):
    return lax.dot_general(a, b, (((1,), (1,)), ((), ())), preferred_element_type=F32, **kw)


def _rms(x):
    return x * lax.rsqrt(jnp.mean(x * x, axis=-1, keepdims=True) + EPS)


def _ada_kernel(c_ref, w_ref, b_ref, o_ref):
    c = c_ref[...]
    ca = (c * jax.nn.sigmoid(c)).astype(BF16)
    o_ref[0] = _dot(ca, w_ref[0].astype(BF16)) + b_ref[0]


def ada_modulation(c, ada_w, ada_b):
    L, d, n = ada_w.shape
    b = c.shape[0]
    tn = 1024
    return pl.pallas_call(
        _ada_kernel,
        grid=(L, n // tn),
        in_specs=[
            pl.BlockSpec((b, d), lambda l, j: (0, 0)),
            pl.BlockSpec((1, d, tn), lambda l, j: (l, 0, j)),
            pl.BlockSpec((1, 1, tn), lambda l, j: (l, 0, j)),
        ],
        out_specs=pl.BlockSpec((1, b, tn), lambda l, j: (l, 0, j)),
        out_shape=jax.ShapeDtypeStruct((L, b, n), F32),
        compiler_params=_cparams(("arbitrary", "arbitrary")),
        name="ada_modulation",
    )(c, ada_w, ada_b.reshape(L, 1, n))


def _norm_mod(x, g, sc, sh):
    return (_rms(x) * g) * (1.0 + sc) + sh


def _in_proj_kernel(x_ref, sh_ref, sc_ref, g_ref, w_ref, o_ref, h_scr):
    @pl.when(pl.program_id(1) == 0)
    def _():
        h_scr[...] = _norm_mod(x_ref[...], g_ref[...], sc_ref[0], sh_ref[0]).astype(BF16)

    o_ref[...] = _dot(h_scr[...], w_ref[...])


def in_projection(x, mod, norm_g, w, seq):
    t, d = x.shape
    n = w.shape[1]
    tm, tn = 512, 1280
    per = seq // tm
    return pl.pallas_call(
        _in_proj_kernel,
        grid=(t // tm, n // tn),
        in_specs=[
            pl.BlockSpec((tm, d), lambda i, j: (i, 0)),
            pl.BlockSpec((1, 1, d), lambda i, j: (i // per, 0, 0)),
            pl.BlockSpec((1, 1, d), lambda i, j: (i // per, 0, 1)),
            pl.BlockSpec((1, d), lambda i, j: (0, 0)),
            pl.BlockSpec((d, tn), lambda i, j: (0, j)),
        ],
        out_specs=pl.BlockSpec((tm, tn), lambda i, j: (i, j)),
        out_shape=jax.ShapeDtypeStruct((t, n), F32),
        scratch_shapes=[pltpu.VMEM((tm, d), BF16)],
        compiler_params=_cparams(("arbitrary", "arbitrary")),
        name="in_projection",
    )(x, mod, mod, norm_g, w)


def _rope_table_kernel(pos_ref, inv_ref, sgn_ref, cos_ref, sin_ref):
    ang = pos_ref[...].astype(F32) * inv_ref[...]
    cos_ref[...] = jnp.cos(ang)
    sin_ref[...] = jnp.sin(ang) * sgn_ref[...]


def rope_tables(pos_col, rot_dim, theta):
    t = pos_col.shape[0]
    half = rot_dim // 2
    inv_half = theta ** (-jnp.arange(half, dtype=F32) / half)
    inv = jnp.concatenate([inv_half, inv_half, jnp.zeros((LANE - rot_dim,), F32)]).reshape(1, LANE)
    sgn = np.zeros((1, LANE), np.float32)
    sgn[0, :half] = -1.0
    sgn[0, half:rot_dim] = 1.0
    ts = 1024
    return pl.pallas_call(
        _rope_table_kernel,
        grid=(t // ts,),
        in_specs=[
            pl.BlockSpec((ts, 1), lambda i: (i, 0)),
            pl.BlockSpec((1, LANE), lambda i: (0, 0)),
            pl.BlockSpec((1, LANE), lambda i: (0, 0)),
        ],
        out_specs=[pl.BlockSpec((ts, LANE), lambda i: (i, 0))] * 2,
        out_shape=[jax.ShapeDtypeStruct((t, LANE), F32)] * 2,
        compiler_params=_cparams(("arbitrary",)),
        name="rope_tables",
    )(pos_col, inv, jnp.asarray(sgn))


def _rotate(x, cos, sin_signed, half):
    if 2 * half == LANE:
        swapped = pltpu.roll(x, half, 1)
    else:
        lane = lax.broadcasted_iota(jnp.int32, x.shape, 1)
        swapped = jnp.where(lane < half, pltpu.roll(x, LANE - half, 1), pltpu.roll(x, half, 1))
    return x * cos + swapped * sin_signed


def _nsa_rope_kernel(q0_ref, q1_ref, kvc_ref, kvs_ref, kvw_ref, cos_ref, sin_ref,
                     q_out, kvc_out, k_out, vt_out):
    cos, sin = cos_ref[...], sin_ref[...]
    half = ROPE_DIM // 2
    hpb = D_CONV // HEAD_DIM
    scale = HEAD_DIM ** -0.5 * LOG2_E
    for blk, src in enumerate((q0_ref, q1_ref)):
        for h in range(hpb):
            sl = slice(h * HEAD_DIM, (h + 1) * HEAD_DIM)
            q_out[:, blk * D_CONV + h * HEAD_DIM: blk * D_CONV + (h + 1) * HEAD_DIM] = (
                (_rotate(src[:, sl], cos, sin, half) * scale).astype(q_out.dtype))
    for g in range(N_NSA_KV):
        sl = slice(g * HEAD_DIM, (g + 1) * HEAD_DIM)
        kvc_out[:, sl] = _rotate(kvc_ref[:, sl], cos, sin, half)
    kvc_out[:, D_NSA_KV:] = kvc_ref[:, D_NSA_KV:]
    for br, src in enumerate((kvs_ref, kvw_ref)):
        for g in range(N_NSA_KV):
            sl = slice(g * HEAD_DIM, (g + 1) * HEAD_DIM)
            dst = slice((br * N_NSA_KV + g) * HEAD_DIM, (br * N_NSA_KV + g + 1) * HEAD_DIM)
            k_out[:, dst] = _rotate(src[:, sl], cos, sin, half).astype(k_out.dtype)
            vsl = slice(D_NSA_KV + g * HEAD_DIM, D_NSA_KV + (g + 1) * HEAD_DIM)
            for tile in range(vt_out.shape[2]):
                rows = slice(tile * NSA_TILE, (tile + 1) * NSA_TILE)
                vt_out[0, br * N_NSA_KV + g, tile] = src[rows, vsl].T.astype(vt_out.dtype)


def nsa_rope(proj, cos, sin, batch, seq):
    t = proj.shape[0]
    ts = 512
    per = seq // ts
    w = D_CONV
    blk = lambda j: pl.BlockSpec((ts, w), lambda i, j=j: (i, j))
    tab = pl.BlockSpec((ts, LANE), lambda i: (i, 0))
    out = pl.BlockSpec((ts, w), lambda i: (i, 0))
    qb = COL_Q // w
    kb = COL_KV // w
    return pl.pallas_call(
        _nsa_rope_kernel,
        grid=(t // ts,),
        in_specs=[blk(qb), blk(qb + 1), blk(kb), blk(kb + 1), blk(kb + 2), tab, tab],
        out_specs=[pl.BlockSpec((ts, D_NSA), lambda i: (i, 0)), out, out,
                   pl.BlockSpec((1, 2 * N_NSA_KV, ts // NSA_TILE, HEAD_DIM, NSA_TILE),
                                lambda i: (i // per, 0, i % per, 0, 0))],
        out_shape=[
            jax.ShapeDtypeStruct((t, D_NSA), BF16),
            jax.ShapeDtypeStruct((t, w), F32),
            jax.ShapeDtypeStruct((t, w), BF16),
            jax.ShapeDtypeStruct((batch, 2 * N_NSA_KV, seq // NSA_TILE, HEAD_DIM, NSA_TILE), BF16),
        ],
        compiler_params=_cparams(("arbitrary",)),
        name="nsa_rope",
    )(proj, proj, proj, proj, proj, cos, sin)


def _conv_kernel(cb_ref, cc_ref, cu_ref, w_ref, g_ref, o_ref, ext):
    ts = cb_ref.shape[0]

    @pl.when(pl.program_id(1) == 0)
    def _():
        ext[0:8, :] = jnp.zeros((8, ext.shape[1]), F32)

    v = cc_ref[...] * cu_ref[...]
    ext[8:, :] = v
    v1 = ext[pl.ds(7, ts), :]
    v2 = ext[pl.ds(6, ts), :]
    y = cb_ref[...] * (w_ref[0:1, :] * v2 + w_ref[1:2, :] * v1 + w_ref[2:3, :] * v)
    ext[0:8, :] = v[ts - 8:, :]
    for gi in range(CONV_GROUPS):
        sl = slice(gi * HEAD_DIM, (gi + 1) * HEAD_DIM)
        o_ref[:, sl] = (_rms(y[:, sl]) * g_ref[:, sl]).astype(o_ref.dtype)


def short_conv(proj, conv_w_t, conv_g, batch, seq):
    t = proj.shape[0]
    ts = 512
    per = seq // ts
    w = D_CONV
    blk = lambda j: pl.BlockSpec((ts, w), lambda b, s, j=j: (b * per + s, j))
    return pl.pallas_call(
        _conv_kernel,
        grid=(batch, per),
        in_specs=[blk(0), blk(1), blk(2),
                  pl.BlockSpec((CONV_WIDTH, w), lambda b, s: (0, 0)),
                  pl.BlockSpec((1, w), lambda b, s: (0, 0))],
        out_specs=pl.BlockSpec((ts, w), lambda b, s: (b * per + s, 0)),
        out_shape=jax.ShapeDtypeStruct((t, w), BF16),
        scratch_shapes=[pltpu.VMEM((8 + ts, w), F32)],
        compiler_params=_cparams(("arbitrary", "arbitrary")),
        name="short_conv",
    )(proj, proj, proj, conv_w_t, conv_g)


def _compress_kernel(h_ref, pe_ref, w1_ref, w2_ref, o_ref, ot_ref):
    h = h_ref[0, 0]
    n_h, half = h.shape
    pe = pe_ref[0]
    a = _dot((h + pe[:, :half]).astype(BF16), w1_ref[0, :half, :])
    b = _dot((h + pe[:, half:]).astype(BF16), w1_ref[0, half:, :])
    pre = a + pltpu.roll(b, n_h - 1, 0)
    out = _dot(jax.nn.gelu(pre).astype(BF16), w2_ref[0])
    o_ref[0, 0] = out.astype(o_ref.dtype)
    ot_ref[0, 0] = out.T.astype(ot_ref.dtype)


def compress(kvc_h, pe_flat, w1, w2):
    b, four, n_h, dh = kvc_h.shape
    return pl.pallas_call(
        _compress_kernel,
        grid=(b, four),
        in_specs=[
            pl.BlockSpec((1, 1, n_h, dh), lambda i, j: (i, j, 0, 0)),
            pl.BlockSpec((1, 1, 2 * dh), lambda i, j: (j // N_NSA_KV, 0, 0)),
            pl.BlockSpec((1, 2 * dh, CMP_HIDDEN), lambda i, j: (j // N_NSA_KV, 0, 0)),
            pl.BlockSpec((1, CMP_HIDDEN, HEAD_DIM), lambda i, j: (j // N_NSA_KV, 0, 0)),
        ],
        out_specs=[pl.BlockSpec((1, 1, n_h, HEAD_DIM), lambda i, j: (i, j, 0, 0)),
                   pl.BlockSpec((1, 1, HEAD_DIM, n_h), lambda i, j: (i, j, 0, 0))],
        out_shape=[jax.ShapeDtypeStruct((b, four, n_h, HEAD_DIM), BF16),
                   jax.ShapeDtypeStruct((b, four, HEAD_DIM, n_h), BF16)],
        compiler_params=_cparams(("arbitrary", "arbitrary")),
        name="nsa_compress",
    )(kvc_h, pe_flat, w1, w2)


def _nsa_kernel(q_ref, kc_ref, vct_ref, ks_ref, kw_ref, vst_ref, vwt_ref, gate_ref, ovt_ref, eaug_ref, g_ref,
                o_ref, kaug_scr, qaug_scr, s_scr, m_scr, l_scr, acc_scr, out_scr, *, n_cmp, n_slc, n_sel):
    tq = q_ref.shape[0]
    rows = NSA_REP * tq
    i = pl.program_id(2)
    t0 = i * tq
    hd = HEAD_DIM

    @pl.when(i == 0)
    def _():
        kaug_scr[:, :hd] = ks_ref[...]
        kaug_scr[:, hd:] = eaug_ref[...]

    for r in range(NSA_REP):
        qaug_scr[r * tq:(r + 1) * tq, :hd] = q_ref[:, r * hd:(r + 1) * hd]

    t_all =t0 + lax.rem(lax.broadcasted_iota(jnp.int32, (1, rows), 1), tq)
    gates_t = jax.nn.sigmoid(gate_ref[...]).T
    gate_rows = [jnp.concatenate([gates_t[3 * r + br:3 * r + br + 1, :] for r in range(NSA_REP)], axis=1)
                 for br in range(3)]

    key_sub = lax.broadcasted_iota(jnp.int32, (tq, 1), 0)
    causal = t0 + key_sub <= t_all

    n_back = WINDOW // tq
    pieces = []
    for back in range(n_back, -1, -1):
        jt = jnp.maximum(i - back, 0)
        sj = _dot_nt(kw_ref[pl.ds(pl.multiple_of(jt * tq, tq), tq), :], qaug_scr[:, :hd])
        if back == 0:
            sj = jnp.where(causal, sj, NEG_INF)
        elif back == n_back:
            sj = jnp.where((jt * tq + key_sub > t_all - WINDOW) & (i >= back), sj, NEG_INF)
        else:
            sj = jnp.where(i >= back, sj, NEG_INF)
        pieces.append((sj, vwt_ref[0, 0, jt]))
    m_w = pieces[0][0].max(axis=0, keepdims=True)
    for sj, _ in pieces[1:]:
        m_w = jnp.maximum(m_w, sj.max(axis=0, keepdims=True))
    l_w = None
    acc_w = None
    for sj, vt_tile in pieces:
        p = jnp.exp2(sj - m_w)
        lj = jnp.sum(p, axis=0, keepdims=True)
        aj = _dot(vt_tile, p.astype(BF16))
        l_w = lj if l_w is None else l_w + lj
        acc_w = aj if acc_w is None else acc_w + aj
    out_scr[...] = (gate_rows[2] / l_w) * acc_w

    kc = kc_ref[0, 0]
    vct = vct_ref[0, 0]
    n_h = kc.shape[0]
    c_idx = lax.broadcasted_iota(jnp.int32, (n_h, 1), 0)
    c_end = jnp.where(c_idx < n_cmp, c_idx * CMP_STRIDE + (CMP_LEN - 1), jnp.iinfo(jnp.int32).max)
    s = _dot_nt(kc, qaug_scr[:, :hd])
    sm = jnp.where(c_end <= t_all, s, NEG_INF)
    e = jnp.exp2(sm - jnp.max(sm, axis=0, keepdims=True))
    inv = jnp.where(t_all >= CMP_LEN - 1, 1.0 / jnp.sum(e, axis=0, keepdims=True), 0.0)
    p = e * inv
    out_scr[...] += gate_rows[0] * _dot(vct, p.astype(BF16))
    psum_t = p[:, 0:tq]
    for r in range(1, NSA_REP):
        psum_t = psum_t + p[:, r * tq:(r + 1) * tq]
    imp_t =_dot(ovt_ref[...], psum_t, precision=lax.Precision.HIGHEST)

    j_idx = lax.broadcasted_iota(jnp.int32, (n_slc, 1), 0)
    jq = (t0 + lax.broadcasted_iota(jnp.int32, (1, tq), 1)) // SLC_LEN
    forced = (j_idx == 0) | (j_idx == jq) | (j_idx == jq - 1)
    val = jnp.where(forced, FORCE_SCORE, jnp.where(j_idx <= jq, imp_t[:n_slc], -1.0))
    bias_t = jnp.full((n_slc, tq), NEG_INF, F32)
    for _ in range(n_sel):
        top = jnp.max(val, axis=0, keepdims=True)
        first = jnp.min(jnp.where(val == top, j_idx, n_slc), axis=0, keepdims=True)
        hit = j_idx == first
        bias_t = jnp.where(hit, 0.0, bias_t)
        val = jnp.where(hit, -jnp.inf, val)
    if n_slc < LANE:
        bias_t = jnp.concatenate([bias_t, jnp.zeros((LANE - n_slc, tq), F32)], axis=0)
    bias = bias_t.T.astype(BF16)
    for r in range(NSA_REP):
        qaug_scr[r * tq:(r + 1) * tq, hd:] = bias

    def sel_scores(jt):
        k0 = pl.multiple_of(jt * tq, tq)
        return _dot_nt(kaug_scr[pl.ds(k0, tq), :], qaug_scr[...])

    def sel_update(s, vt_tile, mask):
        if mask is not None:
            s = jnp.where(mask, s, NEG_INF)
        m_old = m_scr[...]
        m_new = jnp.maximum(m_old, jnp.max(s, axis=0, keepdims=True))
        alpha = jnp.exp2(m_old - m_new)
        p = jnp.exp2(s - m_new)
        l_scr[...] = alpha * l_scr[...] + jnp.sum(p, axis=0, keepdims=True)
        acc_scr[...] = alpha * acc_scr[...] + _dot(vt_tile, p.astype(BF16))
        m_scr[...] = m_new

    m_scr[...] = jnp.full(m_scr.shape, NEG_INF, F32)
    l_scr[...] = jnp.zeros(l_scr.shape, F32)
    acc_scr[...] = jnp.zeros(acc_scr.shape, F32)
    s_scr[...] = sel_scores(0)

    def sel_body(jt, carry):
        s_cur = s_scr[...]
        s_next = sel_scores(jt + 1)
        sel_update(s_cur, vst_ref[0, 0, jt], None)
        s_scr[...] = s_next
        return carry

    lax.fori_loop(0, i, sel_body, 0)
    sel_update(s_scr[...], vst_ref[0, 0, i], causal)
    out_scr[...] += (gate_rows[1] / l_scr[...]) * acc_scr[...]


    for r in range(NSA_REP):
        o_t = out_scr[:, r * tq:(r + 1) * tq]
        o_t = o_t * lax.rsqrt(jnp.mean(o_t * o_t, axis=0, keepdims=True) + EPS)
        sl = slice(r * hd, (r + 1) * hd)
        o_ref[:, sl] = (o_t.T * g_ref[:, sl]).astype(o_ref.dtype)


def _overlap_matrix_t(n_h, n_cmp, n_slc):
    cs = np.arange(n_cmp) * CMP_STRIDE
    js = np.arange(n_slc) * SLC_LEN
    ov = np.minimum(cs[:, None] + CMP_LEN, js[None, :] + SLC_LEN) - np.maximum(cs[:, None], js[None, :])
    out = np.zeros((LANE, n_h), np.float32)
    out[:n_slc, :n_cmp] = (np.clip(ov, 0, None) / CMP_LEN).T
    return out


def nsa_attention(q_rot, kc, kct, k_sw, vt_sw, proj, nsa_g, batch, seq):
    t = q_rot.shape[0]
    tq = NSA_TILE
    per = seq // tq
    n_h = seq // CMP_STRIDE
    n_cmp = (seq - CMP_LEN) // CMP_STRIDE + 1
    n_slc = seq // SLC_LEN
    n_sel = min(N_SLC, n_slc)
    assert n_slc <= LANE and WINDOW % tq == 0 and tq % LANE == 0
    ovt = jnp.asarray(_overlap_matrix_t(n_h, n_cmp, n_slc))
    eaug = np.zeros((seq, LANE), np.float32)
    eaug[np.arange(seq), np.arange(seq) // SLC_LEN] = 1.0
    eaug = jnp.asarray(eaug, BF16)
    hd = HEAD_DIM
    qw = NSA_REP * hd
    rows = NSA_REP * tq
    seq_spec = lambda off: pl.BlockSpec((seq, hd), lambda b, g, i, off=off: (b, off + g))
    vt_spec = lambda off: pl.BlockSpec((1, 1, per, hd, tq), lambda b, g, i, off=off: (b, off + g, 0, 0, 0))
    kern = functools.partial(_nsa_kernel, n_cmp=n_cmp, n_slc=n_slc, n_sel=n_sel)
    return pl.pallas_call(
        kern,
        grid=(batch, N_NSA_KV, per),
        in_specs=[
            pl.BlockSpec((tq, qw), lambda b, g, i: (b * per + i, g)),
            pl.BlockSpec((1, 1, n_h, hd), lambda b, g, i: (b, g, 0, 0)),
            pl.BlockSpec((1, 1, hd, n_h), lambda b, g, i: (b, N_NSA_KV + g, 0, 0)),
            seq_spec(0), seq_spec(N_NSA_KV), vt_spec(0), vt_spec(N_NSA_KV),
            pl.BlockSpec((tq, LANE), lambda b, g, i: (b * per + i, COL_GATE // LANE + g)),
            pl.BlockSpec((LANE, n_h), lambda b, g, i: (0, 0)),
            pl.BlockSpec((seq, LANE), lambda b, g, i: (0, 0)),
            pl.BlockSpec((1, qw), lambda b, g, i: (0, g)),
        ],
        out_specs=pl.BlockSpec((tq, qw), lambda b, g, i: (b * per + i, g)),
        out_shape=jax.ShapeDtypeStruct((t, D_NSA), BF16),
        scratch_shapes=[
            pltpu.VMEM((seq, 2 * hd), BF16),
            pltpu.VMEM((rows, 2 * hd), BF16),
            pltpu.VMEM((tq, rows), F32),
            pltpu.VMEM((1, rows), F32),
            pltpu.VMEM((1, rows), F32),
            pltpu.VMEM((hd, rows), F32),
            pltpu.VMEM((hd, rows), F32),
        ],
        compiler_params=_cparams(("arbitrary", "arbitrary", "arbitrary")),
        name="nsa_attention",
    )(q_rot, kc, kct, k_sw, k_sw, vt_sw, vt_sw, proj, ovt, eaug, nsa_g)


def _retention_kernel(q_ref, k_ref, v_ref, gate_ref, cos_ref, sin_ref, lg_ref, g_ref, o_ref, state):
    c = q_ref.shape[0]
    hd = HEAD_DIM

    @pl.when(pl.program_id(1) == 0)
    def _():
        state[...] = jnp.zeros(state.shape, F32)

    cos, sin = cos_ref[...], sin_ref[...]
    n_row = lax.broadcasted_iota(jnp.int32, (c, 1), 0).astype(F32)
    n_col = lax.broadcasted_iota(jnp.int32, (1, c), 1).astype(F32)
    diff = n_row - n_col
    for h in range(N_RET_HEADS):
        sl = slice(h * hd, (h + 1) * hd)
        lg = lg_ref[h][:, 0:1]
        q = _rotate(q_ref[:, sl], cos, sin, hd // 2)
        k = _rotate(k_ref[:, sl], cos, sin, hd // 2) * hd ** -0.5
        v = v_ref[:, sl].astype(BF16)
        decay = jnp.where(diff >= 0.0, jnp.exp(jnp.maximum(diff, 0.0) * lg), 0.0)
        scores = _dot_nt(q.astype(BF16), k.astype(BF16)) * decay
        o = _dot(scores.astype(BF16), v)
        xi = jnp.exp((n_row + 1.0) * lg)
        o = o + _dot((q * xi).astype(BF16), state[h].astype(BF16))
        zeta = jnp.exp((c - 1.0 - n_row) * lg)
        kz_t = (k * zeta).T.astype(BF16)
        state[h] = state[h] * jnp.exp(c * lg) + _dot(kz_t, v)
        gate = gate_ref[:, sl]
        o_ref[:, sl] = (gate * jax.nn.sigmoid(gate) * (_rms(o) * g_ref[:, sl])).astype(o_ref.dtype)


def retention(proj, cos, sin, ret_g, batch, seq):
    t = proj.shape[0]
    c = 256
    per = seq // c
    hd = HEAD_DIM
    base = COL_RET // D_RET
    blk = lambda off: pl.BlockSpec((c, D_RET), lambda b, s, off=off: (b * per + s, base + off))
    tab = pl.BlockSpec((c, hd), lambda b, s: (b * per + s, 0))
    lg = jnp.log1p(-(2.0 ** (-5.0 - jnp.arange(N_RET_HEADS, dtype=F32))))
    lg = jnp.broadcast_to(lg[:, None, None], (N_RET_HEADS, 1, LANE))
    return pl.pallas_call(
        _retention_kernel,
        grid=(batch, per),
        in_specs=[blk(0), blk(1), blk(2), blk(3), tab, tab,
                  pl.BlockSpec((N_RET_HEADS, 1, LANE), lambda b, s: (0, 0, 0)),
                  pl.BlockSpec((1, D_RET), lambda b, s: (0, 0))],
        out_specs=pl.BlockSpec((c, D_RET), lambda b, s: (b * per + s, 0)),
        out_shape=jax.ShapeDtypeStruct((t, D_RET), BF16),
        scratch_shapes=[pltpu.VMEM((N_RET_HEADS, hd, hd), F32)],
        compiler_params=_cparams(("arbitrary", "arbitrary")),
        name="retention",
    )(proj, proj, proj, proj, cos, sin, lg, ret_g)


def _out_proj_kernel(yc_ref, yn_ref, yr_ref, w_ref, x_ref, g_ref, o_ref):
    acc = _dot(yc_ref[...], w_ref[0:D_CONV, :])
    acc = acc + _dot(yn_ref[...], w_ref[D_CONV:D_CONV + D_NSA, :])
    acc = acc + _dot(yr_ref[...], w_ref[D_CONV + D_NSA:, :])
    o_ref[...] = x_ref[...] + g_ref[0] * acc


def out_projection(y_conv, y_nsa, y_ret, w_out, x, mod, seq):
    t, d = x.shape
    tm = 512
    per = seq // tm
    row = lambda w: pl.BlockSpec((tm, w), lambda i: (i, 0))
    return pl.pallas_call(
        _out_proj_kernel,
        grid=(t // tm,),
        in_specs=[row(D_CONV), row(D_NSA), row(D_RET),
                  pl.BlockSpec((D_MIX, d), lambda i: (0, 0)),
                  row(d),
                  pl.BlockSpec((1, 1, d), lambda i: (i // per, 0, 2))],
        out_specs=row(d),
        out_shape=jax.ShapeDtypeStruct((t, d), F32),
        compiler_params=_cparams(("arbitrary",)),
        name="out_projection",
    )(y_conv, y_nsa, y_ret, w_out, x, mod)


def _rank_rows(val, n):
    idx = lax.broadcasted_iota(jnp.int32, (n, 1), 0)
    rank = jnp.zeros(val.shape, jnp.int32)
    for rp in range(n):
        vp = val[rp:rp + 1, :]
        ahead = (vp > val) | ((vp == val) & (idx > rp))
        rank = rank + ahead.astype(jnp.int32)
    return rank


def _to_token_rows(ref, val):
    for c in range(ROW_CHUNKS):
        ref[pl.ds(c, val.shape[0], stride=ROW_CHUNKS), :] = val[:, c * LANE:(c + 1) * LANE]


def _from_token_rows(ref, n):
    return jnp.concatenate([ref[pl.ds(c, n, stride=ROW_CHUNKS), :] for c in range(ROW_CHUNKS)], axis=1)


def _router_kernel(x_ref, sh_ref, sc_ref, g_ref, rw_ref, rb_ref, tri_ref, h_ref, hrow_ref, pos_ref, w_ref, cnt_ref,
                   carry):
    @pl.when(pl.program_id(0) == 0)
    def _():
        carry[...] = jnp.zeros(carry.shape, F32)

    h = _norm_mod(x_ref[...], g_ref[...], sc_ref[0], sh_ref[0])
    h_ref[...] = h
    _to_token_rows(hrow_ref, h)
    tm = h.shape[0]
    scores = jax.nn.sigmoid(_dot_nt(rw_ref[...], h, precision=lax.Precision.HIGHEST))
    biased = scores + rb_ref[...]
    sub = lax.broadcasted_iota(jnp.int32, (GROUP_SIZE, 1), 0)
    gs = []
    for g in range(N_EXPERT_GROUPS):
        bg = biased[g * GROUP_SIZE:(g + 1) * GROUP_SIZE, :]
        m1 = jnp.max(bg, axis=0, keepdims=True)
        first = jnp.min(jnp.where(bg == m1, sub, GROUP_SIZE), axis=0, keepdims=True)
        m2 = jnp.max(jnp.where(sub == first, -jnp.inf, bg), axis=0, keepdims=True)
        gs.append(m1 + m2)
    gs = jnp.concatenate(gs, axis=0)
    gkeep = _rank_rows(gs, N_EXPERT_GROUPS) < TOPK_GROUPS
    keep = jnp.concatenate(
        [jnp.broadcast_to(gkeep[g:g + 1, :], (GROUP_SIZE, tm)) for g in range(N_EXPERT_GROUPS)], axis=0)
    masked = jnp.where(keep, biased, -jnp.inf)
    sel = _rank_rows(masked, N_EXPERTS) < TOP_K
    w = jnp.where(sel, scores, 0.0)
    w_ref[...] = ROUTED_SCALE * w / jnp.sum(w, axis=0, keepdims=True)
    self32 = sel.astype(F32)
    pos = carry[:, 0:1] + _dot(self32.astype(BF16), tri_ref[...])
    pos_ref[...] = jnp.where(sel, pos.astype(jnp.int32), -1)
    carry[...] = carry[...] + jnp.sum(self32, axis=1, keepdims=True)
    cnt_ref[...] = carry[...].astype(jnp.int32)


def moe_route(x, mod, norm_g, router_w_t, router_bias, seq):
    t, d = x.shape
    tm = 512
    per = seq // tm
    tri = jnp.asarray(np.triu(np.ones((tm, tm), np.float32), 1), BF16)
    ecol = pl.BlockSpec((N_EXPERTS, tm), lambda i: (0, i))
    return pl.pallas_call(
        _router_kernel,
        grid=(t // tm,),
        in_specs=[
            pl.BlockSpec((tm, d), lambda i: (i, 0)),
            pl.BlockSpec((1, 1, d), lambda i: (i // per, 0, 3)),
            pl.BlockSpec((1, 1, d), lambda i: (i // per, 0, 4)),
            pl.BlockSpec((1, d), lambda i: (0, 0)),
            pl.BlockSpec((N_EXPERTS, d), lambda i: (0, 0)),
            pl.BlockSpec((N_EXPERTS, 1), lambda i: (0, 0)),
            pl.BlockSpec((tm, tm), lambda i: (0, 0)),
        ],
        out_specs=[pl.BlockSpec((tm, d), lambda i: (i, 0)),
                   pl.BlockSpec((tm * ROW_CHUNKS, LANE), lambda i: (i, 0)), ecol, ecol,
                   pl.BlockSpec((N_EXPERTS, LANE), lambda i: (0, 0))],
        out_shape=[
            jax.ShapeDtypeStruct((t, d), F32),
            jax.ShapeDtypeStruct((t * ROW_CHUNKS, LANE), F32),
            jax.ShapeDtypeStruct((N_EXPERTS, t), jnp.int32),
            jax.ShapeDtypeStruct((N_EXPERTS, t), F32),
            jax.ShapeDtypeStruct((N_EXPERTS, LANE), jnp.int32),
        ],
        scratch_shapes=[pltpu.VMEM((N_EXPERTS, LANE), F32)],
        compiler_params=_cparams(("arbitrary",)),
        name="moe_route",
    )(x, mod, mod, norm_g, router_w_t, router_bias, tri)


def _slots_kernel(pos_ref, w_ref, start_ref, tri_ref, slot_ref, w8_ref):
    pos = pos_ref[...]
    sel = pos >= 0
    slot = pos + start_ref[...]
    order = _dot(tri_ref[...], sel.astype(F32).astype(BF16))
    w = w_ref[...]
    for k in range(TOP_K):
        mk = sel & (order == float(k))
        slot_ref[k:k + 1, :] = jnp.sum(jnp.where(mk, slot, 0), axis=0, keepdims=True)
        w8_ref[k:k + 1, :] = jnp.sum(jnp.where(mk, w, 0.0), axis=0, keepdims=True)


def moe_slots(pos_t, w_t, pad_start):
    e, t = pos_t.shape
    tm = 2048
    tri = jnp.asarray(np.tril(np.ones((e, e), np.float32), -1), BF16)
    ecol = pl.BlockSpec((e, tm), lambda i: (0, i))
    kcol = pl.BlockSpec((TOP_K, tm), lambda i: (0, i))
    return pl.pallas_call(
        _slots_kernel,
        grid=(t // tm,),
        in_specs=[ecol, ecol, pl.BlockSpec((e, 1), lambda i: (0, 0)), pl.BlockSpec((e, e), lambda i: (0, 0))],
        out_specs=[kcol, kcol],
        out_shape=[jax.ShapeDtypeStruct((TOP_K, t), jnp.int32), jax.ShapeDtypeStruct((TOP_K, t), F32)],
        compiler_params=_cparams(("arbitrary",)),
        name="moe_slots",
    )(pos_t, w_t, pad_start, tri)


def _inverse_kernel(slot_ref, init_ref, inv_ref, sem, *, n_tok):
    tm = slot_ref.shape[0] // TOP_K
    i = pl.program_id(0)

    @pl.when(i == 0)
    def _():
        cp = pltpu.make_async_copy(init_ref, inv_ref, sem)
        cp.start()
        cp.wait()

    def body(t, carry):
        for k in range(TOP_K):
            inv_ref[slot_ref[t * TOP_K + k]] = k * n_tok + i * tm + t
        return carry

    lax.fori_loop(0, tm, body, 0, unroll=2)


def moe_inverse(slot8, n_slots):
    n_tok = slot8.shape[1]
    tm = 2048
    init = TOP_K * n_tok + jnp.arange(n_slots, dtype=jnp.int32)
    slot_flat = slot8.T.reshape(-1)
    return pl.pallas_call(
        functools.partial(_inverse_kernel, n_tok=n_tok),
        grid=(n_tok // tm,),
        in_specs=[pl.BlockSpec((TOP_K * tm,), lambda i: (i,), memory_space=pltpu.SMEM),
                  pl.BlockSpec(memory_space=pl.ANY)],
        out_specs=pl.BlockSpec(memory_space=pltpu.SMEM),
        out_shape=jax.ShapeDtypeStruct((n_slots,), jnp.int32),
        scratch_shapes=[pltpu.SemaphoreType.DMA(())],
        compiler_params=_cparams(("arbitrary",)),
        name="moe_inverse",
    )(slot_flat, init)


def _swiglu(x_bf16, w_gu, w_down, d_hidden):
    gu = _dot(x_bf16, w_gu)
    gate, up = gu[:, :d_hidden], gu[:, d_hidden:]
    return _dot((gate * jax.nn.sigmoid(gate) * up).astype(BF16), w_down)


def _experts_kernel(be_ref, first_ref, nb_ref, src_ref, dst_ref, h_hbm, wgu_ref, wd_ref, y_hbm,
                    x0, x1, y0, y1, wgu_bf, wd_bf, gsem, ssem, *, n_tok, n_blocks):
    del be_ref
    s = pl.program_id(0)
    cb = s - 1
    used = (cb >= 0) & (cb < nb_ref[0])
    rc = ROW_CHUNKS
    rows = x0.shape[0] // rc
    blocks_per_chunk = SMEM_CHUNK // rows
    src_off = lax.rem(jnp.minimum(s, n_blocks - 1), blocks_per_chunk) * rows
    dst_off = lax.rem(s, blocks_per_chunk) * rows

    @pl.when(s == 0)
    def _():
        y0[...] = jnp.zeros(y0.shape, F32)
        y1[...] = jnp.zeros(y1.shape, F32)

    @pl.when(used & (first_ref[jnp.clip(cb, 0, n_blocks - 1)] == 1))
    def _():
        wgu_bf[...] = wgu_ref[0, 0].astype(BF16)
        wd_bf[...] = wd_ref[0, 0].astype(BF16)

    def step(p, x_out, x_in, y_out, y_in):
        def gathered(buf, sem):
            return pltpu.make_async_copy(h_hbm.at[pl.ds(0, rows * rc), :], buf, sem)

        def scattered(buf, sem):
            return pltpu.make_async_copy(buf, y_hbm.at[pl.ds(0, rows * rc), :], sem)

        @pl.when(s >= 1)
        def _():
            gathered(x_in, gsem.at[1 - p]).wait()

        @pl.when(s >= 1)
        def _():
            scattered(y_out, ssem.at[1 - p]).wait()

        def issue_rows():
            for r in range(rows):
                src = pl.multiple_of((src_ref[src_off + r] & (n_tok - 1)) * rc, rc)
                dst = pl.multiple_of(dst_ref[dst_off + r] * rc, rc)
                pltpu.make_async_copy(h_hbm.at[pl.ds(src, rc), :], x_out.at[pl.ds(r * rc, rc), :], gsem.at[p]).start()
                pltpu.make_async_copy(y_in.at[pl.ds(r * rc, rc), :], y_hbm.at[pl.ds(dst, rc), :],
                                      ssem.at[p]).start(priority=1)

        @pl.when(used)
        def _():
            issue_rows()
            x = _from_token_rows(x_in, rows).astype(BF16)
            _to_token_rows(y_out, _swiglu(x, wgu_bf[...], wd_bf[...], D_EXPERT))

        @pl.when(jnp.logical_not(used))
        def _():
            issue_rows()

        @pl.when(s == pl.num_programs(0) - 1)
        def _():
            gathered(x_out, gsem.at[p]).wait()
            scattered(y_in, ssem.at[p]).wait()

    @pl.when(lax.rem(s, 2) == 0)
    def _():
        step(0, x0, x1, y1, y0)

    @pl.when(lax.rem(s, 2) == 1)
    def _():
        step(1, x1, x0, y0, y1)


def moe_experts(block_e, first, n_used, inv, h_rows, w_gu, w_down, layer):
    n_tok, d = h_rows.shape[0] // ROW_CHUNKS, D_MODEL
    n_slots = inv.shape[0]
    n_blocks = n_slots // MOE_BLOCK
    assert n_tok & (n_tok - 1) == 0, "source token = inv mod T uses a power-of-two T"
    assert n_slots % SMEM_CHUNK == 0 and SMEM_CHUNK % MOE_BLOCK == 0
    per_chunk = SMEM_CHUNK // MOE_BLOCK
    n_rows = TOP_K * n_tok + n_slots + 2 * MOE_BLOCK
    warmup = TOP_K * n_tok + n_slots + jnp.arange(2 * MOE_BLOCK, dtype=jnp.int32)
    tail = jnp.zeros((SMEM_CHUNK - 2 * MOE_BLOCK,), jnp.int32)
    dst = jnp.concatenate([warmup, inv, tail])
    clamp = lambda b: jnp.clip(b, 0, n_blocks - 1)
    smem_blk = lambda f: pl.BlockSpec((SMEM_CHUNK,), f, memory_space=pltpu.SMEM)
    grid_spec = pltpu.PrefetchScalarGridSpec(
        num_scalar_prefetch=3,
        grid=(n_blocks + 2,),
        in_specs=[
            smem_blk(lambda s, be, fi, nb: (clamp(s) // per_chunk,)),
            smem_blk(lambda s, be, fi, nb: (s // per_chunk,)),
            pl.BlockSpec(memory_space=pl.ANY),
            pl.BlockSpec((1, 1, d, 2 * D_EXPERT), lambda s, be, fi, nb: (layer, be[clamp(s - 1)], 0, 0)),
            pl.BlockSpec((1, 1, D_EXPERT, d), lambda s, be, fi, nb: (layer, be[clamp(s - 1)], 0, 0)),
        ],
        out_specs=pl.BlockSpec(memory_space=pl.ANY),
        scratch_shapes=[pltpu.VMEM((MOE_BLOCK * ROW_CHUNKS, LANE), F32)] * 4 + [
            pltpu.VMEM((d, 2 * D_EXPERT), BF16), pltpu.VMEM((D_EXPERT, d), BF16),
            pltpu.SemaphoreType.DMA((2,)), pltpu.SemaphoreType.DMA((2,))],
    )
    return pl.pallas_call(
        functools.partial(_experts_kernel, n_tok=n_tok, n_blocks=n_blocks),
        grid_spec=grid_spec,
        out_shape=jax.ShapeDtypeStruct((n_rows * ROW_CHUNKS, LANE), F32),
        compiler_params=_cparams(("arbitrary",)),
        name="moe_experts",
    )(block_e, first, n_used, inv, dst, h_rows, w_gu, w_down)


def _shared_kernel(h_ref, wgu_ref, wd_ref, o_ref):
    o_ref[...] = _swiglu(h_ref[...].astype(BF16), wgu_ref[...], wd_ref[...], D_SHARED)


def shared_expert(h, w_gu, w_down):
    t, d = h.shape
    tm = 512
    return pl.pallas_call(
        _shared_kernel,
        grid=(t // tm,),
        in_specs=[pl.BlockSpec((tm, d), lambda i: (i, 0)),
                  pl.BlockSpec((d, 2 * D_SHARED), lambda i: (0, 0)),
                  pl.BlockSpec((D_SHARED, d), lambda i: (0, 0))],
        out_specs=pl.BlockSpec((tm, d), lambda i: (i, 0)),
        out_shape=jax.ShapeDtypeStruct((t, d), F32),
        compiler_params=_cparams(("arbitrary",)),
        name="shared_expert",
    )(h, w_gu, w_down)


def _combine_kernel(w_ref, sh_ref, x_ref, g_ref, fg_ref, *rest, final_norm):
    y_refs, o_ref, sum_scr = rest[:TOP_K], rest[TOP_K], rest[TOP_K + 1]
    tm = x_ref.shape[0]

    def token(t, carry):
        rows = pl.ds(pl.multiple_of(t * ROW_CHUNKS, ROW_CHUNKS), ROW_CHUNKS)
        acc = w_ref[0, t] * y_refs[0][rows, :]
        for k in range(1, TOP_K):
            acc = acc + w_ref[k, t] * y_refs[k][rows, :]
        sum_scr[rows, :] = acc
        return carry

    lax.fori_loop(0, tm, token, 0, unroll=4)
    routed = _from_token_rows(sum_scr, tm)
    out = x_ref[...] + g_ref[0] * (routed + sh_ref[...])
    if final_norm:
        out = _rms(out) * fg_ref[...]
    o_ref[...] = out


def moe_combine(w8, shared, x, mod, final_g, y, seq, final_norm):
    t, d = x.shape
    tm = 256
    per = seq // tm
    row = pl.BlockSpec((tm, d), lambda i: (i, 0))
    y_specs = [pl.BlockSpec((tm * ROW_CHUNKS, LANE), lambda i, k=k: (k * (t // tm) + i, 0)) for k in range(TOP_K)]
    return pl.pallas_call(
        functools.partial(_combine_kernel, final_norm=final_norm),
        grid=(t // tm,),
        in_specs=[
            pl.BlockSpec((TOP_K, tm), lambda i: (0, i), memory_space=pltpu.SMEM),
            row, row,
            pl.BlockSpec((1, 1, d), lambda i: (i // per, 0, 5)),
            pl.BlockSpec((1, d), lambda i: (0, 0)),
        ] + y_specs,
        out_specs=row,
        out_shape=jax.ShapeDtypeStruct((t, d), F32),
        scratch_shapes=[pltpu.VMEM((tm * ROW_CHUNKS, LANE), F32)],
        compiler_params=_cparams(("arbitrary",)),
        name="moe_combine",
    )(w8, shared, x, mod, final_g, *([y] * TOP_K))


def token_mixer_layer(x, mod, tabs, p, batch, seq):
    cos_n, sin_n, cos_r, sin_r = tabs
    proj = in_projection(x, mod, p["norm1_g"], p["w_in"], seq)
    y_conv = short_conv(proj, p["conv_w_t"], p["conv_g"], batch, seq)
    q_rot, kvc, k_sw, vt_sw = nsa_rope(proj, cos_n, sin_n, batch, seq)
    n_h = seq // CMP_STRIDE
    kvc_h = kvc.reshape(batch, seq, 2 * N_NSA_KV, HEAD_DIM).transpose(0, 2, 1, 3)
    kvc_h = kvc_h.reshape(batch, 2 * N_NSA_KV, n_h, CMP_STRIDE * HEAD_DIM)
    kc, kct = compress(kvc_h, p["cmp_pe"], p["cmp_w1"], p["cmp_w2"])
    y_nsa = nsa_attention(q_rot, kc, kct, k_sw, vt_sw, proj, p["nsa_g"], batch, seq)
    y_ret = retention(proj, cos_r, sin_r, p["ret_g"], batch, seq)
    return out_projection(y_conv, y_nsa, y_ret, p["w_out"], x, mod, seq)


def moe_layer(x, mod, p, seq, final_g, final_norm):
    t, d = x.shape
    n_assign = t * TOP_K
    n_blocks = (n_assign + N_EXPERTS * (MOE_BLOCK - 1) + MOE_BLOCK - 1) // MOE_BLOCK
    h, h_rows, pos_t, w_t, counts = moe_route(x, mod, p["norm2_g"], p["router_w_t"], p["router_bias"], seq)
    counts = counts[:, 0]
    padded = (counts + MOE_BLOCK - 1) // MOE_BLOCK * MOE_BLOCK
    pad_end = jnp.cumsum(padded)
    pad_start = (pad_end - padded).astype(jnp.int32)
    blk0 = jnp.arange(n_blocks, dtype=jnp.int32) * MOE_BLOCK
    block_e = jnp.minimum(jnp.sum(pad_end[None, :] <= blk0[:, None], axis=1), N_EXPERTS - 1).astype(jnp.int32)
    n_used = (pad_end[-1:] // MOE_BLOCK).astype(jnp.int32)
    first = jnp.concatenate([jnp.ones((1,), jnp.int32), (block_e[1:] != block_e[:-1]).astype(jnp.int32)])
    slot8, w8 = moe_slots(pos_t, w_t, pad_start[:, None])
    inv = moe_inverse(slot8, n_blocks * MOE_BLOCK)
    y = moe_experts(block_e, first, n_used, inv, h_rows, p["exp_w_gu"], p["exp_w_down"], p["layer"])
    shared = shared_expert(h, p["shared_w_gu"], p["shared_w_down"])
    return moe_combine(w8, shared, x, mod, final_g, y, seq, final_norm)


def _reorder_w_in(w_in):
    c_gate = COL_KV + 6 * D_NSA_KV
    gates = w_in[:, c_gate:c_gate + 3 * N_NSA_HEADS]
    per_group = 3 * NSA_REP
    padded = [jnp.pad(gates[:, g * per_group:(g + 1) * per_group], ((0, 0), (0, LANE - per_group)))
              for g in range(N_NSA_KV)]
    return jnp.concatenate([w_in[:, :c_gate], w_in[:, c_gate + 3 * N_NSA_HEADS:]] + padded, axis=1)


def kernel(x, c, positions, ada_w, ada_b, norm1_g, norm2_g, w_in, conv_w, conv_g, cmp_pe, cmp_w1, cmp_w2,
           nsa_g, ret_g, w_out, router_w, router_bias, exp_w_gu, exp_w_down, shared_w_gu, shared_w_down, final_g):
    batch, seq, d = x.shape
    depth = ada_w.shape[0]
    t = batch * seq
    mod_all = ada_modulation(c, ada_w, ada_b)
    pos_col = positions.reshape(t, 1)
    tabs = rope_tables(pos_col, ROPE_DIM, ROPE_THETA) + rope_tables(pos_col, HEAD_DIM, RET_THETA)
    xt = x.reshape(t, d)
    final_g2 = final_g.reshape(1, d)
    for l in range(depth):
        mod = mod_all[l].reshape(batch, 1, ADA_CHUNKS * d)
        p = dict(
            norm1_g=norm1_g[l].reshape(1, d),
            norm2_g=norm2_g[l].reshape(1, d),
            w_in=_reorder_w_in(w_in[l]).astype(BF16),
            conv_w_t=conv_w[l].T,
            conv_g=conv_g[l].reshape(1, D_CONV),
            cmp_pe=cmp_pe[l].reshape(2, 1, CMP_LEN * HEAD_DIM),
            cmp_w1=cmp_w1[l].astype(BF16),
            cmp_w2=cmp_w2[l].astype(BF16),
            nsa_g=nsa_g[l].reshape(1, D_NSA),
            ret_g=ret_g[l].reshape(1, D_RET),
            w_out=w_out[l].astype(BF16),
            router_w_t=router_w[l].T,
            router_bias=router_bias[l].reshape(N_EXPERTS, 1),
            layer=l,
            exp_w_gu=exp_w_gu,
            exp_w_down=exp_w_down,
            shared_w_gu=shared_w_gu[l].astype(BF16),
            shared_w_down=shared_w_down[l].astype(BF16),
        )
        xt = token_mixer_layer(xt, mod, tabs, p, batch, seq)
        xt = moe_layer(xt, mod, p, seq, final_g2, final_norm=(l == depth - 1))
    return xt.reshape(batch, seq, d)
```

```python
import functools

import numpy as np
import jax
import jax.numpy as jnp
from jax import lax
from jax.experimental import pallas as pl
from jax.experimental.pallas import tpu as pltpu

F32 = jnp.float32
BF16 = jnp.bfloat16

D_MODEL = 2048
HEAD_DIM = 128
D_CONV = D_MODEL // 4
CONV_GROUPS = D_CONV // HEAD_DIM
CONV_WIDTH = 3
N_NSA_HEADS = D_MODEL // 2 // HEAD_DIM
N_NSA_KV = 2
NSA_REP = N_NSA_HEADS // N_NSA_KV
D_NSA = N_NSA_HEADS * HEAD_DIM
D_NSA_KV = N_NSA_KV * HEAD_DIM
CMP_LEN = 32
CMP_STRIDE = 16
CMP_HIDDEN = 256
SLC_LEN = 64
N_SLC = 16
WINDOW = 512
ROPE_THETA = 500000.0
ROPE_DIM = HEAD_DIM // 4
N_RET_HEADS = D_MODEL // 4 // HEAD_DIM
D_RET = N_RET_HEADS * HEAD_DIM
RET_THETA = 10000.0
D_MIX = D_CONV + D_NSA + D_RET
N_EXPERTS = 64
N_EXPERT_GROUPS = 8
GROUP_SIZE = N_EXPERTS // N_EXPERT_GROUPS
TOPK_GROUPS = 4
TOP_K = 8
D_EXPERT = 512
D_SHARED = 512
ROUTED_SCALE = 2.5
MOE_BLOCK = 256
ADA_CHUNKS = 6
EPS = 1e-6
NEG_INF = -1e30
FORCE_SCORE = 1e4

LANE = 128
ROW_CHUNKS = D_MODEL // LANE
SMEM_CHUNK = 1024
LOG2_E = 1.4426950408889634
NSA_TILE = 256
GATE_COLS = N_NSA_KV * LANE
D_PROJ = 3 * D_CONV + D_NSA + 6 * D_NSA_KV + 4 * D_RET + GATE_COLS
COL_Q = 3 * D_CONV
COL_KV = COL_Q + D_NSA
COL_RET = COL_KV + 6 * D_NSA_KV
COL_GATE = COL_RET + 4 * D_RET

VMEM_LIMIT = 56 * 1024 * 1024


def _cparams(sem):
    return pltpu.CompilerParams(dimension_semantics=sem, vmem_limit_bytes=VMEM_LIMIT)


def _dot(a, b, **kw):
    return jnp.dot(a, b, preferred_element_type=F32, **kw)


def _dot_nt(a, b, **kw):
    return lax.dot_general(a, b, (((1,), (1,)), ((), ())), preferred_element_type=F32, **kw)


def _rms(x):
    return x * lax.rsqrt(jnp.mean(x * x, axis=-1, keepdims=True) + EPS)


def _ada_kernel(c_ref, w_ref, b_ref, o_ref):
    c = c_ref[...]
    ca = (c * jax.nn.sigmoid(c)).astype(BF16)
    o_ref[0] = _dot(ca, w_ref[0].astype(BF16)) + b_ref[0]


def ada_modulation(c, ada_w, ada_b):
    L, d, n = ada_w.shape
    b = c.shape[0]
    tn = 1024
    return pl.pallas_call(
        _ada_kernel,
        grid=(L, n // tn),
        in_specs=[
            pl.BlockSpec((b, d), lambda l, j: (0, 0)),
            pl.BlockSpec((1, d, tn), lambda l, j: (l, 0, j)),
            pl.BlockSpec((1, 1, tn), lambda l, j: (l, 0, j)),
        ],
        out_specs=pl.BlockSpec((1, b, tn), lambda l, j: (l, 0, j)),
        out_shape=jax.ShapeDtypeStruct((L, b, n), F32),
        compiler_params=_cparams(("arbitrary", "arbitrary")),
        name="ada_modulation",
    )(c, ada_w, ada_b.reshape(L, 1, n))


def _norm_mod(x, g, sc, sh):
    return (_rms(x) * g) * (1.0 + sc) + sh


def _in_proj_kernel(x_ref, sh_ref, sc_ref, g_ref, w_ref, o_ref, h_scr):
    @pl.when(pl.program_id(1) == 0)
    def _():
        h_scr[...] = _norm_mod(x_ref[...], g_ref[...], sc_ref[0], sh_ref[0]).astype(BF16)

    o_ref[...] = _dot(h_scr[...], w_ref[...])


def in_projection(x, mod, norm_g, w, seq):
    t, d = x.shape
    n = w.shape[1]
    tm, tn = 512, 1280
    per = seq // tm
    return pl.pallas_call(
        _in_proj_kernel,
        grid=(t // tm, n // tn),
        in_specs=[
            pl.BlockSpec((tm, d), lambda i, j: (i, 0)),
            pl.BlockSpec((1, 1, d), lambda i, j: (i // per, 0, 0)),
            pl.BlockSpec((1, 1, d), lambda i, j: (i // per, 0, 1)),
            pl.BlockSpec((1, d), lambda i, j: (0, 0)),
            pl.BlockSpec((d, tn), lambda i, j: (0, j)),
        ],
        out_specs=pl.BlockSpec((tm, tn), lambda i, j: (i, j)),
        out_shape=jax.ShapeDtypeStruct((t, n), F32),
        scratch_shapes=[pltpu.VMEM((tm, d), BF16)],
        compiler_params=_cparams(("arbitrary", "arbitrary")),
        name="in_projection",
    )(x, mod, mod, norm_g, w)


def _rope_table_kernel(pos_ref, inv_ref, sgn_ref, cos_ref, sin_ref):
    ang = pos_ref[...].astype(F32) * inv_ref[...]
    cos_ref[...] = jnp.cos(ang)
    sin_ref[...] = jnp.sin(ang) * sgn_ref[...]


def rope_tables(pos_col, rot_dim, theta):
    t = pos_col.shape[0]
    half = rot_dim // 2
    inv_half = theta ** (-jnp.arange(half, dtype=F32) / half)
    inv = jnp.concatenate([inv_half, inv_half, jnp.zeros((LANE - rot_dim,), F32)]).reshape(1, LANE)
    sgn = np.zeros((1, LANE), np.float32)
    sgn[0, :half] = -1.0
    sgn[0, half:rot_dim] = 1.0
    ts = 1024
    return pl.pallas_call(
        _rope_table_kernel,
        grid=(t // ts,),
        in_specs=[
            pl.BlockSpec((ts, 1), lambda i: (i, 0)),
            pl.BlockSpec((1, LANE), lambda i: (0, 0)),
            pl.BlockSpec((1, LANE), lambda i: (0, 0)),
        ],
        out_specs=[pl.BlockSpec((ts, LANE), lambda i: (i, 0))] * 2,
        out_shape=[jax.ShapeDtypeStruct((t, LANE), F32)] * 2,
        compiler_params=_cparams(("arbitrary",)),
        name="rope_tables",
    )(pos_col, inv, jnp.asarray(sgn))


def _rotate(x, cos, sin_signed, half):
    if 2 * half == LANE:
        swapped = pltpu.roll(x, half, 1)
    else:
        lane = lax.broadcasted_iota(jnp.int32, x.shape, 1)
        swapped = jnp.where(lane < half, pltpu.roll(x, LANE - half, 1), pltpu.roll(x, half, 1))
    return x * cos + swapped * sin_signed


def _nsa_rope_kernel(q0_ref, q1_ref, kvc_ref, kvs_ref, kvw_ref, cos_ref, sin_ref,
                     q_out, kvc_out, k_out, vt_out):
    cos, sin = cos_ref[...], sin_ref[...]
    half = ROPE_DIM // 2
    hpb = D_CONV // HEAD_DIM
    scale = HEAD_DIM ** -0.5 * LOG2_E
    for blk, src in enumerate((q0_ref, q1_ref)):
        for h in range(hpb):
            sl = slice(h * HEAD_DIM, (h + 1) * HEAD_DIM)
            q_out[:, blk * D_CONV + h * HEAD_DIM: blk * D_CONV + (h + 1) * HEAD_DIM] = (
                (_rotate(src[:, sl], cos, sin, half) * scale).astype(q_out.dtype))
    for g in range(N_NSA_KV):
        sl = slice(g * HEAD_DIM, (g + 1) * HEAD_DIM)
        kvc_out[:, sl] = _rotate(kvc_ref[:, sl], cos, sin, half)
    kvc_out[:, D_NSA_KV:] = kvc_ref[:, D_NSA_KV:]
    for br, src in enumerate((kvs_ref, kvw_ref)):
        for g in range(N_NSA_KV):
            sl = slice(g * HEAD_DIM, (g + 1) * HEAD_DIM)
            dst = slice((br * N_NSA_KV + g) * HEAD_DIM, (br * N_NSA_KV + g + 1) * HEAD_DIM)
            k_out[:, dst] = _rotate(src[:, sl], cos, sin, half).astype(k_out.dtype)
            vsl = slice(D_NSA_KV + g * HEAD_DIM, D_NSA_KV + (g + 1) * HEAD_DIM)
            for tile in range(vt_out.shape[2]):
                rows = slice(tile * NSA_TILE, (tile + 1) * NSA_TILE)
                vt_out[0, br * N_NSA_KV + g, tile] = src[rows, vsl].T.astype(vt_out.dtype)


def nsa_rope(proj, cos, sin, batch, seq):
    t = proj.shape[0]
    ts = 512
    per = seq // ts
    w = D_CONV
    blk = lambda j: pl.BlockSpec((ts, w), lambda i, j=j: (i, j))
    tab = pl.BlockSpec((ts, LANE), lambda i: (i, 0))
    out = pl.BlockSpec((ts, w), lambda i: (i, 0))
    qb = COL_Q // w
    kb = COL_KV // w
    return pl.pallas_call(
        _nsa_rope_kernel,
        grid=(t // ts,),
        in_specs=[blk(qb), blk(qb + 1), blk(kb), blk(kb + 1), blk(kb + 2), tab, tab],
        out_specs=[pl.BlockSpec((ts, D_NSA), lambda i: (i, 0)), out, out,
                   pl.BlockSpec((1, 2 * N_NSA_KV, ts // NSA_TILE, HEAD_DIM, NSA_TILE),
                                lambda i: (i // per, 0, i % per, 0, 0))],
        out_shape=[
            jax.ShapeDtypeStruct((t, D_NSA), BF16),
            jax.ShapeDtypeStruct((t, w), F32),
            jax.ShapeDtypeStruct((t, w), BF16),
            jax.ShapeDtypeStruct((batch, 2 * N_NSA_KV, seq // NSA_TILE, HEAD_DIM, NSA_TILE), BF16),
        ],
        compiler_params=_cparams(("arbitrary",)),
        name="nsa_rope",
    )(proj, proj, proj, proj, proj, cos, sin)


def _conv_kernel(cb_ref, cc_ref, cu_ref, w_ref, g_ref, o_ref, ext):
    ts = cb_ref.shape[0]

    @pl.when(pl.program_id(1) == 0)
    def _():
        ext[0:8, :] = jnp.zeros((8, ext.shape[1]), F32)

    v = cc_ref[...] * cu_ref[...]
    ext[8:, :] = v
    v1 = ext[pl.ds(7, ts), :]
    v2 = ext[pl.ds(6, ts), :]
    y = cb_ref[...] * (w_ref[0:1, :] * v2 + w_ref[1:2, :] * v1 + w_ref[2:3, :] * v)
    ext[0:8, :] = v[ts - 8:, :]
    for gi in range(CONV_GROUPS):
        sl = slice(gi * HEAD_DIM, (gi + 1) * HEAD_DIM)
        o_ref[:, sl] = (_rms(y[:, sl]) * g_ref[:, sl]).astype(o_ref.dtype)


def short_conv(proj, conv_w_t, conv_g, batch, seq):
    t = proj.shape[0]
    ts = 512
    per = seq // ts
    w = D_CONV
    blk = lambda j: pl.BlockSpec((ts, w), lambda b, s, j=j: (b * per + s, j))
    return pl.pallas_call(
        _conv_kernel,
        grid=(batch, per),
        in_specs=[blk(0), blk(1), blk(2),
                  pl.BlockSpec((CONV_WIDTH, w), lambda b, s: (0, 0)),
                  pl.BlockSpec((1, w), lambda b, s: (0, 0))],
        out_specs=pl.BlockSpec((ts, w), lambda b, s: (b * per + s, 0)),
        out_shape=jax.ShapeDtypeStruct((t, w), BF16),
        scratch_shapes=[pltpu.VMEM((8 + ts, w), F32)],
        compiler_params=_cparams(("arbitrary", "arbitrary")),
        name="short_conv",
    )(proj, proj, proj, conv_w_t, conv_g)


def _compress_kernel(h_ref, pe_ref, w1_ref, w2_ref, o_ref, ot_ref):
    h = h_ref[0, 0]
    n_h, half = h.shape
    pe = pe_ref[0]
    a = _dot((h + pe[:, :half]).astype(BF16), w1_ref[0, :half, :])
    b = _dot((h + pe[:, half:]).astype(BF16), w1_ref[0, half:, :])
    pre = a + pltpu.roll(b, n_h - 1, 0)
    out = _dot(jax.nn.gelu(pre).astype(BF16), w2_ref[0])
    o_ref[0, 0] = out.astype(o_ref.dtype)
    ot_ref[0, 0] = out.T.astype(ot_ref.dtype)


def compress(kvc_h, pe_flat, w1, w2):
    b, four, n_h, dh = kvc_h.shape
    return pl.pallas_call(
        _compress_kernel,
        grid=(b, four),
        in_specs=[
            pl.BlockSpec((1, 1, n_h, dh), lambda i, j: (i, j, 0, 0)),
            pl.BlockSpec((1, 1, 2 * dh), lambda i, j: (j // N_NSA_KV, 0, 0)),
            pl.BlockSpec((1, 2 * dh, CMP_HIDDEN), lambda i, j: (j // N_NSA_KV, 0, 0)),
            pl.BlockSpec((1, CMP_HIDDEN, HEAD_DIM), lambda i, j: (j // N_NSA_KV, 0, 0)),
        ],
        out_specs=[pl.BlockSpec((1, 1, n_h, HEAD_DIM), lambda i, j: (i, j, 0, 0)),
                   pl.BlockSpec((1, 1, HEAD_DIM, n_h), lambda i, j: (i, j, 0, 0))],
        out_shape=[jax.ShapeDtypeStruct((b, four, n_h, HEAD_DIM), BF16),
                   jax.ShapeDtypeStruct((b, four, HEAD_DIM, n_h), BF16)],
        compiler_params=_cparams(("arbitrary", "arbitrary")),
        name="nsa_compress",
    )(kvc_h, pe_flat, w1, w2)


def _nsa_kernel(q_ref, kc_ref, vct_ref, ks_ref, kw_ref, vst_ref, vwt_ref, gate_ref, ovt_ref, eaug_ref, g_ref,
                o_ref, kaug_scr, qaug_scr, s_scr, m_scr, l_scr, acc_scr, out_scr, *, n_cmp, n_slc, n_sel):
    tq = q_ref.shape[0]
    rows = NSA_REP * tq
    i = pl.program_id(2)
    t0 = i * tq
    hd = HEAD_DIM

    @pl.when(i == 0)
    def _():
        kaug_scr[:, :hd] = ks_ref[...]
        kaug_scr[:, hd:] = eaug_ref[...]

    for r in range(NSA_REP):
        qaug_scr[r * tq:(r + 1) * tq, :hd] = q_ref[:, r * hd:(r + 1) * hd]

    t_all =t0 + lax.rem(lax.broadcasted_iota(jnp.int32, (1, rows), 1), tq)
    gates_t = jax.nn.sigmoid(gate_ref[...]).T
    gate_rows = [jnp.concatenate([gates_t[3 * r + br:3 * r + br + 1, :] for r in range(NSA_REP)], axis=1)
                 for br in range(3)]

    key_sub = lax.broadcasted_iota(jnp.int32, (tq, 1), 0)
    causal = t0 + key_sub <= t_all

    n_back = WINDOW // tq
    pieces = []
    for back in range(n_back, -1, -1):
        jt = jnp.maximum(i - back, 0)
        sj = _dot_nt(kw_ref[pl.ds(pl.multiple_of(jt * tq, tq), tq), :], qaug_scr[:, :hd])
        if back == 0:
            sj = jnp.where(causal, sj, NEG_INF)
        elif back == n_back:
            sj = jnp.where((jt * tq + key_sub > t_all - WINDOW) & (i >= back), sj, NEG_INF)
        else:
            sj = jnp.where(i >= back, sj, NEG_INF)
        pieces.append((sj, vwt_ref[0, 0, jt]))
    m_w = pieces[0][0].max(axis=0, keepdims=True)
    for sj, _ in pieces[1:]:
        m_w = jnp.maximum(m_w, sj.max(axis=0, keepdims=True))
    l_w = None
    acc_w = None
    for sj, vt_tile in pieces:
        p = jnp.exp2(sj - m_w)
        lj = jnp.sum(p, axis=0, keepdims=True)
        aj = _dot(vt_tile, p.astype(BF16))
        l_w = lj if l_w is None else l_w + lj
        acc_w = aj if acc_w is None else acc_w + aj
    out_scr[...] = (gate_rows[2] / l_w) * acc_w

    kc = kc_ref[0, 0]
    vct = vct_ref[0, 0]
    n_h = kc.shape[0]
    c_idx = lax.broadcasted_iota(jnp.int32, (n_h, 1), 0)
    c_end = jnp.where(c_idx < n_cmp, c_idx * CMP_STRIDE + (CMP_LEN - 1), jnp.iinfo(jnp.int32).max)
    s = _dot_nt(kc, qaug_scr[:, :hd])
    sm = jnp.where(c_end <= t_all, s, NEG_INF)
    e = jnp.exp2(sm - jnp.max(sm, axis=0, keepdims=True))
    inv = jnp.where(t_all >= CMP_LEN - 1, 1.0 / jnp.sum(e, axis=0, keepdims=True), 0.0)
    p = e * inv
    out_scr[...] += gate_rows[0] * _dot(vct, p.astype(BF16))
    psum_t = p[:, 0:tq]
    for r in range(1, NSA_REP):
        psum_t = psum_t + p[:, r * tq:(r + 1) * tq]
    imp_t =_dot(ovt_ref[...], psum_t, precision=lax.Precision.HIGHEST)

    j_idx = lax.broadcasted_iota(jnp.int32, (n_slc, 1), 0)
    jq = (t0 + lax.broadcasted_iota(jnp.int32, (1, tq), 1)) // SLC_LEN
    forced = (j_idx == 0) | (j_idx == jq) | (j_idx == jq - 1)
    val = jnp.where(forced, FORCE_SCORE, jnp.where(j_idx <= jq, imp_t[:n_slc], -1.0))
    bias_t = jnp.full((n_slc, tq), NEG_INF, F32)
    for _ in range(n_sel):
        top = jnp.max(val, axis=0, keepdims=True)
        first = jnp.min(jnp.where(val == top, j_idx, n_slc), axis=0, keepdims=True)
        hit = j_idx == first
        bias_t = jnp.where(hit, 0.0, bias_t)
        val = jnp.where(hit, -jnp.inf, val)
    if n_slc < LANE:
        bias_t = jnp.concatenate([bias_t, jnp.zeros((LANE - n_slc, tq), F32)], axis=0)
    bias = bias_t.T.astype(BF16)
    for r in range(NSA_REP):
        qaug_scr[r * tq:(r + 1) * tq, hd:] = bias

    def sel_scores(jt):
        k0 = pl.multiple_of(jt * tq, tq)
        return _dot_nt(kaug_scr[pl.ds(k0, tq), :], qaug_scr[...])

    def sel_update(s, vt_tile, mask):
        if mask is not None:
            s = jnp.where(mask, s, NEG_INF)
        m_old = m_scr[...]
        m_new = jnp.maximum(m_old, jnp.max(s, axis=0, keepdims=True))
        alpha = jnp.exp2(m_old - m_new)
        p = jnp.exp2(s - m_new)
        l_scr[...] = alpha * l_scr[...] + jnp.sum(p, axis=0, keepdims=True)
        acc_scr[...] = alpha * acc_scr[...] + _dot(vt_tile, p.astype(BF16))
        m_scr[...] = m_new

    m_scr[...] = jnp.full(m_scr.shape, NEG_INF, F32)
    l_scr[...] = jnp.zeros(l_scr.shape, F32)
    acc_scr[...] = jnp.zeros(acc_scr.shape, F32)
    s_scr[...] = sel_scores(0)

    def sel_body(jt, carry):
        s_cur = s_scr[...]
        s_next = sel_scores(jt + 1)
        sel_update(s_cur, vst_ref[0, 0, jt], None)
        s_scr[...] = s_next
        return carry

    lax.fori_loop(0, i, sel_body, 0)
    sel_update(s_scr[...], vst_ref[0, 0, i], causal)
    out_scr[...] += (gate_rows[1] / l_scr[...]) * acc_scr[...]


    for r in range(NSA_REP):
        o_t = out_scr[:, r * tq:(r + 1) * tq]
        o_t = o_t * lax.rsqrt(jnp.mean(o_t * o_t, axis=0, keepdims=True) + EPS)
        sl = slice(r * hd, (r + 1) * hd)
        o_ref[:, sl] = (o_t.T * g_ref[:, sl]).astype(o_ref.dtype)


def _overlap_matrix_t(n_h, n_cmp, n_slc):
    cs = np.arange(n_cmp) * CMP_STRIDE
    js = np.arange(n_slc) * SLC_LEN
    ov = np.minimum(cs[:, None] + CMP_LEN, js[None, :] + SLC_LEN) - np.maximum(cs[:, None], js[None, :])
    out = np.zeros((LANE, n_h), np.float32)
    out[:n_slc, :n_cmp] = (np.clip(ov, 0, None) / CMP_LEN).T
    return out


def nsa_attention(q_rot, kc, kct, k_sw, vt_sw, proj, nsa_g, batch, seq):
    t = q_rot.shape[0]
    tq = NSA_TILE
    per = seq // tq
    n_h = seq // CMP_STRIDE
    n_cmp = (seq - CMP_LEN) // CMP_STRIDE + 1
    n_slc = seq // SLC_LEN
    n_sel = min(N_SLC, n_slc)
    assert n_slc <= LANE and WINDOW % tq == 0 and tq % LANE == 0
    ovt = jnp.asarray(_overlap_matrix_t(n_h, n_cmp, n_slc))
    eaug = np.zeros((seq, LANE), np.float32)
    eaug[np.arange(seq), np.arange(seq) // SLC_LEN] = 1.0
    eaug = jnp.asarray(eaug, BF16)
    hd = HEAD_DIM
    qw = NSA_REP * hd
    rows = NSA_REP * tq
    seq_spec = lambda off: pl.BlockSpec((seq, hd), lambda b, g, i, off=off: (b, off + g))
    vt_spec = lambda off: pl.BlockSpec((1, 1, per, hd, tq), lambda b, g, i, off=off: (b, off + g, 0, 0, 0))
    kern = functools.partial(_nsa_kernel, n_cmp=n_cmp, n_slc=n_slc, n_sel=n_sel)
    return pl.pallas_call(
        kern,
        grid=(batch, N_NSA_KV, per),
        in_specs=[
            pl.BlockSpec((tq, qw), lambda b, g, i: (b * per + i, g)),
            pl.BlockSpec((1, 1, n_h, hd), lambda b, g, i: (b, g, 0, 0)),
            pl.BlockSpec((1, 1, hd, n_h), lambda b, g, i: (b, N_NSA_KV + g, 0, 0)),
            seq_spec(0), seq_spec(N_NSA_KV), vt_spec(0), vt_spec(N_NSA_KV),
            pl.BlockSpec((tq, LANE), lambda b, g, i: (b * per + i, COL_GATE // LANE + g)),
            pl.BlockSpec((LANE, n_h), lambda b, g, i: (0, 0)),
            pl.BlockSpec((seq, LANE), lambda b, g, i: (0, 0)),
            pl.BlockSpec((1, qw), lambda b, g, i: (0, g)),
        ],
        out_specs=pl.BlockSpec((tq, qw), lambda b, g, i: (b * per + i, g)),
        out_shape=jax.ShapeDtypeStruct((t, D_NSA), BF16),
        scratch_shapes=[
            pltpu.VMEM((seq, 2 * hd), BF16),
            pltpu.VMEM((rows, 2 * hd), BF16),
            pltpu.VMEM((tq, rows), F32),
            pltpu.VMEM((1, rows), F32),
            pltpu.VMEM((1, rows), F32),
            pltpu.VMEM((hd, rows), F32),
            pltpu.VMEM((hd, rows), F32),
        ],
        compiler_params=_cparams(("arbitrary", "arbitrary", "arbitrary")),
        name="nsa_attention",
    )(q_rot, kc, kct, k_sw, k_sw, vt_sw, vt_sw, proj, ovt, eaug, nsa_g)


def _retention_kernel(q_ref, k_ref, v_ref, gate_ref, cos_ref, sin_ref, lg_ref, g_ref, o_ref, state):
    c = q_ref.shape[0]
    hd = HEAD_DIM

    @pl.when(pl.program_id(1) == 0)
    def _():
        state[...] = jnp.zeros(state.shape, F32)

    cos, sin = cos_ref[...], sin_ref[...]
    n_row = lax.broadcasted_iota(jnp.int32, (c, 1), 0).astype(F32)
    n_col = lax.broadcasted_iota(jnp.int32, (1, c), 1).astype(F32)
    diff = n_row - n_col
    for h in range(N_RET_HEADS):
        sl = slice(h * hd, (h + 1) * hd)
        lg = lg_ref[h][:, 0:1]
        q = _rotate(q_ref[:, sl], cos, sin, hd // 2)
        k = _rotate(k_ref[:, sl], cos, sin, hd // 2) * hd ** -0.5
        v = v_ref[:, sl].astype(BF16)
        decay = jnp.where(diff >= 0.0, jnp.exp(jnp.maximum(diff, 0.0) * lg), 0.0)
        scores = _dot_nt(q.astype(BF16), k.astype(BF16)) * decay
        o = _dot(scores.astype(BF16), v)
        xi = jnp.exp((n_row + 1.0) * lg)
        o = o + _dot((q * xi).astype(BF16), state[h].astype(BF16))
        zeta = jnp.exp((c - 1.0 - n_row) * lg)
        kz_t = (k * zeta).T.astype(BF16)
        state[h] = state[h] * jnp.exp(c * lg) + _dot(kz_t, v)
        gate = gate_ref[:, sl]
        o_ref[:, sl] = (gate * jax.nn.sigmoid(gate) * (_rms(o) * g_ref[:, sl])).astype(o_ref.dtype)


def retention(proj, cos, sin, ret_g, batch, seq):
    t = proj.shape[0]
    c = 256
    per = seq // c
    hd = HEAD_DIM
    base = COL_RET // D_RET
    blk = lambda off: pl.BlockSpec((c, D_RET), lambda b, s, off=off: (b * per + s, base + off))
    tab = pl.BlockSpec((c, hd), lambda b, s: (b * per + s, 0))
    lg = jnp.log1p(-(2.0 ** (-5.0 - jnp.arange(N_RET_HEADS, dtype=F32))))
    lg = jnp.broadcast_to(lg[:, None, None], (N_RET_HEADS, 1, LANE))
    return pl.pallas_call(
        _retention_kernel,
        grid=(batch, per),
        in_specs=[blk(0), blk(1), blk(2), blk(3), tab, tab,
                  pl.BlockSpec((N_RET_HEADS, 1, LANE), lambda b, s: (0, 0, 0)),
                  pl.BlockSpec((1, D_RET), lambda b, s: (0, 0))],
        out_specs=pl.BlockSpec((c, D_RET), lambda b, s: (b * per + s, 0)),
        out_shape=jax.ShapeDtypeStruct((t, D_RET), BF16),
        scratch_shapes=[pltpu.VMEM((N_RET_HEADS, hd, hd), F32)],
        compiler_params=_cparams(("arbitrary", "arbitrary")),
        name="retention",
    )(proj, proj, proj, proj, cos, sin, lg, ret_g)


def _out_proj_kernel(yc_ref, yn_ref, yr_ref, w_ref, x_ref, g_ref, o_ref):
    acc = _dot(yc_ref[...], w_ref[0:D_CONV, :])
    acc = acc + _dot(yn_ref[...], w_ref[D_CONV:D_CONV + D_NSA, :])
    acc = acc + _dot(yr_ref[...], w_ref[D_CONV + D_NSA:, :])
    o_ref[...] = x_ref[...] + g_ref[0] * acc


def out_projection(y_conv, y_nsa, y_ret, w_out, x, mod, seq):
    t, d = x.shape
    tm = 512
    per = seq // tm
    row = lambda w: pl.BlockSpec((tm, w), lambda i: (i, 0))
    return pl.pallas_call(
        _out_proj_kernel,
        grid=(t // tm,),
        in_specs=[row(D_CONV), row(D_NSA), row(D_RET),
                  pl.BlockSpec((D_MIX, d), lambda i: (0, 0)),
                  row(d),
                  pl.BlockSpec((1, 1, d), lambda i: (i // per, 0, 2))],
        out_specs=row(d),
        out_shape=jax.ShapeDtypeStruct((t, d), F32),
        compiler_params=_cparams(("arbitrary",)),
        name="out_projection",
    )(y_conv, y_nsa, y_ret, w_out, x, mod)


def _rank_rows(val, n):
    idx = lax.broadcasted_iota(jnp.int32, (n, 1), 0)
    rank = jnp.zeros(val.shape, jnp.int32)
    for rp in range(n):
        vp = val[rp:rp + 1, :]
        ahead = (vp > val) | ((vp == val) & (idx > rp))
        rank = rank + ahead.astype(jnp.int32)
    return rank


def _to_token_rows(ref, val):
    for c in range(ROW_CHUNKS):
        ref[pl.ds(c, val.shape[0], stride=ROW_CHUNKS), :] = val[:, c * LANE:(c + 1) * LANE]


def _from_token_rows(ref, n):
    return jnp.concatenate([ref[pl.ds(c, n, stride=ROW_CHUNKS), :] for c in range(ROW_CHUNKS)], axis=1)


def _router_kernel(x_ref, sh_ref, sc_ref, g_ref, rw_ref, rb_ref, tri_ref, swgu_ref, swd_ref,
                   shared_ref, hrow_ref, pos_ref, w_ref, cnt_ref, carry):
    @pl.when(pl.program_id(0) == 0)
    def _():
        carry[...] = jnp.zeros(carry.shape, F32)

    h = _norm_mod(x_ref[...], g_ref[...], sc_ref[0], sh_ref[0])
    _to_token_rows(hrow_ref, h)
    shared_ref[...] = _swiglu(h.astype(BF16), swgu_ref[...], swd_ref[...], D_SHARED)
    tm = h.shape[0]
    scores = jax.nn.sigmoid(_dot_nt(rw_ref[...], h, precision=lax.Precision.HIGHEST))
    biased = scores + rb_ref[...]
    sub = lax.broadcasted_iota(jnp.int32, (GROUP_SIZE, 1), 0)
    gs = []
    for g in range(N_EXPERT_GROUPS):
        bg = biased[g * GROUP_SIZE:(g + 1) * GROUP_SIZE, :]
        m1 = jnp.max(bg, axis=0, keepdims=True)
        first = jnp.min(jnp.where(bg == m1, sub, GROUP_SIZE), axis=0, keepdims=True)
        m2 = jnp.max(jnp.where(sub == first, -jnp.inf, bg), axis=0, keepdims=True)
        gs.append(m1 + m2)
    gs = jnp.concatenate(gs, axis=0)
    gkeep = _rank_rows(gs, N_EXPERT_GROUPS) < TOPK_GROUPS
    keep = jnp.concatenate(
        [jnp.broadcast_to(gkeep[g:g + 1, :], (GROUP_SIZE, tm)) for g in range(N_EXPERT_GROUPS)], axis=0)
    masked = jnp.where(keep, biased, -jnp.inf)
    sel = _rank_rows(masked, N_EXPERTS) < TOP_K
    w = jnp.where(sel, scores, 0.0)
    w_ref[...] = ROUTED_SCALE * w / jnp.sum(w, axis=0, keepdims=True)
    self32 = sel.astype(F32)
    pos = carry[:, 0:1] + _dot(self32.astype(BF16), tri_ref[...])
    pos_ref[...] = jnp.where(sel, pos.astype(jnp.int32), -1)
    carry[...] = carry[...] + jnp.sum(self32, axis=1, keepdims=True)
    cnt_ref[...] = carry[...].astype(jnp.int32)


def moe_route(x, mod, norm_g, router_w_t, router_bias, shared_w_gu, shared_w_down, seq):
    t, d = x.shape
    tm = 512
    per = seq // tm
    tri = jnp.asarray(np.triu(np.ones((tm, tm), np.float32), 1), BF16)
    ecol = pl.BlockSpec((N_EXPERTS, tm), lambda i: (0, i))
    return pl.pallas_call(
        _router_kernel,
        grid=(t // tm,),
        in_specs=[
            pl.BlockSpec((tm, d), lambda i: (i, 0)),
            pl.BlockSpec((1, 1, d), lambda i: (i // per, 0, 3)),
            pl.BlockSpec((1, 1, d), lambda i: (i // per, 0, 4)),
            pl.BlockSpec((1, d), lambda i: (0, 0)),
            pl.BlockSpec((N_EXPERTS, d), lambda i: (0, 0)),
            pl.BlockSpec((N_EXPERTS, 1), lambda i: (0, 0)),
            pl.BlockSpec((tm, tm), lambda i: (0, 0)),
            pl.BlockSpec((d, 2 * D_SHARED), lambda i: (0, 0)),
            pl.BlockSpec((D_SHARED, d), lambda i: (0, 0)),
        ],
        out_specs=[pl.BlockSpec((tm, d), lambda i: (i, 0)),
                   pl.BlockSpec((tm * ROW_CHUNKS, LANE), lambda i: (i, 0)), ecol, ecol,
                   pl.BlockSpec((N_EXPERTS, LANE), lambda i: (0, 0))],
        out_shape=[
            jax.ShapeDtypeStruct((t, d), F32),
            jax.ShapeDtypeStruct((t * ROW_CHUNKS, LANE), F32),
            jax.ShapeDtypeStruct((N_EXPERTS, t), jnp.int32),
            jax.ShapeDtypeStruct((N_EXPERTS, t), F32),
            jax.ShapeDtypeStruct((N_EXPERTS, LANE), jnp.int32),
        ],
        scratch_shapes=[pltpu.VMEM((N_EXPERTS, LANE), F32)],
        compiler_params=_cparams(("arbitrary",)),
        name="moe_route",
    )(x, mod, mod, norm_g, router_w_t, router_bias, tri, shared_w_gu, shared_w_down)


def _slots_kernel(pos_ref, w_ref, start_ref, tri_ref, slot_ref, w8_ref):
    pos = pos_ref[...]
    sel = pos >= 0
    slot = pos + start_ref[...]
    order = _dot(tri_ref[...], sel.astype(F32).astype(BF16))
    w = w_ref[...]
    for k in range(TOP_K):
        mk = sel & (order == float(k))
        slot_ref[k:k + 1, :] = jnp.sum(jnp.where(mk, slot, 0), axis=0, keepdims=True)
        w8_ref[k:k + 1, :] = jnp.sum(jnp.where(mk, w, 0.0), axis=0, keepdims=True)


def moe_slots(pos_t, w_t, pad_start):
    e, t = pos_t.shape
    tm = 2048
    tri = jnp.asarray(np.tril(np.ones((e, e), np.float32), -1), BF16)
    ecol = pl.BlockSpec((e, tm), lambda i: (0, i))
    kcol = pl.BlockSpec((TOP_K, tm), lambda i: (0, i))
    return pl.pallas_call(
        _slots_kernel,
        grid=(t // tm,),
        in_specs=[ecol, ecol, pl.BlockSpec((e, 1), lambda i: (0, 0)), pl.BlockSpec((e, e), lambda i: (0, 0))],
        out_specs=[kcol, kcol],
        out_shape=[jax.ShapeDtypeStruct((TOP_K, t), jnp.int32), jax.ShapeDtypeStruct((TOP_K, t), F32)],
        compiler_params=_cparams(("arbitrary",)),
        name="moe_slots",
    )(pos_t, w_t, pad_start, tri)


def _inverse_kernel(slot_ref, init_ref, inv_ref, sem, *, n_tok):
    tm = slot_ref.shape[0] // TOP_K
    i = pl.program_id(0)

    @pl.when(i == 0)
    def _():
        cp = pltpu.make_async_copy(init_ref, inv_ref, sem)
        cp.start()
        cp.wait()

    def body(t, carry):
        for k in range(TOP_K):
            inv_ref[slot_ref[t * TOP_K + k]] = k * n_tok + i * tm + t
        return carry

    lax.fori_loop(0, tm, body, 0, unroll=2)


def moe_inverse(slot8, n_slots):
    n_tok = slot8.shape[1]
    tm = 2048
    init = TOP_K * n_tok + jnp.arange(n_slots, dtype=jnp.int32)
    slot_flat = slot8.T.reshape(-1)
    return pl.pallas_call(
        functools.partial(_inverse_kernel, n_tok=n_tok),
        grid=(n_tok // tm,),
        in_specs=[pl.BlockSpec((TOP_K * tm,), lambda i: (i,), memory_space=pltpu.SMEM),
                  pl.BlockSpec(memory_space=pl.ANY)],
        out_specs=pl.BlockSpec(memory_space=pltpu.SMEM),
        out_shape=jax.ShapeDtypeStruct((n_slots,), jnp.int32),
        scratch_shapes=[pltpu.SemaphoreType.DMA(())],
        compiler_params=_cparams(("arbitrary",)),
        name="moe_inverse",
    )(slot_flat, init)


def _swiglu(x_bf16, w_gu, w_down, d_hidden):
    gu = _dot(x_bf16, w_gu)
    gate, up = gu[:, :d_hidden], gu[:, d_hidden:]
    return _dot((gate * jax.nn.sigmoid(gate) * up).astype(BF16), w_down)


def _experts_kernel(be_ref, first_ref, nb_ref, src_ref, dst_ref, h_hbm, wgu_ref, wd_ref, y_hbm,
                    x0, x1, y0, y1, wgu_bf, wd_bf, gsem, ssem, *, n_tok, n_blocks):
    del be_ref
    s = pl.program_id(0)
    cb = s - 1
    used = (cb >= 0) & (cb < nb_ref[0])
    rc = ROW_CHUNKS
    rows = x0.shape[0] // rc
    blocks_per_chunk = SMEM_CHUNK // rows
    src_off = lax.rem(jnp.minimum(s, n_blocks - 1), blocks_per_chunk) * rows
    dst_off = lax.rem(s, blocks_per_chunk) * rows

    @pl.when(s == 0)
    def _():
        y0[...] = jnp.zeros(y0.shape, F32)
        y1[...] = jnp.zeros(y1.shape, F32)

    @pl.when(used & (first_ref[jnp.clip(cb, 0, n_blocks - 1)] == 1))
    def _():
        wgu_bf[...] = wgu_ref[0, 0].astype(BF16)
        wd_bf[...] = wd_ref[0, 0].astype(BF16)

    def step(p, x_out, x_in, y_out, y_in):
        def gathered(buf, sem):
            return pltpu.make_async_copy(h_hbm.at[pl.ds(0, rows * rc), :], buf, sem)

        def scattered(buf, sem):
            return pltpu.make_async_copy(buf, y_hbm.at[pl.ds(0, rows * rc), :], sem)

        last_live = nb_ref[0] + 1

        @pl.when((s >= 1) & (s - 1 <= last_live))
        def _():
            gathered(x_in, gsem.at[1 - p]).wait()
            scattered(y_out, ssem.at[1 - p]).wait()

        def issue_rows():
            for r in range(rows):
                src = pl.multiple_of((src_ref[src_off + r] & (n_tok - 1)) * rc, rc)
                dst = pl.multiple_of(dst_ref[dst_off + r] * rc, rc)
                pltpu.make_async_copy(h_hbm.at[pl.ds(src, rc), :], x_out.at[pl.ds(r * rc, rc), :], gsem.at[p]).start()
                pltpu.make_async_copy(y_in.at[pl.ds(r * rc, rc), :], y_hbm.at[pl.ds(dst, rc), :], ssem.at[p]).start()

        @pl.when(used)
        def _():
            issue_rows()
            x = _from_token_rows(x_in, rows).astype(BF16)
            _to_token_rows(y_out, _swiglu(x, wgu_bf[...], wd_bf[...], D_EXPERT))

        @pl.when(jnp.logical_not(used) & (s <= last_live))
        def _():
            issue_rows()

        @pl.when((s == pl.num_programs(0) - 1) & (s <= last_live))
        def _():
            gathered(x_out, gsem.at[p]).wait()
            scattered(y_in, ssem.at[p]).wait()

    @pl.when(lax.rem(s, 2) == 0)
    def _():
        step(0, x0, x1, y1, y0)

    @pl.when(lax.rem(s, 2) == 1)
    def _():
        step(1, x1, x0, y0, y1)


def moe_experts(block_e, first, n_used, inv, h_rows, w_gu, w_down, layer):
    n_tok, d = h_rows.shape[0] // ROW_CHUNKS, D_MODEL
    n_slots = inv.shape[0]
    n_blocks = n_slots // MOE_BLOCK
    assert n_tok & (n_tok - 1) == 0, "source token = inv mod T uses a power-of-two T"
    assert n_slots % SMEM_CHUNK == 0 and SMEM_CHUNK % MOE_BLOCK == 0
    per_chunk = SMEM_CHUNK // MOE_BLOCK
    n_rows = TOP_K * n_tok + n_slots + 2 * MOE_BLOCK
    warmup = TOP_K * n_tok + n_slots + jnp.arange(2 * MOE_BLOCK, dtype=jnp.int32)
    tail = jnp.zeros((SMEM_CHUNK - 2 * MOE_BLOCK,), jnp.int32)
    dst = jnp.concatenate([warmup, inv, tail])
    clamp = lambda b: jnp.clip(b, 0, n_blocks - 1)
    smem_blk = lambda f: pl.BlockSpec((SMEM_CHUNK,), f, memory_space=pltpu.SMEM)
    grid_spec = pltpu.PrefetchScalarGridSpec(
        num_scalar_prefetch=3,
        grid=(n_blocks + 2,),
        in_specs=[
            smem_blk(lambda s, be, fi, nb: (clamp(s) // per_chunk,)),
            smem_blk(lambda s, be, fi, nb: (s // per_chunk,)),
            pl.BlockSpec(memory_space=pl.ANY),
            pl.BlockSpec((1, 1, d, 2 * D_EXPERT), lambda s, be, fi, nb: (layer, be[clamp(s - 1)], 0, 0)),
            pl.BlockSpec((1, 1, D_EXPERT, d), lambda s, be, fi, nb: (layer, be[clamp(s - 1)], 0, 0)),
        ],
        out_specs=pl.BlockSpec(memory_space=pl.ANY),
        scratch_shapes=[pltpu.VMEM((MOE_BLOCK * ROW_CHUNKS, LANE), F32)] * 4 + [
            pltpu.VMEM((d, 2 * D_EXPERT), BF16), pltpu.VMEM((D_EXPERT, d), BF16),
            pltpu.SemaphoreType.DMA((2,)), pltpu.SemaphoreType.DMA((2,))],
    )
    return pl.pallas_call(
        functools.partial(_experts_kernel, n_tok=n_tok, n_blocks=n_blocks),
        grid_spec=grid_spec,
        out_shape=jax.ShapeDtypeStruct((n_rows * ROW_CHUNKS, LANE), F32),
        compiler_params=_cparams(("arbitrary",)),
        name="moe_experts",
    )(block_e, first, n_used, inv, dst, h_rows, w_gu, w_down)


def _combine_kernel(w_ref, sh_ref, x_ref, g_ref, fg_ref, *rest, final_norm):
    y_refs, o_ref, sum_scr = rest[:TOP_K], rest[TOP_K], rest[TOP_K + 1]
    tm = x_ref.shape[0]

    def token(t, carry):
        rows = pl.ds(pl.multiple_of(t * ROW_CHUNKS, ROW_CHUNKS), ROW_CHUNKS)
        acc = w_ref[0, t] * y_refs[0][rows, :]
        for k in range(1, TOP_K):
            acc = acc + w_ref[k, t] * y_refs[k][rows, :]
        sum_scr[rows, :] = acc
        return carry

    lax.fori_loop(0, tm, token, 0, unroll=4)
    routed = _from_token_rows(sum_scr, tm)
    out = x_ref[...] + g_ref[0] * (routed + sh_ref[...])
    if final_norm:
        out = _rms(out) * fg_ref[...]
    o_ref[...] = out


def moe_combine(w8, shared, x, mod, final_g, y, seq, final_norm):
    t, d = x.shape
    tm = 256
    per = seq // tm
    row = pl.BlockSpec((tm, d), lambda i: (i, 0))
    y_specs = [pl.BlockSpec((tm * ROW_CHUNKS, LANE), lambda i, k=k: (k * (t // tm) + i, 0)) for k in range(TOP_K)]
    return pl.pallas_call(
        functools.partial(_combine_kernel, final_norm=final_norm),
        grid=(t // tm,),
        in_specs=[
            pl.BlockSpec((TOP_K, tm), lambda i: (0, i), memory_space=pltpu.SMEM),
            row, row,
            pl.BlockSpec((1, 1, d), lambda i: (i // per, 0, 5)),
            pl.BlockSpec((1, d), lambda i: (0, 0)),
        ] + y_specs,
        out_specs=row,
        out_shape=jax.ShapeDtypeStruct((t, d), F32),
        scratch_shapes=[pltpu.VMEM((tm * ROW_CHUNKS, LANE), F32)],
        compiler_params=_cparams(("arbitrary",)),
        name="moe_combine",
    )(w8, shared, x, mod, final_g, *([y] * TOP_K))


def token_mixer_layer(x, mod, tabs, p, batch, seq):
    cos_n, sin_n, cos_r, sin_r = tabs
    proj = in_projection(x, mod, p["norm1_g"], p["w_in"], seq)
    y_conv = short_conv(proj, p["conv_w_t"], p["conv_g"], batch, seq)
    q_rot, kvc, k_sw, vt_sw = nsa_rope(proj, cos_n, sin_n, batch, seq)
    n_h = seq // CMP_STRIDE
    kvc_h = kvc.reshape(batch, seq, 2 * N_NSA_KV, HEAD_DIM).transpose(0, 2, 1, 3)
    kvc_h = kvc_h.reshape(batch, 2 * N_NSA_KV, n_h, CMP_STRIDE * HEAD_DIM)
    kc, kct = compress(kvc_h, p["cmp_pe"], p["cmp_w1"], p["cmp_w2"])
    y_nsa = nsa_attention(q_rot, kc, kct, k_sw, vt_sw, proj, p["nsa_g"], batch, seq)
    y_ret = retention(proj, cos_r, sin_r, p["ret_g"], batch, seq)
    return out_projection(y_conv, y_nsa, y_ret, p["w_out"], x, mod, seq)


def moe_layer(x, mod, p, seq, final_g, final_norm):
    t, d = x.shape
    n_assign = t * TOP_K
    n_blocks = (n_assign + N_EXPERTS * (MOE_BLOCK - 1) + MOE_BLOCK - 1) // MOE_BLOCK
    shared, h_rows, pos_t, w_t, counts = moe_route(x, mod, p["norm2_g"], p["router_w_t"], p["router_bias"],
                                                   p["shared_w_gu"], p["shared_w_down"], seq)
    counts = counts[:, 0]
    padded = (counts + MOE_BLOCK - 1) // MOE_BLOCK * MOE_BLOCK
    pad_end = jnp.cumsum(padded)
    pad_start = (pad_end - padded).astype(jnp.int32)
    blk0 = jnp.arange(n_blocks, dtype=jnp.int32) * MOE_BLOCK
    block_e = jnp.minimum(jnp.sum(pad_end[None, :] <= blk0[:, None], axis=1), N_EXPERTS - 1).astype(jnp.int32)
    n_used = (pad_end[-1:] // MOE_BLOCK).astype(jnp.int32)
    first = jnp.concatenate([jnp.ones((1,), jnp.int32), (block_e[1:] != block_e[:-1]).astype(jnp.int32)])
    slot8, w8 = moe_slots(pos_t, w_t, pad_start[:, None])
    inv = moe_inverse(slot8, n_blocks * MOE_BLOCK)
    y = moe_experts(block_e, first, n_used, inv, h_rows, p["exp_w_gu"], p["exp_w_down"], p["layer"])
    return moe_combine(w8, shared, x, mod, final_g, y, seq, final_norm)


def _reorder_w_in(w_in):
    c_gate = COL_KV + 6 * D_NSA_KV
    gates = w_in[:, c_gate:c_gate + 3 * N_NSA_HEADS]
    per_group = 3 * NSA_REP
    padded = [jnp.pad(gates[:, g * per_group:(g + 1) * per_group], ((0, 0), (0, LANE - per_group)))
              for g in range(N_NSA_KV)]
    return jnp.concatenate([w_in[:, :c_gate], w_in[:, c_gate + 3 * N_NSA_HEADS:]] + padded, axis=1)


def kernel(x, c, positions, ada_w, ada_b, norm1_g, norm2_g, w_in, conv_w, conv_g, cmp_pe, cmp_w1, cmp_w2,
           nsa_g, ret_g, w_out, router_w, router_bias, exp_w_gu, exp_w_down, shared_w_gu, shared_w_down, final_g):
    batch, seq, d = x.shape
    depth = ada_w.shape[0]
    t = batch * seq
    mod_all = ada_modulation(c, ada_w, ada_b)
    pos_col = positions.reshape(t, 1)
    tabs = rope_tables(pos_col, ROPE_DIM, ROPE_THETA) + rope_tables(pos_col, HEAD_DIM, RET_THETA)
    xt = x.reshape(t, d)
    final_g2 = final_g.reshape(1, d)
    for l in range(depth):
        mod = mod_all[l].reshape(batch, 1, ADA_CHUNKS * d)
        p = dict(
            norm1_g=norm1_g[l].reshape(1, d),
            norm2_g=norm2_g[l].reshape(1, d),
            w_in=_reorder_w_in(w_in[l]).astype(BF16),
            conv_w_t=conv_w[l].T,
            conv_g=conv_g[l].reshape(1, D_CONV),
            cmp_pe=cmp_pe[l].reshape(2, 1, CMP_LEN * HEAD_DIM),
            cmp_w1=cmp_w1[l].astype(BF16),
            cmp_w2=cmp_w2[l].astype(BF16),
            nsa_g=nsa_g[l].reshape(1, D_NSA),
            ret_g=ret_g[l].reshape(1, D_RET),
            w_out=w_out[l].astype(BF16),
            router_w_t=router_w[l].T,
            router_bias=router_bias[l].reshape(N_EXPERTS, 1),
            layer=l,
            exp_w_gu=exp_w_gu,
            exp_w_down=exp_w_down,
            shared_w_gu=shared_w_gu[l].astype(BF16),
            shared_w_down=shared_w_down[l].astype(BF16),
        )
        xt = token_mixer_layer(xt, mod, tabs, p, batch, seq)
        xt = moe_layer(xt, mod, p, seq, final_g2, final_norm=(l == depth - 1))
    return xt.reshape(batch, seq, d)
```

```python
import functools

import numpy as np
import jax
import jax.numpy as jnp
from jax import lax
from jax.experimental import pallas as pl
from jax.experimental.pallas import tpu as pltpu

F32 = jnp.float32
BF16 = jnp.bfloat16

D_MODEL = 2048
HEAD_DIM = 128
D_CONV = D_MODEL // 4
CONV_GROUPS = D_CONV // HEAD_DIM
CONV_WIDTH = 3
N_NSA_HEADS = D_MODEL // 2 // HEAD_DIM
N_NSA_KV = 2
NSA_REP = N_NSA_HEADS // N_NSA_KV
D_NSA = N_NSA_HEADS * HEAD_DIM
D_NSA_KV = N_NSA_KV * HEAD_DIM
CMP_LEN = 32
CMP_STRIDE = 16
CMP_HIDDEN = 256
SLC_LEN = 64
N_SLC = 16
WINDOW = 512
ROPE_THETA = 500000.0
ROPE_DIM = HEAD_DIM // 4
N_RET_HEADS = D_MODEL // 4 // HEAD_DIM
D_RET = N_RET_HEADS * HEAD_DIM
RET_THETA = 10000.0
D_MIX = D_CONV + D_NSA + D_RET
N_EXPERTS = 64
N_EXPERT_GROUPS = 8
GROUP_SIZE = N_EXPERTS // N_EXPERT_GROUPS
TOPK_GROUPS = 4
TOP_K = 8
D_EXPERT = 512
D_SHARED = 512
ROUTED_SCALE = 2.5
MOE_BLOCK = 256
ADA_CHUNKS = 6
EPS = 1e-6
NEG_INF = -1e30
FORCE_SCORE = 1e4

LANE = 128
ROW_CHUNKS = D_MODEL // LANE
PACK_CHUNKS = ROW_CHUNKS // 2
SMEM_CHUNK = 1024
LOG2_E = 1.4426950408889634
NSA_TILE = 256
GATE_COLS = N_NSA_KV * LANE
D_PROJ = 3 * D_CONV + D_NSA + 6 * D_NSA_KV + 4 * D_RET + GATE_COLS
COL_Q = 3 * D_CONV
COL_KV = COL_Q + D_NSA
COL_RET = COL_KV + 6 * D_NSA_KV
COL_GATE = COL_RET + 4 * D_RET

VMEM_LIMIT = 56 * 1024 * 1024


def _cparams(sem):
    return pltpu.CompilerParams(dimension_semantics=sem, vmem_limit_bytes=VMEM_LIMIT)


def _dot(a, b, **kw):
    return jnp.dot(a, b, preferred_element_type=F32, **kw)


def _dot_nt(a, b, **kw):
    return lax.dot_general(a, b, (((1,), (1,)), ((), ())), preferred_element_type=F32, **kw)


def _rms(x):
    return x * lax.rsqrt(jnp.mean(x * x, axis=-1, keepdims=True) + EPS)


def _ada_kernel(c_ref, w_ref, b_ref, o_ref):
    c = c_ref[...]
    ca = (c * jax.nn.sigmoid(c)).astype(BF16)
    o_ref[0] = _dot(ca, w_ref[0].astype(BF16)) + b_ref[0]


def ada_modulation(c, ada_w, ada_b):
    L, d, n = ada_w.shape
    b = c.shape[0]
    tn = 1024
    return pl.pallas_call(
        _ada_kernel,
        grid=(L, n // tn),
        in_specs=[
            pl.BlockSpec((b, d), lambda l, j: (0, 0)),
            pl.BlockSpec((1, d, tn), lambda l, j: (l, 0, j)),
            pl.BlockSpec((1, 1, tn), lambda l, j: (l, 0, j)),
        ],
        out_specs=pl.BlockSpec((1, b, tn), lambda l, j: (l, 0, j)),
        out_shape=jax.ShapeDtypeStruct((L, b, n), F32),
        compiler_params=_cparams(("arbitrary", "arbitrary")),
        name="ada_modulation",
    )(c, ada_w, ada_b.reshape(L, 1, n))


def _norm_mod(x, g, sc, sh):
    return (_rms(x) * g) * (1.0 + sc) + sh


def _in_proj_kernel(x_ref, sh_ref, sc_ref, g_ref, w_ref, o_ref, h_scr):
    @pl.when(pl.program_id(1) == 0)
    def _():
        h_scr[...] = _norm_mod(x_ref[...], g_ref[...], sc_ref[0], sh_ref[0]).astype(BF16)

    o_ref[...] = _dot(h_scr[...], w_ref[...])


def in_projection(x, mod, norm_g, w, seq):
    t, d = x.shape
    n = w.shape[1]
    tm, tn = 512, 1280
    per = seq // tm
    return pl.pallas_call(
        _in_proj_kernel,
        grid=(t // tm, n // tn),
        in_specs=[
            pl.BlockSpec((tm, d), lambda i, j: (i, 0)),
            pl.BlockSpec((1, 1, d), lambda i, j: (i // per, 0, 0)),
            pl.BlockSpec((1, 1, d), lambda i, j: (i // per, 0, 1)),
            pl.BlockSpec((1, d), lambda i, j: (0, 0)),
            pl.BlockSpec((d, tn), lambda i, j: (0, j)),
        ],
        out_specs=pl.BlockSpec((tm, tn), lambda i, j: (i, j)),
        out_shape=jax.ShapeDtypeStruct((t, n), F32),
        scratch_shapes=[pltpu.VMEM((tm, d), BF16)],
        compiler_params=_cparams(("arbitrary", "arbitrary")),
        name="in_projection",
    )(x, mod, mod, norm_g, w)


def _rope_table_kernel(pos_ref, inv_ref, sgn_ref, cos_ref, sin_ref):
    ang = pos_ref[...].astype(F32) * inv_ref[...]
    cos_ref[...] = jnp.cos(ang)
    sin_ref[...] = jnp.sin(ang) * sgn_ref[...]


def rope_tables(pos_col, rot_dim, theta):
    t = pos_col.shape[0]
    half = rot_dim // 2
    inv_half = theta ** (-jnp.arange(half, dtype=F32) / half)
    inv = jnp.concatenate([inv_half, inv_half, jnp.zeros((LANE - rot_dim,), F32)]).reshape(1, LANE)
    sgn = np.zeros((1, LANE), np.float32)
    sgn[0, :half] = -1.0
    sgn[0, half:rot_dim] = 1.0
    ts = 1024
    return pl.pallas_call(
        _rope_table_kernel,
        grid=(t // ts,),
        in_specs=[
            pl.BlockSpec((ts, 1), lambda i: (i, 0)),
            pl.BlockSpec((1, LANE), lambda i: (0, 0)),
            pl.BlockSpec((1, LANE), lambda i: (0, 0)),
        ],
        out_specs=[pl.BlockSpec((ts, LANE), lambda i: (i, 0))] * 2,
        out_shape=[jax.ShapeDtypeStruct((t, LANE), F32)] * 2,
        compiler_params=_cparams(("arbitrary",)),
        name="rope_tables",
    )(pos_col, inv, jnp.asarray(sgn))


def _rotate(x, cos, sin_signed, half):
    if 2 * half == LANE:
        swapped = pltpu.roll(x, half, 1)
    else:
        lane = lax.broadcasted_iota(jnp.int32, x.shape, 1)
        swapped = jnp.where(lane < half, pltpu.roll(x, LANE - half, 1), pltpu.roll(x, half, 1))
    return x * cos + swapped * sin_signed


def _nsa_rope_kernel(q0_ref, q1_ref, kvc_ref, kvs_ref, kvw_ref, cos_ref, sin_ref,
                     q_out, kvc_out, k_out, vt_out):
    cos, sin = cos_ref[...], sin_ref[...]
    half = ROPE_DIM // 2
    hpb = D_CONV // HEAD_DIM
    scale = HEAD_DIM ** -0.5 * LOG2_E
    for blk, src in enumerate((q0_ref, q1_ref)):
        for h in range(hpb):
            sl = slice(h * HEAD_DIM, (h + 1) * HEAD_DIM)
            q_out[:, blk * D_CONV + h * HEAD_DIM: blk * D_CONV + (h + 1) * HEAD_DIM] = (
                (_rotate(src[:, sl], cos, sin, half) * scale).astype(q_out.dtype))
    for g in range(N_NSA_KV):
        sl = slice(g * HEAD_DIM, (g + 1) * HEAD_DIM)
        kvc_out[:, sl] = _rotate(kvc_ref[:, sl], cos, sin, half)
    kvc_out[:, D_NSA_KV:] = kvc_ref[:, D_NSA_KV:]
    for br, src in enumerate((kvs_ref, kvw_ref)):
        for g in range(N_NSA_KV):
            sl = slice(g * HEAD_DIM, (g + 1) * HEAD_DIM)
            dst = slice((br * N_NSA_KV + g) * HEAD_DIM, (br * N_NSA_KV + g + 1) * HEAD_DIM)
            k_out[:, dst] = _rotate(src[:, sl], cos, sin, half).astype(k_out.dtype)
            vsl = slice(D_NSA_KV + g * HEAD_DIM, D_NSA_KV + (g + 1) * HEAD_DIM)
            for tile in range(vt_out.shape[2]):
                rows = slice(tile * NSA_TILE, (tile + 1) * NSA_TILE)
                vt_out[0, br * N_NSA_KV + g, tile] = src[rows, vsl].T.astype(vt_out.dtype)


def nsa_rope(proj, cos, sin, batch, seq):
    t = proj.shape[0]
    ts = 512
    per = seq // ts
    w = D_CONV
    blk = lambda j: pl.BlockSpec((ts, w), lambda i, j=j: (i, j))
    tab = pl.BlockSpec((ts, LANE), lambda i: (i, 0))
    out = pl.BlockSpec((ts, w), lambda i: (i, 0))
    qb = COL_Q // w
    kb = COL_KV // w
    return pl.pallas_call(
        _nsa_rope_kernel,
        grid=(t // ts,),
        in_specs=[blk(qb), blk(qb + 1), blk(kb), blk(kb + 1), blk(kb + 2), tab, tab],
        out_specs=[pl.BlockSpec((ts, D_NSA), lambda i: (i, 0)), out, out,
                   pl.BlockSpec((1, 2 * N_NSA_KV, ts // NSA_TILE, HEAD_DIM, NSA_TILE),
                                lambda i: (i // per, 0, i % per, 0, 0))],
        out_shape=[
            jax.ShapeDtypeStruct((t, D_NSA), BF16),
            jax.ShapeDtypeStruct((t, w), F32),
            jax.ShapeDtypeStruct((t, w), BF16),
            jax.ShapeDtypeStruct((batch, 2 * N_NSA_KV, seq // NSA_TILE, HEAD_DIM, NSA_TILE), BF16),
        ],
        compiler_params=_cparams(("arbitrary",)),
        name="nsa_rope",
    )(proj, proj, proj, proj, proj, cos, sin)


def _conv_kernel(cb_ref, cc_ref, cu_ref, w_ref, g_ref, o_ref, ext):
    ts = cb_ref.shape[0]

    @pl.when(pl.program_id(1) == 0)
    def _():
        ext[0:8, :] = jnp.zeros((8, ext.shape[1]), F32)

    v = cc_ref[...] * cu_ref[...]
    ext[8:, :] = v
    v1 = ext[pl.ds(7, ts), :]
    v2 = ext[pl.ds(6, ts), :]
    y = cb_ref[...] * (w_ref[0:1, :] * v2 + w_ref[1:2, :] * v1 + w_ref[2:3, :] * v)
    ext[0:8, :] = v[ts - 8:, :]
    for gi in range(CONV_GROUPS):
        sl = slice(gi * HEAD_DIM, (gi + 1) * HEAD_DIM)
        o_ref[:, sl] = (_rms(y[:, sl]) * g_ref[:, sl]).astype(o_ref.dtype)


def short_conv(proj, conv_w_t, conv_g, batch, seq):
    t = proj.shape[0]
    ts = 512
    per = seq // ts
    w = D_CONV
    blk = lambda j: pl.BlockSpec((ts, w), lambda b, s, j=j: (b * per + s, j))
    return pl.pallas_call(
        _conv_kernel,
        grid=(batch, per),
        in_specs=[blk(0), blk(1), blk(2),
                  pl.BlockSpec((CONV_WIDTH, w), lambda b, s: (0, 0)),
                  pl.BlockSpec((1, w), lambda b, s: (0, 0))],
        out_specs=pl.BlockSpec((ts, w), lambda b, s: (b * per + s, 0)),
        out_shape=jax.ShapeDtypeStruct((t, w), BF16),
        scratch_shapes=[pltpu.VMEM((8 + ts, w), F32)],
        compiler_params=_cparams(("arbitrary", "arbitrary")),
        name="short_conv",
    )(proj, proj, proj, conv_w_t, conv_g)


def _compress_kernel(h_ref, pe_ref, w1_ref, w2_ref, o_ref, ot_ref):
    h = h_ref[0, 0]
    n_h, half = h.shape
    pe = pe_ref[0]
    a = _dot((h + pe[:, :half]).astype(BF16), w1_ref[0, :half, :])
    b = _dot((h + pe[:, half:]).astype(BF16), w1_ref[0, half:, :])
    pre = a + pltpu.roll(b, n_h - 1, 0)
    out = _dot(jax.nn.gelu(pre).astype(BF16), w2_ref[0])
    o_ref[0, 0] = out.astype(o_ref.dtype)
    ot_ref[0, 0] = out.T.astype(ot_ref.dtype)


def compress(kvc_h, pe_flat, w1, w2):
    b, four, n_h, dh = kvc_h.shape
    return pl.pallas_call(
        _compress_kernel,
        grid=(b, four),
        in_specs=[
            pl.BlockSpec((1, 1, n_h, dh), lambda i, j: (i, j, 0, 0)),
            pl.BlockSpec((1, 1, 2 * dh), lambda i, j: (j // N_NSA_KV, 0, 0)),
            pl.BlockSpec((1, 2 * dh, CMP_HIDDEN), lambda i, j: (j // N_NSA_KV, 0, 0)),
            pl.BlockSpec((1, CMP_HIDDEN, HEAD_DIM), lambda i, j: (j // N_NSA_KV, 0, 0)),
        ],
        out_specs=[pl.BlockSpec((1, 1, n_h, HEAD_DIM), lambda i, j: (i, j, 0, 0)),
                   pl.BlockSpec((1, 1, HEAD_DIM, n_h), lambda i, j: (i, j, 0, 0))],
        out_shape=[jax.ShapeDtypeStruct((b, four, n_h, HEAD_DIM), BF16),
                   jax.ShapeDtypeStruct((b, four, HEAD_DIM, n_h), BF16)],
        compiler_params=_cparams(("arbitrary", "arbitrary")),
        name="nsa_compress",
    )(kvc_h, pe_flat, w1, w2)


def _nsa_kernel(q_ref, kc_ref, vct_ref, ks_ref, kw_ref, vst_ref, vwt_ref, gate_ref, ovt_ref, eaug_ref, g_ref,
                o_ref, kaug_scr, qaug_scr, s_scr, m_scr, l_scr, acc_scr, out_scr, *, n_cmp, n_slc, n_sel):
    tq = q_ref.shape[0]
    rows = NSA_REP * tq
    i = pl.program_id(2)
    t0 = i * tq
    hd = HEAD_DIM

    @pl.when(i == 0)
    def _():
        kaug_scr[:, :hd] = ks_ref[...]
        kaug_scr[:, hd:] = eaug_ref[...]

    for r in range(NSA_REP):
        qaug_scr[r * tq:(r + 1) * tq, :hd] = q_ref[:, r * hd:(r + 1) * hd]

    t_all =t0 + lax.rem(lax.broadcasted_iota(jnp.int32, (1, rows), 1), tq)
    gates_t = jax.nn.sigmoid(gate_ref[...]).T
    gate_rows = [jnp.concatenate([gates_t[3 * r + br:3 * r + br + 1, :] for r in range(NSA_REP)], axis=1)
                 for br in range(3)]

    key_sub = lax.broadcasted_iota(jnp.int32, (tq, 1), 0)
    causal = t0 + key_sub <= t_all

    n_back = WINDOW // tq
    pieces = []
    for back in range(n_back, -1, -1):
        jt = jnp.maximum(i - back, 0)
        sj = _dot_nt(kw_ref[pl.ds(pl.multiple_of(jt * tq, tq), tq), :], qaug_scr[:, :hd])
        if back == 0:
            sj = jnp.where(causal, sj, NEG_INF)
        elif back == n_back:
            sj = jnp.where((jt * tq + key_sub > t_all - WINDOW) & (i >= back), sj, NEG_INF)
        else:
            sj = jnp.where(i >= back, sj, NEG_INF)
        pieces.append((sj, vwt_ref[0, 0, jt]))
    m_w = pieces[0][0].max(axis=0, keepdims=True)
    for sj, _ in pieces[1:]:
        m_w = jnp.maximum(m_w, sj.max(axis=0, keepdims=True))
    l_w = None
    acc_w = None
    for sj, vt_tile in pieces:
        p = jnp.exp2(sj - m_w)
        lj = jnp.sum(p, axis=0, keepdims=True)
        aj = _dot(vt_tile, p.astype(BF16))
        l_w = lj if l_w is None else l_w + lj
        acc_w = aj if acc_w is None else acc_w + aj
    out_scr[...] = (gate_rows[2] / l_w) * acc_w

    kc = kc_ref[0, 0]
    vct = vct_ref[0, 0]
    n_h = kc.shape[0]
    c_idx = lax.broadcasted_iota(jnp.int32, (n_h, 1), 0)
    c_end = jnp.where(c_idx < n_cmp, c_idx * CMP_STRIDE + (CMP_LEN - 1), jnp.iinfo(jnp.int32).max)
    s = _dot_nt(kc, qaug_scr[:, :hd])
    sm = jnp.where(c_end <= t_all, s, NEG_INF)
    e = jnp.exp2(sm - jnp.max(sm, axis=0, keepdims=True))
    inv = jnp.where(t_all >= CMP_LEN - 1, 1.0 / jnp.sum(e, axis=0, keepdims=True), 0.0)
    p = e * inv
    out_scr[...] += gate_rows[0] * _dot(vct, p.astype(BF16))
    psum_t = p[:, 0:tq]
    for r in range(1, NSA_REP):
        psum_t = psum_t + p[:, r * tq:(r + 1) * tq]
    imp_t =_dot(ovt_ref[...], psum_t, precision=lax.Precision.HIGHEST)

    j_idx = lax.broadcasted_iota(jnp.int32, (n_slc, 1), 0)
    jq = (t0 + lax.broadcasted_iota(jnp.int32, (1, tq), 1)) // SLC_LEN
    forced = (j_idx == 0) | (j_idx == jq) | (j_idx == jq - 1)
    val = jnp.where(forced, FORCE_SCORE, jnp.where(j_idx <= jq, imp_t[:n_slc], -1.0))
    bias_t = jnp.full((n_slc, tq), NEG_INF, F32)
    for _ in range(n_sel):
        top = jnp.max(val, axis=0, keepdims=True)
        first = jnp.min(jnp.where(val == top, j_idx, n_slc), axis=0, keepdims=True)
        hit = j_idx == first
        bias_t = jnp.where(hit, 0.0, bias_t)
        val = jnp.where(hit, -jnp.inf, val)
    if n_slc < LANE:
        bias_t = jnp.concatenate([bias_t, jnp.zeros((LANE - n_slc, tq), F32)], axis=0)
    bias = bias_t.T.astype(BF16)
    for r in range(NSA_REP):
        qaug_scr[r * tq:(r + 1) * tq, hd:] = bias

    def sel_scores(jt):
        k0 = pl.multiple_of(jt * tq, tq)
        return _dot_nt(kaug_scr[pl.ds(k0, tq), :], qaug_scr[...])

    def sel_update(s, vt_tile, mask):
        if mask is not None:
            s = jnp.where(mask, s, NEG_INF)
        m_old = m_scr[...]
        m_new = jnp.maximum(m_old, jnp.max(s, axis=0, keepdims=True))
        alpha = jnp.exp2(m_old - m_new)
        p = jnp.exp2(s - m_new)
        l_scr[...] = alpha * l_scr[...] + jnp.sum(p, axis=0, keepdims=True)
        acc_scr[...] = alpha * acc_scr[...] + _dot(vt_tile, p.astype(BF16))
        m_scr[...] = m_new

    m_scr[...] = jnp.full(m_scr.shape, NEG_INF, F32)
    l_scr[...] = jnp.zeros(l_scr.shape, F32)
    acc_scr[...] = jnp.zeros(acc_scr.shape, F32)
    s_scr[...] = sel_scores(0)

    def sel_body(jt, carry):
        s_cur = s_scr[...]
        s_next = sel_scores(jt + 1)
        sel_update(s_cur, vst_ref[0, 0, jt], None)
        s_scr[...] = s_next
        return carry

    lax.fori_loop(0, i, sel_body, 0)
    sel_update(s_scr[...], vst_ref[0, 0, i], causal)
    out_scr[...] += (gate_rows[1] / l_scr[...]) * acc_scr[...]


    for r in range(NSA_REP):
        o_t = out_scr[:, r * tq:(r + 1) * tq]
        o_t = o_t * lax.rsqrt(jnp.mean(o_t * o_t, axis=0, keepdims=True) + EPS)
        sl = slice(r * hd, (r + 1) * hd)
        o_ref[:, sl] = (o_t.T * g_ref[:, sl]).astype(o_ref.dtype)


def _overlap_matrix_t(n_h, n_cmp, n_slc):
    cs = np.arange(n_cmp) * CMP_STRIDE
    js = np.arange(n_slc) * SLC_LEN
    ov = np.minimum(cs[:, None] + CMP_LEN, js[None, :] + SLC_LEN) - np.maximum(cs[:, None], js[None, :])
    out = np.zeros((LANE, n_h), np.float32)
    out[:n_slc, :n_cmp] = (np.clip(ov, 0, None) / CMP_LEN).T
    return out


def nsa_attention(q_rot, kc, kct, k_sw, vt_sw, proj, nsa_g, batch, seq):
    t = q_rot.shape[0]
    tq = NSA_TILE
    per = seq // tq
    n_h = seq // CMP_STRIDE
    n_cmp = (seq - CMP_LEN) // CMP_STRIDE + 1
    n_slc = seq // SLC_LEN
    n_sel = min(N_SLC, n_slc)
    assert n_slc <= LANE and WINDOW % tq == 0 and tq % LANE == 0
    ovt = jnp.asarray(_overlap_matrix_t(n_h, n_cmp, n_slc))
    eaug = np.zeros((seq, LANE), np.float32)
    eaug[np.arange(seq), np.arange(seq) // SLC_LEN] = 1.0
    eaug = jnp.asarray(eaug, BF16)
    hd = HEAD_DIM
    qw = NSA_REP * hd
    rows = NSA_REP * tq
    seq_spec = lambda off: pl.BlockSpec((seq, hd), lambda b, g, i, off=off: (b, off + g))
    vt_spec = lambda off: pl.BlockSpec((1, 1, per, hd, tq), lambda b, g, i, off=off: (b, off + g, 0, 0, 0))
    kern = functools.partial(_nsa_kernel, n_cmp=n_cmp, n_slc=n_slc, n_sel=n_sel)
    return pl.pallas_call(
        kern,
        grid=(batch, N_NSA_KV, per),
        in_specs=[
            pl.BlockSpec((tq, qw), lambda b, g, i: (b * per + i, g)),
            pl.BlockSpec((1, 1, n_h, hd), lambda b, g, i: (b, g, 0, 0)),
            pl.BlockSpec((1, 1, hd, n_h), lambda b, g, i: (b, N_NSA_KV + g, 0, 0)),
            seq_spec(0), seq_spec(N_NSA_KV), vt_spec(0), vt_spec(N_NSA_KV),
            pl.BlockSpec((tq, LANE), lambda b, g, i: (b * per + i, COL_GATE // LANE + g)),
            pl.BlockSpec((LANE, n_h), lambda b, g, i: (0, 0)),
            pl.BlockSpec((seq, LANE), lambda b, g, i: (0, 0)),
            pl.BlockSpec((1, qw), lambda b, g, i: (0, g)),
        ],
        out_specs=pl.BlockSpec((tq, qw), lambda b, g, i: (b * per + i, g)),
        out_shape=jax.ShapeDtypeStruct((t, D_NSA), BF16),
        scratch_shapes=[
            pltpu.VMEM((seq, 2 * hd), BF16),
            pltpu.VMEM((rows, 2 * hd), BF16),
            pltpu.VMEM((tq, rows), F32),
            pltpu.VMEM((1, rows), F32),
            pltpu.VMEM((1, rows), F32),
            pltpu.VMEM((hd, rows), F32),
            pltpu.VMEM((hd, rows), F32),
        ],
        compiler_params=_cparams(("arbitrary", "arbitrary", "arbitrary")),
        name="nsa_attention",
    )(q_rot, kc, kct, k_sw, k_sw, vt_sw, vt_sw, proj, ovt, eaug, nsa_g)


def _retention_kernel(q_ref, k_ref, v_ref, gate_ref, cos_ref, sin_ref, lg_ref, g_ref, o_ref, state):
    c = q_ref.shape[0]
    hd = HEAD_DIM

    @pl.when(pl.program_id(1) == 0)
    def _():
        state[...] = jnp.zeros(state.shape, F32)

    cos, sin = cos_ref[...], sin_ref[...]
    n_row = lax.broadcasted_iota(jnp.int32, (c, 1), 0).astype(F32)
    n_col = lax.broadcasted_iota(jnp.int32, (1, c), 1).astype(F32)
    diff = n_row - n_col
    for h in range(N_RET_HEADS):
        sl = slice(h * hd, (h + 1) * hd)
        lg = lg_ref[h][:, 0:1]
        q = _rotate(q_ref[:, sl], cos, sin, hd // 2)
        k = _rotate(k_ref[:, sl], cos, sin, hd // 2) * hd ** -0.5
        v = v_ref[:, sl].astype(BF16)
        decay = jnp.where(diff >= 0.0, jnp.exp(jnp.maximum(diff, 0.0) * lg), 0.0)
        scores = _dot_nt(q.astype(BF16), k.astype(BF16)) * decay
        o = _dot(scores.astype(BF16), v)
        xi = jnp.exp((n_row + 1.0) * lg)
        o = o + _dot((q * xi).astype(BF16), state[h].astype(BF16))
        zeta = jnp.exp((c - 1.0 - n_row) * lg)
        kz_t = (k * zeta).T.astype(BF16)
        state[h] = state[h] * jnp.exp(c * lg) + _dot(kz_t, v)
        gate = gate_ref[:, sl]
        o_ref[:, sl] = (gate * jax.nn.sigmoid(gate) * (_rms(o) * g_ref[:, sl])).astype(o_ref.dtype)


def retention(proj, cos, sin, ret_g, batch, seq):
    t = proj.shape[0]
    c = 256
    per = seq // c
    hd = HEAD_DIM
    base = COL_RET // D_RET
    blk = lambda off: pl.BlockSpec((c, D_RET), lambda b, s, off=off: (b * per + s, base + off))
    tab = pl.BlockSpec((c, hd), lambda b, s: (b * per + s, 0))
    lg = jnp.log1p(-(2.0 ** (-5.0 - jnp.arange(N_RET_HEADS, dtype=F32))))
    lg = jnp.broadcast_to(lg[:, None, None], (N_RET_HEADS, 1, LANE))
    return pl.pallas_call(
        _retention_kernel,
        grid=(batch, per),
        in_specs=[blk(0), blk(1), blk(2), blk(3), tab, tab,
                  pl.BlockSpec((N_RET_HEADS, 1, LANE), lambda b, s: (0, 0, 0)),
                  pl.BlockSpec((1, D_RET), lambda b, s: (0, 0))],
        out_specs=pl.BlockSpec((c, D_RET), lambda b, s: (b * per + s, 0)),
        out_shape=jax.ShapeDtypeStruct((t, D_RET), BF16),
        scratch_shapes=[pltpu.VMEM((N_RET_HEADS, hd, hd), F32)],
        compiler_params=_cparams(("arbitrary", "arbitrary")),
        name="retention",
    )(proj, proj, proj, proj, cos, sin, lg, ret_g)


def _out_proj_kernel(yc_ref, yn_ref, yr_ref, w_ref, x_ref, g_ref, o_ref):
    acc = _dot(yc_ref[...], w_ref[0:D_CONV, :])
    acc = acc + _dot(yn_ref[...], w_ref[D_CONV:D_CONV + D_NSA, :])
    acc = acc + _dot(yr_ref[...], w_ref[D_CONV + D_NSA:, :])
    o_ref[...] = x_ref[...] + g_ref[0] * acc


def out_projection(y_conv, y_nsa, y_ret, w_out, x, mod, seq):
    t, d = x.shape
    tm = 512
    per = seq // tm
    row = lambda w: pl.BlockSpec((tm, w), lambda i: (i, 0))
    return pl.pallas_call(
        _out_proj_kernel,
        grid=(t // tm,),
        in_specs=[row(D_CONV), row(D_NSA), row(D_RET),
                  pl.BlockSpec((D_MIX, d), lambda i: (0, 0)),
                  row(d),
                  pl.BlockSpec((1, 1, d), lambda i: (i // per, 0, 2))],
        out_specs=row(d),
        out_shape=jax.ShapeDtypeStruct((t, d), F32),
        compiler_params=_cparams(("arbitrary",)),
        name="out_projection",
    )(y_conv, y_nsa, y_ret, w_out, x, mod)


def _rank_rows(val, n):
    idx = lax.broadcasted_iota(jnp.int32, (n, 1), 0)
    rank = jnp.zeros(val.shape, jnp.int32)
    for rp in range(n):
        vp = val[rp:rp + 1, :]
        ahead = (vp > val) | ((vp == val) & (idx > rp))
        rank = rank + ahead.astype(jnp.int32)
    return rank


def _to_token_rows(ref, val):
    chunks = val.shape[1] // LANE
    for c in range(chunks):
        ref[pl.ds(c, val.shape[0], stride=chunks), :] = val[:, c * LANE:(c + 1) * LANE]


def _from_token_rows(ref, n):
    chunks = ref.shape[0] // n
    return jnp.concatenate([ref[pl.ds(c, n, stride=chunks), :] for c in range(chunks)], axis=1)


def _pack_bf16_pairs(v):
    half = v.shape[1] // 2
    lo = pltpu.bitcast(v[:, :half].astype(BF16).astype(F32), jnp.uint32)
    hi = pltpu.bitcast(v[:, half:].astype(BF16).astype(F32), jnp.uint32)
    return hi | (lo >> 16)


def _unpack_bf16_pairs(u):
    return pltpu.bitcast(u << 16, F32), pltpu.bitcast(u & jnp.uint32(0xFFFF0000), F32)


def _router_kernel(x_ref, sh_ref, sc_ref, g_ref, rw_ref, rb_ref, tri_ref, swgu_ref, swd_ref,
                   shared_ref, hrow_ref, pos_ref, w_ref, cnt_ref, carry):
    @pl.when(pl.program_id(0) == 0)
    def _():
        carry[...] = jnp.zeros(carry.shape, F32)

    h = _norm_mod(x_ref[...], g_ref[...], sc_ref[0], sh_ref[0])
    _to_token_rows(hrow_ref, _pack_bf16_pairs(h))
    shared_ref[...] = _swiglu(h.astype(BF16), swgu_ref[...], swd_ref[...], D_SHARED)
    tm = h.shape[0]
    scores = jax.nn.sigmoid(_dot_nt(rw_ref[...], h, precision=lax.Precision.HIGHEST))
    biased = scores + rb_ref[...]
    sub = lax.broadcasted_iota(jnp.int32, (GROUP_SIZE, 1), 0)
    gs = []
    for g in range(N_EXPERT_GROUPS):
        bg = biased[g * GROUP_SIZE:(g + 1) * GROUP_SIZE, :]
        m1 = jnp.max(bg, axis=0, keepdims=True)
        first = jnp.min(jnp.where(bg == m1, sub, GROUP_SIZE), axis=0, keepdims=True)
        m2 = jnp.max(jnp.where(sub == first, -jnp.inf, bg), axis=0, keepdims=True)
        gs.append(m1 + m2)
    gs = jnp.concatenate(gs, axis=0)
    gkeep = _rank_rows(gs, N_EXPERT_GROUPS) < TOPK_GROUPS
    keep = jnp.concatenate(
        [jnp.broadcast_to(gkeep[g:g + 1, :], (GROUP_SIZE, tm)) for g in range(N_EXPERT_GROUPS)], axis=0)
    masked = jnp.where(keep, biased, -jnp.inf)
    sel = _rank_rows(masked, N_EXPERTS) < TOP_K
    w = jnp.where(sel, scores, 0.0)
    w_ref[...] = ROUTED_SCALE * w / jnp.sum(w, axis=0, keepdims=True)
    self32 = sel.astype(F32)
    pos = carry[:, 0:1] + _dot(self32.astype(BF16), tri_ref[...])
    pos_ref[...] = jnp.where(sel, pos.astype(jnp.int32), -1)
    carry[...] = carry[...] + jnp.sum(self32, axis=1, keepdims=True)
    cnt_ref[...] = carry[...].astype(jnp.int32)


def moe_route(x, mod, norm_g, router_w_t, router_bias, shared_w_gu, shared_w_down, seq):
    t, d = x.shape
    tm = 512
    per = seq // tm
    tri = jnp.asarray(np.triu(np.ones((tm, tm), np.float32), 1), BF16)
    ecol = pl.BlockSpec((N_EXPERTS, tm), lambda i: (0, i))
    return pl.pallas_call(
        _router_kernel,
        grid=(t // tm,),
        in_specs=[
            pl.BlockSpec((tm, d), lambda i: (i, 0)),
            pl.BlockSpec((1, 1, d), lambda i: (i // per, 0, 3)),
            pl.BlockSpec((1, 1, d), lambda i: (i // per, 0, 4)),
            pl.BlockSpec((1, d), lambda i: (0, 0)),
            pl.BlockSpec((N_EXPERTS, d), lambda i: (0, 0)),
            pl.BlockSpec((N_EXPERTS, 1), lambda i: (0, 0)),
            pl.BlockSpec((tm, tm), lambda i: (0, 0)),
            pl.BlockSpec((d, 2 * D_SHARED), lambda i: (0, 0)),
            pl.BlockSpec((D_SHARED, d), lambda i: (0, 0)),
        ],
        out_specs=[pl.BlockSpec((tm, d), lambda i: (i, 0)),
                   pl.BlockSpec((tm * PACK_CHUNKS, LANE), lambda i: (i, 0)), ecol, ecol,
                   pl.BlockSpec((N_EXPERTS, LANE), lambda i: (0, 0))],
        out_shape=[
            jax.ShapeDtypeStruct((t, d), F32),
            jax.ShapeDtypeStruct((t * PACK_CHUNKS, LANE), jnp.uint32),
            jax.ShapeDtypeStruct((N_EXPERTS, t), jnp.int32),
            jax.ShapeDtypeStruct((N_EXPERTS, t), F32),
            jax.ShapeDtypeStruct((N_EXPERTS, LANE), jnp.int32),
        ],
        scratch_shapes=[pltpu.VMEM((N_EXPERTS, LANE), F32)],
        compiler_params=_cparams(("arbitrary",)),
        name="moe_route",
    )(x, mod, mod, norm_g, router_w_t, router_bias, tri, shared_w_gu, shared_w_down)


def _slots_kernel(pos_ref, w_ref, start_ref, tri_ref, slot_ref, w8_ref):
    pos = pos_ref[...]
    sel = pos >= 0
    slot = pos + start_ref[...]
    order = _dot(tri_ref[...], sel.astype(F32).astype(BF16))
    w = w_ref[...]
    for k in range(TOP_K):
        mk = sel & (order == float(k))
        slot_ref[k:k + 1, :] = jnp.sum(jnp.where(mk, slot, 0), axis=0, keepdims=True)
        w8_ref[k:k + 1, :] = jnp.sum(jnp.where(mk, w, 0.0), axis=0, keepdims=True)


def moe_slots(pos_t, w_t, pad_start):
    e, t = pos_t.shape
    tm = 2048
    tri = jnp.asarray(np.tril(np.ones((e, e), np.float32), -1), BF16)
    ecol = pl.BlockSpec((e, tm), lambda i: (0, i))
    kcol = pl.BlockSpec((TOP_K, tm), lambda i: (0, i))
    return pl.pallas_call(
        _slots_kernel,
        grid=(t // tm,),
        in_specs=[ecol, ecol, pl.BlockSpec((e, 1), lambda i: (0, 0)), pl.BlockSpec((e, e), lambda i: (0, 0))],
        out_specs=[kcol, kcol],
        out_shape=[jax.ShapeDtypeStruct((TOP_K, t), jnp.int32), jax.ShapeDtypeStruct((TOP_K, t), F32)],
        compiler_params=_cparams(("arbitrary",)),
        name="moe_slots",
    )(pos_t, w_t, pad_start, tri)


def _inverse_kernel(slot_ref, init_ref, inv_ref, sem, *, n_tok):
    tm = slot_ref.shape[0] // TOP_K
    i = pl.program_id(0)

    @pl.when(i == 0)
    def _():
        cp = pltpu.make_async_copy(init_ref, inv_ref, sem)
        cp.start()
        cp.wait()

    def body(t, carry):
        for k in range(TOP_K):
            inv_ref[slot_ref[t * TOP_K + k]] = k * n_tok + i * tm + t
        return carry

    lax.fori_loop(0, tm, body, 0, unroll=2)


def moe_inverse(slot8, n_slots):
    n_tok = slot8.shape[1]
    tm = 2048
    init = TOP_K * n_tok + jnp.arange(n_slots, dtype=jnp.int32)
    slot_flat = slot8.T.reshape(-1)
    return pl.pallas_call(
        functools.partial(_inverse_kernel, n_tok=n_tok),
        grid=(n_tok // tm,),
        in_specs=[pl.BlockSpec((TOP_K * tm,), lambda i: (i,), memory_space=pltpu.SMEM),
                  pl.BlockSpec(memory_space=pl.ANY)],
        out_specs=pl.BlockSpec(memory_space=pltpu.SMEM),
        out_shape=jax.ShapeDtypeStruct((n_slots,), jnp.int32),
        scratch_shapes=[pltpu.SemaphoreType.DMA(())],
        compiler_params=_cparams(("arbitrary",)),
        name="moe_inverse",
    )(slot_flat, init)


def _swiglu(x_bf16, w_gu, w_down, d_hidden):
    gu = _dot(x_bf16, w_gu)
    gate, up = gu[:, :d_hidden], gu[:, d_hidden:]
    return _dot((gate * jax.nn.sigmoid(gate) * up).astype(BF16), w_down)


def _experts_kernel(be_ref, first_ref, nb_ref, src_ref, dst_ref, h_hbm, wgu_ref, wd_ref, y_hbm,
                    x0, x1, y0, y1, wgu_bf, wd_bf, gsem, ssem, *, n_tok, n_blocks):
    del be_ref
    s = pl.program_id(0)
    cb = s - 1
    used = (cb >= 0) & (cb < nb_ref[0])
    rc = PACK_CHUNKS
    rows = x0.shape[0] // rc
    blocks_per_chunk = SMEM_CHUNK // rows
    src_off = lax.rem(jnp.minimum(s, n_blocks - 1), blocks_per_chunk) * rows
    dst_off = lax.rem(s, blocks_per_chunk) * rows

    @pl.when(s == 0)
    def _():
        y0[...] = jnp.zeros(y0.shape, y0.dtype)
        y1[...] = jnp.zeros(y1.shape, y1.dtype)

    @pl.when(used & (first_ref[jnp.clip(cb, 0, n_blocks - 1)] == 1))
    def _():
        wgu_bf[...] = wgu_ref[0, 0].astype(BF16)
        wd_bf[...] = wd_ref[0, 0].astype(BF16)

    def step(p, x_out, x_in, y_out, y_in):
        def gathered(buf, sem):
            return pltpu.make_async_copy(h_hbm.at[pl.ds(0, rows * rc), :], buf, sem)

        def scattered(buf, sem):
            return pltpu.make_async_copy(buf, y_hbm.at[pl.ds(0, rows * rc), :], sem)

        last_live = nb_ref[0] + 1

        @pl.when((s >= 1) & (s - 1 <= last_live))
        def _():
            gathered(x_in, gsem.at[1 - p]).wait()
            scattered(y_out, ssem.at[1 - p]).wait()

        def issue_rows():
            for r in range(rows):
                src = pl.multiple_of((src_ref[src_off + r] & (n_tok - 1)) * rc, rc)
                dst = pl.multiple_of(dst_ref[dst_off + r] * rc, rc)
                pltpu.make_async_copy(h_hbm.at[pl.ds(src, rc), :], x_out.at[pl.ds(r * rc, rc), :], gsem.at[p]).start()
                pltpu.make_async_copy(y_in.at[pl.ds(r * rc, rc), :], y_hbm.at[pl.ds(dst, rc), :], ssem.at[p]).start()

        @pl.when(used)
        def _():
            issue_rows()
            x = jnp.concatenate(_unpack_bf16_pairs(_from_token_rows(x_in, rows)), axis=1).astype(BF16)
            _to_token_rows(y_out, _pack_bf16_pairs(_swiglu(x, wgu_bf[...], wd_bf[...], D_EXPERT)))

        @pl.when(jnp.logical_not(used) & (s <= last_live))
        def _():
            issue_rows()

        @pl.when((s == pl.num_programs(0) - 1) & (s <= last_live))
        def _():
            gathered(x_out, gsem.at[p]).wait()
            scattered(y_in, ssem.at[p]).wait()

    @pl.when(lax.rem(s, 2) == 0)
    def _():
        step(0, x0, x1, y1, y0)

    @pl.when(lax.rem(s, 2) == 1)
    def _():
        step(1, x1, x0, y0, y1)


def moe_experts(block_e, first, n_used, inv, h_rows, w_gu, w_down, layer):
    n_tok, d = h_rows.shape[0] // PACK_CHUNKS, D_MODEL
    n_slots = inv.shape[0]
    n_blocks = n_slots // MOE_BLOCK
    assert n_tok & (n_tok - 1) == 0, "source token = inv mod T uses a power-of-two T"
    assert n_slots % SMEM_CHUNK == 0 and SMEM_CHUNK % MOE_BLOCK == 0
    per_chunk = SMEM_CHUNK // MOE_BLOCK
    n_rows = TOP_K * n_tok + n_slots + 2 * MOE_BLOCK
    warmup = TOP_K * n_tok + n_slots + jnp.arange(2 * MOE_BLOCK, dtype=jnp.int32)
    tail = jnp.zeros((SMEM_CHUNK - 2 * MOE_BLOCK,), jnp.int32)
    dst = jnp.concatenate([warmup, inv, tail])
    clamp = lambda b: jnp.clip(b, 0, n_blocks - 1)
    smem_blk = lambda f: pl.BlockSpec((SMEM_CHUNK,), f, memory_space=pltpu.SMEM)
    grid_spec = pltpu.PrefetchScalarGridSpec(
        num_scalar_prefetch=3,
        grid=(n_blocks + 2,),
        in_specs=[
            smem_blk(lambda s, be, fi, nb: (clamp(s) // per_chunk,)),
            smem_blk(lambda s, be, fi, nb: (s // per_chunk,)),
            pl.BlockSpec(memory_space=pl.ANY),
            pl.BlockSpec((1, 1, d, 2 * D_EXPERT), lambda s, be, fi, nb: (layer, be[clamp(s - 1)], 0, 0)),
            pl.BlockSpec((1, 1, D_EXPERT, d), lambda s, be, fi, nb: (layer, be[clamp(s - 1)], 0, 0)),
        ],
        out_specs=pl.BlockSpec(memory_space=pl.ANY),
        scratch_shapes=[pltpu.VMEM((MOE_BLOCK * PACK_CHUNKS, LANE), jnp.uint32)] * 4 + [
            pltpu.VMEM((d, 2 * D_EXPERT), BF16), pltpu.VMEM((D_EXPERT, d), BF16),
            pltpu.SemaphoreType.DMA((2,)), pltpu.SemaphoreType.DMA((2,))],
    )
    return pl.pallas_call(
        functools.partial(_experts_kernel, n_tok=n_tok, n_blocks=n_blocks),
        grid_spec=grid_spec,
        out_shape=jax.ShapeDtypeStruct((n_rows * PACK_CHUNKS, LANE), jnp.uint32),
        compiler_params=_cparams(("arbitrary",)),
        name="moe_experts",
    )(block_e, first, n_used, inv, dst, h_rows, w_gu, w_down)


def _combine_kernel(w_ref, sh_ref, x_ref, g_ref, fg_ref, *rest, final_norm):
    y_refs, o_ref, sum_scr = rest[:TOP_K], rest[TOP_K], rest[TOP_K + 1]
    tm = x_ref.shape[0]

    def token(t, carry):
        rows = pl.ds(pl.multiple_of(t * PACK_CHUNKS, PACK_CHUNKS), PACK_CHUNKS)
        lo, hi = _unpack_bf16_pairs(y_refs[0][rows, :])
        acc_lo, acc_hi = w_ref[0, t] * lo, w_ref[0, t] * hi
        for k in range(1, TOP_K):
            lo, hi = _unpack_bf16_pairs(y_refs[k][rows, :])
            acc_lo, acc_hi = acc_lo + w_ref[k, t] * lo, acc_hi + w_ref[k, t] * hi
        base = pl.multiple_of(t * ROW_CHUNKS, ROW_CHUNKS)
        sum_scr[pl.ds(base, PACK_CHUNKS), :] = acc_lo
        sum_scr[pl.ds(base + PACK_CHUNKS, PACK_CHUNKS), :] = acc_hi
        return carry

    lax.fori_loop(0, tm, token, 0, unroll=4)
    routed = _from_token_rows(sum_scr, tm)
    out = x_ref[...] + g_ref[0] * (routed + sh_ref[...])
    if final_norm:
        out = _rms(out) * fg_ref[...]
    o_ref[...] = out


def moe_combine(w8, shared, x, mod, final_g, y, seq, final_norm):
    t, d = x.shape
    tm = 256
    per = seq // tm
    row = pl.BlockSpec((tm, d), lambda i: (i, 0))
    y_specs = [pl.BlockSpec((tm * PACK_CHUNKS, LANE), lambda i, k=k: (k * (t // tm) + i, 0)) for k in range(TOP_K)]
    return pl.pallas_call(
        functools.partial(_combine_kernel, final_norm=final_norm),
        grid=(t // tm,),
        in_specs=[
            pl.BlockSpec((TOP_K, tm), lambda i: (0, i), memory_space=pltpu.SMEM),
            row, row,
            pl.BlockSpec((1, 1, d), lambda i: (i // per, 0, 5)),
            pl.BlockSpec((1, d), lambda i: (0, 0)),
        ] + y_specs,
        out_specs=row,
        out_shape=jax.ShapeDtypeStruct((t, d), F32),
        scratch_shapes=[pltpu.VMEM((tm * ROW_CHUNKS, LANE), F32)],
        compiler_params=_cparams(("arbitrary",)),
        name="moe_combine",
    )(w8, shared, x, mod, final_g, *([y] * TOP_K))


def token_mixer_layer(x, mod, tabs, p, batch, seq):
    cos_n, sin_n, cos_r, sin_r = tabs
    proj = in_projection(x, mod, p["norm1_g"], p["w_in"], seq)
    y_conv = short_conv(proj, p["conv_w_t"], p["conv_g"], batch, seq)
    q_rot, kvc, k_sw, vt_sw = nsa_rope(proj, cos_n, sin_n, batch, seq)
    n_h = seq // CMP_STRIDE
    kvc_h = kvc.reshape(batch, seq, 2 * N_NSA_KV, HEAD_DIM).transpose(0, 2, 1, 3)
    kvc_h = kvc_h.reshape(batch, 2 * N_NSA_KV, n_h, CMP_STRIDE * HEAD_DIM)
    kc, kct = compress(kvc_h, p["cmp_pe"], p["cmp_w1"], p["cmp_w2"])
    y_nsa = nsa_attention(q_rot, kc, kct, k_sw, vt_sw, proj, p["nsa_g"], batch, seq)
    y_ret = retention(proj, cos_r, sin_r, p["ret_g"], batch, seq)
    return out_projection(y_conv, y_nsa, y_ret, p["w_out"], x, mod, seq)


def moe_layer(x, mod, p, seq, final_g, final_norm):
    t, d = x.shape
    n_assign = t * TOP_K
    n_blocks = (n_assign + N_EXPERTS * (MOE_BLOCK - 1) + MOE_BLOCK - 1) // MOE_BLOCK
    shared, h_rows, pos_t, w_t, counts = moe_route(x, mod, p["norm2_g"], p["router_w_t"], p["router_bias"],
                                                   p["shared_w_gu"], p["shared_w_down"], seq)
    counts = counts[:, 0]
    padded = (counts + MOE_BLOCK - 1) // MOE_BLOCK * MOE_BLOCK
    pad_end = jnp.cumsum(padded)
    pad_start = (pad_end - padded).astype(jnp.int32)
    blk0 = jnp.arange(n_blocks, dtype=jnp.int32) * MOE_BLOCK
    block_e = jnp.minimum(jnp.sum(pad_end[None, :] <= blk0[:, None], axis=1), N_EXPERTS - 1).astype(jnp.int32)
    n_used = (pad_end[-1:] // MOE_BLOCK).astype(jnp.int32)
    first = jnp.concatenate([jnp.ones((1,), jnp.int32), (block_e[1:] != block_e[:-1]).astype(jnp.int32)])
    slot8, w8 = moe_slots(pos_t, w_t, pad_start[:, None])
    inv = moe_inverse(slot8, n_blocks * MOE_BLOCK)
    y = moe_experts(block_e, first, n_used, inv, h_rows, p["exp_w_gu"], p["exp_w_down"], p["layer"])
    return moe_combine(w8, shared, x, mod, final_g, y, seq, final_norm)


def _reorder_w_in(w_in):
    c_gate = COL_KV + 6 * D_NSA_KV
    gates = w_in[:, c_gate:c_gate + 3 * N_NSA_HEADS]
    per_group = 3 * NSA_REP
    padded = [jnp.pad(gates[:, g * per_group:(g + 1) * per_group], ((0, 0), (0, LANE - per_group)))
              for g in range(N_NSA_KV)]
    return jnp.concatenate([w_in[:, :c_gate], w_in[:, c_gate + 3 * N_NSA_HEADS:]] + padded, axis=1)


def kernel(x, c, positions, ada_w, ada_b, norm1_g, norm2_g, w_in, conv_w, conv_g, cmp_pe, cmp_w1, cmp_w2,
           nsa_g, ret_g, w_out, router_w, router_bias, exp_w_gu, exp_w_down, shared_w_gu, shared_w_down, final_g):
    batch, seq, d = x.shape
    depth = ada_w.shape[0]
    t = batch * seq
    mod_all = ada_modulation(c, ada_w, ada_b)
    pos_col = positions.reshape(t, 1)
    tabs = rope_tables(pos_col, ROPE_DIM, ROPE_THETA) + rope_tables(pos_col, HEAD_DIM, RET_THETA)
    xt = x.reshape(t, d)
    final_g2 = final_g.reshape(1, d)
    for l in range(depth):
        mod = mod_all[l].reshape(batch, 1, ADA_CHUNKS * d)
        p = dict(
            norm1_g=norm1_g[l].reshape(1, d),
            norm2_g=norm2_g[l].reshape(1, d),
            w_in=_reorder_w_in(w_in[l]).astype(BF16),
            conv_w_t=conv_w[l].T,
            conv_g=conv_g[l].reshape(1, D_CONV),
            cmp_pe=cmp_pe[l].reshape(2, 1, CMP_LEN * HEAD_DIM),
            cmp_w1=cmp_w1[l].astype(BF16),
            cmp_w2=cmp_w2[l].astype(BF16),
            nsa_g=nsa_g[l].reshape(1, D_NSA),
            ret_g=ret_g[l].reshape(1, D_RET),
            w_out=w_out[l].astype(BF16),
            router_w_t=router_w[l].T,
            router_bias=router_bias[l].reshape(N_EXPERTS, 1),
            layer=l,
            exp_w_gu=exp_w_gu,
            exp_w_down=exp_w_down,
            shared_w_gu=shared_w_gu[l].astype(BF16),
            shared_w_down=shared_w_down[l].astype(BF16),
        )
        xt = token_mixer_layer(xt, mod, tabs, p, batch, seq)
        xt = moe_layer(xt, mod, p, seq, final_g2, final_norm=(l == depth - 1))
    return xt.reshape(batch, seq, d)
```

```python
import functools

import numpy as np
import jax
import jax.numpy as jnp
from jax import lax
from jax.experimental import pallas as pl
from jax.experimental.pallas import tpu as pltpu

F32 = jnp.float32
BF16 = jnp.bfloat16

D_MODEL = 2048
HEAD_DIM = 128
D_CONV = D_MODEL // 4
CONV_GROUPS = D_CONV // HEAD_DIM
CONV_WIDTH = 3
N_NSA_HEADS = D_MODEL // 2 // HEAD_DIM
N_NSA_KV = 2
NSA_REP = N_NSA_HEADS // N_NSA_KV
D_NSA = N_NSA_HEADS * HEAD_DIM
D_NSA_KV = N_NSA_KV * HEAD_DIM
CMP_LEN = 32
CMP_STRIDE = 16
CMP_HIDDEN = 256
SLC_LEN = 64
N_SLC = 16
WINDOW = 512
ROPE_THETA = 500000.0
ROPE_DIM = HEAD_DIM // 4
N_RET_HEADS = D_MODEL // 4 // HEAD_DIM
D_RET = N_RET_HEADS * HEAD_DIM
RET_THETA = 10000.0
D_MIX = D_CONV + D_NSA + D_RET
N_EXPERTS = 64
N_EXPERT_GROUPS = 8
GROUP_SIZE = N_EXPERTS // N_EXPERT_GROUPS
TOPK_GROUPS = 4
TOP_K = 8
D_EXPERT = 512
D_SHARED = 512
ROUTED_SCALE = 2.5
MOE_BLOCK = 256
ADA_CHUNKS = 6
EPS = 1e-6
NEG_INF = -1e30
FORCE_SCORE = 1e4

LANE = 128
ROW_CHUNKS = D_MODEL // LANE
PACK_CHUNKS = ROW_CHUNKS // 2
SMEM_CHUNK = 1024
LOG2_E = 1.4426950408889634
NSA_TILE = 256
GATE_COLS = N_NSA_KV * LANE
D_PROJ = 3 * D_CONV + D_NSA + 6 * D_NSA_KV + 4 * D_RET + GATE_COLS
COL_Q = 3 * D_CONV
COL_KV = COL_Q + D_NSA
COL_RET = COL_KV + 6 * D_NSA_KV
COL_GATE = COL_RET + 4 * D_RET

VMEM_LIMIT = 56 * 1024 * 1024


def _cparams(sem):
    return pltpu.CompilerParams(dimension_semantics=sem, vmem_limit_bytes=VMEM_LIMIT)


def _dot(a, b, **kw):
    return jnp.dot(a, b, preferred_element_type=F32, **kw)


def _dot_nt(a, b, **kw):
    return lax.dot_general(a, b, (((1,), (1,)), ((), ())), preferred_element_type=F32, **kw)


def _rms(x):
    return x * lax.rsqrt(jnp.mean(x * x, axis=-1, keepdims=True) + EPS)


def _ada_kernel(c_ref, w_ref, b_ref, o_ref):
    c = c_ref[...]
    ca = (c * jax.nn.sigmoid(c)).astype(BF16)
    o_ref[0] = _dot(ca, w_ref[0].astype(BF16)) + b_ref[0]


def ada_modulation(c, ada_w, ada_b):
    L, d, n = ada_w.shape
    b = c.shape[0]
    tn = 1024
    return pl.pallas_call(
        _ada_kernel,
        grid=(L, n // tn),
        in_specs=[
            pl.BlockSpec((b, d), lambda l, j: (0, 0)),
            pl.BlockSpec((1, d, tn), lambda l, j: (l, 0, j)),
            pl.BlockSpec((1, 1, tn), lambda l, j: (l, 0, j)),
        ],
        out_specs=pl.BlockSpec((1, b, tn), lambda l, j: (l, 0, j)),
        out_shape=jax.ShapeDtypeStruct((L, b, n), F32),
        compiler_params=_cparams(("arbitrary", "arbitrary")),
        name="ada_modulation",
    )(c, ada_w, ada_b.reshape(L, 1, n))


def _norm_mod(x, g, sc, sh):
    return (_rms(x) * g) * (1.0 + sc) + sh


def _in_proj_kernel(x_ref, sh_ref, sc_ref, g_ref, w_ref, o_ref, h_scr):
    @pl.when(pl.program_id(1) == 0)
    def _():
        h_scr[...] = _norm_mod(x_ref[...], g_ref[...], sc_ref[0], sh_ref[0]).astype(BF16)

    o_ref[...] = _dot(h_scr[...], w_ref[...]).astype(o_ref.dtype)


def in_projection(x, mod, norm_g, w, seq):
    t, d = x.shape
    n = w.shape[1]
    tm, tn = 512, 1280
    per = seq // tm
    return pl.pallas_call(
        _in_proj_kernel,
        grid=(t // tm, n // tn),
        in_specs=[
            pl.BlockSpec((tm, d), lambda i, j: (i, 0)),
            pl.BlockSpec((1, 1, d), lambda i, j: (i // per, 0, 0)),
            pl.BlockSpec((1, 1, d), lambda i, j: (i // per, 0, 1)),
            pl.BlockSpec((1, d), lambda i, j: (0, 0)),
            pl.BlockSpec((d, tn), lambda i, j: (0, j)),
        ],
        out_specs=pl.BlockSpec((tm, tn), lambda i, j: (i, j)),
        out_shape=jax.ShapeDtypeStruct((t, n), BF16),
        scratch_shapes=[pltpu.VMEM((tm, d), BF16)],
        compiler_params=_cparams(("arbitrary", "arbitrary")),
        name="in_projection",
    )(x, mod, mod, norm_g, w)


def _rope_table_kernel(pos_ref, inv_ref, sgn_ref, cos_ref, sin_ref):
    ang = pos_ref[...].astype(F32) * inv_ref[...]
    cos_ref[...] = jnp.cos(ang)
    sin_ref[...] = jnp.sin(ang) * sgn_ref[...]


def rope_tables(pos_col, rot_dim, theta):
    t = pos_col.shape[0]
    half = rot_dim // 2
    inv_half = theta ** (-jnp.arange(half, dtype=F32) / half)
    inv = jnp.concatenate([inv_half, inv_half, jnp.zeros((LANE - rot_dim,), F32)]).reshape(1, LANE)
    sgn = np.zeros((1, LANE), np.float32)
    sgn[0, :half] = -1.0
    sgn[0, half:rot_dim] = 1.0
    ts = 1024
    return pl.pallas_call(
        _rope_table_kernel,
        grid=(t // ts,),
        in_specs=[
            pl.BlockSpec((ts, 1), lambda i: (i, 0)),
            pl.BlockSpec((1, LANE), lambda i: (0, 0)),
            pl.BlockSpec((1, LANE), lambda i: (0, 0)),
        ],
        out_specs=[pl.BlockSpec((ts, LANE), lambda i: (i, 0))] * 2,
        out_shape=[jax.ShapeDtypeStruct((t, LANE), F32)] * 2,
        compiler_params=_cparams(("arbitrary",)),
        name="rope_tables",
    )(pos_col, inv, jnp.asarray(sgn))


def _rotate(x, cos, sin_signed, half):
    if 2 * half == LANE:
        swapped = pltpu.roll(x, half, 1)
    else:
        lane = lax.broadcasted_iota(jnp.int32, x.shape, 1)
        swapped = jnp.where(lane < half, pltpu.roll(x, LANE - half, 1), pltpu.roll(x, half, 1))
    return x * cos + swapped * sin_signed


def _nsa_rope_kernel(q0_ref, q1_ref, kvc_ref, kvs_ref, kvw_ref, cos_ref, sin_ref,
                     q_out, kvc_out, k_out, vt_out):
    cos, sin = cos_ref[...], sin_ref[...]
    half = ROPE_DIM // 2
    hpb = D_CONV // HEAD_DIM
    scale = HEAD_DIM ** -0.5 * LOG2_E
    for blk, src in enumerate((q0_ref, q1_ref)):
        for h in range(hpb):
            sl = slice(h * HEAD_DIM, (h + 1) * HEAD_DIM)
            q_out[:, blk * D_CONV + h * HEAD_DIM: blk * D_CONV + (h + 1) * HEAD_DIM] = (
                (_rotate(src[:, sl].astype(F32), cos, sin, half) * scale).astype(q_out.dtype))
    for g in range(N_NSA_KV):
        sl = slice(g * HEAD_DIM, (g + 1) * HEAD_DIM)
        kvc_out[:, sl] = _rotate(kvc_ref[:, sl].astype(F32), cos, sin, half)
    kvc_out[:, D_NSA_KV:] = kvc_ref[:, D_NSA_KV:].astype(F32)
    for br, src in enumerate((kvs_ref, kvw_ref)):
        for g in range(N_NSA_KV):
            sl = slice(g * HEAD_DIM, (g + 1) * HEAD_DIM)
            dst = slice((br * N_NSA_KV + g) * HEAD_DIM, (br * N_NSA_KV + g + 1) * HEAD_DIM)
            k_out[:, dst] = _rotate(src[:, sl].astype(F32), cos, sin, half).astype(k_out.dtype)
            vsl = slice(D_NSA_KV + g * HEAD_DIM, D_NSA_KV + (g + 1) * HEAD_DIM)
            for tile in range(vt_out.shape[2]):
                rows = slice(tile * NSA_TILE, (tile + 1) * NSA_TILE)
                vt_out[0, br * N_NSA_KV + g, tile] = src[rows, vsl].astype(F32).T.astype(vt_out.dtype)


def nsa_rope(proj, cos, sin, batch, seq):
    t = proj.shape[0]
    ts = 512
    per = seq // ts
    w = D_CONV
    blk = lambda j: pl.BlockSpec((ts, w), lambda i, j=j: (i, j))
    tab = pl.BlockSpec((ts, LANE), lambda i: (i, 0))
    out = pl.BlockSpec((ts, w), lambda i: (i, 0))
    qb = COL_Q // w
    kb = COL_KV // w
    return pl.pallas_call(
        _nsa_rope_kernel,
        grid=(t // ts,),
        in_specs=[blk(qb), blk(qb + 1), blk(kb), blk(kb + 1), blk(kb + 2), tab, tab],
        out_specs=[pl.BlockSpec((ts, D_NSA), lambda i: (i, 0)), out, out,
                   pl.BlockSpec((1, 2 * N_NSA_KV, ts // NSA_TILE, HEAD_DIM, NSA_TILE),
                                lambda i: (i // per, 0, i % per, 0, 0))],
        out_shape=[
            jax.ShapeDtypeStruct((t, D_NSA), BF16),
            jax.ShapeDtypeStruct((t, w), F32),
            jax.ShapeDtypeStruct((t, w), BF16),
            jax.ShapeDtypeStruct((batch, 2 * N_NSA_KV, seq // NSA_TILE, HEAD_DIM, NSA_TILE), BF16),
        ],
        compiler_params=_cparams(("arbitrary",)),
        name="nsa_rope",
    )(proj, proj, proj, proj, proj, cos, sin)


def _conv_kernel(cb_ref, cc_ref, cu_ref, w_ref, g_ref, o_ref, ext):
    ts = cb_ref.shape[0]

    @pl.when(pl.program_id(1) == 0)
    def _():
        ext[0:8, :] = jnp.zeros((8, ext.shape[1]), F32)

    v = cc_ref[...].astype(F32) * cu_ref[...].astype(F32)
    ext[8:, :] = v
    v1 = ext[pl.ds(7, ts), :]
    v2 = ext[pl.ds(6, ts), :]
    y = cb_ref[...].astype(F32) * (w_ref[0:1, :] * v2 + w_ref[1:2, :] * v1 + w_ref[2:3, :] * v)
    ext[0:8, :] = v[ts - 8:, :]
    for gi in range(CONV_GROUPS):
        sl = slice(gi * HEAD_DIM, (gi + 1) * HEAD_DIM)
        o_ref[:, sl] = (_rms(y[:, sl]) * g_ref[:, sl]).astype(o_ref.dtype)


def short_conv(proj, conv_w_t, conv_g, batch, seq):
    t = proj.shape[0]
    ts = 512
    per = seq // ts
    w = D_CONV
    blk = lambda j: pl.BlockSpec((ts, w), lambda b, s, j=j: (b * per + s, j))
    return pl.pallas_call(
        _conv_kernel,
        grid=(batch, per),
        in_specs=[blk(0), blk(1), blk(2),
                  pl.BlockSpec((CONV_WIDTH, w), lambda b, s: (0, 0)),
                  pl.BlockSpec((1, w), lambda b, s: (0, 0))],
        out_specs=pl.BlockSpec((ts, w), lambda b, s: (b * per + s, 0)),
        out_shape=jax.ShapeDtypeStruct((t, w), BF16),
        scratch_shapes=[pltpu.VMEM((8 + ts, w), F32)],
        compiler_params=_cparams(("arbitrary", "arbitrary")),
        name="short_conv",
    )(proj, proj, proj, conv_w_t, conv_g)


def _compress_kernel(kv_ref, pe_ref, w1_ref, w2_ref, o_ref, ot_ref):
    h = _from_token_rows(kv_ref, kv_ref.shape[0] // CMP_STRIDE)
    n_h, half = h.shape
    pe = pe_ref[0]
    a = _dot((h + pe[:, :half]).astype(BF16), w1_ref[0, :half, :])
    b = _dot((h + pe[:, half:]).astype(BF16), w1_ref[0, half:, :])
    pre = a + pltpu.roll(b, n_h - 1, 0)
    out = _dot(jax.nn.gelu(pre).astype(BF16), w2_ref[0])
    o_ref[0, 0] = out.astype(o_ref.dtype)
    ot_ref[0, 0] = out.T.astype(ot_ref.dtype)


def compress(kvc, pe_flat, w1, w2, batch, seq):
    b, four = batch, 2 * N_NSA_KV
    n_h, dh = seq // CMP_STRIDE, CMP_STRIDE * HEAD_DIM
    return pl.pallas_call(
        _compress_kernel,
        grid=(b, four),
        in_specs=[
            pl.BlockSpec((seq, HEAD_DIM), lambda i, j: (i, j)),
            pl.BlockSpec((1, 1, 2 * dh), lambda i, j: (j // N_NSA_KV, 0, 0)),
            pl.BlockSpec((1, 2 * dh, CMP_HIDDEN), lambda i, j: (j // N_NSA_KV, 0, 0)),
            pl.BlockSpec((1, CMP_HIDDEN, HEAD_DIM), lambda i, j: (j // N_NSA_KV, 0, 0)),
        ],
        out_specs=[pl.BlockSpec((1, 1, n_h, HEAD_DIM), lambda i, j: (i, j, 0, 0)),
                   pl.BlockSpec((1, 1, HEAD_DIM, n_h), lambda i, j: (i, j, 0, 0))],
        out_shape=[jax.ShapeDtypeStruct((b, four, n_h, HEAD_DIM), BF16),
                   jax.ShapeDtypeStruct((b, four, HEAD_DIM, n_h), BF16)],
        compiler_params=_cparams(("arbitrary", "arbitrary")),
        name="nsa_compress",
    )(kvc, pe_flat, w1, w2)


def _nsa_kernel(q_ref, kc_ref, vct_ref, ks_ref, kw_ref, vst_ref, vwt_ref, gate_ref, ovt_ref, eaug_ref, g_ref,
                o_ref, kaug_scr, qaug_scr, s_scr, m_scr, l_scr, acc_scr, out_scr, *, n_cmp, n_slc, n_sel):
    tq = q_ref.shape[0]
    rows = NSA_REP * tq
    i = pl.program_id(2)
    t0 = i * tq
    hd = HEAD_DIM

    @pl.when(i == 0)
    def _():
        kaug_scr[:, :hd] = ks_ref[...]
        kaug_scr[:, hd:] = eaug_ref[...]

    for r in range(NSA_REP):
        qaug_scr[r * tq:(r + 1) * tq, :hd] = q_ref[:, r * hd:(r + 1) * hd]

    t_all =t0 + lax.rem(lax.broadcasted_iota(jnp.int32, (1, rows), 1), tq)
    gates_t = jax.nn.sigmoid(gate_ref[...].astype(F32)).T
    gate_rows = [jnp.concatenate([gates_t[3 * r + br:3 * r + br + 1, :] for r in range(NSA_REP)], axis=1)
                 for br in range(3)]

    key_sub = lax.broadcasted_iota(jnp.int32, (tq, 1), 0)
    causal = t0 + key_sub <= t_all

    n_back = WINDOW // tq
    pieces = []
    for back in range(n_back, -1, -1):
        jt = jnp.maximum(i - back, 0)
        sj = _dot_nt(kw_ref[pl.ds(pl.multiple_of(jt * tq, tq), tq), :], qaug_scr[:, :hd])
        if back == 0:
            sj = jnp.where(causal, sj, NEG_INF)
        elif back == n_back:
            sj = jnp.where((jt * tq + key_sub > t_all - WINDOW) & (i >= back), sj, NEG_INF)
        else:
            sj = jnp.where(i >= back, sj, NEG_INF)
        pieces.append((sj, vwt_ref[0, 0, jt]))
    m_w = pieces[0][0].max(axis=0, keepdims=True)
    for sj, _ in pieces[1:]:
        m_w = jnp.maximum(m_w, sj.max(axis=0, keepdims=True))
    l_w = None
    acc_w = None
    for sj, vt_tile in pieces:
        p = jnp.exp2(sj - m_w)
        lj = jnp.sum(p, axis=0, keepdims=True)
        aj = _dot(vt_tile, p.astype(BF16))
        l_w = lj if l_w is None else l_w + lj
        acc_w = aj if acc_w is None else acc_w + aj
    out_scr[...] = (gate_rows[2] / l_w) * acc_w

    kc = kc_ref[0, 0]
    vct = vct_ref[0, 0]
    n_h = kc.shape[0]
    c_idx = lax.broadcasted_iota(jnp.int32, (n_h, 1), 0)
    c_end = jnp.where(c_idx < n_cmp, c_idx * CMP_STRIDE + (CMP_LEN - 1), jnp.iinfo(jnp.int32).max)
    s = _dot_nt(kc, qaug_scr[:, :hd])
    sm = jnp.where(c_end <= t_all, s, NEG_INF)
    e = jnp.exp2(sm - jnp.max(sm, axis=0, keepdims=True))
    inv = jnp.where(t_all >= CMP_LEN - 1, 1.0 / jnp.sum(e, axis=0, keepdims=True), 0.0)
    p = e * inv
    out_scr[...] += gate_rows[0] * _dot(vct, p.astype(BF16))
    psum_t = p[:, 0:tq]
    for r in range(1, NSA_REP):
        psum_t = psum_t + p[:, r * tq:(r + 1) * tq]
    imp_t =_dot(ovt_ref[...], psum_t, precision=lax.Precision.HIGHEST)

    j_idx = lax.broadcasted_iota(jnp.int32, (n_slc, 1), 0)
    jq = (t0 + lax.broadcasted_iota(jnp.int32, (1, tq), 1)) // SLC_LEN
    forced = (j_idx == 0) | (j_idx == jq) | (j_idx == jq - 1)
    val = jnp.where(forced, FORCE_SCORE, jnp.where(j_idx <= jq, imp_t[:n_slc], -1.0))
    bias_t = jnp.full((n_slc, tq), NEG_INF, F32)
    for _ in range(n_sel):
        top = jnp.max(val, axis=0, keepdims=True)
        first = jnp.min(jnp.where(val == top, j_idx, n_slc), axis=0, keepdims=True)
        hit = j_idx == first
        bias_t = jnp.where(hit, 0.0, bias_t)
        val = jnp.where(hit, -jnp.inf, val)
    if n_slc < LANE:
        bias_t = jnp.concatenate([bias_t, jnp.zeros((LANE - n_slc, tq), F32)], axis=0)
    bias = bias_t.T.astype(BF16)
    for r in range(NSA_REP):
        qaug_scr[r * tq:(r + 1) * tq, hd:] = bias

    def sel_scores(jt):
        k0 = pl.multiple_of(jt * tq, tq)
        return _dot_nt(kaug_scr[pl.ds(k0, tq), :], qaug_scr[...])

    def sel_update(s, vt_tile, mask):
        if mask is not None:
            s = jnp.where(mask, s, NEG_INF)
        m_old = m_scr[...]
        m_new = jnp.maximum(m_old, jnp.max(s, axis=0, keepdims=True))
        alpha = jnp.exp2(m_old - m_new)
        p = jnp.exp2(s - m_new)
        l_scr[...] = alpha * l_scr[...] + jnp.sum(p, axis=0, keepdims=True)
        acc_scr[...] = alpha * acc_scr[...] + _dot(vt_tile, p.astype(BF16))
        m_scr[...] = m_new

    m_scr[...] = jnp.full(m_scr.shape, NEG_INF, F32)
    l_scr[...] = jnp.zeros(l_scr.shape, F32)
    acc_scr[...] = jnp.zeros(acc_scr.shape, F32)
    s_scr[...] = sel_scores(0)

    def sel_body(jt, carry):
        s_cur = s_scr[...]
        s_next = sel_scores(jt + 1)
        sel_update(s_cur, vst_ref[0, 0, jt], None)
        s_scr[...] = s_next
        return carry

    lax.fori_loop(0, i, sel_body, 0)
    sel_update(s_scr[...], vst_ref[0, 0, i], causal)
    out_scr[...] += (gate_rows[1] / l_scr[...]) * acc_scr[...]


    for r in range(NSA_REP):
        o_t = out_scr[:, r * tq:(r + 1) * tq]
        o_t = o_t * lax.rsqrt(jnp.mean(o_t * o_t, axis=0, keepdims=True) + EPS)
        sl = slice(r * hd, (r + 1) * hd)
        o_ref[:, sl] = (o_t.T * g_ref[:, sl]).astype(o_ref.dtype)


def _overlap_matrix_t(n_h, n_cmp, n_slc):
    cs = np.arange(n_cmp) * CMP_STRIDE
    js = np.arange(n_slc) * SLC_LEN
    ov = np.minimum(cs[:, None] + CMP_LEN, js[None, :] + SLC_LEN) - np.maximum(cs[:, None], js[None, :])
    out = np.zeros((LANE, n_h), np.float32)
    out[:n_slc, :n_cmp] = (np.clip(ov, 0, None) / CMP_LEN).T
    return out


def nsa_attention(q_rot, kc, kct, k_sw, vt_sw, proj, nsa_g, batch, seq):
    t = q_rot.shape[0]
    tq = NSA_TILE
    per = seq // tq
    n_h = seq // CMP_STRIDE
    n_cmp = (seq - CMP_LEN) // CMP_STRIDE + 1
    n_slc = seq // SLC_LEN
    n_sel = min(N_SLC, n_slc)
    assert n_slc <= LANE and WINDOW % tq == 0 and tq % LANE == 0
    ovt = jnp.asarray(_overlap_matrix_t(n_h, n_cmp, n_slc))
    eaug = np.zeros((seq, LANE), np.float32)
    eaug[np.arange(seq), np.arange(seq) // SLC_LEN] = 1.0
    eaug = jnp.asarray(eaug, BF16)
    hd = HEAD_DIM
    qw = NSA_REP * hd
    rows = NSA_REP * tq
    seq_spec = lambda off: pl.BlockSpec((seq, hd), lambda b, g, i, off=off: (b, off + g))
    vt_spec = lambda off: pl.BlockSpec((1, 1, per, hd, tq), lambda b, g, i, off=off: (b, off + g, 0, 0, 0))
    kern = functools.partial(_nsa_kernel, n_cmp=n_cmp, n_slc=n_slc, n_sel=n_sel)
    return pl.pallas_call(
        kern,
        grid=(batch, N_NSA_KV, per),
        in_specs=[
            pl.BlockSpec((tq, qw), lambda b, g, i: (b * per + i, g)),
            pl.BlockSpec((1, 1, n_h, hd), lambda b, g, i: (b, g, 0, 0)),
            pl.BlockSpec((1, 1, hd, n_h), lambda b, g, i: (b, N_NSA_KV + g, 0, 0)),
            seq_spec(0), seq_spec(N_NSA_KV), vt_spec(0), vt_spec(N_NSA_KV),
            pl.BlockSpec((tq, LANE), lambda b, g, i: (b * per + i, COL_GATE // LANE + g)),
            pl.BlockSpec((LANE, n_h), lambda b, g, i: (0, 0)),
            pl.BlockSpec((seq, LANE), lambda b, g, i: (0, 0)),
            pl.BlockSpec((1, qw), lambda b, g, i: (0, g)),
        ],
        out_specs=pl.BlockSpec((tq, qw), lambda b, g, i: (b * per + i, g)),
        out_shape=jax.ShapeDtypeStruct((t, D_NSA), BF16),
        scratch_shapes=[
            pltpu.VMEM((seq, 2 * hd), BF16),
            pltpu.VMEM((rows, 2 * hd), BF16),
            pltpu.VMEM((tq, rows), F32),
            pltpu.VMEM((1, rows), F32),
            pltpu.VMEM((1, rows), F32),
            pltpu.VMEM((hd, rows), F32),
            pltpu.VMEM((hd, rows), F32),
        ],
        compiler_params=_cparams(("arbitrary", "arbitrary", "arbitrary")),
        name="nsa_attention",
    )(q_rot, kc, kct, k_sw, k_sw, vt_sw, vt_sw, proj, ovt, eaug, nsa_g)


def _retention_kernel(q_ref, k_ref, v_ref, gate_ref, cos_ref, sin_ref, lg_ref, g_ref, o_ref, state):
    c = q_ref.shape[0]
    hd = HEAD_DIM

    @pl.when(pl.program_id(1) == 0)
    def _():
        state[...] = jnp.zeros(state.shape, F32)

    cos, sin = cos_ref[...], sin_ref[...]
    n_row = lax.broadcasted_iota(jnp.int32, (c, 1), 0).astype(F32)
    n_col = lax.broadcasted_iota(jnp.int32, (1, c), 1).astype(F32)
    diff = n_row - n_col
    for h in range(N_RET_HEADS):
        sl = slice(h * hd, (h + 1) * hd)
        lg = lg_ref[h][:, 0:1]
        q = _rotate(q_ref[:, sl].astype(F32), cos, sin, hd // 2)
        k = _rotate(k_ref[:, sl].astype(F32), cos, sin, hd // 2) * hd ** -0.5
        v = v_ref[:, sl].astype(BF16)
        decay = jnp.where(diff >= 0.0, jnp.exp(jnp.maximum(diff, 0.0) * lg), 0.0)
        scores = _dot_nt(q.astype(BF16), k.astype(BF16)) * decay
        o = _dot(scores.astype(BF16), v)
        xi = jnp.exp((n_row + 1.0) * lg)
        o = o + _dot((q * xi).astype(BF16), state[h].astype(BF16))
        zeta = jnp.exp((c - 1.0 - n_row) * lg)
        kz_t = (k * zeta).T.astype(BF16)
        state[h] = state[h] * jnp.exp(c * lg) + _dot(kz_t, v)
        gate = gate_ref[:, sl].astype(F32)
        o_ref[:, sl] = (gate * jax.nn.sigmoid(gate) * (_rms(o) * g_ref[:, sl])).astype(o_ref.dtype)


def retention(proj, cos, sin, ret_g, batch, seq):
    t = proj.shape[0]
    c = 256
    per = seq // c
    hd = HEAD_DIM
    base = COL_RET // D_RET
    blk = lambda off: pl.BlockSpec((c, D_RET), lambda b, s, off=off: (b * per + s, base + off))
    tab = pl.BlockSpec((c, hd), lambda b, s: (b * per + s, 0))
    lg = jnp.log1p(-(2.0 ** (-5.0 - jnp.arange(N_RET_HEADS, dtype=F32))))
    lg = jnp.broadcast_to(lg[:, None, None], (N_RET_HEADS, 1, LANE))
    return pl.pallas_call(
        _retention_kernel,
        grid=(batch, per),
        in_specs=[blk(0), blk(1), blk(2), blk(3), tab, tab,
                  pl.BlockSpec((N_RET_HEADS, 1, LANE), lambda b, s: (0, 0, 0)),
                  pl.BlockSpec((1, D_RET), lambda b, s: (0, 0))],
        out_specs=pl.BlockSpec((c, D_RET), lambda b, s: (b * per + s, 0)),
        out_shape=jax.ShapeDtypeStruct((t, D_RET), BF16),
        scratch_shapes=[pltpu.VMEM((N_RET_HEADS, hd, hd), F32)],
        compiler_params=_cparams(("arbitrary", "arbitrary")),
        name="retention",
    )(proj, proj, proj, proj, cos, sin, lg, ret_g)


def _out_proj_kernel(yc_ref, yn_ref, yr_ref, w_ref, x_ref, g_ref, o_ref):
    acc = _dot(yc_ref[...], w_ref[0:D_CONV, :])
    acc = acc + _dot(yn_ref[...], w_ref[D_CONV:D_CONV + D_NSA, :])
    acc = acc + _dot(yr_ref[...], w_ref[D_CONV + D_NSA:, :])
    o_ref[...] = x_ref[...] + g_ref[0] * acc


def out_projection(y_conv, y_nsa, y_ret, w_out, x, mod, seq):
    t, d = x.shape
    tm = 512
    per = seq // tm
    row = lambda w: pl.BlockSpec((tm, w), lambda i: (i, 0))
    return pl.pallas_call(
        _out_proj_kernel,
        grid=(t // tm,),
        in_specs=[row(D_CONV), row(D_NSA), row(D_RET),
                  pl.BlockSpec((D_MIX, d), lambda i: (0, 0)),
                  row(d),
                  pl.BlockSpec((1, 1, d), lambda i: (i // per, 0, 2))],
        out_specs=row(d),
        out_shape=jax.ShapeDtypeStruct((t, d), F32),
        compiler_params=_cparams(("arbitrary",)),
        name="out_projection",
    )(y_conv, y_nsa, y_ret, w_out, x, mod)


def _rank_rows(val, n):
    idx = lax.broadcasted_iota(jnp.int32, (n, 1), 0)
    rank = jnp.zeros(val.shape, jnp.int32)
    for rp in range(n):
        vp = val[rp:rp + 1, :]
        ahead = (vp > val) | ((vp == val) & (idx > rp))
        rank = rank + ahead.astype(jnp.int32)
    return rank


def _to_token_rows(ref, val):
    chunks = val.shape[1] // LANE
    for c in range(chunks):
        ref[pl.ds(c, val.shape[0], stride=chunks), :] = val[:, c * LANE:(c + 1) * LANE]


def _from_token_rows(ref, n):
    chunks = ref.shape[0] // n
    return jnp.concatenate([ref[pl.ds(c, n, stride=chunks), :] for c in range(chunks)], axis=1)


def _pack_bf16_pairs(v):
    half = v.shape[1] // 2
    lo = pltpu.bitcast(v[:, :half].astype(BF16).astype(F32), jnp.uint32)
    hi = pltpu.bitcast(v[:, half:].astype(BF16).astype(F32), jnp.uint32)
    return hi | (lo >> 16)


def _unpack_bf16_pairs(u):
    return pltpu.bitcast(u << 16, F32), pltpu.bitcast(u & jnp.uint32(0xFFFF0000), F32)


def _router_kernel(x_ref, sh_ref, sc_ref, g_ref, rw_ref, rb_ref, tri_ref, swgu_ref, swd_ref,
                   shared_ref, hrow_ref, pos_ref, w_ref, cnt_ref, carry):
    @pl.when(pl.program_id(0) == 0)
    def _():
        carry[...] = jnp.zeros(carry.shape, F32)

    h = _norm_mod(x_ref[...], g_ref[...], sc_ref[0], sh_ref[0])
    _to_token_rows(hrow_ref, _pack_bf16_pairs(h))
    shared_ref[...] = _swiglu(h.astype(BF16), swgu_ref[...], swd_ref[...], D_SHARED)
    tm = h.shape[0]
    scores = jax.nn.sigmoid(_dot_nt(rw_ref[...], h, precision=lax.Precision.HIGHEST))
    biased = scores + rb_ref[...]
    sub = lax.broadcasted_iota(jnp.int32, (GROUP_SIZE, 1), 0)
    gs = []
    for g in range(N_EXPERT_GROUPS):
        bg = biased[g * GROUP_SIZE:(g + 1) * GROUP_SIZE, :]
        m1 = jnp.max(bg, axis=0, keepdims=True)
        first = jnp.min(jnp.where(bg == m1, sub, GROUP_SIZE), axis=0, keepdims=True)
        m2 = jnp.max(jnp.where(sub == first, -jnp.inf, bg), axis=0, keepdims=True)
        gs.append(m1 + m2)
    gs = jnp.concatenate(gs, axis=0)
    gkeep = _rank_rows(gs, N_EXPERT_GROUPS) < TOPK_GROUPS
    keep = jnp.concatenate(
        [jnp.broadcast_to(gkeep[g:g + 1, :], (GROUP_SIZE, tm)) for g in range(N_EXPERT_GROUPS)], axis=0)
    masked = jnp.where(keep, biased, -jnp.inf)
    e_idx = lax.broadcasted_iota(jnp.int32, (N_EXPERTS, 1), 0)
    picked = jnp.zeros(masked.shape, F32)
    for _ in range(TOP_K):
        top = jnp.max(masked, axis=0, keepdims=True)
        first = jnp.min(jnp.where(masked == top, e_idx, N_EXPERTS), axis=0, keepdims=True)
        hit = e_idx == first
        picked = jnp.where(hit, 1.0, picked)
        masked = jnp.where(hit, -jnp.inf, masked)
    sel = picked > 0.0
    w = jnp.where(sel, scores, 0.0)
    w_ref[...] = ROUTED_SCALE * w / jnp.sum(w, axis=0, keepdims=True)
    self32 = sel.astype(F32)
    pos = carry[:, 0:1] + _dot(self32.astype(BF16), tri_ref[...])
    pos_ref[...] = jnp.where(sel, pos.astype(jnp.int32), -1)
    carry[...] = carry[...] + jnp.sum(self32, axis=1, keepdims=True)
    cnt_ref[...] = carry[...].astype(jnp.int32)


def moe_route(x, mod, norm_g, router_w_t, router_bias, shared_w_gu, shared_w_down, seq):
    t, d = x.shape
    tm = 512
    per = seq // tm
    tri = jnp.asarray(np.triu(np.ones((tm, tm), np.float32), 1), BF16)
    ecol = pl.BlockSpec((N_EXPERTS, tm), lambda i: (0, i))
    return pl.pallas_call(
        _router_kernel,
        grid=(t // tm,),
        in_specs=[
            pl.BlockSpec((tm, d), lambda i: (i, 0)),
            pl.BlockSpec((1, 1, d), lambda i: (i // per, 0, 3)),
            pl.BlockSpec((1, 1, d), lambda i: (i // per, 0, 4)),
            pl.BlockSpec((1, d), lambda i: (0, 0)),
            pl.BlockSpec((N_EXPERTS, d), lambda i: (0, 0)),
            pl.BlockSpec((N_EXPERTS, 1), lambda i: (0, 0)),
            pl.BlockSpec((tm, tm), lambda i: (0, 0)),
            pl.BlockSpec((d, 2 * D_SHARED), lambda i: (0, 0)),
            pl.BlockSpec((D_SHARED, d), lambda i: (0, 0)),
        ],
        out_specs=[pl.BlockSpec((tm, d), lambda i: (i, 0)),
                   pl.BlockSpec((tm * PACK_CHUNKS, LANE), lambda i: (i, 0)), ecol, ecol,
                   pl.BlockSpec((N_EXPERTS, LANE), lambda i: (0, 0))],
        out_shape=[
            jax.ShapeDtypeStruct((t, d), F32),
            jax.ShapeDtypeStruct((t * PACK_CHUNKS, LANE), jnp.uint32),
            jax.ShapeDtypeStruct((N_EXPERTS, t), jnp.int32),
            jax.ShapeDtypeStruct((N_EXPERTS, t), F32),
            jax.ShapeDtypeStruct((N_EXPERTS, LANE), jnp.int32),
        ],
        scratch_shapes=[pltpu.VMEM((N_EXPERTS, LANE), F32)],
        compiler_params=_cparams(("arbitrary",)),
        name="moe_route",
    )(x, mod, mod, norm_g, router_w_t, router_bias, tri, shared_w_gu, shared_w_down)


def _slots_kernel(pos_ref, w_ref, start_ref, tri_ref, slot_ref, w8_ref):
    pos = pos_ref[...]
    sel = pos >= 0
    slot = pos + start_ref[...]
    order = _dot(tri_ref[...], sel.astype(F32).astype(BF16))
    w = w_ref[...]
    for k in range(TOP_K):
        mk = sel & (order == float(k))
        slot_ref[k:k + 1, :] = jnp.sum(jnp.where(mk, slot, 0), axis=0, keepdims=True)
        w8_ref[k:k + 1, :] = jnp.sum(jnp.where(mk, w, 0.0), axis=0, keepdims=True)


def moe_slots(pos_t, w_t, pad_start):
    e, t = pos_t.shape
    tm = 2048
    tri = jnp.asarray(np.tril(np.ones((e, e), np.float32), -1), BF16)
    ecol = pl.BlockSpec((e, tm), lambda i: (0, i))
    kcol = pl.BlockSpec((TOP_K, tm), lambda i: (0, i))
    return pl.pallas_call(
        _slots_kernel,
        grid=(t // tm,),
        in_specs=[ecol, ecol, pl.BlockSpec((e, 1), lambda i: (0, 0)), pl.BlockSpec((e, e), lambda i: (0, 0))],
        out_specs=[kcol, kcol],
        out_shape=[jax.ShapeDtypeStruct((TOP_K, t), jnp.int32), jax.ShapeDtypeStruct((TOP_K, t), F32)],
        compiler_params=_cparams(("arbitrary",)),
        name="moe_slots",
    )(pos_t, w_t, pad_start, tri)


def _inverse_kernel(slot_ref, init_ref, inv_ref, sem, *, n_tok):
    tm = slot_ref.shape[0] // TOP_K
    i = pl.program_id(0)

    @pl.when(i == 0)
    def _():
        cp = pltpu.make_async_copy(init_ref, inv_ref, sem)
        cp.start()
        cp.wait()

    def body(t, carry):
        for k in range(TOP_K):
            inv_ref[slot_ref[t * TOP_K + k]] = k * n_tok + i * tm + t
        return carry

    lax.fori_loop(0, tm, body, 0, unroll=2)


def moe_inverse(slot8, n_slots):
    n_tok = slot8.shape[1]
    tm = 2048
    init = TOP_K * n_tok + jnp.arange(n_slots, dtype=jnp.int32)
    slot_flat = slot8.T.reshape(-1)
    return pl.pallas_call(
        functools.partial(_inverse_kernel, n_tok=n_tok),
        grid=(n_tok // tm,),
        in_specs=[pl.BlockSpec((TOP_K * tm,), lambda i: (i,), memory_space=pltpu.SMEM),
                  pl.BlockSpec(memory_space=pl.ANY)],
        out_specs=pl.BlockSpec(memory_space=pltpu.SMEM),
        out_shape=jax.ShapeDtypeStruct((n_slots,), jnp.int32),
        scratch_shapes=[pltpu.SemaphoreType.DMA(())],
        compiler_params=_cparams(("arbitrary",)),
        name="moe_inverse",
    )(slot_flat, init)


def _swiglu(x_bf16, w_gu, w_down, d_hidden):
    gu = _dot(x_bf16, w_gu)
    gate, up = gu[:, :d_hidden], gu[:, d_hidden:]
    return _dot((gate * jax.nn.sigmoid(gate) * up).astype(BF16), w_down)


def _experts_kernel(be_ref, first_ref, nb_ref, src_ref, dst_ref, h_hbm, wgu_ref, wd_ref, y_hbm,
                    x0, x1, y0, y1, wgu_bf, wd_bf, gsem, ssem, *, n_tok, n_blocks):
    del be_ref
    s = pl.program_id(0)
    cb = s - 1
    used = (cb >= 0) & (cb < nb_ref[0])
    rc = PACK_CHUNKS
    rows = x0.shape[0] // rc
    blocks_per_chunk = SMEM_CHUNK // rows
    src_off = lax.rem(jnp.minimum(s, n_blocks - 1), blocks_per_chunk) * rows
    dst_off = lax.rem(s, blocks_per_chunk) * rows

    @pl.when(s == 0)
    def _():
        y0[...] = jnp.zeros(y0.shape, y0.dtype)
        y1[...] = jnp.zeros(y1.shape, y1.dtype)

    @pl.when(used & (first_ref[jnp.clip(cb, 0, n_blocks - 1)] == 1))
    def _():
        wgu_bf[...] = wgu_ref[0, 0].astype(BF16)
        wd_bf[...] = wd_ref[0, 0].astype(BF16)

    def step(p, x_out, x_in, y_out, y_in):
        def gathered(buf, sem):
            return pltpu.make_async_copy(h_hbm.at[pl.ds(0, rows * rc), :], buf, sem)

        def scattered(buf, sem):
            return pltpu.make_async_copy(buf, y_hbm.at[pl.ds(0, rows * rc), :], sem)

        last_live = nb_ref[0] + 1

        @pl.when((s >= 1) & (s - 1 <= last_live))
        def _():
            gathered(x_in, gsem.at[1 - p]).wait()
            scattered(y_out, ssem.at[1 - p]).wait()

        def issue_rows():
            for r in range(rows):
                src = pl.multiple_of((src_ref[src_off + r] & (n_tok - 1)) * rc, rc)
                dst = pl.multiple_of(dst_ref[dst_off + r] * rc, rc)
                pltpu.make_async_copy(h_hbm.at[pl.ds(src, rc), :], x_out.at[pl.ds(r * rc, rc), :], gsem.at[p]).start()
                pltpu.make_async_copy(y_in.at[pl.ds(r * rc, rc), :], y_hbm.at[pl.ds(dst, rc), :], ssem.at[p]).start()

        @pl.when(used)
        def _():
            issue_rows()
            x = jnp.concatenate(_unpack_bf16_pairs(_from_token_rows(x_in, rows)), axis=1).astype(BF16)
            _to_token_rows(y_out, _pack_bf16_pairs(_swiglu(x, wgu_bf[...], wd_bf[...], D_EXPERT)))

        @pl.when(jnp.logical_not(used) & (s <= last_live))
        def _():
            issue_rows()

        @pl.when((s == pl.num_programs(0) - 1) & (s <= last_live))
        def _():
            gathered(x_out, gsem.at[p]).wait()
            scattered(y_in, ssem.at[p]).wait()

    @pl.when(lax.rem(s, 2) == 0)
    def _():
        step(0, x0, x1, y1, y0)

    @pl.when(lax.rem(s, 2) == 1)
    def _():
        step(1, x1, x0, y0, y1)


def moe_experts(block_e, first, n_used, inv, h_rows, w_gu, w_down, layer):
    n_tok, d = h_rows.shape[0] // PACK_CHUNKS, D_MODEL
    n_slots = inv.shape[0]
    n_blocks = n_slots // MOE_BLOCK
    assert n_tok & (n_tok - 1) == 0, "source token = inv mod T uses a power-of-two T"
    assert n_slots % SMEM_CHUNK == 0 and SMEM_CHUNK % MOE_BLOCK == 0
    per_chunk = SMEM_CHUNK // MOE_BLOCK
    n_rows = TOP_K * n_tok + n_slots + 2 * MOE_BLOCK
    warmup = TOP_K * n_tok + n_slots + jnp.arange(2 * MOE_BLOCK, dtype=jnp.int32)
    tail = jnp.zeros((SMEM_CHUNK - 2 * MOE_BLOCK,), jnp.int32)
    dst = jnp.concatenate([warmup, inv, tail])
    clamp = lambda b: jnp.clip(b, 0, n_blocks - 1)
    smem_blk = lambda f: pl.BlockSpec((SMEM_CHUNK,), f, memory_space=pltpu.SMEM)
    grid_spec = pltpu.PrefetchScalarGridSpec(
        num_scalar_prefetch=3,
        grid=(n_blocks + 2,),
        in_specs=[
            smem_blk(lambda s, be, fi, nb: (clamp(s) // per_chunk,)),
            smem_blk(lambda s, be, fi, nb: (s // per_chunk,)),
            pl.BlockSpec(memory_space=pl.ANY),
            pl.BlockSpec((1, 1, d, 2 * D_EXPERT), lambda s, be, fi, nb: (layer, be[clamp(s - 1)], 0, 0)),
            pl.BlockSpec((1, 1, D_EXPERT, d), lambda s, be, fi, nb: (layer, be[clamp(s - 1)], 0, 0)),
        ],
        out_specs=pl.BlockSpec(memory_space=pl.ANY),
        scratch_shapes=[pltpu.VMEM((MOE_BLOCK * PACK_CHUNKS, LANE), jnp.uint32)] * 4 + [
            pltpu.VMEM((d, 2 * D_EXPERT), BF16), pltpu.VMEM((D_EXPERT, d), BF16),
            pltpu.SemaphoreType.DMA((2,)), pltpu.SemaphoreType.DMA((2,))],
    )
    return pl.pallas_call(
        functools.partial(_experts_kernel, n_tok=n_tok, n_blocks=n_blocks),
        grid_spec=grid_spec,
        out_shape=jax.ShapeDtypeStruct((n_rows * PACK_CHUNKS, LANE), jnp.uint32),
        compiler_params=_cparams(("arbitrary",)),
        name="moe_experts",
    )(block_e, first, n_used, inv, dst, h_rows, w_gu, w_down)


def _combine_kernel(w_ref, sh_ref, x_ref, g_ref, fg_ref, *rest, final_norm):
    y_refs, o_ref, sum_scr = rest[:TOP_K], rest[TOP_K], rest[TOP_K + 1]
    tm = x_ref.shape[0]

    def token(t, carry):
        rows = pl.ds(pl.multiple_of(t * PACK_CHUNKS, PACK_CHUNKS), PACK_CHUNKS)
        lo, hi = _unpack_bf16_pairs(y_refs[0][rows, :])
        acc_lo, acc_hi = w_ref[0, t] * lo, w_ref[0, t] * hi
        for k in range(1, TOP_K):
            lo, hi = _unpack_bf16_pairs(y_refs[k][rows, :])
            acc_lo, acc_hi = acc_lo + w_ref[k, t] * lo, acc_hi + w_ref[k, t] * hi
        base = pl.multiple_of(t * ROW_CHUNKS, ROW_CHUNKS)
        sum_scr[pl.ds(base, PACK_CHUNKS), :] = acc_lo
        sum_scr[pl.ds(base + PACK_CHUNKS, PACK_CHUNKS), :] = acc_hi
        return carry

    lax.fori_loop(0, tm, token, 0, unroll=4)
    routed = _from_token_rows(sum_scr, tm)
    out = x_ref[...] + g_ref[0] * (routed + sh_ref[...])
    if final_norm:
        out = _rms(out) * fg_ref[...]
    o_ref[...] = out


def moe_combine(w8, shared, x, mod, final_g, y, seq, final_norm):
    t, d = x.shape
    tm = 256
    per = seq // tm
    row = pl.BlockSpec((tm, d), lambda i: (i, 0))
    y_specs = [pl.BlockSpec((tm * PACK_CHUNKS, LANE), lambda i, k=k: (k * (t // tm) + i, 0)) for k in range(TOP_K)]
    return pl.pallas_call(
        functools.partial(_combine_kernel, final_norm=final_norm),
        grid=(t // tm,),
        in_specs=[
            pl.BlockSpec((TOP_K, tm), lambda i: (0, i), memory_space=pltpu.SMEM),
            row, row,
            pl.BlockSpec((1, 1, d), lambda i: (i // per, 0, 5)),
            pl.BlockSpec((1, d), lambda i: (0, 0)),
        ] + y_specs,
        out_specs=row,
        out_shape=jax.ShapeDtypeStruct((t, d), F32),
        scratch_shapes=[pltpu.VMEM((tm * ROW_CHUNKS, LANE), F32)],
        compiler_params=_cparams(("arbitrary",)),
        name="moe_combine",
    )(w8, shared, x, mod, final_g, *([y] * TOP_K))


def token_mixer_layer(x, mod, tabs, p, batch, seq):
    cos_n, sin_n, cos_r, sin_r = tabs
    proj = in_projection(x, mod, p["norm1_g"], p["w_in"], seq)
    y_conv = short_conv(proj, p["conv_w_t"], p["conv_g"], batch, seq)
    q_rot, kvc, k_sw, vt_sw = nsa_rope(proj, cos_n, sin_n, batch, seq)
    kc, kct = compress(kvc, p["cmp_pe"], p["cmp_w1"], p["cmp_w2"], batch, seq)
    y_nsa = nsa_attention(q_rot, kc, kct, k_sw, vt_sw, proj, p["nsa_g"], batch, seq)
    y_ret = retention(proj, cos_r, sin_r, p["ret_g"], batch, seq)
    return out_projection(y_conv, y_nsa, y_ret, p["w_out"], x, mod, seq)


def moe_layer(x, mod, p, seq, final_g, final_norm):
    t, d = x.shape
    n_assign = t * TOP_K
    n_blocks = (n_assign + N_EXPERTS * (MOE_BLOCK - 1) + MOE_BLOCK - 1) // MOE_BLOCK
    shared, h_rows, pos_t, w_t, counts = moe_route(x, mod, p["norm2_g"], p["router_w_t"], p["router_bias"],
                                                   p["shared_w_gu"], p["shared_w_down"], seq)
    counts = counts[:, 0]
    padded = (counts + MOE_BLOCK - 1) // MOE_BLOCK * MOE_BLOCK
    pad_end = jnp.cumsum(padded)
    pad_start = (pad_end - padded).astype(jnp.int32)
    blk0 = jnp.arange(n_blocks, dtype=jnp.int32) * MOE_BLOCK
    block_e = jnp.minimum(jnp.sum(pad_end[None, :] <= blk0[:, None], axis=1), N_EXPERTS - 1).astype(jnp.int32)
    n_used = (pad_end[-1:] // MOE_BLOCK).astype(jnp.int32)
    first = jnp.concatenate([jnp.ones((1,), jnp.int32), (block_e[1:] != block_e[:-1]).astype(jnp.int32)])
    slot8, w8 = moe_slots(pos_t, w_t, pad_start[:, None])
    inv = moe_inverse(slot8, n_blocks * MOE_BLOCK)
    y = moe_experts(block_e, first, n_used, inv, h_rows, p["exp_w_gu"], p["exp_w_down"], p["layer"])
    return moe_combine(w8, shared, x, mod, final_g, y, seq, final_norm)


def _reorder_w_in(w_in):
    c_gate = COL_KV + 6 * D_NSA_KV
    gates = w_in[:, c_gate:c_gate + 3 * N_NSA_HEADS]
    per_group = 3 * NSA_REP
    padded = [jnp.pad(gates[:, g * per_group:(g + 1) * per_group], ((0, 0), (0, LANE - per_group)))
              for g in range(N_NSA_KV)]
    return jnp.concatenate([w_in[:, :c_gate], w_in[:, c_gate + 3 * N_NSA_HEADS:]] + padded, axis=1)


def kernel(x, c, positions, ada_w, ada_b, norm1_g, norm2_g, w_in, conv_w, conv_g, cmp_pe, cmp_w1, cmp_w2,
           nsa_g, ret_g, w_out, router_w, router_bias, exp_w_gu, exp_w_down, shared_w_gu, shared_w_down, final_g):
    batch, seq, d = x.shape
    depth = ada_w.shape[0]
    t = batch * seq
    mod_all = ada_modulation(c, ada_w, ada_b)
    pos_col = positions.reshape(t, 1)
    tabs = rope_tables(pos_col, ROPE_DIM, ROPE_THETA) + rope_tables(pos_col, HEAD_DIM, RET_THETA)
    xt = x.reshape(t, d)
    final_g2 = final_g.reshape(1, d)
    for l in range(depth):
        mod = mod_all[l].reshape(batch, 1, ADA_CHUNKS * d)
        p = dict(
            norm1_g=norm1_g[l].reshape(1, d),
            norm2_g=norm2_g[l].reshape(1, d),
            w_in=_reorder_w_in(w_in[l]).astype(BF16),
            conv_w_t=conv_w[l].T,
            conv_g=conv_g[l].reshape(1, D_CONV),
            cmp_pe=cmp_pe[l].reshape(2, 1, CMP_LEN * HEAD_DIM),
            cmp_w1=cmp_w1[l].astype(BF16),
            cmp_w2=cmp_w2[l].astype(BF16),
            nsa_g=nsa_g[l].reshape(1, D_NSA),
            ret_g=ret_g[l].reshape(1, D_RET),
            w_out=w_out[l].astype(BF16),
            router_w_t=router_w[l].T,
            router_bias=router_bias[l].reshape(N_EXPERTS, 1),
            layer=l,
            exp_w_gu=exp_w_gu,
            exp_w_down=exp_w_down,
            shared_w_gu=shared_w_gu[l].astype(BF16),
            shared_w_down=shared_w_down[l].astype(BF16),
        )
        xt = token_mixer_layer(xt, mod, tabs, p, batch, seq)
        xt = moe_layer(xt, mod, p, seq, final_g2, final_norm=(l == depth - 1))
    return xt.reshape(batch, seq, d)
```

```python
import functools

import numpy as np
import jax
import jax.numpy as jnp
from jax import lax
from jax.experimental import pallas as pl
from jax.experimental.pallas import tpu as pltpu

F32 = jnp.float32
BF16 = jnp.bfloat16

D_MODEL = 2048
HEAD_DIM = 128
D_CONV = D_MODEL // 4
CONV_GROUPS = D_CONV // HEAD_DIM
CONV_WIDTH = 3
N_NSA_HEADS = D_MODEL // 2 // HEAD_DIM
N_NSA_KV = 2
NSA_REP = N_NSA_HEADS // N_NSA_KV
D_NSA = N_NSA_HEADS * HEAD_DIM
D_NSA_KV = N_NSA_KV * HEAD_DIM
CMP_LEN = 32
CMP_STRIDE = 16
CMP_HIDDEN = 256
SLC_LEN = 64
N_SLC = 16
WINDOW = 512
ROPE_THETA = 500000.0
ROPE_DIM = HEAD_DIM // 4
N_RET_HEADS = D_MODEL // 4 // HEAD_DIM
D_RET = N_RET_HEADS * HEAD_DIM
RET_THETA = 10000.0
D_MIX = D_CONV + D_NSA + D_RET
N_EXPERTS = 64
N_EXPERT_GROUPS = 8
GROUP_SIZE = N_EXPERTS // N_EXPERT_GROUPS
TOPK_GROUPS = 4
TOP_K = 8
D_EXPERT = 512
D_SHARED = 512
ROUTED_SCALE = 2.5
MOE_BLOCK = 256
ADA_CHUNKS = 6
EPS = 1e-6
NEG_INF = -1e30
FORCE_SCORE = 1e4

LANE = 128
ROW_CHUNKS = D_MODEL // LANE
PACK_CHUNKS = ROW_CHUNKS // 2
SMEM_CHUNK = 1024
LOG2_E = 1.4426950408889634
NSA_TILE = 256
GATE_COLS = N_NSA_KV * LANE
D_PROJ = 3 * D_CONV + D_NSA + 6 * D_NSA_KV + 4 * D_RET + GATE_COLS
COL_Q = 3 * D_CONV
COL_KV = COL_Q + D_NSA
COL_RET = COL_KV + 6 * D_NSA_KV
COL_GATE = COL_RET + 4 * D_RET

VMEM_LIMIT = 56 * 1024 * 1024


def _cparams(sem):
    return pltpu.CompilerParams(dimension_semantics=sem, vmem_limit_bytes=VMEM_LIMIT)


def _dot(a, b, **kw):
    return jnp.dot(a, b, preferred_element_type=F32, **kw)


def _dot_nt(a, b, **kw):
    return lax.dot_general(a, b, (((1,), (1,)), ((), ())), preferred_element_type=F32, **kw)


def _rms(x):
    return x * lax.rsqrt(jnp.mean(x * x, axis=-1, keepdims=True) + EPS)


def _ada_kernel(c_ref, w_ref, b_ref, o_ref):
    c = c_ref[...]
    ca = (c * jax.nn.sigmoid(c)).astype(BF16)
    o_ref[0] = _dot(ca, w_ref[0].astype(BF16)) + b_ref[0]


def ada_modulation(c, ada_w, ada_b):
    L, d, n = ada_w.shape
    b = c.shape[0]
    tn = 1024
    return pl.pallas_call(
        _ada_kernel,
        grid=(L, n // tn),
        in_specs=[
            pl.BlockSpec((b, d), lambda l, j: (0, 0)),
            pl.BlockSpec((1, d, tn), lambda l, j: (l, 0, j)),
            pl.BlockSpec((1, 1, tn), lambda l, j: (l, 0, j)),
        ],
        out_specs=pl.BlockSpec((1, b, tn), lambda l, j: (l, 0, j)),
        out_shape=jax.ShapeDtypeStruct((L, b, n), F32),
        compiler_params=_cparams(("arbitrary", "arbitrary")),
        name="ada_modulation",
    )(c, ada_w, ada_b.reshape(L, 1, n))


def _norm_mod(x, g, sc, sh):
    return (_rms(x) * g) * (1.0 + sc) + sh


def _in_proj_kernel(x_ref, sh_ref, sc_ref, g_ref, w_ref, o_ref, h_scr):
    @pl.when(pl.program_id(1) == 0)
    def _():
        h_scr[...] = _norm_mod(x_ref[...], g_ref[...], sc_ref[0], sh_ref[0]).astype(BF16)

    o_ref[...] = _dot(h_scr[...], w_ref[...]).astype(o_ref.dtype)


def in_projection(x, mod, norm_g, w, seq):
    t, d = x.shape
    n = w.shape[1]
    tm, tn = 512, 1280
    per = seq // tm
    return pl.pallas_call(
        _in_proj_kernel,
        grid=(t // tm, n // tn),
        in_specs=[
            pl.BlockSpec((tm, d), lambda i, j: (i, 0)),
            pl.BlockSpec((1, 1, d), lambda i, j: (i // per, 0, 0)),
            pl.BlockSpec((1, 1, d), lambda i, j: (i // per, 0, 1)),
            pl.BlockSpec((1, d), lambda i, j: (0, 0)),
            pl.BlockSpec((d, tn), lambda i, j: (0, j)),
        ],
        out_specs=pl.BlockSpec((tm, tn), lambda i, j: (i, j)),
        out_shape=jax.ShapeDtypeStruct((t, n), BF16),
        scratch_shapes=[pltpu.VMEM((tm, d), BF16)],
        compiler_params=_cparams(("arbitrary", "arbitrary")),
        name="in_projection",
    )(x, mod, mod, norm_g, w)


def _rope_table_kernel(pos_ref, inv_ref, sgn_ref, cos_ref, sin_ref):
    ang = pos_ref[...].astype(F32) * inv_ref[...]
    cos_ref[...] = jnp.cos(ang)
    sin_ref[...] = jnp.sin(ang) * sgn_ref[...]


def rope_tables(pos_col, rot_dim, theta):
    t = pos_col.shape[0]
    half = rot_dim // 2
    inv_half = theta ** (-jnp.arange(half, dtype=F32) / half)
    inv = jnp.concatenate([inv_half, inv_half, jnp.zeros((LANE - rot_dim,), F32)]).reshape(1, LANE)
    sgn = np.zeros((1, LANE), np.float32)
    sgn[0, :half] = -1.0
    sgn[0, half:rot_dim] = 1.0
    ts = 1024
    return pl.pallas_call(
        _rope_table_kernel,
        grid=(t // ts,),
        in_specs=[
            pl.BlockSpec((ts, 1), lambda i: (i, 0)),
            pl.BlockSpec((1, LANE), lambda i: (0, 0)),
            pl.BlockSpec((1, LANE), lambda i: (0, 0)),
        ],
        out_specs=[pl.BlockSpec((ts, LANE), lambda i: (i, 0))] * 2,
        out_shape=[jax.ShapeDtypeStruct((t, LANE), F32)] * 2,
        compiler_params=_cparams(("arbitrary",)),
        name="rope_tables",
    )(pos_col, inv, jnp.asarray(sgn))


def _rotate(x, cos, sin_signed, half):
    if 2 * half == LANE:
        swapped = pltpu.roll(x, half, 1)
    else:
        lane = lax.broadcasted_iota(jnp.int32, x.shape, 1)
        swapped = jnp.where(lane < half, pltpu.roll(x, LANE - half, 1), pltpu.roll(x, half, 1))
    return x * cos + swapped * sin_signed


def _nsa_rope_kernel(q0_ref, q1_ref, kvc_ref, kvs_ref, kvw_ref, cos_ref, sin_ref,
                     q_out, kvc_out, k_out, vt_out):
    cos, sin = cos_ref[...], sin_ref[...]
    half = ROPE_DIM // 2
    hpb = D_CONV // HEAD_DIM
    scale = HEAD_DIM ** -0.5 * LOG2_E
    for blk, src in enumerate((q0_ref, q1_ref)):
        for h in range(hpb):
            sl = slice(h * HEAD_DIM, (h + 1) * HEAD_DIM)
            q_out[:, blk * D_CONV + h * HEAD_DIM: blk * D_CONV + (h + 1) * HEAD_DIM] = (
                (_rotate(src[:, sl].astype(F32), cos, sin, half) * scale).astype(q_out.dtype))
    for g in range(N_NSA_KV):
        sl = slice(g * HEAD_DIM, (g + 1) * HEAD_DIM)
        kvc_out[:, sl] = _rotate(kvc_ref[:, sl].astype(F32), cos, sin, half)
    kvc_out[:, D_NSA_KV:] = kvc_ref[:, D_NSA_KV:].astype(F32)
    for br, src in enumerate((kvs_ref, kvw_ref)):
        for g in range(N_NSA_KV):
            sl = slice(g * HEAD_DIM, (g + 1) * HEAD_DIM)
            dst = slice((br * N_NSA_KV + g) * HEAD_DIM, (br * N_NSA_KV + g + 1) * HEAD_DIM)
            k_out[:, dst] = _rotate(src[:, sl].astype(F32), cos, sin, half).astype(k_out.dtype)
            vsl = slice(D_NSA_KV + g * HEAD_DIM, D_NSA_KV + (g + 1) * HEAD_DIM)
            for tile in range(vt_out.shape[2]):
                rows = slice(tile * NSA_TILE, (tile + 1) * NSA_TILE)
                vt_out[0, br * N_NSA_KV + g, tile] = src[rows, vsl].astype(F32).T.astype(vt_out.dtype)


def nsa_rope(proj, cos, sin, batch, seq):
    t = proj.shape[0]
    ts = 512
    per = seq // ts
    w = D_CONV
    blk = lambda j: pl.BlockSpec((ts, w), lambda i, j=j: (i, j))
    tab = pl.BlockSpec((ts, LANE), lambda i: (i, 0))
    out = pl.BlockSpec((ts, w), lambda i: (i, 0))
    qb = COL_Q // w
    kb = COL_KV // w
    return pl.pallas_call(
        _nsa_rope_kernel,
        grid=(t // ts,),
        in_specs=[blk(qb), blk(qb + 1), blk(kb), blk(kb + 1), blk(kb + 2), tab, tab],
        out_specs=[pl.BlockSpec((ts, D_NSA), lambda i: (i, 0)), out, out,
                   pl.BlockSpec((1, 2 * N_NSA_KV, ts // NSA_TILE, HEAD_DIM, NSA_TILE),
                                lambda i: (i // per, 0, i % per, 0, 0))],
        out_shape=[
            jax.ShapeDtypeStruct((t, D_NSA), BF16),
            jax.ShapeDtypeStruct((t, w), F32),
            jax.ShapeDtypeStruct((t, w), BF16),
            jax.ShapeDtypeStruct((batch, 2 * N_NSA_KV, seq // NSA_TILE, HEAD_DIM, NSA_TILE), BF16),
        ],
        compiler_params=_cparams(("arbitrary",)),
        name="nsa_rope",
    )(proj, proj, proj, proj, proj, cos, sin)


def _conv_kernel(cb_ref, cc_ref, cu_ref, w_ref, g_ref, o_ref, ext):
    ts = cb_ref.shape[0]

    @pl.when(pl.program_id(1) == 0)
    def _():
        ext[0:8, :] = jnp.zeros((8, ext.shape[1]), F32)

    v = cc_ref[...].astype(F32) * cu_ref[...].astype(F32)
    ext[8:, :] = v
    v1 = ext[pl.ds(7, ts), :]
    v2 = ext[pl.ds(6, ts), :]
    y = cb_ref[...].astype(F32) * (w_ref[0:1, :] * v2 + w_ref[1:2, :] * v1 + w_ref[2:3, :] * v)
    ext[0:8, :] = v[ts - 8:, :]
    for gi in range(CONV_GROUPS):
        sl = slice(gi * HEAD_DIM, (gi + 1) * HEAD_DIM)
        o_ref[:, sl] = (_rms(y[:, sl]) * g_ref[:, sl]).astype(o_ref.dtype)


def short_conv(proj, conv_w_t, conv_g, batch, seq):
    t = proj.shape[0]
    ts = 512
    per = seq // ts
    w = D_CONV
    blk = lambda j: pl.BlockSpec((ts, w), lambda b, s, j=j: (b * per + s, j))
    return pl.pallas_call(
        _conv_kernel,
        grid=(batch, per),
        in_specs=[blk(0), blk(1), blk(2),
                  pl.BlockSpec((CONV_WIDTH, w), lambda b, s: (0, 0)),
                  pl.BlockSpec((1, w), lambda b, s: (0, 0))],
        out_specs=pl.BlockSpec((ts, w), lambda b, s: (b * per + s, 0)),
        out_shape=jax.ShapeDtypeStruct((t, w), BF16),
        scratch_shapes=[pltpu.VMEM((8 + ts, w), F32)],
        compiler_params=_cparams(("arbitrary", "arbitrary")),
        name="short_conv",
    )(proj, proj, proj, conv_w_t, conv_g)


def _compress_kernel(kv_ref, pe_ref, w1_ref, w2_ref, o_ref, ot_ref):
    h = _from_token_rows(kv_ref, kv_ref.shape[0] // CMP_STRIDE)
    n_h, half = h.shape
    pe = pe_ref[0]
    a = _dot((h + pe[:, :half]).astype(BF16), w1_ref[0, :half, :])
    b = _dot((h + pe[:, half:]).astype(BF16), w1_ref[0, half:, :])
    pre = a + pltpu.roll(b, n_h - 1, 0)
    out = _dot(jax.nn.gelu(pre).astype(BF16), w2_ref[0])
    o_ref[0, 0] = out.astype(o_ref.dtype)
    ot_ref[0, 0] = out.T.astype(ot_ref.dtype)


def compress(kvc, pe_flat, w1, w2, batch, seq):
    b, four = batch, 2 * N_NSA_KV
    n_h, dh = seq // CMP_STRIDE, CMP_STRIDE * HEAD_DIM
    return pl.pallas_call(
        _compress_kernel,
        grid=(b, four),
        in_specs=[
            pl.BlockSpec((seq, HEAD_DIM), lambda i, j: (i, j)),
            pl.BlockSpec((1, 1, 2 * dh), lambda i, j: (j // N_NSA_KV, 0, 0)),
            pl.BlockSpec((1, 2 * dh, CMP_HIDDEN), lambda i, j: (j // N_NSA_KV, 0, 0)),
            pl.BlockSpec((1, CMP_HIDDEN, HEAD_DIM), lambda i, j: (j // N_NSA_KV, 0, 0)),
        ],
        out_specs=[pl.BlockSpec((1, 1, n_h, HEAD_DIM), lambda i, j: (i, j, 0, 0)),
                   pl.BlockSpec((1, 1, HEAD_DIM, n_h), lambda i, j: (i, j, 0, 0))],
        out_shape=[jax.ShapeDtypeStruct((b, four, n_h, HEAD_DIM), BF16),
                   jax.ShapeDtypeStruct((b, four, HEAD_DIM, n_h), BF16)],
        compiler_params=_cparams(("arbitrary", "arbitrary")),
        name="nsa_compress",
    )(kvc, pe_flat, w1, w2)


def _nsa_kernel(q_ref, kc_ref, vct_ref, ks_ref, kw_ref, vst_ref, vwt_ref, gate_ref, ovt_ref, eaug_ref, g_ref,
                o_ref, kaug_scr, qaug_scr, s_scr, m_scr, l_scr, acc_scr, out_scr, *, n_cmp, n_slc, n_sel):
    tq = q_ref.shape[0]
    rows = NSA_REP * tq
    i = pl.program_id(2)
    t0 = i * tq
    hd = HEAD_DIM

    @pl.when(i == 0)
    def _():
        kaug_scr[:, :hd] = ks_ref[...]
        kaug_scr[:, hd:] = eaug_ref[...]

    for r in range(NSA_REP):
        qaug_scr[r * tq:(r + 1) * tq, :hd] = q_ref[:, r * hd:(r + 1) * hd]

    t_all =t0 + lax.rem(lax.broadcasted_iota(jnp.int32, (1, rows), 1), tq)
    gates_t = jax.nn.sigmoid(gate_ref[...].astype(F32)).T
    gate_rows = [jnp.concatenate([gates_t[3 * r + br:3 * r + br + 1, :] for r in range(NSA_REP)], axis=1)
                 for br in range(3)]

    key_sub = lax.broadcasted_iota(jnp.int32, (tq, 1), 0)
    causal = t0 + key_sub <= t_all

    n_back = WINDOW // tq
    pieces = []
    for back in range(n_back, -1, -1):
        jt = jnp.maximum(i - back, 0)
        sj = _dot_nt(kw_ref[pl.ds(pl.multiple_of(jt * tq, tq), tq), :], qaug_scr[:, :hd])
        if back == 0:
            sj = jnp.where(causal, sj, NEG_INF)
        elif back == n_back:
            sj = jnp.where((jt * tq + key_sub > t_all - WINDOW) & (i >= back), sj, NEG_INF)
        else:
            sj = jnp.where(i >= back, sj, NEG_INF)
        pieces.append((sj, vwt_ref[0, 0, jt]))
    m_w = pieces[0][0].max(axis=0, keepdims=True)
    for sj, _ in pieces[1:]:
        m_w = jnp.maximum(m_w, sj.max(axis=0, keepdims=True))
    l_w = None
    acc_w = None
    for sj, vt_tile in pieces:
        p = jnp.exp2(sj - m_w)
        lj = jnp.sum(p, axis=0, keepdims=True)
        aj = _dot(vt_tile, p.astype(BF16))
        l_w = lj if l_w is None else l_w + lj
        acc_w = aj if acc_w is None else acc_w + aj
    out_scr[...] = (gate_rows[2] / l_w) * acc_w

    kc = kc_ref[0, 0]
    vct = vct_ref[0, 0]
    n_h = kc.shape[0]
    c_idx = lax.broadcasted_iota(jnp.int32, (n_h, 1), 0)
    c_end = jnp.where(c_idx < n_cmp, c_idx * CMP_STRIDE + (CMP_LEN - 1), jnp.iinfo(jnp.int32).max)
    s = _dot_nt(kc, qaug_scr[:, :hd])
    sm = jnp.where(c_end <= t_all, s, NEG_INF)
    e = jnp.exp2(sm - jnp.max(sm, axis=0, keepdims=True))
    inv = jnp.where(t_all >= CMP_LEN - 1, 1.0 / jnp.sum(e, axis=0, keepdims=True), 0.0)
    p = e * inv
    out_scr[...] += gate_rows[0] * _dot(vct, p.astype(BF16))
    psum_t = p[:, 0:tq]
    for r in range(1, NSA_REP):
        psum_t = psum_t + p[:, r * tq:(r + 1) * tq]
    imp_t =_dot(ovt_ref[...], psum_t, precision=lax.Precision.HIGHEST)

    j_idx = lax.broadcasted_iota(jnp.int32, (n_slc, 1), 0)
    jq = (t0 + lax.broadcasted_iota(jnp.int32, (1, tq), 1)) // SLC_LEN
    forced = (j_idx == 0) | (j_idx == jq) | (j_idx == jq - 1)
    val = jnp.where(forced, FORCE_SCORE, jnp.where(j_idx <= jq, imp_t[:n_slc], -1.0))
    bias_t = jnp.full((n_slc, tq), NEG_INF, F32)
    for _ in range(n_sel):
        top = jnp.max(val, axis=0, keepdims=True)
        first = jnp.min(jnp.where(val == top, j_idx, n_slc), axis=0, keepdims=True)
        hit = j_idx == first
        bias_t = jnp.where(hit, 0.0, bias_t)
        val = jnp.where(hit, -jnp.inf, val)
    if n_slc < LANE:
        bias_t = jnp.concatenate([bias_t, jnp.zeros((LANE - n_slc, tq), F32)], axis=0)
    bias = bias_t.T.astype(BF16)
    for r in range(NSA_REP):
        qaug_scr[r * tq:(r + 1) * tq, hd:] = bias

    def sel_scores(jt):
        k0 = pl.multiple_of(jt * tq, tq)
        return _dot_nt(kaug_scr[pl.ds(k0, tq), :], qaug_scr[...])

    def sel_update(s, vt_tile, mask):
        if mask is not None:
            s = jnp.where(mask, s, NEG_INF)
        m_old = m_scr[...]
        m_new = jnp.maximum(m_old, jnp.max(s, axis=0, keepdims=True))
        alpha = jnp.exp2(m_old - m_new)
        p = jnp.exp2(s - m_new)
        l_scr[...] = alpha * l_scr[...] + jnp.sum(p, axis=0, keepdims=True)
        acc_scr[...] = alpha * acc_scr[...] + _dot(vt_tile, p.astype(BF16))
        m_scr[...] = m_new

    m_scr[...] = jnp.full(m_scr.shape, NEG_INF, F32)
    l_scr[...] = jnp.zeros(l_scr.shape, F32)
    acc_scr[...] = jnp.zeros(acc_scr.shape, F32)
    s_scr[...] = sel_scores(0)

    def sel_body(jt, carry):
        s_cur = s_scr[...]
        s_next = sel_scores(jt + 1)
        sel_update(s_cur, vst_ref[0, 0, jt], None)
        s_scr[...] = s_next
        return carry

    lax.fori_loop(0, i, sel_body, 0)
    sel_update(s_scr[...], vst_ref[0, 0, i], causal)
    out_scr[...] += (gate_rows[1] / l_scr[...]) * acc_scr[...]


    for r in range(NSA_REP):
        o_t = out_scr[:, r * tq:(r + 1) * tq]
        o_t = o_t * lax.rsqrt(jnp.mean(o_t * o_t, axis=0, keepdims=True) + EPS)
        sl = slice(r * hd, (r + 1) * hd)
        o_ref[:, sl] = (o_t.T * g_ref[:, sl]).astype(o_ref.dtype)


def _overlap_matrix_t(n_h, n_cmp, n_slc):
    cs = np.arange(n_cmp) * CMP_STRIDE
    js = np.arange(n_slc) * SLC_LEN
    ov = np.minimum(cs[:, None] + CMP_LEN, js[None, :] + SLC_LEN) - np.maximum(cs[:, None], js[None, :])
    out = np.zeros((LANE, n_h), np.float32)
    out[:n_slc, :n_cmp] = (np.clip(ov, 0, None) / CMP_LEN).T
    return out


def nsa_attention(q_rot, kc, kct, k_sw, vt_sw, proj, nsa_g, batch, seq):
    t = q_rot.shape[0]
    tq = NSA_TILE
    per = seq // tq
    n_h = seq // CMP_STRIDE
    n_cmp = (seq - CMP_LEN) // CMP_STRIDE + 1
    n_slc = seq // SLC_LEN
    n_sel = min(N_SLC, n_slc)
    assert n_slc <= LANE and WINDOW % tq == 0 and tq % LANE == 0
    ovt = jnp.asarray(_overlap_matrix_t(n_h, n_cmp, n_slc))
    eaug = np.zeros((seq, LANE), np.float32)
    eaug[np.arange(seq), np.arange(seq) // SLC_LEN] = 1.0
    eaug = jnp.asarray(eaug, BF16)
    hd = HEAD_DIM
    qw = NSA_REP * hd
    rows = NSA_REP * tq
    seq_spec = lambda off: pl.BlockSpec((seq, hd), lambda b, g, i, off=off: (b, off + g))
    vt_spec = lambda off: pl.BlockSpec((1, 1, per, hd, tq), lambda b, g, i, off=off: (b, off + g, 0, 0, 0))
    kern = functools.partial(_nsa_kernel, n_cmp=n_cmp, n_slc=n_slc, n_sel=n_sel)
    return pl.pallas_call(
        kern,
        grid=(batch, N_NSA_KV, per),
        in_specs=[
            pl.BlockSpec((tq, qw), lambda b, g, i: (b * per + i, g)),
            pl.BlockSpec((1, 1, n_h, hd), lambda b, g, i: (b, g, 0, 0)),
            pl.BlockSpec((1, 1, hd, n_h), lambda b, g, i: (b, N_NSA_KV + g, 0, 0)),
            seq_spec(0), seq_spec(N_NSA_KV), vt_spec(0), vt_spec(N_NSA_KV),
            pl.BlockSpec((tq, LANE), lambda b, g, i: (b * per + i, COL_GATE // LANE + g)),
            pl.BlockSpec((LANE, n_h), lambda b, g, i: (0, 0)),
            pl.BlockSpec((seq, LANE), lambda b, g, i: (0, 0)),
            pl.BlockSpec((1, qw), lambda b, g, i: (0, g)),
        ],
        out_specs=pl.BlockSpec((tq, qw), lambda b, g, i: (b * per + i, g)),
        out_shape=jax.ShapeDtypeStruct((t, D_NSA), BF16),
        scratch_shapes=[
            pltpu.VMEM((seq, 2 * hd), BF16),
            pltpu.VMEM((rows, 2 * hd), BF16),
            pltpu.VMEM((tq, rows), F32),
            pltpu.VMEM((1, rows), F32),
            pltpu.VMEM((1, rows), F32),
            pltpu.VMEM((hd, rows), F32),
            pltpu.VMEM((hd, rows), F32),
        ],
        compiler_params=_cparams(("arbitrary", "arbitrary", "arbitrary")),
        name="nsa_attention",
    )(q_rot, kc, kct, k_sw, k_sw, vt_sw, vt_sw, proj, ovt, eaug, nsa_g)


def _retention_kernel(q_ref, k_ref, v_ref, gate_ref, cos_ref, sin_ref, lg_ref, g_ref, o_ref, state):
    c = q_ref.shape[0]
    hd = HEAD_DIM

    @pl.when(pl.program_id(1) == 0)
    def _():
        state[...] = jnp.zeros(state.shape, F32)

    cos, sin = cos_ref[...], sin_ref[...]
    n_row = lax.broadcasted_iota(jnp.int32, (c, 1), 0).astype(F32)
    n_col = lax.broadcasted_iota(jnp.int32, (1, c), 1).astype(F32)
    diff = n_row - n_col
    for h in range(N_RET_HEADS):
        sl = slice(h * hd, (h + 1) * hd)
        lg = lg_ref[h][:, 0:1]
        q = _rotate(q_ref[:, sl].astype(F32), cos, sin, hd // 2)
        k = _rotate(k_ref[:, sl].astype(F32), cos, sin, hd // 2) * hd ** -0.5
        v = v_ref[:, sl].astype(BF16)
        decay = jnp.where(diff >= 0.0, jnp.exp(jnp.maximum(diff, 0.0) * lg), 0.0)
        scores = _dot_nt(q.astype(BF16), k.astype(BF16)) * decay
        o = _dot(scores.astype(BF16), v)
        xi = jnp.exp((n_row + 1.0) * lg)
        o = o + _dot((q * xi).astype(BF16), state[h].astype(BF16))
        zeta = jnp.exp((c - 1.0 - n_row) * lg)
        kz_t = (k * zeta).T.astype(BF16)
        state[h] = state[h] * jnp.exp(c * lg) + _dot(kz_t, v)
        gate = gate_ref[:, sl].astype(F32)
        o_ref[:, sl] = (gate * jax.nn.sigmoid(gate) * (_rms(o) * g_ref[:, sl])).astype(o_ref.dtype)


def retention(proj, cos, sin, ret_g, batch, seq):
    t = proj.shape[0]
    c = 256
    per = seq // c
    hd = HEAD_DIM
    base = COL_RET // D_RET
    blk = lambda off: pl.BlockSpec((c, D_RET), lambda b, s, off=off: (b * per + s, base + off))
    tab = pl.BlockSpec((c, hd), lambda b, s: (b * per + s, 0))
    lg = jnp.log1p(-(2.0 ** (-5.0 - jnp.arange(N_RET_HEADS, dtype=F32))))
    lg = jnp.broadcast_to(lg[:, None, None], (N_RET_HEADS, 1, LANE))
    return pl.pallas_call(
        _retention_kernel,
        grid=(batch, per),
        in_specs=[blk(0), blk(1), blk(2), blk(3), tab, tab,
                  pl.BlockSpec((N_RET_HEADS, 1, LANE), lambda b, s: (0, 0, 0)),
                  pl.BlockSpec((1, D_RET), lambda b, s: (0, 0))],
        out_specs=pl.BlockSpec((c, D_RET), lambda b, s: (b * per + s, 0)),
        out_shape=jax.ShapeDtypeStruct((t, D_RET), BF16),
        scratch_shapes=[pltpu.VMEM((N_RET_HEADS, hd, hd), F32)],
        compiler_params=_cparams(("arbitrary", "arbitrary")),
        name="retention",
    )(proj, proj, proj, proj, cos, sin, lg, ret_g)


def _out_proj_kernel(yc_ref, yn_ref, yr_ref, w_ref, x_ref, g_ref, o_ref):
    acc = _dot(yc_ref[...], w_ref[0:D_CONV, :])
    acc = acc + _dot(yn_ref[...], w_ref[D_CONV:D_CONV + D_NSA, :])
    acc = acc + _dot(yr_ref[...], w_ref[D_CONV + D_NSA:, :])
    o_ref[...] = x_ref[...] + g_ref[0] * acc


def out_projection(y_conv, y_nsa, y_ret, w_out, x, mod, seq):
    t, d = x.shape
    tm = 512
    per = seq // tm
    row = lambda w: pl.BlockSpec((tm, w), lambda i: (i, 0))
    return pl.pallas_call(
        _out_proj_kernel,
        grid=(t // tm,),
        in_specs=[row(D_CONV), row(D_NSA), row(D_RET),
                  pl.BlockSpec((D_MIX, d), lambda i: (0, 0)),
                  row(d),
                  pl.BlockSpec((1, 1, d), lambda i: (i // per, 0, 2))],
        out_specs=row(d),
        out_shape=jax.ShapeDtypeStruct((t, d), F32),
        compiler_params=_cparams(("arbitrary",)),
        name="out_projection",
    )(y_conv, y_nsa, y_ret, w_out, x, mod)


def _rank_rows(val, n):
    idx = lax.broadcasted_iota(jnp.int32, (n, 1), 0)
    rank = jnp.zeros(val.shape, jnp.int32)
    for rp in range(n):
        vp = val[rp:rp + 1, :]
        ahead = (vp > val) | ((vp == val) & (idx > rp))
        rank = rank + ahead.astype(jnp.int32)
    return rank


def _to_token_rows(ref, val):
    chunks = val.shape[1] // LANE
    for c in range(chunks):
        ref[pl.ds(c, val.shape[0], stride=chunks), :] = val[:, c * LANE:(c + 1) * LANE]


def _from_token_rows(ref, n):
    chunks = ref.shape[0] // n
    return jnp.concatenate([ref[pl.ds(c, n, stride=chunks), :] for c in range(chunks)], axis=1)


def _pack_bf16_pairs(v):
    half = v.shape[1] // 2
    lo = pltpu.bitcast(v[:, :half].astype(BF16).astype(F32), jnp.uint32)
    hi = pltpu.bitcast(v[:, half:].astype(BF16).astype(F32), jnp.uint32)
    return hi | (lo >> 16)


def _unpack_bf16_pairs(u):
    return pltpu.bitcast(u << 16, F32), pltpu.bitcast(u & jnp.uint32(0xFFFF0000), F32)


def _router_kernel(x_ref, sh_ref, sc_ref, g_ref, rw_ref, rb_ref, tri_ref, swgu_ref, swd_ref,
                   shared_ref, hrow_ref, pos_ref, w_ref, cnt_ref, carry):
    @pl.when(pl.program_id(0) == 0)
    def _():
        carry[...] = jnp.zeros(carry.shape, F32)

    h = _norm_mod(x_ref[...], g_ref[...], sc_ref[0], sh_ref[0])
    _to_token_rows(hrow_ref, _pack_bf16_pairs(h))
    h_hi = h.astype(BF16)
    shared_ref[...] = _swiglu(h_hi, swgu_ref[...], swd_ref[...], D_SHARED)
    tm = h.shape[0]
    rw = rw_ref[...]
    rw_hi = rw.astype(BF16)
    rw_lo = (rw - rw_hi.astype(F32)).astype(BF16)
    h_lo = (h - h_hi.astype(F32)).astype(BF16)
    both = _dot_nt(jnp.concatenate([rw_hi, rw_lo], axis=0), h_hi)
    scores = jax.nn.sigmoid(both[:N_EXPERTS] + both[N_EXPERTS:] + _dot_nt(rw_hi, h_lo))
    biased = scores + rb_ref[...]
    sub = lax.broadcasted_iota(jnp.int32, (GROUP_SIZE, 1), 0)
    gs = []
    for g in range(N_EXPERT_GROUPS):
        bg = biased[g * GROUP_SIZE:(g + 1) * GROUP_SIZE, :]
        m1 = jnp.max(bg, axis=0, keepdims=True)
        first = jnp.min(jnp.where(bg == m1, sub, GROUP_SIZE), axis=0, keepdims=True)
        m2 = jnp.max(jnp.where(sub == first, -jnp.inf, bg), axis=0, keepdims=True)
        gs.append(m1 + m2)
    gs = jnp.concatenate(gs, axis=0)
    gkeep = _rank_rows(gs, N_EXPERT_GROUPS) < TOPK_GROUPS
    keep = jnp.concatenate(
        [jnp.broadcast_to(gkeep[g:g + 1, :], (GROUP_SIZE, tm)) for g in range(N_EXPERT_GROUPS)], axis=0)
    masked = jnp.where(keep, biased, -jnp.inf)
    e_idx = lax.broadcasted_iota(jnp.int32, (N_EXPERTS, 1), 0)
    picked = jnp.zeros(masked.shape, F32)
    for _ in range(TOP_K):
        top = jnp.max(masked, axis=0, keepdims=True)
        first = jnp.min(jnp.where(masked == top, e_idx, N_EXPERTS), axis=0, keepdims=True)
        hit = e_idx == first
        picked = jnp.where(hit, 1.0, picked)
        masked = jnp.where(hit, -jnp.inf, masked)
    sel = picked > 0.0
    w = jnp.where(sel, scores, 0.0)
    w_ref[...] = ROUTED_SCALE * w / jnp.sum(w, axis=0, keepdims=True)
    self32 = sel.astype(F32)
    pos = carry[:, 0:1] + _dot(self32.astype(BF16), tri_ref[...])
    pos_ref[...] = jnp.where(sel, pos.astype(jnp.int32), -1)
    carry[...] = carry[...] + jnp.sum(self32, axis=1, keepdims=True)
    cnt_ref[...] = carry[...].astype(jnp.int32)


def moe_route(x, mod, norm_g, router_w_t, router_bias, shared_w_gu, shared_w_down, seq):
    t, d = x.shape
    tm = 512
    per = seq // tm
    tri = jnp.asarray(np.triu(np.ones((tm, tm), np.float32), 1), BF16)
    ecol = pl.BlockSpec((N_EXPERTS, tm), lambda i: (0, i))
    return pl.pallas_call(
        _router_kernel,
        grid=(t // tm,),
        in_specs=[
            pl.BlockSpec((tm, d), lambda i: (i, 0)),
            pl.BlockSpec((1, 1, d), lambda i: (i // per, 0, 3)),
            pl.BlockSpec((1, 1, d), lambda i: (i // per, 0, 4)),
            pl.BlockSpec((1, d), lambda i: (0, 0)),
            pl.BlockSpec((N_EXPERTS, d), lambda i: (0, 0)),
            pl.BlockSpec((N_EXPERTS, 1), lambda i: (0, 0)),
            pl.BlockSpec((tm, tm), lambda i: (0, 0)),
            pl.BlockSpec((d, 2 * D_SHARED), lambda i: (0, 0)),
            pl.BlockSpec((D_SHARED, d), lambda i: (0, 0)),
        ],
        out_specs=[pl.BlockSpec((tm, d), lambda i: (i, 0)),
                   pl.BlockSpec((tm * PACK_CHUNKS, LANE), lambda i: (i, 0)), ecol, ecol,
                   pl.BlockSpec((N_EXPERTS, LANE), lambda i: (0, 0))],
        out_shape=[
            jax.ShapeDtypeStruct((t, d), F32),
            jax.ShapeDtypeStruct((t * PACK_CHUNKS, LANE), jnp.uint32),
            jax.ShapeDtypeStruct((N_EXPERTS, t), jnp.int32),
            jax.ShapeDtypeStruct((N_EXPERTS, t), F32),
            jax.ShapeDtypeStruct((N_EXPERTS, LANE), jnp.int32),
        ],
        scratch_shapes=[pltpu.VMEM((N_EXPERTS, LANE), F32)],
        compiler_params=_cparams(("arbitrary",)),
        name="moe_route",
    )(x, mod, mod, norm_g, router_w_t, router_bias, tri, shared_w_gu, shared_w_down)


def _slots_kernel(pos_ref, w_ref, start_ref, tri_ref, slot_ref, w8_ref):
    pos = pos_ref[...]
    sel = pos >= 0
    slot = pos + start_ref[...]
    order = _dot(tri_ref[...], sel.astype(F32).astype(BF16))
    w = w_ref[...]
    for k in range(TOP_K):
        mk = sel & (order == float(k))
        slot_ref[k:k + 1, :] = jnp.sum(jnp.where(mk, slot, 0), axis=0, keepdims=True)
        w8_ref[k:k + 1, :] = jnp.sum(jnp.where(mk, w, 0.0), axis=0, keepdims=True)


def moe_slots(pos_t, w_t, pad_start):
    e, t = pos_t.shape
    tm = 2048
    tri = jnp.asarray(np.tril(np.ones((e, e), np.float32), -1), BF16)
    ecol = pl.BlockSpec((e, tm), lambda i: (0, i))
    kcol = pl.BlockSpec((TOP_K, tm), lambda i: (0, i))
    return pl.pallas_call(
        _slots_kernel,
        grid=(t // tm,),
        in_specs=[ecol, ecol, pl.BlockSpec((e, 1), lambda i: (0, 0)), pl.BlockSpec((e, e), lambda i: (0, 0))],
        out_specs=[kcol, kcol],
        out_shape=[jax.ShapeDtypeStruct((TOP_K, t), jnp.int32), jax.ShapeDtypeStruct((TOP_K, t), F32)],
        compiler_params=_cparams(("arbitrary",)),
        name="moe_slots",
    )(pos_t, w_t, pad_start, tri)


def _inverse_kernel(slot_ref, init_ref, inv_ref, sem, *, n_tok):
    tm = slot_ref.shape[0] // TOP_K
    i = pl.program_id(0)

    @pl.when(i == 0)
    def _():
        cp = pltpu.make_async_copy(init_ref, inv_ref, sem)
        cp.start()
        cp.wait()

    def body(t, carry):
        for k in range(TOP_K):
            inv_ref[slot_ref[t * TOP_K + k]] = k * n_tok + i * tm + t
        return carry

    lax.fori_loop(0, tm, body, 0, unroll=2)


def moe_inverse(slot8, n_slots):
    n_tok = slot8.shape[1]
    tm = 2048
    init = TOP_K * n_tok + jnp.arange(n_slots, dtype=jnp.int32)
    slot_flat = slot8.T.reshape(-1)
    return pl.pallas_call(
        functools.partial(_inverse_kernel, n_tok=n_tok),
        grid=(n_tok // tm,),
        in_specs=[pl.BlockSpec((TOP_K * tm,), lambda i: (i,), memory_space=pltpu.SMEM),
                  pl.BlockSpec(memory_space=pl.ANY)],
        out_specs=pl.BlockSpec(memory_space=pltpu.SMEM),
        out_shape=jax.ShapeDtypeStruct((n_slots,), jnp.int32),
        scratch_shapes=[pltpu.SemaphoreType.DMA(())],
        compiler_params=_cparams(("arbitrary",)),
        name="moe_inverse",
    )(slot_flat, init)


def _swiglu(x_bf16, w_gu, w_down, d_hidden):
    gu = _dot(x_bf16, w_gu)
    gate, up = gu[:, :d_hidden], gu[:, d_hidden:]
    return _dot((gate * jax.nn.sigmoid(gate) * up).astype(BF16), w_down)


def _experts_kernel(be_ref, first_ref, nb_ref, src_ref, dst_ref, h_hbm, wgu_ref, wd_ref, y_hbm,
                    x0, x1, y0, y1, wgu_bf, wd_bf, gsem, ssem, *, n_tok, n_blocks):
    del be_ref
    s = pl.program_id(0)
    cb = s - 1
    used = (cb >= 0) & (cb < nb_ref[0])
    rc = PACK_CHUNKS
    rows = x0.shape[0] // rc
    blocks_per_chunk = SMEM_CHUNK // rows
    src_off = lax.rem(jnp.minimum(s, n_blocks - 1), blocks_per_chunk) * rows
    dst_off = lax.rem(s, blocks_per_chunk) * rows

    @pl.when(s == 0)
    def _():
        y0[...] = jnp.zeros(y0.shape, y0.dtype)
        y1[...] = jnp.zeros(y1.shape, y1.dtype)

    @pl.when(used & (first_ref[jnp.clip(cb, 0, n_blocks - 1)] == 1))
    def _():
        wgu_bf[...] = wgu_ref[0, 0].astype(BF16)
        wd_bf[...] = wd_ref[0, 0].astype(BF16)

    def step(p, x_out, x_in, y_out, y_in):
        def gathered(buf, sem):
            return pltpu.make_async_copy(h_hbm.at[pl.ds(0, rows * rc), :], buf, sem)

        def scattered(buf, sem):
            return pltpu.make_async_copy(buf, y_hbm.at[pl.ds(0, rows * rc), :], sem)

        last_live = nb_ref[0] + 1

        @pl.when((s >= 1) & (s - 1 <= last_live))
        def _():
            gathered(x_in, gsem.at[1 - p]).wait()
            scattered(y_out, ssem.at[1 - p]).wait()

        def issue_rows():
            for r in range(rows):
                src = pl.multiple_of((src_ref[src_off + r] & (n_tok - 1)) * rc, rc)
                dst = pl.multiple_of(dst_ref[dst_off + r] * rc, rc)
                pltpu.make_async_copy(h_hbm.at[pl.ds(src, rc), :], x_out.at[pl.ds(r * rc, rc), :], gsem.at[p]).start()
                pltpu.make_async_copy(y_in.at[pl.ds(r * rc, rc), :], y_hbm.at[pl.ds(dst, rc), :], ssem.at[p]).start()

        @pl.when(used)
        def _():
            issue_rows()
            x = jnp.concatenate(_unpack_bf16_pairs(_from_token_rows(x_in, rows)), axis=1).astype(BF16)
            _to_token_rows(y_out, _pack_bf16_pairs(_swiglu(x, wgu_bf[...], wd_bf[...], D_EXPERT)))

        @pl.when(jnp.logical_not(used) & (s <= last_live))
        def _():
            issue_rows()

        @pl.when((s == pl.num_programs(0) - 1) & (s <= last_live))
        def _():
            gathered(x_out, gsem.at[p]).wait()
            scattered(y_in, ssem.at[p]).wait()

    @pl.when(lax.rem(s, 2) == 0)
    def _():
        step(0, x0, x1, y1, y0)

    @pl.when(lax.rem(s, 2) == 1)
    def _():
        step(1, x1, x0, y0, y1)


def moe_experts(block_e, first, n_used, inv, h_rows, w_gu, w_down, layer):
    n_tok, d = h_rows.shape[0] // PACK_CHUNKS, D_MODEL
    n_slots = inv.shape[0]
    n_blocks = n_slots // MOE_BLOCK
    assert n_tok & (n_tok - 1) == 0, "source token = inv mod T uses a power-of-two T"
    assert n_slots % SMEM_CHUNK == 0 and SMEM_CHUNK % MOE_BLOCK == 0
    per_chunk = SMEM_CHUNK // MOE_BLOCK
    n_rows = TOP_K * n_tok + n_slots + 2 * MOE_BLOCK
    warmup = TOP_K * n_tok + n_slots + jnp.arange(2 * MOE_BLOCK, dtype=jnp.int32)
    tail = jnp.zeros((SMEM_CHUNK - 2 * MOE_BLOCK,), jnp.int32)
    dst = jnp.concatenate([warmup, inv, tail])
    clamp = lambda b: jnp.clip(b, 0, n_blocks - 1)
    smem_blk = lambda f: pl.BlockSpec((SMEM_CHUNK,), f, memory_space=pltpu.SMEM)
    grid_spec = pltpu.PrefetchScalarGridSpec(
        num_scalar_prefetch=3,
        grid=(n_blocks + 2,),
        in_specs=[
            smem_blk(lambda s, be, fi, nb: (clamp(s) // per_chunk,)),
            smem_blk(lambda s, be, fi, nb: (s // per_chunk,)),
            pl.BlockSpec(memory_space=pl.ANY),
            pl.BlockSpec((1, 1, d, 2 * D_EXPERT), lambda s, be, fi, nb: (layer, be[clamp(s - 1)], 0, 0)),
            pl.BlockSpec((1, 1, D_EXPERT, d), lambda s, be, fi, nb: (layer, be[clamp(s - 1)], 0, 0)),
        ],
        out_specs=pl.BlockSpec(memory_space=pl.ANY),
        scratch_shapes=[pltpu.VMEM((MOE_BLOCK * PACK_CHUNKS, LANE), jnp.uint32)] * 4 + [
            pltpu.VMEM((d, 2 * D_EXPERT), BF16), pltpu.VMEM((D_EXPERT, d), BF16),
            pltpu.SemaphoreType.DMA((2,)), pltpu.SemaphoreType.DMA((2,))],
    )
    return pl.pallas_call(
        functools.partial(_experts_kernel, n_tok=n_tok, n_blocks=n_blocks),
        grid_spec=grid_spec,
        out_shape=jax.ShapeDtypeStruct((n_rows * PACK_CHUNKS, LANE), jnp.uint32),
        compiler_params=_cparams(("arbitrary",)),
        name="moe_experts",
    )(block_e, first, n_used, inv, dst, h_rows, w_gu, w_down)


def _combine_kernel(w_ref, sh_ref, x_ref, g_ref, fg_ref, *rest, final_norm):
    y_refs, o_ref, sum_scr = rest[:TOP_K], rest[TOP_K], rest[TOP_K + 1]
    tm = x_ref.shape[0]

    def token(t, carry):
        rows = pl.ds(pl.multiple_of(t * PACK_CHUNKS, PACK_CHUNKS), PACK_CHUNKS)
        lo, hi = _unpack_bf16_pairs(y_refs[0][rows, :])
        acc_lo, acc_hi = w_ref[0, t] * lo, w_ref[0, t] * hi
        for k in range(1, TOP_K):
            lo, hi = _unpack_bf16_pairs(y_refs[k][rows, :])
            acc_lo, acc_hi = acc_lo + w_ref[k, t] * lo, acc_hi + w_ref[k, t] * hi
        base = pl.multiple_of(t * ROW_CHUNKS, ROW_CHUNKS)
        sum_scr[pl.ds(base, PACK_CHUNKS), :] = acc_lo
        sum_scr[pl.ds(base + PACK_CHUNKS, PACK_CHUNKS), :] = acc_hi
        return carry

    lax.fori_loop(0, tm, token, 0, unroll=4)
    routed = _from_token_rows(sum_scr, tm)
    out = x_ref[...] + g_ref[0] * (routed + sh_ref[...])
    if final_norm:
        out = _rms(out) * fg_ref[...]
    o_ref[...] = out


def moe_combine(w8, shared, x, mod, final_g, y, seq, final_norm):
    t, d = x.shape
    tm = 256
    per = seq // tm
    row = pl.BlockSpec((tm, d), lambda i: (i, 0))
    y_specs = [pl.BlockSpec((tm * PACK_CHUNKS, LANE), lambda i, k=k: (k * (t // tm) + i, 0)) for k in range(TOP_K)]
    return pl.pallas_call(
        functools.partial(_combine_kernel, final_norm=final_norm),
        grid=(t // tm,),
        in_specs=[
            pl.BlockSpec((TOP_K, tm), lambda i: (0, i), memory_space=pltpu.SMEM),
            row, row,
            pl.BlockSpec((1, 1, d), lambda i: (i // per, 0, 5)),
            pl.BlockSpec((1, d), lambda i: (0, 0)),
        ] + y_specs,
        out_specs=row,
        out_shape=jax.ShapeDtypeStruct((t, d), F32),
        scratch_shapes=[pltpu.VMEM((tm * ROW_CHUNKS, LANE), F32)],
        compiler_params=_cparams(("arbitrary",)),
        name="moe_combine",
    )(w8, shared, x, mod, final_g, *([y] * TOP_K))


def token_mixer_layer(x, mod, tabs, p, batch, seq):
    cos_n, sin_n, cos_r, sin_r = tabs
    proj = in_projection(x, mod, p["norm1_g"], p["w_in"], seq)
    y_conv = short_conv(proj, p["conv_w_t"], p["conv_g"], batch, seq)
    q_rot, kvc, k_sw, vt_sw = nsa_rope(proj, cos_n, sin_n, batch, seq)
    kc, kct = compress(kvc, p["cmp_pe"], p["cmp_w1"], p["cmp_w2"], batch, seq)
    y_nsa = nsa_attention(q_rot, kc, kct, k_sw, vt_sw, proj, p["nsa_g"], batch, seq)
    y_ret = retention(proj, cos_r, sin_r, p["ret_g"], batch, seq)
    return out_projection(y_conv, y_nsa, y_ret, p["w_out"], x, mod, seq)


def moe_layer(x, mod, p, seq, final_g, final_norm):
    t, d = x.shape
    n_assign = t * TOP_K
    n_blocks = (n_assign + N_EXPERTS * (MOE_BLOCK - 1) + MOE_BLOCK - 1) // MOE_BLOCK
    shared, h_rows, pos_t, w_t, counts = moe_route(x, mod, p["norm2_g"], p["router_w_t"], p["router_bias"],
                                                   p["shared_w_gu"], p["shared_w_down"], seq)
    counts = counts[:, 0]
    padded = (counts + MOE_BLOCK - 1) // MOE_BLOCK * MOE_BLOCK
    pad_end = jnp.cumsum(padded)
    pad_start = (pad_end - padded).astype(jnp.int32)
    blk0 = jnp.arange(n_blocks, dtype=jnp.int32) * MOE_BLOCK
    block_e = jnp.minimum(jnp.sum(pad_end[None, :] <= blk0[:, None], axis=1), N_EXPERTS - 1).astype(jnp.int32)
    n_used = (pad_end[-1:] // MOE_BLOCK).astype(jnp.int32)
    first = jnp.concatenate([jnp.ones((1,), jnp.int32), (block_e[1:] != block_e[:-1]).astype(jnp.int32)])
    slot8, w8 = moe_slots(pos_t, w_t, pad_start[:, None])
    inv = moe_inverse(slot8, n_blocks * MOE_BLOCK)
    y = moe_experts(block_e, first, n_used, inv, h_rows, p["exp_w_gu"], p["exp_w_down"], p["layer"])
    return moe_combine(w8, shared, x, mod, final_g, y, seq, final_norm)


def _reorder_w_in(w_in):
    c_gate = COL_KV + 6 * D_NSA_KV
    gates = w_in[:, c_gate:c_gate + 3 * N_NSA_HEADS]
    per_group = 3 * NSA_REP
    padded = [jnp.pad(gates[:, g * per_group:(g + 1) * per_group], ((0, 0), (0, LANE - per_group)))
              for g in range(N_NSA_KV)]
    return jnp.concatenate([w_in[:, :c_gate], w_in[:, c_gate + 3 * N_NSA_HEADS:]] + padded, axis=1)


def kernel(x, c, positions, ada_w, ada_b, norm1_g, norm2_g, w_in, conv_w, conv_g, cmp_pe, cmp_w1, cmp_w2,
           nsa_g, ret_g, w_out, router_w, router_bias, exp_w_gu, exp_w_down, shared_w_gu, shared_w_down, final_g):
    batch, seq, d = x.shape
    depth = ada_w.shape[0]
    t = batch * seq
    mod_all = ada_modulation(c, ada_w, ada_b)
    pos_col = positions.reshape(t, 1)
    tabs = rope_tables(pos_col, ROPE_DIM, ROPE_THETA) + rope_tables(pos_col, HEAD_DIM, RET_THETA)
    xt = x.reshape(t, d)
    final_g2 = final_g.reshape(1, d)
    for l in range(depth):
        mod = mod_all[l].reshape(batch, 1, ADA_CHUNKS * d)
        p = dict(
            norm1_g=norm1_g[l].reshape(1, d),
            norm2_g=norm2_g[l].reshape(1, d),
            w_in=_reorder_w_in(w_in[l]).astype(BF16),
            conv_w_t=conv_w[l].T,
            conv_g=conv_g[l].reshape(1, D_CONV),
            cmp_pe=cmp_pe[l].reshape(2, 1, CMP_LEN * HEAD_DIM),
            cmp_w1=cmp_w1[l].astype(BF16),
            cmp_w2=cmp_w2[l].astype(BF16),
            nsa_g=nsa_g[l].reshape(1, D_NSA),
            ret_g=ret_g[l].reshape(1, D_RET),
            w_out=w_out[l].astype(BF16),
            router_w_t=router_w[l].T,
            router_bias=router_bias[l].reshape(N_EXPERTS, 1),
            layer=l,
            exp_w_gu=exp_w_gu,
            exp_w_down=exp_w_down,
            shared_w_gu=shared_w_gu[l].astype(BF16),
            shared_w_down=shared_w_down[l].astype(BF16),
        )
        xt = token_mixer_layer(xt, mod, tabs, p, batch, seq)
        xt = moe_layer(xt, mod, p, seq, final_g2, final_norm=(l == depth - 1))
    return xt.reshape(batch, seq, d)
```

```python
import functools

import numpy as np
import jax
import jax.numpy as jnp
from jax import lax
from jax.experimental import pallas as pl
from jax.experimental.pallas import tpu as pltpu

F32 = jnp.float32
BF16 = jnp.bfloat16

D_MODEL = 2048
HEAD_DIM = 128
D_CONV = D_MODEL // 4
CONV_GROUPS = D_CONV // HEAD_DIM
CONV_WIDTH = 3
N_NSA_HEADS = D_MODEL // 2 // HEAD_DIM
N_NSA_KV = 2
NSA_REP = N_NSA_HEADS // N_NSA_KV
D_NSA = N_NSA_HEADS * HEAD_DIM
D_NSA_KV = N_NSA_KV * HEAD_DIM
CMP_LEN = 32
CMP_STRIDE = 16
CMP_HIDDEN = 256
SLC_LEN = 64
N_SLC = 16
WINDOW = 512
ROPE_THETA = 500000.0
ROPE_DIM = HEAD_DIM // 4
N_RET_HEADS = D_MODEL // 4 // HEAD_DIM
D_RET = N_RET_HEADS * HEAD_DIM
RET_THETA = 10000.0
D_MIX = D_CONV + D_NSA + D_RET
N_EXPERTS = 64
N_EXPERT_GROUPS = 8
GROUP_SIZE = N_EXPERTS // N_EXPERT_GROUPS
TOPK_GROUPS = 4
TOP_K = 8
D_EXPERT = 512
D_SHARED = 512
ROUTED_SCALE = 2.5
MOE_BLOCK = 256
ADA_CHUNKS = 6
EPS = 1e-6
NEG_INF = -1e30
FORCE_SCORE = 1e4

LANE = 128
ROW_CHUNKS = D_MODEL // LANE
PACK_CHUNKS = ROW_CHUNKS // 2
SMEM_CHUNK = 1024
LOG2_E = 1.4426950408889634
NSA_TILE = 256
GATE_COLS = N_NSA_KV * LANE
D_PROJ = 3 * D_CONV + D_NSA + 6 * D_NSA_KV + 4 * D_RET + GATE_COLS
COL_Q = 3 * D_CONV
COL_KV = COL_Q + D_NSA
COL_RET = COL_KV + 6 * D_NSA_KV
COL_GATE = COL_RET + 4 * D_RET

VMEM_LIMIT = 56 * 1024 * 1024


def _cparams(sem):
    return pltpu.CompilerParams(dimension_semantics=sem, vmem_limit_bytes=VMEM_LIMIT)


def _dot(a, b, **kw):
    return jnp.dot(a, b, preferred_element_type=F32, **kw)


def _dot_nt(a, b, **kw):
    return lax.dot_general(a, b, (((1,), (1,)), ((), ())), preferred_element_type=F32, **kw)


def _rms(x):
    return x * lax.rsqrt(jnp.mean(x * x, axis=-1, keepdims=True) + EPS)


def _ada_kernel(c_ref, w_ref, b_ref, o_ref):
    c = c_ref[...]
    ca = (c * jax.nn.sigmoid(c)).astype(BF16)
    o_ref[0] = _dot(ca, w_ref[0].astype(BF16)) + b_ref[0]


def ada_modulation(c, ada_w, ada_b):
    L, d, n = ada_w.shape
    b = c.shape[0]
    tn = 1024
    return pl.pallas_call(
        _ada_kernel,
        grid=(L, n // tn),
        in_specs=[
            pl.BlockSpec((b, d), lambda l, j: (0, 0)),
            pl.BlockSpec((1, d, tn), lambda l, j: (l, 0, j)),
            pl.BlockSpec((1, 1, tn), lambda l, j: (l, 0, j)),
        ],
        out_specs=pl.BlockSpec((1, b, tn), lambda l, j: (l, 0, j)),
        out_shape=jax.ShapeDtypeStruct((L, b, n), F32),
        compiler_params=_cparams(("arbitrary", "arbitrary")),
        name="ada_modulation",
    )(c, ada_w, ada_b.reshape(L, 1, n))


def _norm_mod(x, g, sc, sh):
    return (_rms(x) * g) * (1.0 + sc) + sh


def _in_proj_kernel(x_ref, sh_ref, sc_ref, g_ref, w_ref, o_ref, h_scr):
    @pl.when(pl.program_id(1) == 0)
    def _():
        h_scr[...] = _norm_mod(x_ref[...], g_ref[...], sc_ref[0], sh_ref[0]).astype(BF16)

    o_ref[...] = _dot(h_scr[...], w_ref[...]).astype(o_ref.dtype)


def in_projection(x, mod, norm_g, w, seq):
    t, d = x.shape
    n = w.shape[1]
    tm, tn = 512, 1280
    per = seq // tm
    return pl.pallas_call(
        _in_proj_kernel,
        grid=(t // tm, n // tn),
        in_specs=[
            pl.BlockSpec((tm, d), lambda i, j: (i, 0)),
            pl.BlockSpec((1, 1, d), lambda i, j: (i // per, 0, 0)),
            pl.BlockSpec((1, 1, d), lambda i, j: (i // per, 0, 1)),
            pl.BlockSpec((1, d), lambda i, j: (0, 0)),
            pl.BlockSpec((d, tn), lambda i, j: (0, j)),
        ],
        out_specs=pl.BlockSpec((tm, tn), lambda i, j: (i, j)),
        out_shape=jax.ShapeDtypeStruct((t, n), BF16),
        scratch_shapes=[pltpu.VMEM((tm, d), BF16)],
        compiler_params=_cparams(("arbitrary", "arbitrary")),
        name="in_projection",
    )(x, mod, mod, norm_g, w)


def _rope_table_kernel(pos_ref, inv_ref, sgn_ref, cos_ref, sin_ref):
    ang = pos_ref[...].astype(F32) * inv_ref[...]
    cos_ref[...] = jnp.cos(ang)
    sin_ref[...] = jnp.sin(ang) * sgn_ref[...]


def rope_tables(pos_col, rot_dim, theta):
    t = pos_col.shape[0]
    half = rot_dim // 2
    inv_half = theta ** (-jnp.arange(half, dtype=F32) / half)
    inv = jnp.concatenate([inv_half, inv_half, jnp.zeros((LANE - rot_dim,), F32)]).reshape(1, LANE)
    sgn = np.zeros((1, LANE), np.float32)
    sgn[0, :half] = -1.0
    sgn[0, half:rot_dim] = 1.0
    ts = 1024
    return pl.pallas_call(
        _rope_table_kernel,
        grid=(t // ts,),
        in_specs=[
            pl.BlockSpec((ts, 1), lambda i: (i, 0)),
            pl.BlockSpec((1, LANE), lambda i: (0, 0)),
            pl.BlockSpec((1, LANE), lambda i: (0, 0)),
        ],
        out_specs=[pl.BlockSpec((ts, LANE), lambda i: (i, 0))] * 2,
        out_shape=[jax.ShapeDtypeStruct((t, LANE), F32)] * 2,
        compiler_params=_cparams(("arbitrary",)),
        name="rope_tables",
    )(pos_col, inv, jnp.asarray(sgn))


def _rotate(x, cos, sin_signed, half):
    if 2 * half == LANE:
        swapped = pltpu.roll(x, half, 1)
    else:
        lane = lax.broadcasted_iota(jnp.int32, x.shape, 1)
        swapped = jnp.where(lane < half, pltpu.roll(x, LANE - half, 1), pltpu.roll(x, half, 1))
    return x * cos + swapped * sin_signed


def _nsa_rope_kernel(q0_ref, q1_ref, kvc_ref, kvs_ref, kvw_ref, cos_ref, sin_ref,
                     q_out, kvc_out, k_out, vt_out):
    cos, sin = cos_ref[...], sin_ref[...]
    half = ROPE_DIM // 2
    hpb = D_CONV // HEAD_DIM
    scale = HEAD_DIM ** -0.5 * LOG2_E
    for blk, src in enumerate((q0_ref, q1_ref)):
        for h in range(hpb):
            sl = slice(h * HEAD_DIM, (h + 1) * HEAD_DIM)
            q_out[:, blk * D_CONV + h * HEAD_DIM: blk * D_CONV + (h + 1) * HEAD_DIM] = (
                (_rotate(src[:, sl].astype(F32), cos, sin, half) * scale).astype(q_out.dtype))
    for g in range(N_NSA_KV):
        sl = slice(g * HEAD_DIM, (g + 1) * HEAD_DIM)
        kvc_out[:, sl] = _rotate(kvc_ref[:, sl].astype(F32), cos, sin, half)
    kvc_out[:, D_NSA_KV:] = kvc_ref[:, D_NSA_KV:].astype(F32)
    for br, src in enumerate((kvs_ref, kvw_ref)):
        for g in range(N_NSA_KV):
            sl = slice(g * HEAD_DIM, (g + 1) * HEAD_DIM)
            dst = slice((br * N_NSA_KV + g) * HEAD_DIM, (br * N_NSA_KV + g + 1) * HEAD_DIM)
            k_out[:, dst] = _rotate(src[:, sl].astype(F32), cos, sin, half).astype(k_out.dtype)
            vsl = slice(D_NSA_KV + g * HEAD_DIM, D_NSA_KV + (g + 1) * HEAD_DIM)
            for tile in range(vt_out.shape[2]):
                rows = slice(tile * NSA_TILE, (tile + 1) * NSA_TILE)
                vt_out[0, br * N_NSA_KV + g, tile] = src[rows, vsl].astype(F32).T.astype(vt_out.dtype)


def nsa_rope(proj, cos, sin, batch, seq):
    t = proj.shape[0]
    ts = 512
    per = seq // ts
    w = D_CONV
    blk = lambda j: pl.BlockSpec((ts, w), lambda i, j=j: (i, j))
    tab = pl.BlockSpec((ts, LANE), lambda i: (i, 0))
    out = pl.BlockSpec((ts, w), lambda i: (i, 0))
    qb = COL_Q // w
    kb = COL_KV // w
    return pl.pallas_call(
        _nsa_rope_kernel,
        grid=(t // ts,),
        in_specs=[blk(qb), blk(qb + 1), blk(kb), blk(kb + 1), blk(kb + 2), tab, tab],
        out_specs=[pl.BlockSpec((ts, D_NSA), lambda i: (i, 0)), out, out,
                   pl.BlockSpec((1, 2 * N_NSA_KV, ts // NSA_TILE, HEAD_DIM, NSA_TILE),
                                lambda i: (i // per, 0, i % per, 0, 0))],
        out_shape=[
            jax.ShapeDtypeStruct((t, D_NSA), BF16),
            jax.ShapeDtypeStruct((t, w), F32),
            jax.ShapeDtypeStruct((t, w), BF16),
            jax.ShapeDtypeStruct((batch, 2 * N_NSA_KV, seq // NSA_TILE, HEAD_DIM, NSA_TILE), BF16),
        ],
        compiler_params=_cparams(("arbitrary",)),
        name="nsa_rope",
    )(proj, proj, proj, proj, proj, cos, sin)


def _conv_kernel(cb_ref, cc_ref, cu_ref, w_ref, g_ref, o_ref, ext):
    ts = cb_ref.shape[0]

    @pl.when(pl.program_id(1) == 0)
    def _():
        ext[0:8, :] = jnp.zeros((8, ext.shape[1]), F32)

    v = cc_ref[...].astype(F32) * cu_ref[...].astype(F32)
    ext[8:, :] = v
    v1 = ext[pl.ds(7, ts), :]
    v2 = ext[pl.ds(6, ts), :]
    y = cb_ref[...].astype(F32) * (w_ref[0:1, :] * v2 + w_ref[1:2, :] * v1 + w_ref[2:3, :] * v)
    ext[0:8, :] = v[ts - 8:, :]
    for gi in range(CONV_GROUPS):
        sl = slice(gi * HEAD_DIM, (gi + 1) * HEAD_DIM)
        o_ref[:, sl] = (_rms(y[:, sl]) * g_ref[:, sl]).astype(o_ref.dtype)


def short_conv(proj, conv_w_t, conv_g, batch, seq):
    t = proj.shape[0]
    ts = 512
    per = seq // ts
    w = D_CONV
    blk = lambda j: pl.BlockSpec((ts, w), lambda b, s, j=j: (b * per + s, j))
    return pl.pallas_call(
        _conv_kernel,
        grid=(batch, per),
        in_specs=[blk(0), blk(1), blk(2),
                  pl.BlockSpec((CONV_WIDTH, w), lambda b, s: (0, 0)),
                  pl.BlockSpec((1, w), lambda b, s: (0, 0))],
        out_specs=pl.BlockSpec((ts, w), lambda b, s: (b * per + s, 0)),
        out_shape=jax.ShapeDtypeStruct((t, w), BF16),
        scratch_shapes=[pltpu.VMEM((8 + ts, w), F32)],
        compiler_params=_cparams(("arbitrary", "arbitrary")),
        name="short_conv",
    )(proj, proj, proj, conv_w_t, conv_g)


def _compress_kernel(kv_ref, pe_ref, w1_ref, w2_ref, o_ref, ot_ref):
    h = _from_token_rows(kv_ref, kv_ref.shape[0] // CMP_STRIDE)
    n_h, half = h.shape
    pe = pe_ref[0]
    a = _dot((h + pe[:, :half]).astype(BF16), w1_ref[0, :half, :])
    b = _dot((h + pe[:, half:]).astype(BF16), w1_ref[0, half:, :])
    pre = a + pltpu.roll(b, n_h - 1, 0)
    out = _dot(jax.nn.gelu(pre).astype(BF16), w2_ref[0])
    o_ref[0, 0] = out.astype(o_ref.dtype)
    ot_ref[0, 0] = out.T.astype(ot_ref.dtype)


def compress(kvc, pe_flat, w1, w2, batch, seq):
    b, four = batch, 2 * N_NSA_KV
    n_h, dh = seq // CMP_STRIDE, CMP_STRIDE * HEAD_DIM
    return pl.pallas_call(
        _compress_kernel,
        grid=(b, four),
        in_specs=[
            pl.BlockSpec((seq, HEAD_DIM), lambda i, j: (i, j)),
            pl.BlockSpec((1, 1, 2 * dh), lambda i, j: (j // N_NSA_KV, 0, 0)),
            pl.BlockSpec((1, 2 * dh, CMP_HIDDEN), lambda i, j: (j // N_NSA_KV, 0, 0)),
            pl.BlockSpec((1, CMP_HIDDEN, HEAD_DIM), lambda i, j: (j // N_NSA_KV, 0, 0)),
        ],
        out_specs=[pl.BlockSpec((1, 1, n_h, HEAD_DIM), lambda i, j: (i, j, 0, 0)),
                   pl.BlockSpec((1, 1, HEAD_DIM, n_h), lambda i, j: (i, j, 0, 0))],
        out_shape=[jax.ShapeDtypeStruct((b, four, n_h, HEAD_DIM), BF16),
                   jax.ShapeDtypeStruct((b, four, HEAD_DIM, n_h), BF16)],
        compiler_params=_cparams(("arbitrary", "arbitrary")),
        name="nsa_compress",
    )(kvc, pe_flat, w1, w2)


def _nsa_kernel(q_ref, kc_ref, vct_ref, ks_ref, kw_ref, vst_ref, vwt_ref, gate_ref, ovt_ref, eaug_ref, g_ref,
                o_ref, kaug_scr, qaug_scr, s_scr, m_scr, l_scr, acc_scr, out_scr, *, n_cmp, n_slc, n_sel):
    tq = q_ref.shape[0]
    rows = NSA_REP * tq
    i = pl.program_id(2)
    t0 = i * tq
    hd = HEAD_DIM

    @pl.when(i == 0)
    def _():
        kaug_scr[:, :hd] = ks_ref[...]
        kaug_scr[:, hd:] = eaug_ref[...]

    for r in range(NSA_REP):
        qaug_scr[r * tq:(r + 1) * tq, :hd] = q_ref[:, r * hd:(r + 1) * hd]

    t_all =t0 + lax.rem(lax.broadcasted_iota(jnp.int32, (1, rows), 1), tq)
    gates_t = jax.nn.sigmoid(gate_ref[...].astype(F32)).T
    gate_rows = [jnp.concatenate([gates_t[3 * r + br:3 * r + br + 1, :] for r in range(NSA_REP)], axis=1)
                 for br in range(3)]

    key_sub = lax.broadcasted_iota(jnp.int32, (tq, 1), 0)
    causal = t0 + key_sub <= t_all

    n_back = WINDOW // tq
    pieces = []
    for back in range(n_back, -1, -1):
        jt = jnp.maximum(i - back, 0)
        sj = _dot_nt(kw_ref[pl.ds(pl.multiple_of(jt * tq, tq), tq), :], qaug_scr[:, :hd])
        if back == 0:
            sj = jnp.where(causal, sj, NEG_INF)
        elif back == n_back:
            sj = jnp.where((jt * tq + key_sub > t_all - WINDOW) & (i >= back), sj, NEG_INF)
        else:
            sj = jnp.where(i >= back, sj, NEG_INF)
        pieces.append((sj, vwt_ref[0, 0, jt]))
    m_w = pieces[0][0].max(axis=0, keepdims=True)
    for sj, _ in pieces[1:]:
        m_w = jnp.maximum(m_w, sj.max(axis=0, keepdims=True))
    l_w = None
    acc_w = None
    for sj, vt_tile in pieces:
        p = jnp.exp2(sj - m_w)
        lj = jnp.sum(p, axis=0, keepdims=True)
        aj = _dot(vt_tile, p.astype(BF16))
        l_w = lj if l_w is None else l_w + lj
        acc_w = aj if acc_w is None else acc_w + aj
    out_scr[...] = (gate_rows[2] / l_w) * acc_w

    kc = kc_ref[0, 0]
    vct = vct_ref[0, 0]
    n_h = kc.shape[0]
    c_idx = lax.broadcasted_iota(jnp.int32, (n_h, 1), 0)
    c_end = jnp.where(c_idx < n_cmp, c_idx * CMP_STRIDE + (CMP_LEN - 1), jnp.iinfo(jnp.int32).max)
    s = _dot_nt(kc, qaug_scr[:, :hd])
    sm = jnp.where(c_end <= t_all, s, NEG_INF)
    e = jnp.exp2(sm - jnp.max(sm, axis=0, keepdims=True))
    inv = jnp.where(t_all >= CMP_LEN - 1, 1.0 / jnp.sum(e, axis=0, keepdims=True), 0.0)
    p = e * inv
    out_scr[...] += gate_rows[0] * _dot(vct, p.astype(BF16))
    psum_t = p[:, 0:tq]
    for r in range(1, NSA_REP):
        psum_t = psum_t + p[:, r * tq:(r + 1) * tq]
    imp_t =_dot(ovt_ref[...], psum_t, precision=lax.Precision.HIGHEST)

    j_idx = lax.broadcasted_iota(jnp.int32, (n_slc, 1), 0)
    jq = (t0 + lax.broadcasted_iota(jnp.int32, (1, tq), 1)) // SLC_LEN
    forced = (j_idx == 0) | (j_idx == jq) | (j_idx == jq - 1)
    val = jnp.where(forced, FORCE_SCORE, jnp.where(j_idx <= jq, imp_t[:n_slc], -1.0))
    bias_t = jnp.full((n_slc, tq), NEG_INF, F32)
    for _ in range(n_sel):
        top = jnp.max(val, axis=0, keepdims=True)
        first = jnp.min(jnp.where(val == top, j_idx, n_slc), axis=0, keepdims=True)
        hit = j_idx == first
        bias_t = jnp.where(hit, 0.0, bias_t)
        val = jnp.where(hit, -jnp.inf, val)
    if n_slc < LANE:
        bias_t = jnp.concatenate([bias_t, jnp.zeros((LANE - n_slc, tq), F32)], axis=0)
    bias = bias_t.T.astype(BF16)
    for r in range(NSA_REP):
        qaug_scr[r * tq:(r + 1) * tq, hd:] = bias

    def sel_scores(jt):
        k0 = pl.multiple_of(jt * tq, tq)
        return _dot_nt(kaug_scr[pl.ds(k0, tq), :], qaug_scr[...])

    def sel_update(s, vt_tile, mask):
        if mask is not None:
            s = jnp.where(mask, s, NEG_INF)
        m_old = m_scr[...]
        m_new = jnp.maximum(m_old, jnp.max(s, axis=0, keepdims=True))
        alpha = jnp.exp2(m_old - m_new)
        p = jnp.exp2(s - m_new)
        l_scr[...] = alpha * l_scr[...] + jnp.sum(p, axis=0, keepdims=True)
        acc_scr[...] = alpha * acc_scr[...] + _dot(vt_tile, p.astype(BF16))
        m_scr[...] = m_new

    m_scr[...] = jnp.full(m_scr.shape, NEG_INF, F32)
    l_scr[...] = jnp.zeros(l_scr.shape, F32)
    acc_scr[...] = jnp.zeros(acc_scr.shape, F32)
    s_scr[...] = sel_scores(0)

    def sel_body(jt, carry):
        s_cur = s_scr[...]
        s_next = sel_scores(jt + 1)
        sel_update(s_cur, vst_ref[0, 0, jt], None)
        s_scr[...] = s_next
        return carry

    lax.fori_loop(0, i, sel_body, 0)
    sel_update(s_scr[...], vst_ref[0, 0, i], causal)
    out_scr[...] += (gate_rows[1] / l_scr[...]) * acc_scr[...]


    for r in range(NSA_REP):
        o_t = out_scr[:, r * tq:(r + 1) * tq]
        o_t = o_t * lax.rsqrt(jnp.mean(o_t * o_t, axis=0, keepdims=True) + EPS)
        sl = slice(r * hd, (r + 1) * hd)
        o_ref[:, sl] = (o_t.T * g_ref[:, sl]).astype(o_ref.dtype)


def _overlap_matrix_t(n_h, n_cmp, n_slc):
    cs = np.arange(n_cmp) * CMP_STRIDE
    js = np.arange(n_slc) * SLC_LEN
    ov = np.minimum(cs[:, None] + CMP_LEN, js[None, :] + SLC_LEN) - np.maximum(cs[:, None], js[None, :])
    out = np.zeros((LANE, n_h), np.float32)
    out[:n_slc, :n_cmp] = (np.clip(ov, 0, None) / CMP_LEN).T
    return out


def nsa_attention(q_rot, kc, kct, k_sw, vt_sw, proj, nsa_g, batch, seq):
    t = q_rot.shape[0]
    tq = NSA_TILE
    per = seq // tq
    n_h = seq // CMP_STRIDE
    n_cmp = (seq - CMP_LEN) // CMP_STRIDE + 1
    n_slc = seq // SLC_LEN
    n_sel = min(N_SLC, n_slc)
    assert n_slc <= LANE and WINDOW % tq == 0 and tq % LANE == 0
    ovt = jnp.asarray(_overlap_matrix_t(n_h, n_cmp, n_slc))
    eaug = np.zeros((seq, LANE), np.float32)
    eaug[np.arange(seq), np.arange(seq) // SLC_LEN] = 1.0
    eaug = jnp.asarray(eaug, BF16)
    hd = HEAD_DIM
    qw = NSA_REP * hd
    rows = NSA_REP * tq
    seq_spec = lambda off: pl.BlockSpec((seq, hd), lambda b, g, i, off=off: (b, off + g))
    vt_spec = lambda off: pl.BlockSpec((1, 1, per, hd, tq), lambda b, g, i, off=off: (b, off + g, 0, 0, 0))
    kern = functools.partial(_nsa_kernel, n_cmp=n_cmp, n_slc=n_slc, n_sel=n_sel)
    return pl.pallas_call(
        kern,
        grid=(batch, N_NSA_KV, per),
        in_specs=[
            pl.BlockSpec((tq, qw), lambda b, g, i: (b * per + i, g)),
            pl.BlockSpec((1, 1, n_h, hd), lambda b, g, i: (b, g, 0, 0)),
            pl.BlockSpec((1, 1, hd, n_h), lambda b, g, i: (b, N_NSA_KV + g, 0, 0)),
            seq_spec(0), seq_spec(N_NSA_KV), vt_spec(0), vt_spec(N_NSA_KV),
            pl.BlockSpec((tq, LANE), lambda b, g, i: (b * per + i, COL_GATE // LANE + g)),
            pl.BlockSpec((LANE, n_h), lambda b, g, i: (0, 0)),
            pl.BlockSpec((seq, LANE), lambda b, g, i: (0, 0)),
            pl.BlockSpec((1, qw), lambda b, g, i: (0, g)),
        ],
        out_specs=pl.BlockSpec((tq, qw), lambda b, g, i: (b * per + i, g)),
        out_shape=jax.ShapeDtypeStruct((t, D_NSA), BF16),
        scratch_shapes=[
            pltpu.VMEM((seq, 2 * hd), BF16),
            pltpu.VMEM((rows, 2 * hd), BF16),
            pltpu.VMEM((tq, rows), F32),
            pltpu.VMEM((1, rows), F32),
            pltpu.VMEM((1, rows), F32),
            pltpu.VMEM((hd, rows), F32),
            pltpu.VMEM((hd, rows), F32),
        ],
        compiler_params=_cparams(("arbitrary", "arbitrary", "arbitrary")),
        name="nsa_attention",
    )(q_rot, kc, kct, k_sw, k_sw, vt_sw, vt_sw, proj, ovt, eaug, nsa_g)


def _retention_kernel(q_ref, k_ref, v_ref, gate_ref, cos_ref, sin_ref, lg_ref, g_ref, o_ref, state):
    c = q_ref.shape[0]
    hd = HEAD_DIM

    @pl.when(pl.program_id(1) == 0)
    def _():
        state[...] = jnp.zeros(state.shape, F32)

    cos, sin = cos_ref[...], sin_ref[...]
    n_row = lax.broadcasted_iota(jnp.int32, (c, 1), 0).astype(F32)
    n_col = lax.broadcasted_iota(jnp.int32, (1, c), 1).astype(F32)
    diff = n_row - n_col
    for h in range(N_RET_HEADS):
        sl = slice(h * hd, (h + 1) * hd)
        lg = lg_ref[h][:, 0:1]
        q = _rotate(q_ref[:, sl].astype(F32), cos, sin, hd // 2)
        k = _rotate(k_ref[:, sl].astype(F32), cos, sin, hd // 2) * hd ** -0.5
        v = v_ref[:, sl].astype(BF16)
        decay = jnp.where(diff >= 0.0, jnp.exp(jnp.maximum(diff, 0.0) * lg), 0.0)
        scores = _dot_nt(q.astype(BF16), k.astype(BF16)) * decay
        o = _dot(scores.astype(BF16), v)
        xi = jnp.exp((n_row + 1.0) * lg)
        o = o + _dot((q * xi).astype(BF16), state[h].astype(BF16))
        zeta = jnp.exp((c - 1.0 - n_row) * lg)
        kz_t = (k * zeta).T.astype(BF16)
        state[h] = state[h] * jnp.exp(c * lg) + _dot(kz_t, v)
        gate = gate_ref[:, sl].astype(F32)
        o_ref[:, sl] = (gate * jax.nn.sigmoid(gate) * (_rms(o) * g_ref[:, sl])).astype(o_ref.dtype)


def retention(proj, cos, sin, ret_g, batch, seq):
    t = proj.shape[0]
    c = 256
    per = seq // c
    hd = HEAD_DIM
    base = COL_RET // D_RET
    blk = lambda off: pl.BlockSpec((c, D_RET), lambda b, s, off=off: (b * per + s, base + off))
    tab = pl.BlockSpec((c, hd), lambda b, s: (b * per + s, 0))
    lg = jnp.log1p(-(2.0 ** (-5.0 - jnp.arange(N_RET_HEADS, dtype=F32))))
    lg = jnp.broadcast_to(lg[:, None, None], (N_RET_HEADS, 1, LANE))
    return pl.pallas_call(
        _retention_kernel,
        grid=(batch, per),
        in_specs=[blk(0), blk(1), blk(2), blk(3), tab, tab,
                  pl.BlockSpec((N_RET_HEADS, 1, LANE), lambda b, s: (0, 0, 0)),
                  pl.BlockSpec((1, D_RET), lambda b, s: (0, 0))],
        out_specs=pl.BlockSpec((c, D_RET), lambda b, s: (b * per + s, 0)),
        out_shape=jax.ShapeDtypeStruct((t, D_RET), BF16),
        scratch_shapes=[pltpu.VMEM((N_RET_HEADS, hd, hd), F32)],
        compiler_params=_cparams(("arbitrary", "arbitrary")),
        name="retention",
    )(proj, proj, proj, proj, cos, sin, lg, ret_g)


def _out_proj_kernel(yc_ref, yn_ref, yr_ref, w_ref, x_ref, g_ref, o_ref):
    acc = _dot(yc_ref[...], w_ref[0:D_CONV, :])
    acc = acc + _dot(yn_ref[...], w_ref[D_CONV:D_CONV + D_NSA, :])
    acc = acc + _dot(yr_ref[...], w_ref[D_CONV + D_NSA:, :])
    o_ref[...] = x_ref[...] + g_ref[0] * acc


def out_projection(y_conv, y_nsa, y_ret, w_out, x, mod, seq):
    t, d = x.shape
    tm = 512
    per = seq // tm
    row = lambda w: pl.BlockSpec((tm, w), lambda i: (i, 0))
    return pl.pallas_call(
        _out_proj_kernel,
        grid=(t // tm,),
        in_specs=[row(D_CONV), row(D_NSA), row(D_RET),
                  pl.BlockSpec((D_MIX, d), lambda i: (0, 0)),
                  row(d),
                  pl.BlockSpec((1, 1, d), lambda i: (i // per, 0, 2))],
        out_specs=row(d),
        out_shape=jax.ShapeDtypeStruct((t, d), F32),
        compiler_params=_cparams(("arbitrary",)),
        name="out_projection",
    )(y_conv, y_nsa, y_ret, w_out, x, mod)


def _rank_rows(val, n):
    idx = lax.broadcasted_iota(jnp.int32, (n, 1), 0)
    rank = jnp.zeros(val.shape, jnp.int32)
    for rp in range(n):
        vp = val[rp:rp + 1, :]
        ahead = (vp > val) | ((vp == val) & (idx > rp))
        rank = rank + ahead.astype(jnp.int32)
    return rank


def _to_token_rows(ref, val):
    chunks = val.shape[1] // LANE
    for c in range(chunks):
        ref[pl.ds(c, val.shape[0], stride=chunks), :] = val[:, c * LANE:(c + 1) * LANE]


def _from_token_rows(ref, n):
    chunks = ref.shape[0] // n
    return jnp.concatenate([ref[pl.ds(c, n, stride=chunks), :] for c in range(chunks)], axis=1)


def _pack_bf16_pairs(v):
    half = v.shape[1] // 2
    lo = pltpu.bitcast(v[:, :half].astype(BF16).astype(F32), jnp.uint32)
    hi = pltpu.bitcast(v[:, half:].astype(BF16).astype(F32), jnp.uint32)
    return hi | (lo >> 16)


def _unpack_bf16_pairs(u):
    return pltpu.bitcast(u << 16, F32), pltpu.bitcast(u & jnp.uint32(0xFFFF0000), F32)


def _router_kernel(x_ref, sh_ref, sc_ref, g_ref, rw_ref, rb_ref, tri_ref, swgu_ref, swd_ref,
                   shared_ref, hrow_ref, pos_ref, w_ref, cnt_ref, carry):
    @pl.when(pl.program_id(0) == 0)
    def _():
        carry[...] = jnp.zeros(carry.shape, F32)

    h = _norm_mod(x_ref[...], g_ref[...], sc_ref[0], sh_ref[0])
    _to_token_rows(hrow_ref, _pack_bf16_pairs(h))
    h_hi = h.astype(BF16)
    shared_ref[...] = _swiglu(h_hi, swgu_ref[...], swd_ref[...], D_SHARED)
    tm = h.shape[0]
    rw = rw_ref[...]
    rw_hi = rw.astype(BF16)
    rw_lo = (rw - rw_hi.astype(F32)).astype(BF16)
    h_lo = (h - h_hi.astype(F32)).astype(BF16)
    both = _dot_nt(jnp.concatenate([rw_hi, rw_lo], axis=0), h_hi)
    scores = jax.nn.sigmoid(both[:N_EXPERTS] + both[N_EXPERTS:] + _dot_nt(rw_hi, h_lo))
    biased = scores + rb_ref[...]
    sub = lax.broadcasted_iota(jnp.int32, (GROUP_SIZE, 1), 0)
    gs = []
    for g in range(N_EXPERT_GROUPS):
        bg = biased[g * GROUP_SIZE:(g + 1) * GROUP_SIZE, :]
        m1 = jnp.max(bg, axis=0, keepdims=True)
        first = jnp.min(jnp.where(bg == m1, sub, GROUP_SIZE), axis=0, keepdims=True)
        m2 = jnp.max(jnp.where(sub == first, -jnp.inf, bg), axis=0, keepdims=True)
        gs.append(m1 + m2)
    gs = jnp.concatenate(gs, axis=0)
    gkeep = _rank_rows(gs, N_EXPERT_GROUPS) < TOPK_GROUPS
    keep = jnp.concatenate(
        [jnp.broadcast_to(gkeep[g:g + 1, :], (GROUP_SIZE, tm)) for g in range(N_EXPERT_GROUPS)], axis=0)
    masked = jnp.where(keep, biased, -jnp.inf)
    e_idx = lax.broadcasted_iota(jnp.int32, (N_EXPERTS, 1), 0)
    picked = jnp.zeros(masked.shape, F32)
    for _ in range(TOP_K):
        top = jnp.max(masked, axis=0, keepdims=True)
        first = jnp.min(jnp.where(masked == top, e_idx, N_EXPERTS), axis=0, keepdims=True)
        hit = e_idx == first
        picked = jnp.where(hit, 1.0, picked)
        masked = jnp.where(hit, -jnp.inf, masked)
    sel = picked > 0.0
    w = jnp.where(sel, scores, 0.0)
    w_ref[...] = ROUTED_SCALE * w / jnp.sum(w, axis=0, keepdims=True)
    self32 = sel.astype(F32)
    pos = carry[:, 0:1] + _dot(self32.astype(BF16), tri_ref[...])
    pos_ref[...] = jnp.where(sel, pos.astype(jnp.int32), -1)
    carry[...] = carry[...] + jnp.sum(self32, axis=1, keepdims=True)
    cnt_ref[...] = carry[...].astype(jnp.int32)


def moe_route(x, mod, norm_g, router_w_t, router_bias, shared_w_gu, shared_w_down, seq):
    t, d = x.shape
    tm = 512
    per = seq // tm
    tri = jnp.asarray(np.triu(np.ones((tm, tm), np.float32), 1), BF16)
    ecol = pl.BlockSpec((N_EXPERTS, tm), lambda i: (0, i))
    return pl.pallas_call(
        _router_kernel,
        grid=(t // tm,),
        in_specs=[
            pl.BlockSpec((tm, d), lambda i: (i, 0)),
            pl.BlockSpec((1, 1, d), lambda i: (i // per, 0, 3)),
            pl.BlockSpec((1, 1, d), lambda i: (i // per, 0, 4)),
            pl.BlockSpec((1, d), lambda i: (0, 0)),
            pl.BlockSpec((N_EXPERTS, d), lambda i: (0, 0)),
            pl.BlockSpec((N_EXPERTS, 1), lambda i: (0, 0)),
            pl.BlockSpec((tm, tm), lambda i: (0, 0)),
            pl.BlockSpec((d, 2 * D_SHARED), lambda i: (0, 0)),
            pl.BlockSpec((D_SHARED, d), lambda i: (0, 0)),
        ],
        out_specs=[pl.BlockSpec((tm, d), lambda i: (i, 0)),
                   pl.BlockSpec((tm * PACK_CHUNKS, LANE), lambda i: (i, 0)), ecol, ecol,
                   pl.BlockSpec((N_EXPERTS, LANE), lambda i: (0, 0))],
        out_shape=[
            jax.ShapeDtypeStruct((t, d), F32),
            jax.ShapeDtypeStruct((t * PACK_CHUNKS, LANE), jnp.uint32),
            jax.ShapeDtypeStruct((N_EXPERTS, t), jnp.int32),
            jax.ShapeDtypeStruct((N_EXPERTS, t), F32),
            jax.ShapeDtypeStruct((N_EXPERTS, LANE), jnp.int32),
        ],
        scratch_shapes=[pltpu.VMEM((N_EXPERTS, LANE), F32)],
        compiler_params=_cparams(("arbitrary",)),
        name="moe_route",
    )(x, mod, mod, norm_g, router_w_t, router_bias, tri, shared_w_gu, shared_w_down)


def _slots_kernel(pos_ref, w_ref, start_ref, tri_ref, slot_ref, w8_ref):
    pos = pos_ref[...]
    sel = pos >= 0
    slot = pos + start_ref[...]
    order = _dot(tri_ref[...], sel.astype(F32).astype(BF16))
    w = w_ref[...]
    for k in range(TOP_K):
        mk = sel & (order == float(k))
        slot_ref[k:k + 1, :] = jnp.sum(jnp.where(mk, slot, 0), axis=0, keepdims=True)
        w8_ref[k:k + 1, :] = jnp.sum(jnp.where(mk, w, 0.0), axis=0, keepdims=True)


def moe_slots(pos_t, w_t, pad_start):
    e, t = pos_t.shape
    tm = 2048
    tri = jnp.asarray(np.tril(np.ones((e, e), np.float32), -1), BF16)
    ecol = pl.BlockSpec((e, tm), lambda i: (0, i))
    kcol = pl.BlockSpec((TOP_K, tm), lambda i: (0, i))
    return pl.pallas_call(
        _slots_kernel,
        grid=(t // tm,),
        in_specs=[ecol, ecol, pl.BlockSpec((e, 1), lambda i: (0, 0)), pl.BlockSpec((e, e), lambda i: (0, 0))],
        out_specs=[kcol, kcol],
        out_shape=[jax.ShapeDtypeStruct((TOP_K, t), jnp.int32), jax.ShapeDtypeStruct((TOP_K, t), F32)],
        compiler_params=_cparams(("arbitrary",)),
        name="moe_slots",
    )(pos_t, w_t, pad_start, tri)


def _inverse_kernel(slot_ref, init_ref, inv_ref, sem, *, n_tok):
    tm = slot_ref.shape[0] // TOP_K
    i = pl.program_id(0)

    @pl.when(i == 0)
    def _():
        cp = pltpu.make_async_copy(init_ref, inv_ref, sem)
        cp.start()
        cp.wait()

    def body(t, carry):
        for k in range(TOP_K):
            inv_ref[slot_ref[t * TOP_K + k]] = k * n_tok + i * tm + t
        return carry

    lax.fori_loop(0, tm, body, 0, unroll=2)


def moe_inverse(slot8, n_slots):
    n_tok = slot8.shape[1]
    tm = 2048
    init = TOP_K * n_tok + jnp.arange(n_slots, dtype=jnp.int32)
    slot_flat = slot8.T.reshape(-1)
    return pl.pallas_call(
        functools.partial(_inverse_kernel, n_tok=n_tok),
        grid=(n_tok // tm,),
        in_specs=[pl.BlockSpec((TOP_K * tm,), lambda i: (i,), memory_space=pltpu.SMEM),
                  pl.BlockSpec(memory_space=pl.ANY)],
        out_specs=pl.BlockSpec(memory_space=pltpu.SMEM),
        out_shape=jax.ShapeDtypeStruct((n_slots,), jnp.int32),
        scratch_shapes=[pltpu.SemaphoreType.DMA(())],
        compiler_params=_cparams(("arbitrary",)),
        name="moe_inverse",
    )(slot_flat, init)


def _swiglu(x_bf16, w_gu, w_down, d_hidden):
    gu = _dot(x_bf16, w_gu)
    gate, up = gu[:, :d_hidden], gu[:, d_hidden:]
    return _dot((gate * jax.nn.sigmoid(gate) * up).astype(BF16), w_down)


def _experts_kernel(be_ref, first_ref, nb_ref, src_ref, dst_ref, h_hbm, wgu_ref, wd_ref, y_hbm,
                    x0, x1, x2, y0, y1, y2, wgu_bf, wd_bf, gsem, ssem, *, n_tok, n_blocks):
    del be_ref
    s = pl.program_id(0)
    cb = s - 2
    used = (cb >= 0) & (cb < nb_ref[0])
    rc = PACK_CHUNKS
    rows = x0.shape[0] // rc
    blocks_per_chunk = SMEM_CHUNK // rows
    src_off = lax.rem(jnp.minimum(s, n_blocks - 1), blocks_per_chunk) * rows
    dst_off = lax.rem(s, blocks_per_chunk) * rows

    @pl.when(s == 0)
    def _():
        for y in (y0, y1, y2):
            y[...] = jnp.zeros(y.shape, y.dtype)

    @pl.when(used & (first_ref[jnp.clip(cb, 0, n_blocks - 1)] == 1))
    def _():
        wgu_bf[...] = wgu_ref[0, 0].astype(BF16)
        wd_bf[...] = wd_ref[0, 0].astype(BF16)

    def step(p, x_out, x_in, y_out, y_in, x_mid, y_mid):
        q = (p + 1) % 3

        def gathered(buf, sem):
            return pltpu.make_async_copy(h_hbm.at[pl.ds(0, rows * rc), :], buf, sem)

        def scattered(buf, sem):
            return pltpu.make_async_copy(buf, y_hbm.at[pl.ds(0, rows * rc), :], sem)

        last_live = nb_ref[0] + 2
        last = pl.num_programs(0) - 1

        @pl.when((s >= 2) & (s - 2 <= last_live))
        def _():
            gathered(x_in, gsem.at[q]).wait()
            scattered(y_out, ssem.at[q]).wait()

        def issue_rows():
            for r in range(rows):
                src = pl.multiple_of((src_ref[src_off + r] & (n_tok - 1)) * rc, rc)
                dst = pl.multiple_of(dst_ref[dst_off + r] * rc, rc)
                pltpu.make_async_copy(h_hbm.at[pl.ds(src, rc), :], x_out.at[pl.ds(r * rc, rc), :], gsem.at[p]).start()
                pltpu.make_async_copy(y_in.at[pl.ds(r * rc, rc), :], y_hbm.at[pl.ds(dst, rc), :], ssem.at[p]).start()

        @pl.when(used)
        def _():
            issue_rows()
            x = jnp.concatenate(_unpack_bf16_pairs(_from_token_rows(x_in, rows)), axis=1).astype(BF16)
            _to_token_rows(y_out, _pack_bf16_pairs(_swiglu(x, wgu_bf[...], wd_bf[...], D_EXPERT)))

        @pl.when(jnp.logical_not(used) & (s <= last_live))
        def _():
            issue_rows()

        r = (p + 2) % 3

        @pl.when((s == last) & (s - 1 <= last_live))
        def _():
            gathered(x_mid, gsem.at[r]).wait()
            scattered(y_mid, ssem.at[r]).wait()

        @pl.when((s == last) & (s <= last_live))
        def _():
            gathered(x_out, gsem.at[p]).wait()
            scattered(y_in, ssem.at[p]).wait()

    xs, ys = (x0, x1, x2), (y0, y1, y2)
    for p in range(3):
        @pl.when(lax.rem(s, 3) == p)
        def _(p=p):
            step(p, xs[p], xs[(p + 1) % 3], ys[(p + 1) % 3], ys[p], xs[(p + 2) % 3], ys[(p + 2) % 3])


def moe_experts(block_e, first, n_used, inv, h_rows, w_gu, w_down, layer):
    n_tok, d = h_rows.shape[0] // PACK_CHUNKS, D_MODEL
    n_slots = inv.shape[0]
    n_blocks = n_slots // MOE_BLOCK
    assert n_tok & (n_tok - 1) == 0, "source token = inv mod T uses a power-of-two T"
    assert n_slots % SMEM_CHUNK == 0 and SMEM_CHUNK % MOE_BLOCK == 0
    per_chunk = SMEM_CHUNK // MOE_BLOCK
    lag = 3
    n_rows = TOP_K * n_tok + n_slots + lag * MOE_BLOCK
    warmup = TOP_K * n_tok + n_slots + jnp.arange(lag * MOE_BLOCK, dtype=jnp.int32)
    tail = jnp.zeros((SMEM_CHUNK - lag * MOE_BLOCK,), jnp.int32)
    dst = jnp.concatenate([warmup, inv, tail])
    clamp = lambda b: jnp.clip(b, 0, n_blocks - 1)
    smem_blk = lambda f: pl.BlockSpec((SMEM_CHUNK,), f, memory_space=pltpu.SMEM)
    grid_spec = pltpu.PrefetchScalarGridSpec(
        num_scalar_prefetch=3,
        grid=(n_blocks + lag,),
        in_specs=[
            smem_blk(lambda s, be, fi, nb: (clamp(s) // per_chunk,)),
            smem_blk(lambda s, be, fi, nb: (s // per_chunk,)),
            pl.BlockSpec(memory_space=pl.ANY),
            pl.BlockSpec((1, 1, d, 2 * D_EXPERT), lambda s, be, fi, nb: (layer, be[clamp(s - 2)], 0, 0)),
            pl.BlockSpec((1, 1, D_EXPERT, d), lambda s, be, fi, nb: (layer, be[clamp(s - 2)], 0, 0)),
        ],
        out_specs=pl.BlockSpec(memory_space=pl.ANY),
        scratch_shapes=[pltpu.VMEM((MOE_BLOCK * PACK_CHUNKS, LANE), jnp.uint32)] * 6 + [
            pltpu.VMEM((d, 2 * D_EXPERT), BF16), pltpu.VMEM((D_EXPERT, d), BF16),
            pltpu.SemaphoreType.DMA((3,)), pltpu.SemaphoreType.DMA((3,))],
    )
    return pl.pallas_call(
        functools.partial(_experts_kernel, n_tok=n_tok, n_blocks=n_blocks),
        grid_spec=grid_spec,
        out_shape=jax.ShapeDtypeStruct((n_rows * PACK_CHUNKS, LANE), jnp.uint32),
        compiler_params=_cparams(("arbitrary",)),
        name="moe_experts",
    )(block_e, first, n_used, inv, dst, h_rows, w_gu, w_down)


def _combine_kernel(w_ref, sh_ref, x_ref, g_ref, fg_ref, *rest, final_norm):
    y_refs, o_ref, sum_scr = rest[:TOP_K], rest[TOP_K], rest[TOP_K + 1]
    tm = x_ref.shape[0]

    def token(t, carry):
        rows = pl.ds(pl.multiple_of(t * PACK_CHUNKS, PACK_CHUNKS), PACK_CHUNKS)
        lo, hi = _unpack_bf16_pairs(y_refs[0][rows, :])
        acc_lo, acc_hi = w_ref[0, t] * lo, w_ref[0, t] * hi
        for k in range(1, TOP_K):
            lo, hi = _unpack_bf16_pairs(y_refs[k][rows, :])
            acc_lo, acc_hi = acc_lo + w_ref[k, t] * lo, acc_hi + w_ref[k, t] * hi
        base = pl.multiple_of(t * ROW_CHUNKS, ROW_CHUNKS)
        sum_scr[pl.ds(base, PACK_CHUNKS), :] = acc_lo
        sum_scr[pl.ds(base + PACK_CHUNKS, PACK_CHUNKS), :] = acc_hi
        return carry

    lax.fori_loop(0, tm, token, 0, unroll=4)
    routed = _from_token_rows(sum_scr, tm)
    out = x_ref[...] + g_ref[0] * (routed + sh_ref[...])
    if final_norm:
        out = _rms(out) * fg_ref[...]
    o_ref[...] = out


def moe_combine(w8, shared, x, mod, final_g, y, seq, final_norm):
    t, d = x.shape
    tm = 256
    per = seq // tm
    row = pl.BlockSpec((tm, d), lambda i: (i, 0))
    y_specs = [pl.BlockSpec((tm * PACK_CHUNKS, LANE), lambda i, k=k: (k * (t // tm) + i, 0)) for k in range(TOP_K)]
    return pl.pallas_call(
        functools.partial(_combine_kernel, final_norm=final_norm),
        grid=(t // tm,),
        in_specs=[
            pl.BlockSpec((TOP_K, tm), lambda i: (0, i), memory_space=pltpu.SMEM),
            row, row,
            pl.BlockSpec((1, 1, d), lambda i: (i // per, 0, 5)),
            pl.BlockSpec((1, d), lambda i: (0, 0)),
        ] + y_specs,
        out_specs=row,
        out_shape=jax.ShapeDtypeStruct((t, d), F32),
        scratch_shapes=[pltpu.VMEM((tm * ROW_CHUNKS, LANE), F32)],
        compiler_params=_cparams(("arbitrary",)),
        name="moe_combine",
    )(w8, shared, x, mod, final_g, *([y] * TOP_K))


def token_mixer_layer(x, mod, tabs, p, batch, seq):
    cos_n, sin_n, cos_r, sin_r = tabs
    proj = in_projection(x, mod, p["norm1_g"], p["w_in"], seq)
    y_conv = short_conv(proj, p["conv_w_t"], p["conv_g"], batch, seq)
    q_rot, kvc, k_sw, vt_sw = nsa_rope(proj, cos_n, sin_n, batch, seq)
    kc, kct = compress(kvc, p["cmp_pe"], p["cmp_w1"], p["cmp_w2"], batch, seq)
    y_nsa = nsa_attention(q_rot, kc, kct, k_sw, vt_sw, proj, p["nsa_g"], batch, seq)
    y_ret = retention(proj, cos_r, sin_r, p["ret_g"], batch, seq)
    return out_projection(y_conv, y_nsa, y_ret, p["w_out"], x, mod, seq)


def moe_layer(x, mod, p, seq, final_g, final_norm):
    t, d = x.shape
    n_assign = t * TOP_K
    n_blocks = (n_assign + N_EXPERTS * (MOE_BLOCK - 1) + MOE_BLOCK - 1) // MOE_BLOCK
    shared, h_rows, pos_t, w_t, counts = moe_route(x, mod, p["norm2_g"], p["router_w_t"], p["router_bias"],
                                                   p["shared_w_gu"], p["shared_w_down"], seq)
    counts = counts[:, 0]
    padded = (counts + MOE_BLOCK - 1) // MOE_BLOCK * MOE_BLOCK
    pad_end = jnp.cumsum(padded)
    pad_start = (pad_end - padded).astype(jnp.int32)
    blk0 = jnp.arange(n_blocks, dtype=jnp.int32) * MOE_BLOCK
    block_e = jnp.minimum(jnp.sum(pad_end[None, :] <= blk0[:, None], axis=1), N_EXPERTS - 1).astype(jnp.int32)
    n_used = (pad_end[-1:] // MOE_BLOCK).astype(jnp.int32)
    first = jnp.concatenate([jnp.ones((1,), jnp.int32), (block_e[1:] != block_e[:-1]).astype(jnp.int32)])
    slot8, w8 = moe_slots(pos_t, w_t, pad_start[:, None])
    inv = moe_inverse(slot8, n_blocks * MOE_BLOCK)
    y = moe_experts(block_e, first, n_used, inv, h_rows, p["exp_w_gu"], p["exp_w_down"], p["layer"])
    return moe_combine(w8, shared, x, mod, final_g, y, seq, final_norm)


def _reorder_w_in(w_in):
    c_gate = COL_KV + 6 * D_NSA_KV
    gates = w_in[:, c_gate:c_gate + 3 * N_NSA_HEADS]
    per_group = 3 * NSA_REP
    padded = [jnp.pad(gates[:, g * per_group:(g + 1) * per_group], ((0, 0), (0, LANE - per_group)))
              for g in range(N_NSA_KV)]
    return jnp.concatenate([w_in[:, :c_gate], w_in[:, c_gate + 3 * N_NSA_HEADS:]] + padded, axis=1)


def kernel(x, c, positions, ada_w, ada_b, norm1_g, norm2_g, w_in, conv_w, conv_g, cmp_pe, cmp_w1, cmp_w2,
           nsa_g, ret_g, w_out, router_w, router_bias, exp_w_gu, exp_w_down, shared_w_gu, shared_w_down, final_g):
    batch, seq, d = x.shape
    depth = ada_w.shape[0]
    t = batch * seq
    mod_all = ada_modulation(c, ada_w, ada_b)
    pos_col = positions.reshape(t, 1)
    tabs = rope_tables(pos_col, ROPE_DIM, ROPE_THETA) + rope_tables(pos_col, HEAD_DIM, RET_THETA)
    xt = x.reshape(t, d)
    final_g2 = final_g.reshape(1, d)
    for l in range(depth):
        mod = mod_all[l].reshape(batch, 1, ADA_CHUNKS * d)
        p = dict(
            norm1_g=norm1_g[l].reshape(1, d),
            norm2_g=norm2_g[l].reshape(1, d),
            w_in=_reorder_w_in(w_in[l]).astype(BF16),
            conv_w_t=conv_w[l].T,
            conv_g=conv_g[l].reshape(1, D_CONV),
            cmp_pe=cmp_pe[l].reshape(2, 1, CMP_LEN * HEAD_DIM),
            cmp_w1=cmp_w1[l].astype(BF16),
            cmp_w2=cmp_w2[l].astype(BF16),
            nsa_g=nsa_g[l].reshape(1, D_NSA),
            ret_g=ret_g[l].reshape(1, D_RET),
            w_out=w_out[l].astype(BF16),
            router_w_t=router_w[l].T,
            router_bias=router_bias[l].reshape(N_EXPERTS, 1),
            layer=l,
            exp_w_gu=exp_w_gu,
            exp_w_down=exp_w_down,
            shared_w_gu=shared_w_gu[l].astype(BF16),
            shared_w_down=shared_w_down[l].astype(BF16),
        )
        xt = token_mixer_layer(xt, mod, tabs, p, batch, seq)
        xt = moe_layer(xt, mod, p, seq, final_g2, final_norm=(l == depth - 1))
    return xt.reshape(batch, seq, d)
```

```python
import functools

import numpy as np
import jax
import jax.numpy as jnp
from jax import lax
from jax.experimental import pallas as pl
from jax.experimental.pallas import tpu as pltpu

F32 = jnp.float32
BF16 = jnp.bfloat16

D_MODEL = 2048
HEAD_DIM = 128
D_CONV = D_MODEL // 4
CONV_GROUPS = D_CONV // HEAD_DIM
CONV_WIDTH = 3
N_NSA_HEADS = D_MODEL // 2 // HEAD_DIM
N_NSA_KV = 2
NSA_REP = N_NSA_HEADS // N_NSA_KV
D_NSA = N_NSA_HEADS * HEAD_DIM
D_NSA_KV = N_NSA_KV * HEAD_DIM
CMP_LEN = 32
CMP_STRIDE = 16
CMP_HIDDEN = 256
SLC_LEN = 64
N_SLC = 16
WINDOW = 512
ROPE_THETA = 500000.0
ROPE_DIM = HEAD_DIM // 4
N_RET_HEADS = D_MODEL // 4 // HEAD_DIM
D_RET = N_RET_HEADS * HEAD_DIM
RET_THETA = 10000.0
D_MIX = D_CONV + D_NSA + D_RET
N_EXPERTS = 64
N_EXPERT_GROUPS = 8
GROUP_SIZE = N_EXPERTS // N_EXPERT_GROUPS
TOPK_GROUPS = 4
TOP_K = 8
D_EXPERT = 512
D_SHARED = 512
ROUTED_SCALE = 2.5
MOE_BLOCK = 256
ADA_CHUNKS = 6
EPS = 1e-6
NEG_INF = -1e30
FORCE_SCORE = 1e4

LANE = 128
ROW_CHUNKS = D_MODEL // LANE
PACK_CHUNKS = ROW_CHUNKS // 2
SMEM_CHUNK = 1024
LOG2_E = 1.4426950408889634
NSA_TILE = 256
GATE_COLS = N_NSA_KV * LANE
D_PROJ = 3 * D_CONV + D_NSA + 6 * D_NSA_KV + 4 * D_RET + GATE_COLS
COL_Q = 3 * D_CONV
COL_KV = COL_Q + D_NSA
COL_RET = COL_KV + 6 * D_NSA_KV
COL_GATE = COL_RET + 4 * D_RET

VMEM_LIMIT = 56 * 1024 * 1024


def _cparams(sem):
    return pltpu.CompilerParams(dimension_semantics=sem, vmem_limit_bytes=VMEM_LIMIT)


def _dot(a, b, **kw):
    return jnp.dot(a, b, preferred_element_type=F32, **kw)


def _dot_nt(a, b, **kw):
    return lax.dot_general(a, b, (((1,), (1,)), ((), ())), preferred_element_type=F32, **kw)


def _rms(x):
    return x * lax.rsqrt(jnp.mean(x * x, axis=-1, keepdims=True) + EPS)


def _ada_kernel(c_ref, w_ref, b_ref, o_ref):
    c = c_ref[...]
    ca = (c * jax.nn.sigmoid(c)).astype(BF16)
    o_ref[0] = _dot(ca, w_ref[0].astype(BF16)) + b_ref[0]


def ada_modulation(c, ada_w, ada_b):
    L, d, n = ada_w.shape
    b = c.shape[0]
    tn = 1024
    return pl.pallas_call(
        _ada_kernel,
        grid=(L, n // tn),
        in_specs=[
            pl.BlockSpec((b, d), lambda l, j: (0, 0)),
            pl.BlockSpec((1, d, tn), lambda l, j: (l, 0, j)),
            pl.BlockSpec((1, 1, tn), lambda l, j: (l, 0, j)),
        ],
        out_specs=pl.BlockSpec((1, b, tn), lambda l, j: (l, 0, j)),
        out_shape=jax.ShapeDtypeStruct((L, b, n), F32),
        compiler_params=_cparams(("arbitrary", "arbitrary")),
        name="ada_modulation",
    )(c, ada_w, ada_b.reshape(L, 1, n))


def _norm_mod(x, g, sc, sh):
    return (_rms(x) * g) * (1.0 + sc) + sh


def _in_proj_kernel(x_ref, sh_ref, sc_ref, g_ref, w_ref, o_ref, h_scr):
    @pl.when(pl.program_id(1) == 0)
    def _():
        h_scr[...] = _norm_mod(x_ref[...], g_ref[...], sc_ref[0], sh_ref[0]).astype(BF16)

    o_ref[...] = _dot(h_scr[...], w_ref[...]).astype(o_ref.dtype)


def in_projection(x, mod, norm_g, w, seq):
    t, d = x.shape
    n = w.shape[1]
    tm, tn = 512, 1280
    per = seq // tm
    return pl.pallas_call(
        _in_proj_kernel,
        grid=(t // tm, n // tn),
        in_specs=[
            pl.BlockSpec((tm, d), lambda i, j: (i, 0)),
            pl.BlockSpec((1, 1, d), lambda i, j: (i // per, 0, 0)),
            pl.BlockSpec((1, 1, d), lambda i, j: (i // per, 0, 1)),
            pl.BlockSpec((1, d), lambda i, j: (0, 0)),
            pl.BlockSpec((d, tn), lambda i, j: (0, j)),
        ],
        out_specs=pl.BlockSpec((tm, tn), lambda i, j: (i, j)),
        out_shape=jax.ShapeDtypeStruct((t, n), BF16),
        scratch_shapes=[pltpu.VMEM((tm, d), BF16)],
        compiler_params=_cparams(("arbitrary", "arbitrary")),
        name="in_projection",
    )(x, mod, mod, norm_g, w)


def _rope_table_kernel(pos_ref, inv_ref, sgn_ref, cos_ref, sin_ref):
    ang = pos_ref[...].astype(F32) * inv_ref[...]
    cos_ref[...] = jnp.cos(ang)
    sin_ref[...] = jnp.sin(ang) * sgn_ref[...]


def rope_tables(pos_col, rot_dim, theta):
    t = pos_col.shape[0]
    half = rot_dim // 2
    inv_half = theta ** (-jnp.arange(half, dtype=F32) / half)
    inv = jnp.concatenate([inv_half, inv_half, jnp.zeros((LANE - rot_dim,), F32)]).reshape(1, LANE)
    sgn = np.zeros((1, LANE), np.float32)
    sgn[0, :half] = -1.0
    sgn[0, half:rot_dim] = 1.0
    ts = 1024
    return pl.pallas_call(
        _rope_table_kernel,
        grid=(t // ts,),
        in_specs=[
            pl.BlockSpec((ts, 1), lambda i: (i, 0)),
            pl.BlockSpec((1, LANE), lambda i: (0, 0)),
            pl.BlockSpec((1, LANE), lambda i: (0, 0)),
        ],
        out_specs=[pl.BlockSpec((ts, LANE), lambda i: (i, 0))] * 2,
        out_shape=[jax.ShapeDtypeStruct((t, LANE), F32)] * 2,
        compiler_params=_cparams(("arbitrary",)),
        name="rope_tables",
    )(pos_col, inv, jnp.asarray(sgn))


def _rotate(x, cos, sin_signed, half):
    if 2 * half == LANE:
        swapped = pltpu.roll(x, half, 1)
    else:
        lane = lax.broadcasted_iota(jnp.int32, x.shape, 1)
        swapped = jnp.where(lane < half, pltpu.roll(x, LANE - half, 1), pltpu.roll(x, half, 1))
    return x * cos + swapped * sin_signed


def _nsa_rope_kernel(q0_ref, q1_ref, kvc_ref, kvs_ref, kvw_ref, cos_ref, sin_ref,
                     q_out, kvc_out, k_out, vt_out):
    cos, sin = cos_ref[...], sin_ref[...]
    half = ROPE_DIM // 2
    hpb = D_CONV // HEAD_DIM
    scale = HEAD_DIM ** -0.5 * LOG2_E
    for blk, src in enumerate((q0_ref, q1_ref)):
        for h in range(hpb):
            sl = slice(h * HEAD_DIM, (h + 1) * HEAD_DIM)
            q_out[:, blk * D_CONV + h * HEAD_DIM: blk * D_CONV + (h + 1) * HEAD_DIM] = (
                (_rotate(src[:, sl].astype(F32), cos, sin, half) * scale).astype(q_out.dtype))
    for g in range(N_NSA_KV):
        sl = slice(g * HEAD_DIM, (g + 1) * HEAD_DIM)
        kvc_out[:, sl] = _rotate(kvc_ref[:, sl].astype(F32), cos, sin, half)
    kvc_out[:, D_NSA_KV:] = kvc_ref[:, D_NSA_KV:].astype(F32)
    for br, src in enumerate((kvs_ref, kvw_ref)):
        for g in range(N_NSA_KV):
            sl = slice(g * HEAD_DIM, (g + 1) * HEAD_DIM)
            dst = slice((br * N_NSA_KV + g) * HEAD_DIM, (br * N_NSA_KV + g + 1) * HEAD_DIM)
            k_out[:, dst] = _rotate(src[:, sl].astype(F32), cos, sin, half).astype(k_out.dtype)
            vsl = slice(D_NSA_KV + g * HEAD_DIM, D_NSA_KV + (g + 1) * HEAD_DIM)
            for tile in range(vt_out.shape[2]):
                rows = slice(tile * NSA_TILE, (tile + 1) * NSA_TILE)
                vt_out[0, br * N_NSA_KV + g, tile] = src[rows, vsl].astype(F32).T.astype(vt_out.dtype)


def nsa_rope(proj, cos, sin, batch, seq):
    t = proj.shape[0]
    ts = 512
    per = seq // ts
    w = D_CONV
    blk = lambda j: pl.BlockSpec((ts, w), lambda i, j=j: (i, j))
    tab = pl.BlockSpec((ts, LANE), lambda i: (i, 0))
    out = pl.BlockSpec((ts, w), lambda i: (i, 0))
    qb = COL_Q // w
    kb = COL_KV // w
    return pl.pallas_call(
        _nsa_rope_kernel,
        grid=(t // ts,),
        in_specs=[blk(qb), blk(qb + 1), blk(kb), blk(kb + 1), blk(kb + 2), tab, tab],
        out_specs=[pl.BlockSpec((ts, D_NSA), lambda i: (i, 0)), out, out,
                   pl.BlockSpec((1, 2 * N_NSA_KV, ts // NSA_TILE, HEAD_DIM, NSA_TILE),
                                lambda i: (i // per, 0, i % per, 0, 0))],
        out_shape=[
            jax.ShapeDtypeStruct((t, D_NSA), BF16),
            jax.ShapeDtypeStruct((t, w), F32),
            jax.ShapeDtypeStruct((t, w), BF16),
            jax.ShapeDtypeStruct((batch, 2 * N_NSA_KV, seq // NSA_TILE, HEAD_DIM, NSA_TILE), BF16),
        ],
        compiler_params=_cparams(("arbitrary",)),
        name="nsa_rope",
    )(proj, proj, proj, proj, proj, cos, sin)


def _conv_kernel(cb_ref, cc_ref, cu_ref, w_ref, g_ref, o_ref, ext):
    ts = cb_ref.shape[0]

    @pl.when(pl.program_id(1) == 0)
    def _():
        ext[0:8, :] = jnp.zeros((8, ext.shape[1]), F32)

    v = cc_ref[...].astype(F32) * cu_ref[...].astype(F32)
    ext[8:, :] = v
    v1 = ext[pl.ds(7, ts), :]
    v2 = ext[pl.ds(6, ts), :]
    y = cb_ref[...].astype(F32) * (w_ref[0:1, :] * v2 + w_ref[1:2, :] * v1 + w_ref[2:3, :] * v)
    ext[0:8, :] = v[ts - 8:, :]
    for gi in range(CONV_GROUPS):
        sl = slice(gi * HEAD_DIM, (gi + 1) * HEAD_DIM)
        o_ref[:, sl] = (_rms(y[:, sl]) * g_ref[:, sl]).astype(o_ref.dtype)


def short_conv(proj, conv_w_t, conv_g, batch, seq):
    t = proj.shape[0]
    ts = 512
    per = seq // ts
    w = D_CONV
    blk = lambda j: pl.BlockSpec((ts, w), lambda b, s, j=j: (b * per + s, j))
    return pl.pallas_call(
        _conv_kernel,
        grid=(batch, per),
        in_specs=[blk(0), blk(1), blk(2),
                  pl.BlockSpec((CONV_WIDTH, w), lambda b, s: (0, 0)),
                  pl.BlockSpec((1, w), lambda b, s: (0, 0))],
        out_specs=pl.BlockSpec((ts, w), lambda b, s: (b * per + s, 0)),
        out_shape=jax.ShapeDtypeStruct((t, w), BF16),
        scratch_shapes=[pltpu.VMEM((8 + ts, w), F32)],
        compiler_params=_cparams(("arbitrary", "arbitrary")),
        name="short_conv",
    )(proj, proj, proj, conv_w_t, conv_g)


def _compress_kernel(kv_ref, pe_ref, w1_ref, w2_ref, o_ref, ot_ref):
    h = _from_token_rows(kv_ref, kv_ref.shape[0] // CMP_STRIDE)
    n_h, half = h.shape
    pe = pe_ref[0]
    a = _dot((h + pe[:, :half]).astype(BF16), w1_ref[0, :half, :])
    b = _dot((h + pe[:, half:]).astype(BF16), w1_ref[0, half:, :])
    pre = a + pltpu.roll(b, n_h - 1, 0)
    out = _dot(jax.nn.gelu(pre).astype(BF16), w2_ref[0])
    o_ref[0, 0] = out.astype(o_ref.dtype)
    ot_ref[0, 0] = out.T.astype(ot_ref.dtype)


def compress(kvc, pe_flat, w1, w2, batch, seq):
    b, four = batch, 2 * N_NSA_KV
    n_h, dh = seq // CMP_STRIDE, CMP_STRIDE * HEAD_DIM
    return pl.pallas_call(
        _compress_kernel,
        grid=(b, four),
        in_specs=[
            pl.BlockSpec((seq, HEAD_DIM), lambda i, j: (i, j)),
            pl.BlockSpec((1, 1, 2 * dh), lambda i, j: (j // N_NSA_KV, 0, 0)),
            pl.BlockSpec((1, 2 * dh, CMP_HIDDEN), lambda i, j: (j // N_NSA_KV, 0, 0)),
            pl.BlockSpec((1, CMP_HIDDEN, HEAD_DIM), lambda i, j: (j // N_NSA_KV, 0, 0)),
        ],
        out_specs=[pl.BlockSpec((1, 1, n_h, HEAD_DIM), lambda i, j: (i, j, 0, 0)),
                   pl.BlockSpec((1, 1, HEAD_DIM, n_h), lambda i, j: (i, j, 0, 0))],
        out_shape=[jax.ShapeDtypeStruct((b, four, n_h, HEAD_DIM), BF16),
                   jax.ShapeDtypeStruct((b, four, HEAD_DIM, n_h), BF16)],
        compiler_params=_cparams(("arbitrary", "arbitrary")),
        name="nsa_compress",
    )(kvc, pe_flat, w1, w2)


def _nsa_kernel(q_ref, kc_ref, vct_ref, ks_ref, kw_ref, vst_ref, vwt_ref, gate_ref, ovt_ref, eaug_ref, g_ref,
                o_ref, kaug_scr, qaug_scr, s_scr, m_scr, l_scr, acc_scr, out_scr, *, n_cmp, n_slc, n_sel):
    tq = q_ref.shape[0]
    rows = NSA_REP * tq
    i = pl.program_id(2)
    t0 = i * tq
    hd = HEAD_DIM

    @pl.when(i == 0)
    def _():
        kaug_scr[:, :hd] = ks_ref[...]
        kaug_scr[:, hd:] = eaug_ref[...]

    for r in range(NSA_REP):
        qaug_scr[r * tq:(r + 1) * tq, :hd] = q_ref[:, r * hd:(r + 1) * hd]

    t_all =t0 + lax.rem(lax.broadcasted_iota(jnp.int32, (1, rows), 1), tq)
    gates_t = jax.nn.sigmoid(gate_ref[...].astype(F32)).T
    gate_rows = [jnp.concatenate([gates_t[3 * r + br:3 * r + br + 1, :] for r in range(NSA_REP)], axis=1)
                 for br in range(3)]

    key_sub = lax.broadcasted_iota(jnp.int32, (tq, 1), 0)
    causal = t0 + key_sub <= t_all

    n_back = WINDOW // tq
    pieces = []
    for back in range(n_back, -1, -1):
        jt = jnp.maximum(i - back, 0)
        sj = _dot_nt(kw_ref[pl.ds(pl.multiple_of(jt * tq, tq), tq), :], qaug_scr[:, :hd])
        if back == 0:
            sj = jnp.where(causal, sj, NEG_INF)
        elif back == n_back:
            sj = jnp.where((jt * tq + key_sub > t_all - WINDOW) & (i >= back), sj, NEG_INF)
        else:
            sj = jnp.where(i >= back, sj, NEG_INF)
        pieces.append((sj, vwt_ref[0, 0, jt]))
    m_w = pieces[0][0].max(axis=0, keepdims=True)
    for sj, _ in pieces[1:]:
        m_w = jnp.maximum(m_w, sj.max(axis=0, keepdims=True))
    l_w = None
    acc_w = None
    for sj, vt_tile in pieces:
        p = jnp.exp2(sj - m_w)
        lj = jnp.sum(p, axis=0, keepdims=True)
        aj = _dot(vt_tile, p.astype(BF16))
        l_w = lj if l_w is None else l_w + lj
        acc_w = aj if acc_w is None else acc_w + aj
    out_scr[...] = (gate_rows[2] / l_w) * acc_w

    kc = kc_ref[0, 0]
    vct = vct_ref[0, 0]
    n_h = kc.shape[0]
    c_idx = lax.broadcasted_iota(jnp.int32, (n_h, 1), 0)
    c_end = jnp.where(c_idx < n_cmp, c_idx * CMP_STRIDE + (CMP_LEN - 1), jnp.iinfo(jnp.int32).max)
    s = _dot_nt(kc, qaug_scr[:, :hd])
    sm = jnp.where(c_end <= t_all, s, NEG_INF)
    e = jnp.exp2(sm - jnp.max(sm, axis=0, keepdims=True))
    inv = jnp.where(t_all >= CMP_LEN - 1, 1.0 / jnp.sum(e, axis=0, keepdims=True), 0.0)
    p = e * inv
    out_scr[...] += gate_rows[0] * _dot(vct, p.astype(BF16))
    psum_t = p[:, 0:tq]
    for r in range(1, NSA_REP):
        psum_t = psum_t + p[:, r * tq:(r + 1) * tq]
    ov_bf = ovt_ref[...].astype(BF16)
    ps_hi = psum_t.astype(BF16)
    ps_lo = (psum_t - ps_hi.astype(F32)).astype(BF16)
    imp_t = _dot(ov_bf, ps_hi) + _dot(ov_bf, ps_lo)

    j_idx = lax.broadcasted_iota(jnp.int32, (n_slc, 1), 0)
    jq = (t0 + lax.broadcasted_iota(jnp.int32, (1, tq), 1)) // SLC_LEN
    forced = (j_idx == 0) | (j_idx == jq) | (j_idx == jq - 1)
    val = jnp.where(forced, FORCE_SCORE, jnp.where(j_idx <= jq, imp_t[:n_slc], -1.0))
    bias_t = jnp.full((n_slc, tq), NEG_INF, F32)
    for _ in range(n_sel):
        top = jnp.max(val, axis=0, keepdims=True)
        first = jnp.min(jnp.where(val == top, j_idx, n_slc), axis=0, keepdims=True)
        hit = j_idx == first
        bias_t = jnp.where(hit, 0.0, bias_t)
        val = jnp.where(hit, -jnp.inf, val)
    if n_slc < LANE:
        bias_t = jnp.concatenate([bias_t, jnp.zeros((LANE - n_slc, tq), F32)], axis=0)
    bias = bias_t.T.astype(BF16)
    for r in range(NSA_REP):
        qaug_scr[r * tq:(r + 1) * tq, hd:] = bias

    def sel_scores(jt):
        k0 = pl.multiple_of(jt * tq, tq)
        return _dot_nt(kaug_scr[pl.ds(k0, tq), :], qaug_scr[...])

    def sel_update(s, vt_tile, mask):
        if mask is not None:
            s = jnp.where(mask, s, NEG_INF)
        m_old = m_scr[...]
        m_new = jnp.maximum(m_old, jnp.max(s, axis=0, keepdims=True))
        alpha = jnp.exp2(m_old - m_new)
        p = jnp.exp2(s - m_new)
        l_scr[...] = alpha * l_scr[...] + jnp.sum(p, axis=0, keepdims=True)
        acc_scr[...] = alpha * acc_scr[...] + _dot(vt_tile, p.astype(BF16))
        m_scr[...] = m_new

    m_scr[...] = jnp.full(m_scr.shape, NEG_INF, F32)
    l_scr[...] = jnp.zeros(l_scr.shape, F32)
    acc_scr[...] = jnp.zeros(acc_scr.shape, F32)
    s_scr[...] = sel_scores(0)

    def sel_body(jt, carry):
        s_cur = s_scr[...]
        s_next = sel_scores(jt + 1)
        sel_update(s_cur, vst_ref[0, 0, jt], None)
        s_scr[...] = s_next
        return carry

    lax.fori_loop(0, i, sel_body, 0)
    sel_update(s_scr[...], vst_ref[0, 0, i], causal)
    out_scr[...] += (gate_rows[1] / l_scr[...]) * acc_scr[...]


    for r in range(NSA_REP):
        o_t = out_scr[:, r * tq:(r + 1) * tq]
        o_t = o_t * lax.rsqrt(jnp.mean(o_t * o_t, axis=0, keepdims=True) + EPS)
        sl = slice(r * hd, (r + 1) * hd)
        o_ref[:, sl] = (o_t.T * g_ref[:, sl]).astype(o_ref.dtype)


def _overlap_matrix_t(n_h, n_cmp, n_slc):
    cs = np.arange(n_cmp) * CMP_STRIDE
    js = np.arange(n_slc) * SLC_LEN
    ov = np.minimum(cs[:, None] + CMP_LEN, js[None, :] + SLC_LEN) - np.maximum(cs[:, None], js[None, :])
    out = np.zeros((LANE, n_h), np.float32)
    out[:n_slc, :n_cmp] = (np.clip(ov, 0, None) / CMP_LEN).T
    return out


def nsa_attention(q_rot, kc, kct, k_sw, vt_sw, proj, nsa_g, batch, seq):
    t = q_rot.shape[0]
    tq = NSA_TILE
    per = seq // tq
    n_h = seq // CMP_STRIDE
    n_cmp = (seq - CMP_LEN) // CMP_STRIDE + 1
    n_slc = seq // SLC_LEN
    n_sel = min(N_SLC, n_slc)
    assert n_slc <= LANE and WINDOW % tq == 0 and tq % LANE == 0
    ovt = jnp.asarray(_overlap_matrix_t(n_h, n_cmp, n_slc))
    eaug = np.zeros((seq, LANE), np.float32)
    eaug[np.arange(seq), np.arange(seq) // SLC_LEN] = 1.0
    eaug = jnp.asarray(eaug, BF16)
    hd = HEAD_DIM
    qw = NSA_REP * hd
    rows = NSA_REP * tq
    seq_spec = lambda off: pl.BlockSpec((seq, hd), lambda b, g, i, off=off: (b, off + g))
    vt_spec = lambda off: pl.BlockSpec((1, 1, per, hd, tq), lambda b, g, i, off=off: (b, off + g, 0, 0, 0))
    kern = functools.partial(_nsa_kernel, n_cmp=n_cmp, n_slc=n_slc, n_sel=n_sel)
    return pl.pallas_call(
        kern,
        grid=(batch, N_NSA_KV, per),
        in_specs=[
            pl.BlockSpec((tq, qw), lambda b, g, i: (b * per + i, g)),
            pl.BlockSpec((1, 1, n_h, hd), lambda b, g, i: (b, g, 0, 0)),
            pl.BlockSpec((1, 1, hd, n_h), lambda b, g, i: (b, N_NSA_KV + g, 0, 0)),
            seq_spec(0), seq_spec(N_NSA_KV), vt_spec(0), vt_spec(N_NSA_KV),
            pl.BlockSpec((tq, LANE), lambda b, g, i: (b * per + i, COL_GATE // LANE + g)),
            pl.BlockSpec((LANE, n_h), lambda b, g, i: (0, 0)),
            pl.BlockSpec((seq, LANE), lambda b, g, i: (0, 0)),
            pl.BlockSpec((1, qw), lambda b, g, i: (0, g)),
        ],
        out_specs=pl.BlockSpec((tq, qw), lambda b, g, i: (b * per + i, g)),
        out_shape=jax.ShapeDtypeStruct((t, D_NSA), BF16),
        scratch_shapes=[
            pltpu.VMEM((seq, 2 * hd), BF16),
            pltpu.VMEM((rows, 2 * hd), BF16),
            pltpu.VMEM((tq, rows), F32),
            pltpu.VMEM((1, rows), F32),
            pltpu.VMEM((1, rows), F32),
            pltpu.VMEM((hd, rows), F32),
            pltpu.VMEM((hd, rows), F32),
        ],
        compiler_params=_cparams(("arbitrary", "arbitrary", "arbitrary")),
        name="nsa_attention",
    )(q_rot, kc, kct, k_sw, k_sw, vt_sw, vt_sw, proj, ovt, eaug, nsa_g)


def _retention_kernel(q_ref, k_ref, v_ref, gate_ref, cos_ref, sin_ref, lg_ref, g_ref, o_ref, state):
    c = q_ref.shape[0]
    hd = HEAD_DIM

    @pl.when(pl.program_id(1) == 0)
    def _():
        state[...] = jnp.zeros(state.shape, F32)

    cos, sin = cos_ref[...], sin_ref[...]
    n_row = lax.broadcasted_iota(jnp.int32, (c, 1), 0).astype(F32)
    n_col = lax.broadcasted_iota(jnp.int32, (1, c), 1).astype(F32)
    diff = n_row - n_col
    for h in range(N_RET_HEADS):
        sl = slice(h * hd, (h + 1) * hd)
        lg = lg_ref[h][:, 0:1]
        q = _rotate(q_ref[:, sl].astype(F32), cos, sin, hd // 2)
        k = _rotate(k_ref[:, sl].astype(F32), cos, sin, hd // 2) * hd ** -0.5
        v = v_ref[:, sl].astype(BF16)
        decay = jnp.where(diff >= 0.0, jnp.exp(jnp.maximum(diff, 0.0) * lg), 0.0)
        scores = _dot_nt(q.astype(BF16), k.astype(BF16)) * decay
        o = _dot(scores.astype(BF16), v)
        xi = jnp.exp((n_row + 1.0) * lg)
        o = o + _dot((q * xi).astype(BF16), state[h].astype(BF16))
        zeta = jnp.exp((c - 1.0 - n_row) * lg)
        kz_t = (k * zeta).T.astype(BF16)
        state[h] = state[h] * jnp.exp(c * lg) + _dot(kz_t, v)
        gate = gate_ref[:, sl].astype(F32)
        o_ref[:, sl] = (gate * jax.nn.sigmoid(gate) * (_rms(o) * g_ref[:, sl])).astype(o_ref.dtype)


def retention(proj, cos, sin, ret_g, batch, seq):
    t = proj.shape[0]
    c = 256
    per = seq // c
    hd = HEAD_DIM
    base = COL_RET // D_RET
    blk = lambda off: pl.BlockSpec((c, D_RET), lambda b, s, off=off: (b * per + s, base + off))
    tab = pl.BlockSpec((c, hd), lambda b, s: (b * per + s, 0))
    lg = jnp.log1p(-(2.0 ** (-5.0 - jnp.arange(N_RET_HEADS, dtype=F32))))
    lg = jnp.broadcast_to(lg[:, None, None], (N_RET_HEADS, 1, LANE))
    return pl.pallas_call(
        _retention_kernel,
        grid=(batch, per),
        in_specs=[blk(0), blk(1), blk(2), blk(3), tab, tab,
                  pl.BlockSpec((N_RET_HEADS, 1, LANE), lambda b, s: (0, 0, 0)),
                  pl.BlockSpec((1, D_RET), lambda b, s: (0, 0))],
        out_specs=pl.BlockSpec((c, D_RET), lambda b, s: (b * per + s, 0)),
        out_shape=jax.ShapeDtypeStruct((t, D_RET), BF16),
        scratch_shapes=[pltpu.VMEM((N_RET_HEADS, hd, hd), F32)],
        compiler_params=_cparams(("arbitrary", "arbitrary")),
        name="retention",
    )(proj, proj, proj, proj, cos, sin, lg, ret_g)


def _out_proj_kernel(yc_ref, yn_ref, yr_ref, w_ref, x_ref, g_ref, o_ref):
    acc = _dot(yc_ref[...], w_ref[0:D_CONV, :])
    acc = acc + _dot(yn_ref[...], w_ref[D_CONV:D_CONV + D_NSA, :])
    acc = acc + _dot(yr_ref[...], w_ref[D_CONV + D_NSA:, :])
    o_ref[...] = x_ref[...] + g_ref[0] * acc


def out_projection(y_conv, y_nsa, y_ret, w_out, x, mod, seq):
    t, d = x.shape
    tm = 512
    per = seq // tm
    row = lambda w: pl.BlockSpec((tm, w), lambda i: (i, 0))
    return pl.pallas_call(
        _out_proj_kernel,
        grid=(t // tm,),
        in_specs=[row(D_CONV), row(D_NSA), row(D_RET),
                  pl.BlockSpec((D_MIX, d), lambda i: (0, 0)),
                  row(d),
                  pl.BlockSpec((1, 1, d), lambda i: (i // per, 0, 2))],
        out_specs=row(d),
        out_shape=jax.ShapeDtypeStruct((t, d), F32),
        compiler_params=_cparams(("arbitrary",)),
        name="out_projection",
    )(y_conv, y_nsa, y_ret, w_out, x, mod)


def _rank_rows(val, n):
    idx = lax.broadcasted_iota(jnp.int32, (n, 1), 0)
    rank = jnp.zeros(val.shape, jnp.int32)
    for rp in range(n):
        vp = val[rp:rp + 1, :]
        ahead = (vp > val) | ((vp == val) & (idx > rp))
        rank = rank + ahead.astype(jnp.int32)
    return rank


def _to_token_rows(ref, val):
    chunks = val.shape[1] // LANE
    for c in range(chunks):
        ref[pl.ds(c, val.shape[0], stride=chunks), :] = val[:, c * LANE:(c + 1) * LANE]


def _from_token_rows(ref, n):
    chunks = ref.shape[0] // n
    return jnp.concatenate([ref[pl.ds(c, n, stride=chunks), :] for c in range(chunks)], axis=1)


def _pack_bf16_pairs(v):
    half = v.shape[1] // 2
    lo = pltpu.bitcast(v[:, :half].astype(BF16).astype(F32), jnp.uint32)
    hi = pltpu.bitcast(v[:, half:].astype(BF16).astype(F32), jnp.uint32)
    return hi | (lo >> 16)


def _unpack_bf16_pairs(u):
    return pltpu.bitcast(u << 16, F32), pltpu.bitcast(u & jnp.uint32(0xFFFF0000), F32)


def _router_kernel(x_ref, sh_ref, sc_ref, g_ref, rw_ref, rb_ref, tri_ref, swgu_ref, swd_ref,
                   shared_ref, hrow_ref, pos_ref, w_ref, cnt_ref, carry):
    @pl.when(pl.program_id(0) == 0)
    def _():
        carry[...] = jnp.zeros(carry.shape, F32)

    h = _norm_mod(x_ref[...], g_ref[...], sc_ref[0], sh_ref[0])
    _to_token_rows(hrow_ref, _pack_bf16_pairs(h))
    h_hi = h.astype(BF16)
    shared_ref[...] = _swiglu(h_hi, swgu_ref[...], swd_ref[...], D_SHARED)
    tm = h.shape[0]
    rw = rw_ref[...]
    rw_hi = rw.astype(BF16)
    rw_lo = (rw - rw_hi.astype(F32)).astype(BF16)
    h_lo = (h - h_hi.astype(F32)).astype(BF16)
    both = _dot_nt(jnp.concatenate([rw_hi, rw_lo], axis=0), h_hi)
    scores = jax.nn.sigmoid(both[:N_EXPERTS] + both[N_EXPERTS:] + _dot_nt(rw_hi, h_lo))
    biased = scores + rb_ref[...]
    sub = lax.broadcasted_iota(jnp.int32, (GROUP_SIZE, 1), 0)
    gs = []
    for g in range(N_EXPERT_GROUPS):
        bg = biased[g * GROUP_SIZE:(g + 1) * GROUP_SIZE, :]
        m1 = jnp.max(bg, axis=0, keepdims=True)
        first = jnp.min(jnp.where(bg == m1, sub, GROUP_SIZE), axis=0, keepdims=True)
        m2 = jnp.max(jnp.where(sub == first, -jnp.inf, bg), axis=0, keepdims=True)
        gs.append(m1 + m2)
    gs = jnp.concatenate(gs, axis=0)
    gkeep = _rank_rows(gs, N_EXPERT_GROUPS) < TOPK_GROUPS
    keep = jnp.concatenate(
        [jnp.broadcast_to(gkeep[g:g + 1, :], (GROUP_SIZE, tm)) for g in range(N_EXPERT_GROUPS)], axis=0)
    masked = jnp.where(keep, biased, -jnp.inf)
    e_idx = lax.broadcasted_iota(jnp.int32, (N_EXPERTS, 1), 0)
    picked = jnp.zeros(masked.shape, F32)
    for _ in range(TOP_K):
        top = jnp.max(masked, axis=0, keepdims=True)
        first = jnp.min(jnp.where(masked == top, e_idx, N_EXPERTS), axis=0, keepdims=True)
        hit = e_idx == first
        picked = jnp.where(hit, 1.0, picked)
        masked = jnp.where(hit, -jnp.inf, masked)
    sel = picked > 0.0
    w = jnp.where(sel, scores, 0.0)
    w_ref[...] = ROUTED_SCALE * w / jnp.sum(w, axis=0, keepdims=True)
    self32 = sel.astype(F32)
    pos = carry[:, 0:1] + _dot(self32.astype(BF16), tri_ref[...])
    pos_ref[...] = jnp.where(sel, pos.astype(jnp.int32), -1)
    carry[...] = carry[...] + jnp.sum(self32, axis=1, keepdims=True)
    cnt_ref[...] = carry[...].astype(jnp.int32)


def moe_route(x, mod, norm_g, router_w_t, router_bias, shared_w_gu, shared_w_down, seq):
    t, d = x.shape
    tm = 512
    per = seq // tm
    tri = jnp.asarray(np.triu(np.ones((tm, tm), np.float32), 1), BF16)
    ecol = pl.BlockSpec((N_EXPERTS, tm), lambda i: (0, i))
    return pl.pallas_call(
        _router_kernel,
        grid=(t // tm,),
        in_specs=[
            pl.BlockSpec((tm, d), lambda i: (i, 0)),
            pl.BlockSpec((1, 1, d), lambda i: (i // per, 0, 3)),
            pl.BlockSpec((1, 1, d), lambda i: (i // per, 0, 4)),
            pl.BlockSpec((1, d), lambda i: (0, 0)),
            pl.BlockSpec((N_EXPERTS, d), lambda i: (0, 0)),
            pl.BlockSpec((N_EXPERTS, 1), lambda i: (0, 0)),
            pl.BlockSpec((tm, tm), lambda i: (0, 0)),
            pl.BlockSpec((d, 2 * D_SHARED), lambda i: (0, 0)),
            pl.BlockSpec((D_SHARED, d), lambda i: (0, 0)),
        ],
        out_specs=[pl.BlockSpec((tm, d), lambda i: (i, 0)),
                   pl.BlockSpec((tm * PACK_CHUNKS, LANE), lambda i: (i, 0)), ecol, ecol,
                   pl.BlockSpec((N_EXPERTS, LANE), lambda i: (0, 0))],
        out_shape=[
            jax.ShapeDtypeStruct((t, d), F32),
            jax.ShapeDtypeStruct((t * PACK_CHUNKS, LANE), jnp.uint32),
            jax.ShapeDtypeStruct((N_EXPERTS, t), jnp.int32),
            jax.ShapeDtypeStruct((N_EXPERTS, t), F32),
            jax.ShapeDtypeStruct((N_EXPERTS, LANE), jnp.int32),
        ],
        scratch_shapes=[pltpu.VMEM((N_EXPERTS, LANE), F32)],
        compiler_params=_cparams(("arbitrary",)),
        name="moe_route",
    )(x, mod, mod, norm_g, router_w_t, router_bias, tri, shared_w_gu, shared_w_down)


def _slots_kernel(pos_ref, w_ref, start_ref, tri_ref, slot_ref, w8_ref):
    pos = pos_ref[...]
    sel = pos >= 0
    slot = pos + start_ref[...]
    order = _dot(tri_ref[...], sel.astype(F32).astype(BF16))
    w = w_ref[...]
    for k in range(TOP_K):
        mk = sel & (order == float(k))
        slot_ref[k:k + 1, :] = jnp.sum(jnp.where(mk, slot, 0), axis=0, keepdims=True)
        w8_ref[k:k + 1, :] = jnp.sum(jnp.where(mk, w, 0.0), axis=0, keepdims=True)


def moe_slots(pos_t, w_t, pad_start):
    e, t = pos_t.shape
    tm = 2048
    tri = jnp.asarray(np.tril(np.ones((e, e), np.float32), -1), BF16)
    ecol = pl.BlockSpec((e, tm), lambda i: (0, i))
    kcol = pl.BlockSpec((TOP_K, tm), lambda i: (0, i))
    return pl.pallas_call(
        _slots_kernel,
        grid=(t // tm,),
        in_specs=[ecol, ecol, pl.BlockSpec((e, 1), lambda i: (0, 0)), pl.BlockSpec((e, e), lambda i: (0, 0))],
        out_specs=[kcol, kcol],
        out_shape=[jax.ShapeDtypeStruct((TOP_K, t), jnp.int32), jax.ShapeDtypeStruct((TOP_K, t), F32)],
        compiler_params=_cparams(("arbitrary",)),
        name="moe_slots",
    )(pos_t, w_t, pad_start, tri)


def _inverse_kernel(slot_ref, init_ref, inv_ref, sem, *, n_tok):
    tm = slot_ref.shape[0] // TOP_K
    i = pl.program_id(0)

    @pl.when(i == 0)
    def _():
        cp = pltpu.make_async_copy(init_ref, inv_ref, sem)
        cp.start()
        cp.wait()

    def body(t, carry):
        for k in range(TOP_K):
            inv_ref[slot_ref[t * TOP_K + k]] = k * n_tok + i * tm + t
        return carry

    lax.fori_loop(0, tm, body, 0, unroll=2)


def moe_inverse(slot8, n_slots):
    n_tok = slot8.shape[1]
    tm = 2048
    init = TOP_K * n_tok + jnp.arange(n_slots, dtype=jnp.int32)
    slot_flat = slot8.T.reshape(-1)
    return pl.pallas_call(
        functools.partial(_inverse_kernel, n_tok=n_tok),
        grid=(n_tok // tm,),
        in_specs=[pl.BlockSpec((TOP_K * tm,), lambda i: (i,), memory_space=pltpu.SMEM),
                  pl.BlockSpec(memory_space=pl.ANY)],
        out_specs=pl.BlockSpec(memory_space=pltpu.SMEM),
        out_shape=jax.ShapeDtypeStruct((n_slots,), jnp.int32),
        scratch_shapes=[pltpu.SemaphoreType.DMA(())],
        compiler_params=_cparams(("arbitrary",)),
        name="moe_inverse",
    )(slot_flat, init)


def _swiglu(x_bf16, w_gu, w_down, d_hidden):
    gu = _dot(x_bf16, w_gu)
    gate, up = gu[:, :d_hidden], gu[:, d_hidden:]
    return _dot((gate * jax.nn.sigmoid(gate) * up).astype(BF16), w_down)


def _experts_kernel(be_ref, first_ref, nb_ref, src_ref, dst_ref, h_hbm, wgu_ref, wd_ref, y_hbm,
                    x0, x1, x2, y0, y1, y2, wgu_bf, wd_bf, gsem, ssem, *, n_tok, n_blocks):
    del be_ref
    s = pl.program_id(0)
    cb = s - 2
    used = (cb >= 0) & (cb < nb_ref[0])
    rc = PACK_CHUNKS
    rows = x0.shape[0] // rc
    blocks_per_chunk = SMEM_CHUNK // rows
    src_off = lax.rem(jnp.minimum(s, n_blocks - 1), blocks_per_chunk) * rows
    dst_off = lax.rem(s, blocks_per_chunk) * rows

    @pl.when(s == 0)
    def _():
        for y in (y0, y1, y2):
            y[...] = jnp.zeros(y.shape, y.dtype)

    @pl.when(used & (first_ref[jnp.clip(cb, 0, n_blocks - 1)] == 1))
    def _():
        wgu_bf[...] = wgu_ref[0, 0].astype(BF16)
        wd_bf[...] = wd_ref[0, 0].astype(BF16)

    def step(p, x_out, x_in, y_out, y_in, x_mid, y_mid):
        q = (p + 1) % 3

        def gathered(buf, sem):
            return pltpu.make_async_copy(h_hbm.at[pl.ds(0, rows * rc), :], buf, sem)

        def scattered(buf, sem):
            return pltpu.make_async_copy(buf, y_hbm.at[pl.ds(0, rows * rc), :], sem)

        last_live = nb_ref[0] + 2
        last = pl.num_programs(0) - 1

        @pl.when((s >= 2) & (s - 2 <= last_live))
        def _():
            gathered(x_in, gsem.at[q]).wait()
            scattered(y_out, ssem.at[q]).wait()

        def issue_rows():
            for r in range(rows):
                src = pl.multiple_of((src_ref[src_off + r] & (n_tok - 1)) * rc, rc)
                dst = pl.multiple_of(dst_ref[dst_off + r] * rc, rc)
                pltpu.make_async_copy(h_hbm.at[pl.ds(src, rc), :], x_out.at[pl.ds(r * rc, rc), :], gsem.at[p]).start()
                pltpu.make_async_copy(y_in.at[pl.ds(r * rc, rc), :], y_hbm.at[pl.ds(dst, rc), :], ssem.at[p]).start()

        @pl.when(used)
        def _():
            issue_rows()
            x = jnp.concatenate(_unpack_bf16_pairs(_from_token_rows(x_in, rows)), axis=1).astype(BF16)
            _to_token_rows(y_out, _pack_bf16_pairs(_swiglu(x, wgu_bf[...], wd_bf[...], D_EXPERT)))

        @pl.when(jnp.logical_not(used) & (s <= last_live))
        def _():
            issue_rows()

        r = (p + 2) % 3

        @pl.when((s == last) & (s - 1 <= last_live))
        def _():
            gathered(x_mid, gsem.at[r]).wait()
            scattered(y_mid, ssem.at[r]).wait()

        @pl.when((s == last) & (s <= last_live))
        def _():
            gathered(x_out, gsem.at[p]).wait()
            scattered(y_in, ssem.at[p]).wait()

    xs, ys = (x0, x1, x2), (y0, y1, y2)
    for p in range(3):
        @pl.when(lax.rem(s, 3) == p)
        def _(p=p):
            step(p, xs[p], xs[(p + 1) % 3], ys[(p + 1) % 3], ys[p], xs[(p + 2) % 3], ys[(p + 2) % 3])


def moe_experts(block_e, first, n_used, inv, h_rows, w_gu, w_down, layer):
    n_tok, d = h_rows.shape[0] // PACK_CHUNKS, D_MODEL
    n_slots = inv.shape[0]
    n_blocks = n_slots // MOE_BLOCK
    assert n_tok & (n_tok - 1) == 0, "source token = inv mod T uses a power-of-two T"
    assert n_slots % SMEM_CHUNK == 0 and SMEM_CHUNK % MOE_BLOCK == 0
    per_chunk = SMEM_CHUNK // MOE_BLOCK
    lag = 3
    n_rows = TOP_K * n_tok + n_slots + lag * MOE_BLOCK
    warmup = TOP_K * n_tok + n_slots + jnp.arange(lag * MOE_BLOCK, dtype=jnp.int32)
    tail = jnp.zeros((SMEM_CHUNK - lag * MOE_BLOCK,), jnp.int32)
    dst = jnp.concatenate([warmup, inv, tail])
    clamp = lambda b: jnp.clip(b, 0, n_blocks - 1)
    smem_blk = lambda f: pl.BlockSpec((SMEM_CHUNK,), f, memory_space=pltpu.SMEM)
    grid_spec = pltpu.PrefetchScalarGridSpec(
        num_scalar_prefetch=3,
        grid=(n_blocks + lag,),
        in_specs=[
            smem_blk(lambda s, be, fi, nb: (clamp(s) // per_chunk,)),
            smem_blk(lambda s, be, fi, nb: (s // per_chunk,)),
            pl.BlockSpec(memory_space=pl.ANY),
            pl.BlockSpec((1, 1, d, 2 * D_EXPERT), lambda s, be, fi, nb: (layer, be[clamp(s - 2)], 0, 0)),
            pl.BlockSpec((1, 1, D_EXPERT, d), lambda s, be, fi, nb: (layer, be[clamp(s - 2)], 0, 0)),
        ],
        out_specs=pl.BlockSpec(memory_space=pl.ANY),
        scratch_shapes=[pltpu.VMEM((MOE_BLOCK * PACK_CHUNKS, LANE), jnp.uint32)] * 6 + [
            pltpu.VMEM((d, 2 * D_EXPERT), BF16), pltpu.VMEM((D_EXPERT, d), BF16),
            pltpu.SemaphoreType.DMA((3,)), pltpu.SemaphoreType.DMA((3,))],
    )
    return pl.pallas_call(
        functools.partial(_experts_kernel, n_tok=n_tok, n_blocks=n_blocks),
        grid_spec=grid_spec,
        out_shape=jax.ShapeDtypeStruct((n_rows * PACK_CHUNKS, LANE), jnp.uint32),
        compiler_params=_cparams(("arbitrary",)),
        name="moe_experts",
    )(block_e, first, n_used, inv, dst, h_rows, w_gu, w_down)


def _combine_kernel(w_ref, sh_ref, x_ref, g_ref, fg_ref, *rest, final_norm):
    y_refs, o_ref, sum_scr = rest[:TOP_K], rest[TOP_K], rest[TOP_K + 1]
    tm = x_ref.shape[0]

    def token(t, carry):
        rows = pl.ds(pl.multiple_of(t * PACK_CHUNKS, PACK_CHUNKS), PACK_CHUNKS)
        lo, hi = _unpack_bf16_pairs(y_refs[0][rows, :])
        acc_lo, acc_hi = w_ref[0, t] * lo, w_ref[0, t] * hi
        for k in range(1, TOP_K):
            lo, hi = _unpack_bf16_pairs(y_refs[k][rows, :])
            acc_lo, acc_hi = acc_lo + w_ref[k, t] * lo, acc_hi + w_ref[k, t] * hi
        base = pl.multiple_of(t * ROW_CHUNKS, ROW_CHUNKS)
        sum_scr[pl.ds(base, PACK_CHUNKS), :] = acc_lo
        sum_scr[pl.ds(base + PACK_CHUNKS, PACK_CHUNKS), :] = acc_hi
        return carry

    lax.fori_loop(0, tm, token, 0, unroll=4)
    routed = _from_token_rows(sum_scr, tm)
    out = x_ref[...] + g_ref[0] * (routed + sh_ref[...])
    if final_norm:
        out = _rms(out) * fg_ref[...]
    o_ref[...] = out


def moe_combine(w8, shared, x, mod, final_g, y, seq, final_norm):
    t, d = x.shape
    tm = 256
    per = seq // tm
    row = pl.BlockSpec((tm, d), lambda i: (i, 0))
    y_specs = [pl.BlockSpec((tm * PACK_CHUNKS, LANE), lambda i, k=k: (k * (t // tm) + i, 0)) for k in range(TOP_K)]
    return pl.pallas_call(
        functools.partial(_combine_kernel, final_norm=final_norm),
        grid=(t // tm,),
        in_specs=[
            pl.BlockSpec((TOP_K, tm), lambda i: (0, i), memory_space=pltpu.SMEM),
            row, row,
            pl.BlockSpec((1, 1, d), lambda i: (i // per, 0, 5)),
            pl.BlockSpec((1, d), lambda i: (0, 0)),
        ] + y_specs,
        out_specs=row,
        out_shape=jax.ShapeDtypeStruct((t, d), F32),
        scratch_shapes=[pltpu.VMEM((tm * ROW_CHUNKS, LANE), F32)],
        compiler_params=_cparams(("arbitrary",)),
        name="moe_combine",
    )(w8, shared, x, mod, final_g, *([y] * TOP_K))


def token_mixer_layer(x, mod, tabs, p, batch, seq):
    cos_n, sin_n, cos_r, sin_r = tabs
    proj = in_projection(x, mod, p["norm1_g"], p["w_in"], seq)
    y_conv = short_conv(proj, p["conv_w_t"], p["conv_g"], batch, seq)
    q_rot, kvc, k_sw, vt_sw = nsa_rope(proj, cos_n, sin_n, batch, seq)
    kc, kct = compress(kvc, p["cmp_pe"], p["cmp_w1"], p["cmp_w2"], batch, seq)
    y_nsa = nsa_attention(q_rot, kc, kct, k_sw, vt_sw, proj, p["nsa_g"], batch, seq)
    y_ret = retention(proj, cos_r, sin_r, p["ret_g"], batch, seq)
    return out_projection(y_conv, y_nsa, y_ret, p["w_out"], x, mod, seq)


def moe_layer(x, mod, p, seq, final_g, final_norm):
    t, d = x.shape
    n_assign = t * TOP_K
    n_blocks = (n_assign + N_EXPERTS * (MOE_BLOCK - 1) + MOE_BLOCK - 1) // MOE_BLOCK
    shared, h_rows, pos_t, w_t, counts = moe_route(x, mod, p["norm2_g"], p["router_w_t"], p["router_bias"],
                                                   p["shared_w_gu"], p["shared_w_down"], seq)
    counts = counts[:, 0]
    padded = (counts + MOE_BLOCK - 1) // MOE_BLOCK * MOE_BLOCK
    pad_end = jnp.cumsum(padded)
    pad_start = (pad_end - padded).astype(jnp.int32)
    blk0 = jnp.arange(n_blocks, dtype=jnp.int32) * MOE_BLOCK
    block_e = jnp.minimum(jnp.sum(pad_end[None, :] <= blk0[:, None], axis=1), N_EXPERTS - 1).astype(jnp.int32)
    n_used = (pad_end[-1:] // MOE_BLOCK).astype(jnp.int32)
    first = jnp.concatenate([jnp.ones((1,), jnp.int32), (block_e[1:] != block_e[:-1]).astype(jnp.int32)])
    slot8, w8 = moe_slots(pos_t, w_t, pad_start[:, None])
    inv = moe_inverse(slot8, n_blocks * MOE_BLOCK)
    y = moe_experts(block_e, first, n_used, inv, h_rows, p["exp_w_gu"], p["exp_w_down"], p["layer"])
    return moe_combine(w8, shared, x, mod, final_g, y, seq, final_norm)


def _reorder_w_in(w_in):
    c_gate = COL_KV + 6 * D_NSA_KV
    gates = w_in[:, c_gate:c_gate + 3 * N_NSA_HEADS]
    per_group = 3 * NSA_REP
    padded = [jnp.pad(gates[:, g * per_group:(g + 1) * per_group], ((0, 0), (0, LANE - per_group)))
              for g in range(N_NSA_KV)]
    return jnp.concatenate([w_in[:, :c_gate], w_in[:, c_gate + 3 * N_NSA_HEADS:]] + padded, axis=1)


def kernel(x, c, positions, ada_w, ada_b, norm1_g, norm2_g, w_in, conv_w, conv_g, cmp_pe, cmp_w1, cmp_w2,
           nsa_g, ret_g, w_out, router_w, router_bias, exp_w_gu, exp_w_down, shared_w_gu, shared_w_down, final_g):
    batch, seq, d = x.shape
    depth = ada_w.shape[0]
    t = batch * seq
    mod_all = ada_modulation(c, ada_w, ada_b)
    pos_col = positions.reshape(t, 1)
    tabs = rope_tables(pos_col, ROPE_DIM, ROPE_THETA) + rope_tables(pos_col, HEAD_DIM, RET_THETA)
    xt = x.reshape(t, d)
    final_g2 = final_g.reshape(1, d)
    for l in range(depth):
        mod = mod_all[l].reshape(batch, 1, ADA_CHUNKS * d)
        p = dict(
            norm1_g=norm1_g[l].reshape(1, d),
            norm2_g=norm2_g[l].reshape(1, d),
            w_in=_reorder_w_in(w_in[l]).astype(BF16),
            conv_w_t=conv_w[l].T,
            conv_g=conv_g[l].reshape(1, D_CONV),
            cmp_pe=cmp_pe[l].reshape(2, 1, CMP_LEN * HEAD_DIM),
            cmp_w1=cmp_w1[l].astype(BF16),
            cmp_w2=cmp_w2[l].astype(BF16),
            nsa_g=nsa_g[l].reshape(1, D_NSA),
            ret_g=ret_g[l].reshape(1, D_RET),
            w_out=w_out[l].astype(BF16),
            router_w_t=router_w[l].T,
            router_bias=router_bias[l].reshape(N_EXPERTS, 1),
            layer=l,
            exp_w_gu=exp_w_gu,
            exp_w_down=exp_w_down,
            shared_w_gu=shared_w_gu[l].astype(BF16),
            shared_w_down=shared_w_down[l].astype(BF16),
        )
        xt = token_mixer_layer(xt, mod, tabs, p, batch, seq)
        xt = moe_layer(xt, mod, p, seq, final_g2, final_norm=(l == depth - 1))
    return xt.reshape(batch, seq, d)
```
